```python
import jax, jax.numpy as jnp
from jax import lax
import numpy as np

D_MODEL = 2048
BATCH = 8
SEQ = 2048
DEPTH = 4

N_MIXERS = 2
N_LAYERS_A = (DEPTH + 1) // 2
N_LAYERS_B = DEPTH // 2
EPS = 1e-6

D_FF = 5632

GM_CHUNK = 128
GM_INNER = 2 * D_MODEL
GM_GROUPS = 16
GM_GROUP_DIM = GM_INNER // GM_GROUPS

SSM_INNER = 2 * D_MODEL
SSM_HEAD_DIM = 64
SSM_HEADS = SSM_INNER // SSM_HEAD_DIM
SSM_GROUPS = 8
SSM_HEADS_PER_GROUP = SSM_HEADS // SSM_GROUPS
SSM_STATE = 128
SSM_CONV = 4
SSM_CHUNK = 128
SSM_BC = SSM_GROUPS * SSM_STATE
SSM_CONV_DIM = SSM_INNER + 2 * SSM_BC
SSM_PROJ = SSM_INNER + SSM_CONV_DIM + SSM_HEADS

kernel_name = 'hybrid_gmlp_ssd_macaron_trunk'


def rmsnorm(x, g):
    xf = x.astype(jnp.float32)
    y = xf * lax.rsqrt(jnp.mean(xf * xf, axis=-1, keepdims=True) + EPS)
    return (y * g.astype(jnp.float32)).astype(x.dtype)


def swiglu(h, w_in, w_out):
    gate, up = jnp.split(h @ w_in, 2, axis=-1)
    return (jax.nn.silu(gate) * up) @ w_out


def gmlp_mixer(h, w_in, v_norm, w_s, b_s, w_out):
    bsz, seq, _ = h.shape
    n_chunks = seq // GM_CHUNK
    z = jax.nn.gelu(h @ w_in, approximate=False)
    u, v = jnp.split(z, 2, axis=-1)
    v = rmsnorm(v, v_norm)
    v = v.reshape(bsz, n_chunks, GM_CHUNK, GM_GROUPS, GM_GROUP_DIM)
    causal = jnp.tril(jnp.ones((GM_CHUNK, GM_CHUNK), dtype=bool))
    w_causal = jnp.where(causal[None], w_s, jnp.zeros_like(w_s))
    mixed = jnp.einsum('gts,bnsgc->bntgc', w_causal, v) + b_s.T[:, :, None]
    gated = u * mixed.reshape(bsz, seq, GM_INNER)
    return gated @ w_out


def ssd_mixer(h, w_in, conv_w, conv_b, dt_bias, a_log, d_skip, norm_g, w_out):
    bsz, seq, _ = h.shape
    n_chunks = seq // SSM_CHUNK
    f32 = jnp.float32
    proj = h @ w_in
    z, xbc, dt = jnp.split(proj, [SSM_INNER, SSM_INNER + SSM_CONV_DIM], axis=-1)
    xbc = lax.conv_general_dilated(
        xbc, conv_w.T[:, None, :], window_strides=(1,), padding=[(SSM_CONV - 1, 0)],
        dimension_numbers=('NWC', 'WIO', 'NWC'), feature_group_count=SSM_CONV_DIM) + conv_b
    xbc = jax.nn.silu(xbc)
    xs, bm, cm = jnp.split(xbc, [SSM_INNER, SSM_INNER + SSM_BC], axis=-1)

    dt = jax.nn.softplus(dt.astype(f32) + dt_bias.astype(f32))
    a = -jnp.exp(a_log.astype(f32)).reshape(SSM_GROUPS, SSM_HEADS_PER_GROUP)
    x_h = xs.astype(f32).reshape(bsz, n_chunks, SSM_CHUNK, SSM_GROUPS, SSM_HEADS_PER_GROUP, SSM_HEAD_DIM)
    b_g = bm.astype(f32).reshape(bsz, n_chunks, SSM_CHUNK, SSM_GROUPS, SSM_STATE)
    c_g = cm.astype(f32).reshape(bsz, n_chunks, SSM_CHUNK, SSM_GROUPS, SSM_STATE)
    dt_c = dt.reshape(bsz, n_chunks, SSM_CHUNK, SSM_GROUPS, SSM_HEADS_PER_GROUP)
    xdt = x_h * dt_c[..., None]
    a_cum = jnp.cumsum(dt_c * a, axis=2).transpose(0, 1, 3, 4, 2)

    causal = jnp.tril(jnp.ones((SSM_CHUNK, SSM_CHUNK), dtype=bool))
    seg = a_cum[..., :, None] - a_cum[..., None, :]
    decay = jnp.exp(jnp.where(causal, seg, -jnp.inf))
    cb = jnp.einsum('bclgn,bcsgn->bcgls', c_g, b_g)
    scores = cb[:, :, :, None] * decay
    y_diag = jnp.einsum('bcgjls,bcsgjp->bclgjp', scores, xdt)

    decay_to_end = jnp.exp(a_cum[..., -1:] - a_cum)
    states = jnp.einsum('bclgn,bcgjl,bclgjp->bcgjpn', b_g, decay_to_end, xdt)
    chunk_decay = jnp.exp(a_cum[..., -1])

    def step(carry, inp):
        st, dec = inp
        return carry * dec[..., None, None] + st, carry

    init = jnp.zeros((bsz, SSM_GROUPS, SSM_HEADS_PER_GROUP, SSM_HEAD_DIM, SSM_STATE), f32)
    _, prev_states = lax.scan(step, init, (jnp.moveaxis(states, 1, 0), jnp.moveaxis(chunk_decay, 1, 0)))
    prev_states = jnp.moveaxis(prev_states, 0, 1)
    y_off = jnp.einsum('bclgn,bcgjpn,bcgjl->bclgjp', c_g, prev_states, jnp.exp(a_cum))

    y = y_diag + y_off + x_h * d_skip.astype(f32).reshape(SSM_GROUPS, SSM_HEADS_PER_GROUP)[:, :, None]
    y = y.reshape(bsz, seq, SSM_INNER) * jax.nn.silu(z.astype(f32))
    yg = y.reshape(bsz, seq, SSM_GROUPS, SSM_INNER // SSM_GROUPS)
    yg = yg * lax.rsqrt(jnp.mean(yg * yg, axis=-1, keepdims=True) + EPS)
    y = (yg.reshape(bsz, seq, SSM_INNER) * norm_g.astype(f32)).astype(h.dtype)
    return y @ w_out


def _fwd_setup_inputs(seed: int = 0) -> dict:
    key = jax.random.key(seed)
    ks = jax.random.split(key, 32)

    def dense(k, shape, fan_in):
        return jax.random.normal(k, shape, jnp.float32) * (fan_in ** -0.5)

    def gain(k, shape):
        return 1.0 + 0.05 * jax.random.normal(k, shape, jnp.float32)

    x = jax.random.normal(ks[0], (BATCH, SEQ, D_MODEL), jnp.float32)
    dt0 = jnp.exp(jax.random.uniform(ks[20], (N_LAYERS_B, SSM_HEADS), jnp.float32)
                  * (np.log(0.1) - np.log(0.001)) + np.log(0.001))
    return {
        'x': x,
        'ln_ffn_pre': gain(ks[1], (DEPTH, D_MODEL)),
        'ffn_pre_w_in': dense(ks[2], (DEPTH, D_MODEL, 2 * D_FF), D_MODEL),
        'ffn_pre_w_out': dense(ks[3], (DEPTH, D_FF, D_MODEL), D_FF),
        'ln_mix': gain(ks[4], (DEPTH, D_MODEL)),
        'ln_ffn_post': gain(ks[5], (DEPTH, D_MODEL)),
        'ffn_post_w_in': dense(ks[6], (DEPTH, D_MODEL, 2 * D_FF), D_MODEL),
        'ffn_post_w_out': dense(ks[7], (DEPTH, D_FF, D_MODEL), D_FF),
        'gm_w_in': dense(ks[8], (N_LAYERS_A, D_MODEL, 2 * GM_INNER), D_MODEL),
        'gm_v_norm': gain(ks[9], (N_LAYERS_A, GM_INNER)),
        'gm_w_s': dense(ks[10], (N_LAYERS_A, GM_GROUPS, GM_CHUNK, GM_CHUNK), GM_CHUNK),
        'gm_b_s': 1.0 + 0.1 * jax.random.normal(ks[11], (N_LAYERS_A, GM_GROUPS, GM_CHUNK), jnp.float32),
        'gm_w_out': dense(ks[12], (N_LAYERS_A, GM_INNER, D_MODEL), GM_INNER),
        'ssm_w_in': dense(ks[13], (N_LAYERS_B, D_MODEL, SSM_PROJ), D_MODEL),
        'ssm_conv_w': dense(ks[14], (N_LAYERS_B, SSM_CONV_DIM, SSM_CONV), SSM_CONV),
        'ssm_conv_b': 0.01 * jax.random.normal(ks[15], (N_LAYERS_B, SSM_CONV_DIM), jnp.float32),
        'ssm_dt_bias': dt0 + jnp.log(-jnp.expm1(-dt0)),
        'ssm_a_log': jnp.log(jax.random.uniform(ks[16], (N_LAYERS_B, SSM_HEADS), jnp.float32, 1.0, 16.0)),
        'ssm_d': gain(ks[17], (N_LAYERS_B, SSM_HEADS)),
        'ssm_norm': gain(ks[18], (N_LAYERS_B, SSM_INNER)),
        'ssm_w_out': dense(ks[19], (N_LAYERS_B, SSM_INNER, D_MODEL), SSM_INNER),
        'ln_final': gain(ks[21], (D_MODEL,)),
    }


def _fwd_reference(x, ln_ffn_pre, ffn_pre_w_in, ffn_pre_w_out, ln_mix, ln_ffn_post, ffn_post_w_in,
              ffn_post_w_out, gm_w_in, gm_v_norm, gm_w_s, gm_b_s, gm_w_out, ssm_w_in, ssm_conv_w,
              ssm_conv_b, ssm_dt_bias, ssm_a_log, ssm_d, ssm_norm, ssm_w_out, ln_final):
    for i in range(DEPTH):
        x = x + 0.5 * swiglu(rmsnorm(x, ln_ffn_pre[i]), ffn_pre_w_in[i], ffn_pre_w_out[i])
        h = rmsnorm(x, ln_mix[i])
        j = i // N_MIXERS
        if i % N_MIXERS == 0:
            m = gmlp_mixer(h, gm_w_in[j], gm_v_norm[j], gm_w_s[j], gm_b_s[j], gm_w_out[j])
        else:
            m = ssd_mixer(h, ssm_w_in[j], ssm_conv_w[j], ssm_conv_b[j], ssm_dt_bias[j],
                          ssm_a_log[j], ssm_d[j], ssm_norm[j], ssm_w_out[j])
        x = x + m
        x = x + 0.5 * swiglu(rmsnorm(x, ln_ffn_post[i]), ffn_post_w_in[i], ffn_post_w_out[i])
    return rmsnorm(x, ln_final)


import jax as _jax
import jax.numpy as _jnp

TWIN_FORMAT = 'train_step'
FWD_PARAMS = ['x', 'ln_ffn_pre', 'ffn_pre_w_in', 'ffn_pre_w_out', 'ln_mix', 'ln_ffn_post', 'ffn_post_w_in', 'ffn_post_w_out', 'gm_w_in', 'gm_v_norm', 'gm_w_s', 'gm_b_s', 'gm_w_out', 'ssm_w_in', 'ssm_conv_w', 'ssm_conv_b', 'ssm_dt_bias', 'ssm_a_log', 'ssm_d', 'ssm_norm', 'ssm_w_out', 'ln_final']
TWIN_WEIGHTS = ['ln_ffn_pre', 'ffn_pre_w_in', 'ffn_pre_w_out', 'ln_mix', 'ln_ffn_post', 'ffn_post_w_in', 'ffn_post_w_out', 'gm_w_in', 'gm_v_norm', 'gm_w_s', 'gm_b_s', 'gm_w_out', 'ssm_w_in', 'ssm_conv_w', 'ssm_conv_b', 'ssm_dt_bias', 'ssm_a_log', 'ssm_d', 'ssm_norm', 'ssm_w_out', 'ln_final']
TWIN_DIFF_INPUT = 'x'
TWIN_INPUTS = ['x', 'ln_ffn_pre', 'ffn_pre_w_in', 'ffn_pre_w_out', 'ln_mix', 'ln_ffn_post', 'ffn_post_w_in', 'ffn_post_w_out', 'gm_w_in', 'gm_v_norm', 'gm_w_s', 'gm_b_s', 'gm_w_out', 'ssm_w_in', 'ssm_conv_w', 'ssm_conv_b', 'ssm_dt_bias', 'ssm_a_log', 'ssm_d', 'ssm_norm', 'ssm_w_out', 'ln_final', 'loss_target', 'm_ln_ffn_pre', 'm_ffn_pre_w_in', 'm_ffn_pre_w_out', 'm_ln_mix', 'm_ln_ffn_post', 'm_ffn_post_w_in', 'm_ffn_post_w_out', 'm_gm_w_in', 'm_gm_v_norm', 'm_gm_w_s', 'm_gm_b_s', 'm_gm_w_out', 'm_ssm_w_in', 'm_ssm_conv_w', 'm_ssm_conv_b', 'm_ssm_dt_bias', 'm_ssm_a_log', 'm_ssm_d', 'm_ssm_norm', 'm_ssm_w_out', 'm_ln_final', 'v_ln_ffn_pre', 'v_ffn_pre_w_in', 'v_ffn_pre_w_out', 'v_ln_mix', 'v_ln_ffn_post', 'v_ffn_post_w_in', 'v_ffn_post_w_out', 'v_gm_w_in', 'v_gm_v_norm', 'v_gm_w_s', 'v_gm_b_s', 'v_gm_w_out', 'v_ssm_w_in', 'v_ssm_conv_w', 'v_ssm_conv_b', 'v_ssm_dt_bias', 'v_ssm_a_log', 'v_ssm_d', 'v_ssm_norm', 'v_ssm_w_out', 'v_ln_final']
TWIN_OUTPUTS = ['loss', 'grad_x', 'grad_ln_ffn_pre', 'grad_ffn_pre_w_in', 'grad_ffn_pre_w_out', 'grad_ln_mix', 'grad_ln_ffn_post', 'grad_ffn_post_w_in', 'grad_ffn_post_w_out', 'grad_gm_w_in', 'grad_gm_v_norm', 'grad_gm_w_s', 'grad_gm_b_s', 'grad_gm_w_out', 'grad_ssm_w_in', 'grad_ssm_conv_w', 'grad_ssm_conv_b', 'grad_ssm_dt_bias', 'grad_ssm_a_log', 'grad_ssm_d', 'grad_ssm_norm', 'grad_ssm_w_out', 'grad_ln_final', 'delta_ln_ffn_pre', 'delta_ffn_pre_w_in', 'delta_ffn_pre_w_out', 'delta_ln_mix', 'delta_ln_ffn_post', 'delta_ffn_post_w_in', 'delta_ffn_post_w_out', 'delta_gm_w_in', 'delta_gm_v_norm', 'delta_gm_w_s', 'delta_gm_b_s', 'delta_gm_w_out', 'delta_ssm_w_in', 'delta_ssm_conv_w', 'delta_ssm_conv_b', 'delta_ssm_dt_bias', 'delta_ssm_a_log', 'delta_ssm_d', 'delta_ssm_norm', 'delta_ssm_w_out', 'delta_ln_final', 'new_m_ln_ffn_pre', 'new_m_ffn_pre_w_in', 'new_m_ffn_pre_w_out', 'new_m_ln_mix', 'new_m_ln_ffn_post', 'new_m_ffn_post_w_in', 'new_m_ffn_post_w_out', 'new_m_gm_w_in', 'new_m_gm_v_norm', 'new_m_gm_w_s', 'new_m_gm_b_s', 'new_m_gm_w_out', 'new_m_ssm_w_in', 'new_m_ssm_conv_w', 'new_m_ssm_conv_b', 'new_m_ssm_dt_bias', 'new_m_ssm_a_log', 'new_m_ssm_d', 'new_m_ssm_norm', 'new_m_ssm_w_out', 'new_m_ln_final', 'new_v_ln_ffn_pre', 'new_v_ffn_pre_w_in', 'new_v_ffn_pre_w_out', 'new_v_ln_mix', 'new_v_ln_ffn_post', 'new_v_ffn_post_w_in', 'new_v_ffn_post_w_out', 'new_v_gm_w_in', 'new_v_gm_v_norm', 'new_v_gm_w_s', 'new_v_gm_b_s', 'new_v_gm_w_out', 'new_v_ssm_w_in', 'new_v_ssm_conv_w', 'new_v_ssm_conv_b', 'new_v_ssm_dt_bias', 'new_v_ssm_a_log', 'new_v_ssm_d', 'new_v_ssm_norm', 'new_v_ssm_w_out', 'new_v_ln_final']
TWIN_LEAF_KINDS = {'loss': 'loss', 'grad_x': 'grad_x', 'grad_ln_ffn_pre': 'grad_w', 'grad_ffn_pre_w_in': 'grad_w', 'grad_ffn_pre_w_out': 'grad_w', 'grad_ln_mix': 'grad_w', 'grad_ln_ffn_post': 'grad_w', 'grad_ffn_post_w_in': 'grad_w', 'grad_ffn_post_w_out': 'grad_w', 'grad_gm_w_in': 'grad_w', 'grad_gm_v_norm': 'grad_w', 'grad_gm_w_s': 'grad_w', 'grad_gm_b_s': 'grad_w', 'grad_gm_w_out': 'grad_w', 'grad_ssm_w_in': 'grad_w', 'grad_ssm_conv_w': 'grad_w', 'grad_ssm_conv_b': 'grad_w', 'grad_ssm_dt_bias': 'grad_w', 'grad_ssm_a_log': 'grad_w', 'grad_ssm_d': 'grad_w', 'grad_ssm_norm': 'grad_w', 'grad_ssm_w_out': 'grad_w', 'grad_ln_final': 'grad_w', 'delta_ln_ffn_pre': 'delta_w', 'delta_ffn_pre_w_in': 'delta_w', 'delta_ffn_pre_w_out': 'delta_w', 'delta_ln_mix': 'delta_w', 'delta_ln_ffn_post': 'delta_w', 'delta_ffn_post_w_in': 'delta_w', 'delta_ffn_post_w_out': 'delta_w', 'delta_gm_w_in': 'delta_w', 'delta_gm_v_norm': 'delta_w', 'delta_gm_w_s': 'delta_w', 'delta_gm_b_s': 'delta_w', 'delta_gm_w_out': 'delta_w', 'delta_ssm_w_in': 'delta_w', 'delta_ssm_conv_w': 'delta_w', 'delta_ssm_conv_b': 'delta_w', 'delta_ssm_dt_bias': 'delta_w', 'delta_ssm_a_log': 'delta_w', 'delta_ssm_d': 'delta_w', 'delta_ssm_norm': 'delta_w', 'delta_ssm_w_out': 'delta_w', 'delta_ln_final': 'delta_w', 'new_m_ln_ffn_pre': 'new_m', 'new_m_ffn_pre_w_in': 'new_m', 'new_m_ffn_pre_w_out': 'new_m', 'new_m_ln_mix': 'new_m', 'new_m_ln_ffn_post': 'new_m', 'new_m_ffn_post_w_in': 'new_m', 'new_m_ffn_post_w_out': 'new_m', 'new_m_gm_w_in': 'new_m', 'new_m_gm_v_norm': 'new_m', 'new_m_gm_w_s': 'new_m', 'new_m_gm_b_s': 'new_m', 'new_m_gm_w_out': 'new_m', 'new_m_ssm_w_in': 'new_m', 'new_m_ssm_conv_w': 'new_m', 'new_m_ssm_conv_b': 'new_m', 'new_m_ssm_dt_bias': 'new_m', 'new_m_ssm_a_log': 'new_m', 'new_m_ssm_d': 'new_m', 'new_m_ssm_norm': 'new_m', 'new_m_ssm_w_out': 'new_m', 'new_m_ln_final': 'new_m', 'new_v_ln_ffn_pre': 'new_v', 'new_v_ffn_pre_w_in': 'new_v', 'new_v_ffn_pre_w_out': 'new_v', 'new_v_ln_mix': 'new_v', 'new_v_ln_ffn_post': 'new_v', 'new_v_ffn_post_w_in': 'new_v', 'new_v_ffn_post_w_out': 'new_v', 'new_v_gm_w_in': 'new_v', 'new_v_gm_v_norm': 'new_v', 'new_v_gm_w_s': 'new_v', 'new_v_gm_b_s': 'new_v', 'new_v_gm_w_out': 'new_v', 'new_v_ssm_w_in': 'new_v', 'new_v_ssm_conv_w': 'new_v', 'new_v_ssm_conv_b': 'new_v', 'new_v_ssm_dt_bias': 'new_v', 'new_v_ssm_a_log': 'new_v', 'new_v_ssm_d': 'new_v', 'new_v_ssm_norm': 'new_v', 'new_v_ssm_w_out': 'new_v', 'new_v_ln_final': 'new_v'}


def _forward(args):
    return _fwd_reference(*[args[k] for k in FWD_PARAMS])


def _output_shape():
    out = _jax.eval_shape(lambda: _forward(_fwd_setup_inputs(0)))
    return out.shape, out.dtype

N_MICROBATCH = 1
ADAM_LR = 0.001
ADAM_B1 = 0.9
ADAM_B2 = 0.999
ADAM_EPS = 1e-08
ADAM_WD = 0.01
ADAM_STEP = 10
PER_EXAMPLE_BATCH_AXIS = {'x': 0, 'loss_target': 0}
SHARED_INPUTS = []
_WEIGHT_DTYPES = {'ln_ffn_pre': _jnp.float32, 'ffn_pre_w_in': _jnp.float32, 'ffn_pre_w_out': _jnp.float32, 'ln_mix': _jnp.float32, 'ln_ffn_post': _jnp.float32, 'ffn_post_w_in': _jnp.float32, 'ffn_post_w_out': _jnp.float32, 'gm_w_in': _jnp.float32, 'gm_v_norm': _jnp.float32, 'gm_w_s': _jnp.float32, 'gm_b_s': _jnp.float32, 'gm_w_out': _jnp.float32, 'ssm_w_in': _jnp.float32, 'ssm_conv_w': _jnp.float32, 'ssm_conv_b': _jnp.float32, 'ssm_dt_bias': _jnp.float32, 'ssm_a_log': _jnp.float32, 'ssm_d': _jnp.float32, 'ssm_norm': _jnp.float32, 'ssm_w_out': _jnp.float32, 'ln_final': _jnp.float32}
MOMENT_SCALE = {'ln_ffn_pre': 2.837764e-02, 'ffn_pre_w_in': 1.216828e-02, 'ffn_pre_w_out': 1.984200e-02, 'ln_mix': 5.743437e-02, 'ln_ffn_post': 2.020366e-02, 'ffn_post_w_in': 8.740490e-03, 'ffn_post_w_out': 1.432041e-02, 'gm_w_in': 2.774444e-02, 'gm_v_norm': 1.892058e-02, 'gm_w_s': 2.621967e-02, 'gm_b_s': 3.833626e-02, 'gm_w_out': 5.119580e-02, 'ssm_w_in': 2.559776e-02, 'ssm_conv_w': 2.415186e-02, 'ssm_conv_b': 3.662205e-02, 'ssm_dt_bias': 5.351392e-02, 'ssm_a_log': 8.645291e-02, 'ssm_d': 1.701377e-01, 'ssm_norm': 2.768470e-02, 'ssm_w_out': 4.120640e-02, 'ln_final': 8.038771e+00}


def _to_microbatches(a, axis):
    t = _jnp.moveaxis(a, axis, 0)
    t = t.reshape((N_MICROBATCH, t.shape[0] // N_MICROBATCH) + t.shape[1:])
    return _jnp.moveaxis(t, 1, axis + 1)


def setup_inputs(seed: int = 0) -> dict:
    inp = _fwd_setup_inputs(seed)
    key = _jax.random.fold_in(_jax.random.key(seed), 7919)
    shape, _ = _output_shape()
    out = dict(inp)
    out["loss_target"] = _jax.random.normal(_jax.random.fold_in(key, 0), shape, _jnp.float32)
    for i, name in enumerate(TWIN_WEIGHTS):
        w = inp[name].astype(_jnp.float32)
        if MOMENT_SCALE is None:
            s = _jnp.sqrt(_jnp.mean(_jnp.square(w)) + 1e-30)
        else:
            s = MOMENT_SCALE[name]
        km, kv = _jax.random.split(_jax.random.fold_in(key, i + 1))
        out[name] = w
        out["m_" + name] = s * _jax.random.normal(km, w.shape, _jnp.float32)
        out["v_" + name] = (s * s) * _jax.random.uniform(kv, w.shape, _jnp.float32, 0.5, 1.5)
    if N_MICROBATCH > 1:
        for name, axis in PER_EXAMPLE_BATCH_AXIS.items():
            out[name] = _to_microbatches(out[name], axis)
    return {'x': out['x'], 'ln_ffn_pre': out['ln_ffn_pre'], 'ffn_pre_w_in': out['ffn_pre_w_in'], 'ffn_pre_w_out': out['ffn_pre_w_out'], 'ln_mix': out['ln_mix'], 'ln_ffn_post': out['ln_ffn_post'], 'ffn_post_w_in': out['ffn_post_w_in'], 'ffn_post_w_out': out['ffn_post_w_out'], 'gm_w_in': out['gm_w_in'], 'gm_v_norm': out['gm_v_norm'], 'gm_w_s': out['gm_w_s'], 'gm_b_s': out['gm_b_s'], 'gm_w_out': out['gm_w_out'], 'ssm_w_in': out['ssm_w_in'], 'ssm_conv_w': out['ssm_conv_w'], 'ssm_conv_b': out['ssm_conv_b'], 'ssm_dt_bias': out['ssm_dt_bias'], 'ssm_a_log': out['ssm_a_log'], 'ssm_d': out['ssm_d'], 'ssm_norm': out['ssm_norm'], 'ssm_w_out': out['ssm_w_out'], 'ln_final': out['ln_final'], 'loss_target': out['loss_target'], 'm_ln_ffn_pre': out['m_ln_ffn_pre'], 'm_ffn_pre_w_in': out['m_ffn_pre_w_in'], 'm_ffn_pre_w_out': out['m_ffn_pre_w_out'], 'm_ln_mix': out['m_ln_mix'], 'm_ln_ffn_post': out['m_ln_ffn_post'], 'm_ffn_post_w_in': out['m_ffn_post_w_in'], 'm_ffn_post_w_out': out['m_ffn_post_w_out'], 'm_gm_w_in': out['m_gm_w_in'], 'm_gm_v_norm': out['m_gm_v_norm'], 'm_gm_w_s': out['m_gm_w_s'], 'm_gm_b_s': out['m_gm_b_s'], 'm_gm_w_out': out['m_gm_w_out'], 'm_ssm_w_in': out['m_ssm_w_in'], 'm_ssm_conv_w': out['m_ssm_conv_w'], 'm_ssm_conv_b': out['m_ssm_conv_b'], 'm_ssm_dt_bias': out['m_ssm_dt_bias'], 'm_ssm_a_log': out['m_ssm_a_log'], 'm_ssm_d': out['m_ssm_d'], 'm_ssm_norm': out['m_ssm_norm'], 'm_ssm_w_out': out['m_ssm_w_out'], 'm_ln_final': out['m_ln_final'], 'v_ln_ffn_pre': out['v_ln_ffn_pre'], 'v_ffn_pre_w_in': out['v_ffn_pre_w_in'], 'v_ffn_pre_w_out': out['v_ffn_pre_w_out'], 'v_ln_mix': out['v_ln_mix'], 'v_ln_ffn_post': out['v_ln_ffn_post'], 'v_ffn_post_w_in': out['v_ffn_post_w_in'], 'v_ffn_post_w_out': out['v_ffn_post_w_out'], 'v_gm_w_in': out['v_gm_w_in'], 'v_gm_v_norm': out['v_gm_v_norm'], 'v_gm_w_s': out['v_gm_w_s'], 'v_gm_b_s': out['v_gm_b_s'], 'v_gm_w_out': out['v_gm_w_out'], 'v_ssm_w_in': out['v_ssm_w_in'], 'v_ssm_conv_w': out['v_ssm_conv_w'], 'v_ssm_conv_b': out['v_ssm_conv_b'], 'v_ssm_dt_bias': out['v_ssm_dt_bias'], 'v_ssm_a_log': out['v_ssm_a_log'], 'v_ssm_d': out['v_ssm_d'], 'v_ssm_norm': out['v_ssm_norm'], 'v_ssm_w_out': out['v_ssm_w_out'], 'v_ln_final': out['v_ln_final']}


def _loss(weights, diff, rest, loss_target):
    with _jax.named_scope("forward"):
        args = {**rest, TWIN_DIFF_INPUT: diff, **{k: w.astype(_WEIGHT_DTYPES[k]) for k, w in weights.items()}}
        y = _forward(args)
    with _jax.named_scope("loss_head"):
        err = _jnp.square(y.astype(_jnp.float32) - loss_target)
        return 0.5 * _jnp.sum(_jnp.mean(err, axis=-1)) if err.ndim else 0.5 * err


def _adamw(w, g, m, v):
    m = ADAM_B1 * m + (1.0 - ADAM_B1) * g
    v = ADAM_B2 * v + (1.0 - ADAM_B2) * _jnp.square(g)
    m_hat = m / (1.0 - ADAM_B1 ** ADAM_STEP)
    v_hat = v / (1.0 - ADAM_B2 ** ADAM_STEP)
    delta = -ADAM_LR * (m_hat / (_jnp.sqrt(v_hat) + ADAM_EPS) + ADAM_WD * w)
    return delta, m, v


def reference(x, ln_ffn_pre, ffn_pre_w_in, ffn_pre_w_out, ln_mix, ln_ffn_post, ffn_post_w_in, ffn_post_w_out, gm_w_in, gm_v_norm, gm_w_s, gm_b_s, gm_w_out, ssm_w_in, ssm_conv_w, ssm_conv_b, ssm_dt_bias, ssm_a_log, ssm_d, ssm_norm, ssm_w_out, ln_final, loss_target, m_ln_ffn_pre, m_ffn_pre_w_in, m_ffn_pre_w_out, m_ln_mix, m_ln_ffn_post, m_ffn_post_w_in, m_ffn_post_w_out, m_gm_w_in, m_gm_v_norm, m_gm_w_s, m_gm_b_s, m_gm_w_out, m_ssm_w_in, m_ssm_conv_w, m_ssm_conv_b, m_ssm_dt_bias, m_ssm_a_log, m_ssm_d, m_ssm_norm, m_ssm_w_out, m_ln_final, v_ln_ffn_pre, v_ffn_pre_w_in, v_ffn_pre_w_out, v_ln_mix, v_ln_ffn_post, v_ffn_post_w_in, v_ffn_post_w_out, v_gm_w_in, v_gm_v_norm, v_gm_w_s, v_gm_b_s, v_gm_w_out, v_ssm_w_in, v_ssm_conv_w, v_ssm_conv_b, v_ssm_dt_bias, v_ssm_a_log, v_ssm_d, v_ssm_norm, v_ssm_w_out, v_ln_final):
    given = dict(x=x, ln_ffn_pre=ln_ffn_pre, ffn_pre_w_in=ffn_pre_w_in, ffn_pre_w_out=ffn_pre_w_out, ln_mix=ln_mix, ln_ffn_post=ln_ffn_post, ffn_post_w_in=ffn_post_w_in, ffn_post_w_out=ffn_post_w_out, gm_w_in=gm_w_in, gm_v_norm=gm_v_norm, gm_w_s=gm_w_s, gm_b_s=gm_b_s, gm_w_out=gm_w_out, ssm_w_in=ssm_w_in, ssm_conv_w=ssm_conv_w, ssm_conv_b=ssm_conv_b, ssm_dt_bias=ssm_dt_bias, ssm_a_log=ssm_a_log, ssm_d=ssm_d, ssm_norm=ssm_norm, ssm_w_out=ssm_w_out, ln_final=ln_final, loss_target=loss_target, m_ln_ffn_pre=m_ln_ffn_pre, m_ffn_pre_w_in=m_ffn_pre_w_in, m_ffn_pre_w_out=m_ffn_pre_w_out, m_ln_mix=m_ln_mix, m_ln_ffn_post=m_ln_ffn_post, m_ffn_post_w_in=m_ffn_post_w_in, m_ffn_post_w_out=m_ffn_post_w_out, m_gm_w_in=m_gm_w_in, m_gm_v_norm=m_gm_v_norm, m_gm_w_s=m_gm_w_s, m_gm_b_s=m_gm_b_s, m_gm_w_out=m_gm_w_out, m_ssm_w_in=m_ssm_w_in, m_ssm_conv_w=m_ssm_conv_w, m_ssm_conv_b=m_ssm_conv_b, m_ssm_dt_bias=m_ssm_dt_bias, m_ssm_a_log=m_ssm_a_log, m_ssm_d=m_ssm_d, m_ssm_norm=m_ssm_norm, m_ssm_w_out=m_ssm_w_out, m_ln_final=m_ln_final, v_ln_ffn_pre=v_ln_ffn_pre, v_ffn_pre_w_in=v_ffn_pre_w_in, v_ffn_pre_w_out=v_ffn_pre_w_out, v_ln_mix=v_ln_mix, v_ln_ffn_post=v_ln_ffn_post, v_ffn_post_w_in=v_ffn_post_w_in, v_ffn_post_w_out=v_ffn_post_w_out, v_gm_w_in=v_gm_w_in, v_gm_v_norm=v_gm_v_norm, v_gm_w_s=v_gm_w_s, v_gm_b_s=v_gm_b_s, v_gm_w_out=v_gm_w_out, v_ssm_w_in=v_ssm_w_in, v_ssm_conv_w=v_ssm_conv_w, v_ssm_conv_b=v_ssm_conv_b, v_ssm_dt_bias=v_ssm_dt_bias, v_ssm_a_log=v_ssm_a_log, v_ssm_d=v_ssm_d, v_ssm_norm=v_ssm_norm, v_ssm_w_out=v_ssm_w_out, v_ln_final=v_ln_final)
    weights = {n: given[n] for n in TWIN_WEIGHTS}
    shared = {n: given[n] for n in SHARED_INPUTS}
    per_example = {n: given[n] for n in ['x']}
    grad_fn = _jax.value_and_grad(_loss, argnums=(0, 1))

    def one_microbatch(ex, loss_target):
        ex = dict(ex)
        diff = ex.pop(TWIN_DIFF_INPUT)
        return grad_fn(weights, diff, {**shared, **ex}, loss_target)

    if N_MICROBATCH == 1:
        loss, (grad_w, grad_x) = one_microbatch(per_example, given["loss_target"])
    else:
        def body(carry, xs):
            loss_sum, grad_sum = carry
            l_k, (gw_k, gx_k) = one_microbatch(xs[0], xs[1])
            with _jax.named_scope("update"):
                return (loss_sum + l_k, _jax.tree.map(_jnp.add, grad_sum, gw_k)), gx_k

        init = (_jnp.zeros((), _jnp.float32), _jax.tree.map(_jnp.zeros_like, weights))
        (loss, grad_w), grad_x = _jax.lax.scan(body, init, (per_example, given["loss_target"]))
    with _jax.named_scope("update"):
        delta_w, new_m, new_v = {}, {}, {}
        for n in TWIN_WEIGHTS:
            delta_w[n], new_m[n], new_v[n] = _adamw(weights[n], grad_w[n], given["m_" + n], given["v_" + n])
    return (loss, grad_x, *[grad_w[n] for n in TWIN_WEIGHTS], *[delta_w[n] for n in TWIN_WEIGHTS],
            *[new_m[n] for n in TWIN_WEIGHTS], *[new_v[n] for n in TWIN_WEIGHTS])
```

```python
import jax
import jax.numpy as jnp
from jax import lax
from jax.experimental import pallas as pl
from jax.experimental.pallas import tpu as pltpu

F32 = jnp.float32
BF16 = jnp.bfloat16
HIGHEST = lax.Precision.HIGHEST
MESH = pl.DeviceIdType.MESH

EPS = 1e-6
ADAM_LR, ADAM_B1, ADAM_B2, ADAM_EPS, ADAM_WD, ADAM_STEP = 0.001, 0.9, 0.999, 1e-08, 0.01, 10

LANES = 128
CHUNK = 128
SSM_STATE = 128
SSM_HEAD_DIM = 64
SSM_HPG = 8
SSM_CONV = 4
VMEM_LIMIT = 56 * 1024 * 1024
MM_TILE = 1408

_ANY = pl.BlockSpec(memory_space=pl.ANY)


def _cp(*sem):
    return pltpu.CompilerParams(dimension_semantics=sem if sem else None, vmem_limit_bytes=VMEM_LIMIT)


def _tile(n, target, mult=LANES):
    best = None
    t = mult
    while t <= min(n, target):
        if n % t == 0:
            best = t
        t += mult
    return n if best is None else best


def _gcd(*v):
    import math
    g = 0
    for a in v:
        g = math.gcd(g, a)
    return g


def _sds(shape, dtype):
    return jax.ShapeDtypeStruct(tuple(shape), dtype)


def _ldims(shape):
    return tuple(shape) if len(shape) == 2 else (shape[1], shape[0] * shape[2])


def _colblock(shape):
    return None if len(shape) == 2 else shape[2]


def _mspec(shape, tr, tc, rc):
    if len(shape) == 2:
        return pl.BlockSpec((tr, tc), rc)
    per = shape[2] // tc

    def im(i, j, k):
        r, c = rc(i, j, k)
        return (c // per, r, c % per)

    return pl.BlockSpec((None, tr, tc), im)


def _matmul(name, a, b, mode, out_dtype, out_blocks=None, res=None, scale=1.0):
    la, lb = _ldims(a.shape), _ldims(b.shape)
    if mode == "nn":
        (M, K), (K2, N) = la, lb
    elif mode == "nt":
        (M, K), (N, K2) = la, lb
    else:
        (K, M), (K2, N) = la, lb
    assert K == K2, (name, a.shape, b.shape, mode)
    ca, cb = _colblock(a.shape), _colblock(b.shape)
    out_shape = (M, N) if out_blocks is None else (out_blocks, M, N // out_blocks)
    co = _colblock(out_shape)
    m_c, n_c, k_c = [M], [N], [K]
    if ca is not None:
        (m_c if mode == "tn" else k_c).append(ca)
    if cb is not None:
        (k_c if mode == "nt" else n_c).append(cb)
    if co is not None:
        n_c.append(co)
    tm, tn, tk = _tile(_gcd(*m_c), MM_TILE), _tile(_gcd(*n_c), MM_TILE), _tile(_gcd(*k_c), MM_TILE)
    nk = K // tk
    if mode == "tn":
        a_spec = _mspec(a.shape, tk, tm, lambda i, j, k: (k, i))
        dims = (((0,), (0,)), ((), ()))
    else:
        a_spec = _mspec(a.shape, tm, tk, lambda i, j, k: (i, k))
        dims = (((1,), (1,)), ((), ())) if mode == "nt" else (((1,), (0,)), ((), ()))
    if mode == "nt":
        b_spec = _mspec(b.shape, tn, tk, lambda i, j, k: (j, k))
    else:
        b_spec = _mspec(b.shape, tk, tn, lambda i, j, k: (k, j))
    o_spec = _mspec(out_shape, tm, tn, lambda i, j, k: (i, j))
    in_specs, args = [a_spec, b_spec], [a, b]
    if res is not None:
        in_specs.append(pl.BlockSpec((tm, tn), lambda i, j, k: (i, j)))
        args.append(res)

    def body(*refs):
        a_ref, b_ref = refs[0], refs[1]
        res_ref = refs[2] if res is not None else None
        o_ref, acc_ref = refs[-2], refs[-1]
        kk = pl.program_id(2)

        @pl.when(kk == 0)
        def _():
            acc_ref[...] = jnp.zeros_like(acc_ref)

        acc_ref[...] += lax.dot_general(a_ref[...].astype(BF16), b_ref[...].astype(BF16), dims,
                                        preferred_element_type=F32)

        @pl.when(kk == nk - 1)
        def _():
            r = acc_ref[...]
            if scale != 1.0:
                r = r * scale
            if res_ref is not None:
                r = res_ref[...] + r
            o_ref[...] = r.astype(o_ref.dtype)

    return pl.pallas_call(
        body, name=name, grid=(M // tm, N // tn, nk), in_specs=in_specs, out_specs=o_spec,
        out_shape=_sds(out_shape, out_dtype), scratch_shapes=[pltpu.VMEM((tm, tn), F32)],
        compiler_params=_cp("parallel", "parallel", "arbitrary"))(*args)


def _rows(i):
    return (i, 0)


def _row0(i):
    return (0, 0)


def _rms_fwd(name, x, g):
    T, D = x.shape
    tr = _tile(T, 256, 8)

    def body(x_ref, g_ref, o_ref):
        xv = x_ref[...]
        r = lax.rsqrt(jnp.mean(xv * xv, axis=-1, keepdims=True) + EPS)
        o_ref[...] = (xv * r * g_ref[...]).astype(o_ref.dtype)

    return pl.pallas_call(
        body, name=name, grid=(T // tr,),
        in_specs=[pl.BlockSpec((tr, D), _rows), pl.BlockSpec((1, D), _row0)],
        out_specs=pl.BlockSpec((tr, D), _rows), out_shape=_sds((T, D), BF16),
        compiler_params=_cp("parallel"))(x, g)


def _rms_bwd(name, x, g, dh, dres):
    T, D = x.shape
    tr = _tile(T, 256, 8)

    def body(x_ref, g_ref, dh_ref, dres_ref, dx_ref, dxb_ref, dg_ref):
        xv = x_ref[...]
        r = lax.rsqrt(jnp.mean(xv * xv, axis=-1, keepdims=True) + EPS)
        xh = xv * r
        dhv = dh_ref[...]
        dxh = dhv * g_ref[...]
        dx = dres_ref[...] + r * (dxh - xh * jnp.mean(dxh * xh, axis=-1, keepdims=True))
        dx_ref[...] = dx
        dxb_ref[...] = dx.astype(BF16)

        @pl.when(pl.program_id(0) == 0)
        def _():
            dg_ref[...] = jnp.zeros_like(dg_ref)

        dg_ref[...] += jnp.sum(dhv * xh, axis=0, keepdims=True)

    return pl.pallas_call(
        body, name=name, grid=(T // tr,),
        in_specs=[pl.BlockSpec((tr, D), _rows), pl.BlockSpec((1, D), _row0),
                  pl.BlockSpec((tr, D), _rows), pl.BlockSpec((tr, D), _rows)],
        out_specs=[pl.BlockSpec((tr, D), _rows), pl.BlockSpec((tr, D), _rows), pl.BlockSpec((1, D), _row0)],
        out_shape=[_sds((T, D), F32), _sds((T, D), BF16), _sds((1, D), F32)],
        compiler_params=_cp("arbitrary"))(x, g, dh, dres)


def _sigmoid(v):
    return 1.0 / (1.0 + jnp.exp(-v))


def _swiglu_fwd(name, a):
    _, T, F = a.shape
    tr, tc = _tile(T, 512, 8), _tile(F, MM_TILE)

    def body(a_ref, o_ref):
        gate = a_ref[0].astype(F32)
        up = a_ref[1].astype(F32)
        o_ref[...] = (gate * _sigmoid(gate) * up).astype(o_ref.dtype)

    return pl.pallas_call(
        body, name=name, grid=(T // tr, F // tc),
        in_specs=[pl.BlockSpec((2, tr, tc), lambda i, j: (0, i, j))],
        out_specs=pl.BlockSpec((tr, tc), lambda i, j: (i, j)), out_shape=_sds((T, F), BF16),
        compiler_params=_cp("parallel", "parallel"))(a)


def _swiglu_bwd(name, a, ds):
    _, T, F = a.shape
    tr, tc = _tile(T, 512, 8), _tile(F, MM_TILE)

    def body(a_ref, ds_ref, o_ref):
        gate = a_ref[0].astype(F32)
        up = a_ref[1].astype(F32)
        dsv = ds_ref[...].astype(F32)
        sg = _sigmoid(gate)
        o_ref[0] = (dsv * up * sg * (1.0 + gate * (1.0 - sg))).astype(o_ref.dtype)
        o_ref[1] = (dsv * gate * sg).astype(o_ref.dtype)

    return pl.pallas_call(
        body, name=name, grid=(T // tr, F // tc),
        in_specs=[pl.BlockSpec((2, tr, tc), lambda i, j: (0, i, j)), pl.BlockSpec((tr, tc), lambda i, j: (i, j))],
        out_specs=pl.BlockSpec((2, tr, tc), lambda i, j: (0, i, j)), out_shape=_sds((2, T, F), BF16),
        compiler_params=_cp("parallel", "parallel"))(a, ds)


def _loss_head(name, x, g, tgt):
    T, D = x.shape
    tr = _tile(T, 256, 8)

    def body(x_ref, g_ref, t_ref, loss_ref, dx_ref, dxb_ref, dg_ref):
        xv = x_ref[...]
        gv = g_ref[...]
        r = lax.rsqrt(jnp.mean(xv * xv, axis=-1, keepdims=True) + EPS)
        xh = xv * r
        err = xh * gv - t_ref[...]
        dy = err * (1.0 / D)
        dxh = dy * gv
        dx = r * (dxh - xh * jnp.mean(dxh * xh, axis=-1, keepdims=True))
        dx_ref[...] = dx
        dxb_ref[...] = dx.astype(BF16)

        @pl.when(pl.program_id(0) == 0)
        def _():
            dg_ref[...] = jnp.zeros_like(dg_ref)
            loss_ref[...] = jnp.zeros_like(loss_ref)

        dg_ref[...] += jnp.sum(dy * xh, axis=0, keepdims=True)
        part = jnp.sum(jnp.sum(err * err, axis=-1, keepdims=True), axis=0, keepdims=True) * (0.5 / D)
        loss_ref[...] += part

    return pl.pallas_call(
        body, name=name, grid=(T // tr,),
        in_specs=[pl.BlockSpec((tr, D), _rows), pl.BlockSpec((1, D), _row0), pl.BlockSpec((tr, D), _rows)],
        out_specs=[pl.BlockSpec((8, LANES), _row0), pl.BlockSpec((tr, D), _rows), pl.BlockSpec((tr, D), _rows),
                   pl.BlockSpec((1, D), _row0)],
        out_shape=[_sds((8, LANES), F32), _sds((T, D), F32), _sds((T, D), BF16), _sds((1, D), F32)],
        compiler_params=_cp("arbitrary"))(x, g, tgt)


_SQRT_HALF = 0.7071067811865476
_INV_SQRT_2PI = 0.3989422804014327


def _gelu(v):
    return 0.5 * v * (1.0 + lax.erf(v * _SQRT_HALF))


def _gelu_grad(v):
    return 0.5 * (1.0 + lax.erf(v * _SQRT_HALF)) + v * _INV_SQRT_2PI * jnp.exp(-0.5 * v * v)


def _group_expand(rows, width, gd):
    gi = lax.broadcasted_iota(jnp.int32, (rows, width), 0)
    fi = lax.broadcasted_iota(jnp.int32, (rows, width), 1)
    return ((fi >= gi * gd) & (fi < (gi + 1) * gd)).astype(F32)


def _gm_act_fwd(name, zpre, vnorm):
    _, T, GI = zpre.shape
    tr = _tile(T, 128, 8)

    def body(z_ref, g_ref, u_ref, v_ref):
        u_ref[...] = _gelu(z_ref[0].astype(F32)).astype(BF16)
        zv = _gelu(z_ref[1].astype(F32))
        r = lax.rsqrt(jnp.mean(zv * zv, axis=-1, keepdims=True) + EPS)
        v_ref[...] = (zv * r * g_ref[...]).astype(BF16)

    return pl.pallas_call(
        body, name=name, grid=(T // tr,),
        in_specs=[pl.BlockSpec((2, tr, GI), lambda i: (0, i, 0)), pl.BlockSpec((1, GI), _row0)],
        out_specs=[pl.BlockSpec((tr, GI), _rows), pl.BlockSpec((tr, GI), _rows)],
        out_shape=[_sds((T, GI), BF16), _sds((T, GI), BF16)], compiler_params=_cp("parallel"))(zpre, vnorm)


def _gm_act_bwd(name, zpre, du, dvn, vnorm):
    _, T, GI = zpre.shape
    tr = _tile(T, 128, 8)

    def body(z_ref, du_ref, dvn_ref, g_ref, dz_ref, dg_ref):
        xu = z_ref[0].astype(F32)
        xv = z_ref[1].astype(F32)
        zv = _gelu(xv)
        r = lax.rsqrt(jnp.mean(zv * zv, axis=-1, keepdims=True) + EPS)
        xh = zv * r
        dv = dvn_ref[...]

        @pl.when(pl.program_id(0) == 0)
        def _():
            dg_ref[...] = jnp.zeros_like(dg_ref)

        dg_ref[...] += jnp.sum(dv * xh, axis=0, keepdims=True)
        dxh = dv * g_ref[...]
        dzv = r * (dxh - xh * jnp.mean(dxh * xh, axis=-1, keepdims=True))
        dz_ref[0] = (du_ref[...] * _gelu_grad(xu)).astype(BF16)
        dz_ref[1] = (dzv * _gelu_grad(xv)).astype(BF16)

    return pl.pallas_call(
        body, name=name, grid=(T // tr,),
        in_specs=[pl.BlockSpec((2, tr, GI), lambda i: (0, i, 0)), pl.BlockSpec((tr, GI), _rows),
                  pl.BlockSpec((tr, GI), _rows), pl.BlockSpec((1, GI), _row0)],
        out_specs=[pl.BlockSpec((2, tr, GI), lambda i: (0, i, 0)), pl.BlockSpec((1, GI), _row0)],
        out_shape=[_sds((2, T, GI), BF16), _sds((1, GI), F32)], compiler_params=_cp("arbitrary"))(zpre, du, dvn, vnorm)


def _causal_mask():
    r = lax.broadcasted_iota(jnp.int32, (CHUNK, CHUNK), 0)
    c = lax.broadcasted_iota(jnp.int32, (CHUNK, CHUNK), 1)
    return r >= c


def _gm_spatial_fwd(name, u, vn, ws, bt):
    T, GI = u.shape
    G = ws.shape[0]
    gd = GI // G

    def body(u_ref, v_ref, ws_ref, bt_ref, o_ref, bias_scr):
        @pl.when(pl.program_id(0) == 0)
        def _():
            bias_scr[...] = jnp.dot(bt_ref[...], _group_expand(LANES, GI, gd), precision=HIGHEST,
                                    preferred_element_type=F32)

        causal = _causal_mask()
        for g in range(G):
            sl = slice(g * gd, (g + 1) * gd)
            wc = jnp.where(causal, ws_ref[g], 0.0).astype(BF16)
            mixed = jnp.dot(wc, v_ref[:, sl], preferred_element_type=F32) + bias_scr[:, sl]
            o_ref[:, sl] = (u_ref[:, sl].astype(F32) * mixed).astype(o_ref.dtype)

    return pl.pallas_call(
        body, name=name, grid=(T // CHUNK,),
        in_specs=[pl.BlockSpec((CHUNK, GI), _rows), pl.BlockSpec((CHUNK, GI), _rows),
                  pl.BlockSpec((G, CHUNK, CHUNK), lambda i: (0, 0, 0)), pl.BlockSpec((CHUNK, LANES), _row0)],
        out_specs=pl.BlockSpec((CHUNK, GI), _rows), out_shape=_sds((T, GI), BF16),
        scratch_shapes=[pltpu.VMEM((CHUNK, GI), F32)], compiler_params=_cp("arbitrary"))(u, vn, ws, bt)


def _gm_spatial_bwd(name, dgated, u, vn, ws, bt):
    T, GI = u.shape
    G = ws.shape[0]
    gd = GI // G
    nc = T // CHUNK

    def body(dg_ref, u_ref, v_ref, ws_ref, bt_ref, du_ref, dv_ref, dws_ref, dbt_ref, bias_scr, dm_scr):
        step = pl.program_id(0)

        @pl.when(step == 0)
        def _():
            bias_scr[...] = jnp.dot(bt_ref[...], _group_expand(LANES, GI, gd), precision=HIGHEST,
                                    preferred_element_type=F32)
            dm_scr[...] = jnp.zeros_like(dm_scr)
            dws_ref[...] = jnp.zeros_like(dws_ref)

        causal = _causal_mask()
        for g in range(G):
            sl = slice(g * gd, (g + 1) * gd)
            wc = jnp.where(causal, ws_ref[g], 0.0).astype(BF16)
            vv = v_ref[:, sl]
            dgv = dg_ref[:, sl].astype(F32)
            mixed = jnp.dot(wc, vv, preferred_element_type=F32) + bias_scr[:, sl]
            du_ref[:, sl] = dgv * mixed
            dm = dgv * u_ref[:, sl].astype(F32)
            dmb = dm.astype(BF16)
            dv_ref[:, sl] = lax.dot_general(wc, dmb, (((0,), (0,)), ((), ())), preferred_element_type=F32)
            dw = lax.dot_general(dmb, vv, (((1,), (1,)), ((), ())), preferred_element_type=F32)
            dws_ref[g] += jnp.where(causal, dw, 0.0)
            dm_scr[:, sl] += dm

        @pl.when(step == nc - 1)
        def _():
            dbt_ref[...] = lax.dot_general(dm_scr[...], _group_expand(LANES, GI, gd), (((1,), (1,)), ((), ())),
                                           precision=HIGHEST, preferred_element_type=F32)

    return pl.pallas_call(
        body, name=name, grid=(nc,),
        in_specs=[pl.BlockSpec((CHUNK, GI), _rows), pl.BlockSpec((CHUNK, GI), _rows), pl.BlockSpec((CHUNK, GI), _rows),
                  pl.BlockSpec((G, CHUNK, CHUNK), lambda i: (0, 0, 0)), pl.BlockSpec((CHUNK, LANES), _row0)],
        out_specs=[pl.BlockSpec((CHUNK, GI), _rows), pl.BlockSpec((CHUNK, GI), _rows),
                   pl.BlockSpec((G, CHUNK, CHUNK), lambda i: (0, 0, 0)), pl.BlockSpec((CHUNK, LANES), _row0)],
        out_shape=[_sds((T, GI), F32), _sds((T, GI), F32), _sds((G, CHUNK, CHUNK), F32), _sds((CHUNK, LANES), F32)],
        scratch_shapes=[pltpu.VMEM((CHUNK, GI), F32), pltpu.VMEM((CHUNK, GI), F32)],
        compiler_params=_cp("arbitrary"))(dgated, u, vn, ws, bt)


def _conv_taps(xv, w_ref, b_ref):
    rows = lax.broadcasted_iota(jnp.int32, xv.shape, 0)
    acc = xv * w_ref[pl.ds(SSM_CONV - 1, 1), :] + b_ref[...]
    for k in range(1, SSM_CONV):
        sh = jnp.where(rows >= k, pltpu.roll(xv, k, 0), 0.0)
        acc = acc + sh * w_ref[pl.ds(SSM_CONV - 1 - k, 1), :]
    return acc


def _ssd_conv_fwd(name, proj, wt, b, inner, cd):
    T = proj.shape[0]
    tc = _tile(_gcd(inner, cd), 512)
    off = inner // tc

    def body(x_ref, w_ref, b_ref, o_ref):
        pre = _conv_taps(x_ref[...], w_ref, b_ref)
        o_ref[...] = pre * _sigmoid(pre)

    return pl.pallas_call(
        body, name=name, grid=(cd // tc,),
        in_specs=[pl.BlockSpec((T, tc), lambda j: (0, off + j)), pl.BlockSpec((SSM_CONV, tc), lambda j: (0, j)),
                  pl.BlockSpec((1, tc), lambda j: (0, j))],
        out_specs=pl.BlockSpec((T, tc), lambda j: (0, j)), out_shape=_sds((T, cd), F32),
        compiler_params=_cp("parallel"))(proj, wt, b)


def _ssd_conv_bwd(name, dact, proj, wt, b, inner, cd):
    T = proj.shape[0]
    tc = _tile(_gcd(inner, cd), 512)
    off = inner // tc

    def body(da_ref, x_ref, w_ref, b_ref, dx_ref, dw_ref, db_ref):
        xv = x_ref[...]
        pre = _conv_taps(xv, w_ref, b_ref)
        sg = _sigmoid(pre)
        dpre = da_ref[...] * sg * (1.0 + pre * (1.0 - sg))
        rows = lax.broadcasted_iota(jnp.int32, xv.shape, 0)
        db_ref[...] = jnp.sum(dpre, axis=0, keepdims=True)
        dx = dpre * w_ref[pl.ds(SSM_CONV - 1, 1), :]
        dw_ref[pl.ds(SSM_CONV - 1, 1), :] = jnp.sum(dpre * xv, axis=0, keepdims=True)
        for k in range(1, SSM_CONV):
            sh = jnp.where(rows >= k, pltpu.roll(xv, k, 0), 0.0)
            dw_ref[pl.ds(SSM_CONV - 1 - k, 1), :] = jnp.sum(dpre * sh, axis=0, keepdims=True)
            fw = jnp.where(rows < T - k, pltpu.roll(dpre, T - k, 0), 0.0)
            dx = dx + fw * w_ref[pl.ds(SSM_CONV - 1 - k, 1), :]
        dx_ref[...] = dx.astype(BF16)

    return pl.pallas_call(
        body, name=name, grid=(cd // tc,),
        in_specs=[pl.BlockSpec((T, tc), lambda j: (0, j)), pl.BlockSpec((T, tc), lambda j: (0, off + j)),
                  pl.BlockSpec((SSM_CONV, tc), lambda j: (0, j)), pl.BlockSpec((1, tc), lambda j: (0, j))],
        out_specs=[pl.BlockSpec((T, tc), lambda j: (0, j)), pl.BlockSpec((SSM_CONV, tc), lambda j: (0, j)),
                   pl.BlockSpec((1, tc), lambda j: (0, j))],
        out_shape=[_sds((T, cd), BF16), _sds((SSM_CONV, cd), F32), _sds((1, cd), F32)],
        compiler_params=_cp("parallel"))(dact, proj, wt, b)


def _softplus(v):
    return jnp.maximum(v, 0.0) + jnp.log(1.0 + jnp.exp(-jnp.abs(v)))


def _tri(lower):
    r = lax.broadcasted_iota(jnp.int32, (CHUNK, CHUNK), 0)
    c = lax.broadcasted_iota(jnp.int32, (CHUNK, CHUNK), 1)
    return ((c <= r) if lower else (c >= r)).astype(F32)


def _ssd_dt_fwd(name, proj, bias, alog, dtcol, heads):
    T = proj.shape[0]

    def body(dt_ref, b_ref, al_ref, dtp_ref, ac_ref):
        live = lax.broadcasted_iota(jnp.int32, (CHUNK, LANES), 1) < heads
        dtp = jnp.where(live, _softplus(dt_ref[...] + b_ref[...]), 0.0)
        da = dtp * (-jnp.exp(al_ref[...]))
        dtp_ref[...] = dtp
        ac_ref[...] = jnp.dot(_tri(True), da, precision=HIGHEST, preferred_element_type=F32)

    return pl.pallas_call(
        body, name=name, grid=(T // CHUNK,),
        in_specs=[pl.BlockSpec((CHUNK, LANES), lambda i: (i, dtcol)), pl.BlockSpec((1, LANES), _row0),
                  pl.BlockSpec((1, LANES), _row0)],
        out_specs=[pl.BlockSpec((CHUNK, LANES), _rows), pl.BlockSpec((CHUNK, LANES), _rows)],
        out_shape=[_sds((T, LANES), F32), _sds((T, LANES), F32)], compiler_params=_cp("parallel"))(proj, bias, alog)


def _ssd_dt_bwd(name, ddtp_g, dacum_g, dd_g, proj, bias, alog, dtp, dtcol, heads):
    T = proj.shape[0]
    G = ddtp_g.shape[0]

    def body(ddtp_ref, dac_ref, dd_ref, dt_ref, b_ref, al_ref, dtp_ref, ddt_ref, db_ref, dal_ref, dds_ref, da_scr):
        step = pl.program_id(0)

        @pl.when(step == 0)
        def _():
            db_ref[...] = jnp.zeros_like(db_ref)
            da_scr[...] = jnp.zeros_like(da_scr)
            dds_ref[...] = jnp.sum(dd_ref[...], axis=0)

        live = lax.broadcasted_iota(jnp.int32, (CHUNK, LANES), 1) < heads
        a = -jnp.exp(al_ref[...])
        dac = jnp.sum(dac_ref[...], axis=0)
        dda = jnp.dot(_tri(False), dac, precision=HIGHEST, preferred_element_type=F32)
        dtp_v = dtp_ref[...]
        ddtp = jnp.sum(ddtp_ref[...], axis=0) + dda * a
        da_scr[...] += jnp.sum(dda * dtp_v, axis=0, keepdims=True)
        ddt = jnp.where(live, ddtp * _sigmoid(dt_ref[...] + b_ref[...]), 0.0)
        ddt_ref[...] = ddt.astype(BF16)
        db_ref[...] += jnp.sum(ddt, axis=0, keepdims=True)
        dal_ref[...] = da_scr[...] * a

    return pl.pallas_call(
        body, name=name, grid=(T // CHUNK,),
        in_specs=[pl.BlockSpec((G, CHUNK, LANES), lambda i: (0, i, 0)), pl.BlockSpec((G, CHUNK, LANES), lambda i: (0, i, 0)),
                  pl.BlockSpec((G, 8, LANES), lambda i: (0, 0, 0)),
                  pl.BlockSpec((CHUNK, LANES), lambda i: (i, dtcol)), pl.BlockSpec((1, LANES), _row0),
                  pl.BlockSpec((1, LANES), _row0), pl.BlockSpec((CHUNK, LANES), _rows)],
        out_specs=[pl.BlockSpec((CHUNK, LANES), _rows), pl.BlockSpec((8, LANES), _row0), pl.BlockSpec((8, LANES), _row0),
                   pl.BlockSpec((8, LANES), _row0)],
        out_shape=[_sds((T, LANES), BF16), _sds((8, LANES), F32), _sds((8, LANES), F32), _sds((8, LANES), F32)],
        scratch_shapes=[pltpu.VMEM((8, LANES), F32)],
        compiler_params=_cp("arbitrary"))(ddtp_g, dacum_g, dd_g, proj, bias, alog, dtp)


def _head_expand(g, gw):
    hi = lax.broadcasted_iota(jnp.int32, (LANES, gw), 0) - g * SSM_HPG
    fi = lax.broadcasted_iota(jnp.int32, (LANES, gw), 1)
    return ((fi >= hi * SSM_HEAD_DIM) & (fi < (hi + 1) * SSM_HEAD_DIM)).astype(F32)


def _dot(a, b, dims, exact=False):
    if exact:
        return lax.dot_general(a, b, (dims, ((), ())), precision=HIGHEST, preferred_element_type=F32)
    return lax.dot_general(a.astype(BF16), b.astype(BF16), (dims, ((), ())), preferred_element_type=F32)


_NN = ((1,), (0,))
_NT = ((1,), (1,))
_TN = ((0,), (0,))


def _pair_decay(g, q, acum, acum_t_ref, causal):
    lane = lax.broadcasted_iota(jnp.int32, (CHUNK, LANES), 1)
    out = []
    for e in range(2):
        h = g * SSM_HPG + 2 * q + e
        acol = jnp.sum(jnp.where(lane == h, acum, 0.0), axis=1, keepdims=True)
        arow = acum_t_ref[pl.ds(h, 1), :]
        out.append(jnp.exp(jnp.where(causal, acol - arow, -1e30)))
    return out


def _ssd_core_fwd(name, act, dtp, acum, dexp, inner, groups):
    T = act.shape[0]
    nc = T // CHUNK
    gw = SSM_HPG * SSM_HEAD_DIM
    npair = gw // LANES
    bcol, ccol = inner // SSM_STATE, inner // SSM_STATE + groups

    def body(x_ref, b_ref, c_ref, dtp_ref, ac_ref, d_ref, y_ref, sp_ref, st_scr, act_scr, ae_scr):
        g = pl.program_id(0)

        @pl.when(pl.program_id(1) == 0)
        def _():
            st_scr[...] = jnp.zeros_like(st_scr)

        st = st_scr[...]
        sp_ref[...] = st
        e = _head_expand(g, gw)
        acum = ac_ref[...]
        ae = _dot(acum, e, _NN, exact=True)
        dte = _dot(dtp_ref[...], e, _NN, exact=True)
        ae_scr[...] = ae
        act_scr[...] = acum.T
        xv = x_ref[...]
        xdt = xv * dte
        bm, cm = b_ref[...], c_ref[...]
        cb = _dot(cm, bm, _NT)
        causal = _causal_mask()
        lane = lax.broadcasted_iota(jnp.int32, (CHUNK, LANES), 1)
        yoff = _dot(cm, st, _NN) * jnp.exp(ae)
        skip = xv * d_ref[...]
        for q in range(npair):
            sl = slice(q * LANES, (q + 1) * LANES)
            dec = _pair_decay(g, q, acum, act_scr, causal)
            x2 = xdt[:, sl]
            xa = jnp.where(lane < SSM_HEAD_DIM, x2, 0.0)
            yd = _dot(dec[0] * cb, xa, _NN) + _dot(dec[1] * cb, x2 - xa, _NN)
            y_ref[:, sl] = yd + yoff[:, sl] + skip[:, sl]
        alast = ae_scr[pl.ds(CHUNK - 1, 1), :]
        z = xdt * jnp.exp(alast - ae)
        st_scr[...] = st * jnp.exp(alast) + _dot(bm, z, _TN)

    return pl.pallas_call(
        body, name=name, grid=(groups, nc),
        in_specs=[pl.BlockSpec((CHUNK, gw), lambda g, c: (c, g)),
                  pl.BlockSpec((CHUNK, SSM_STATE), lambda g, c: (c, bcol + g)),
                  pl.BlockSpec((CHUNK, SSM_STATE), lambda g, c: (c, ccol + g)),
                  pl.BlockSpec((CHUNK, LANES), lambda g, c: (c, 0)), pl.BlockSpec((CHUNK, LANES), lambda g, c: (c, 0)),
                  pl.BlockSpec((1, gw), lambda g, c: (0, g))],
        out_specs=[pl.BlockSpec((CHUNK, gw), lambda g, c: (c, g)),
                   pl.BlockSpec((None, SSM_STATE, gw), lambda g, c: (c, 0, g))],
        out_shape=[_sds((T, inner), F32), _sds((nc, SSM_STATE, inner), F32)],
        scratch_shapes=[pltpu.VMEM((SSM_STATE, gw), F32), pltpu.VMEM((CHUNK, LANES), F32), pltpu.VMEM((CHUNK, gw), F32)],
        compiler_params=_cp("arbitrary", "arbitrary"))(act, act, act, dtp, acum, dexp)


def _ssd_core_bwd(name, dy, act, dtp, acum, dexp, sprev, inner, groups):
    T = act.shape[0]
    nc = T // CHUNK
    gw = SSM_HPG * SSM_HEAD_DIM
    npair = gw // LANES
    bcol, ccol = inner // SSM_STATE, inner // SSM_STATE + groups

    def rc(g, c):
        return nc - 1 - c

    def body(dy_ref, x_ref, b_ref, c_ref, dtp_ref, ac_ref, d_ref, sp_ref,
             dx_ref, db_ref, dc_ref, ddtp_ref, dac_ref, dd_ref,
             dst_scr, act_scr, ae_scr, dxdt_scr, dd_scr, dact_scr):
        g = pl.program_id(0)
        step = pl.program_id(1)

        @pl.when(step == 0)
        def _():
            dst_scr[...] = jnp.zeros_like(dst_scr)
            dd_scr[...] = jnp.zeros_like(dd_scr)

        dst = dst_scr[...]
        sp = sp_ref[...]
        e = _head_expand(g, gw)
        acum = ac_ref[...]
        ae = _dot(acum, e, _NN, exact=True)
        dte = _dot(dtp_ref[...], e, _NN, exact=True)
        ae_scr[...] = ae
        act_scr[...] = acum.T
        alast = ae_scr[pl.ds(CHUNK - 1, 1), :]
        xv = x_ref[...]
        xdt = xv * dte
        bm, cm = b_ref[...], c_ref[...]
        dyv = dy_ref[...]
        cb = _dot(cm, bm, _NT)
        causal = _causal_mask()
        lane = lax.broadcasted_iota(jnp.int32, (CHUNK, LANES), 1)
        ea = jnp.exp(ae)
        cde = jnp.exp(alast)
        w = jnp.exp(alast - ae)
        z = xdt * w

        dd_scr[...] += jnp.sum(dyv * xv, axis=0, keepdims=True)
        qm = _dot(cm, sp, _NN)
        dq = dyv * ea
        dae = dq * qm
        dc = _dot(dq, sp, _NT)
        dsp = _dot(cm, dq, _TN) + dst * cde
        dal = jnp.sum(dst * sp, axis=0, keepdims=True) * cde
        db = _dot(z, dst, _NT)
        dz = _dot(bm, dst, _NN)
        gw_ = dz * z
        dae = dae - gw_
        dal = dal + jnp.sum(gw_, axis=0, keepdims=True)
        dxdt_scr[...] = dz * w
        dcb = jnp.zeros((CHUNK, CHUNK), F32)
        dacol = jnp.zeros((CHUNK, LANES), F32)
        dact_scr[...] = jnp.zeros_like(dact_scr)
        sub = lax.broadcasted_iota(jnp.int32, (CHUNK, LANES), 0)
        for q in range(npair):
            sl = slice(q * LANES, (q + 1) * LANES)
            dec = _pair_decay(g, q, acum, act_scr, causal)
            x2, dy2 = xdt[:, sl], dyv[:, sl]
            xs_ = (jnp.where(lane < SSM_HEAD_DIM, x2, 0.0),)
            xs_ = xs_ + (x2 - xs_[0],)
            dys = (jnp.where(lane < SSM_HEAD_DIM, dy2, 0.0),)
            dys = dys + (dy2 - dys[0],)
            dx2 = jnp.zeros((CHUNK, LANES), F32)
            for hh in range(2):
                h = g * SSM_HPG + 2 * q + hh
                m = dec[hh] * cb
                dm = _dot(dys[hh], xs_[hh], _NT)
                dx2 = dx2 + _dot(m, dys[hh], _TN)
                dcb = dcb + dm * dec[hh]
                r = dm * m
                dacol = dacol + jnp.where(lane == h, jnp.sum(r, axis=1, keepdims=True), 0.0)
                dact_scr[...] -= jnp.where(sub == h, jnp.sum(r, axis=0, keepdims=True), 0.0)
            dxdt_scr[:, sl] += dx2
        dc = dc + _dot(dcb, bm, _NN)
        db = db + _dot(dcb, cm, _TN)
        dxdt = dxdt_scr[...]
        dx_ref[...] = dyv * d_ref[...] + dxdt * dte
        db_ref[...] = db
        dc_ref[...] = dc
        ddtp_ref[...] = _dot(dxdt * xv, e, _NT, exact=True)
        dal_h = _dot(jnp.broadcast_to(dal, (8, gw)), e, _NT, exact=True)
        dal_row = jnp.max(dal_h, axis=0, keepdims=True)
        dac = _dot(dae, e, _NT, exact=True) + dacol + dact_scr[...].T
        dac_ref[...] = dac + jnp.where(sub == CHUNK - 1, dal_row, 0.0)
        dst_scr[...] = dsp

        @pl.when(step == nc - 1)
        def _():
            dd_ref[...] = _dot(jnp.broadcast_to(dd_scr[...], (8, gw)), e, _NT, exact=True)

    return pl.pallas_call(
        body, name=name, grid=(groups, nc),
        in_specs=[pl.BlockSpec((CHUNK, gw), lambda g, c: (rc(g, c), g)),
                  pl.BlockSpec((CHUNK, gw), lambda g, c: (rc(g, c), g)),
                  pl.BlockSpec((CHUNK, SSM_STATE), lambda g, c: (rc(g, c), bcol + g)),
                  pl.BlockSpec((CHUNK, SSM_STATE), lambda g, c: (rc(g, c), ccol + g)),
                  pl.BlockSpec((CHUNK, LANES), lambda g, c: (rc(g, c), 0)),
                  pl.BlockSpec((CHUNK, LANES), lambda g, c: (rc(g, c), 0)),
                  pl.BlockSpec((1, gw), lambda g, c: (0, g)),
                  pl.BlockSpec((None, SSM_STATE, gw), lambda g, c: (rc(g, c), 0, g))],
        out_specs=[pl.BlockSpec((CHUNK, gw), lambda g, c: (rc(g, c), g)),
                   pl.BlockSpec((CHUNK, SSM_STATE), lambda g, c: (rc(g, c), g)),
                   pl.BlockSpec((CHUNK, SSM_STATE), lambda g, c: (rc(g, c), g)),
                   pl.BlockSpec((None, CHUNK, LANES), lambda g, c: (g, rc(g, c), 0)),
                   pl.BlockSpec((None, CHUNK, LANES), lambda g, c: (g, rc(g, c), 0)),
                   pl.BlockSpec((None, 8, LANES), lambda g, c: (g, 0, 0))],
        out_shape=[_sds((T, inner), F32), _sds((T, groups * SSM_STATE), F32), _sds((T, groups * SSM_STATE), F32),
                   _sds((groups, T, LANES), F32), _sds((groups, T, LANES), F32), _sds((groups, 8, LANES), F32)],
        scratch_shapes=[pltpu.VMEM((SSM_STATE, gw), F32), pltpu.VMEM((CHUNK, LANES), F32), pltpu.VMEM((CHUNK, gw), F32),
                        pltpu.VMEM((CHUNK, gw), F32), pltpu.VMEM((1, gw), F32), pltpu.VMEM((CHUNK, LANES), F32)],
        compiler_params=_cp("arbitrary", "arbitrary"))(dy, act, act, act, dtp, acum, dexp, sprev)


def _ssd_post_fwd(name, y, proj, ng, inner, groups):
    T = y.shape[0]
    tr = _tile(T, 128, 8)
    gs = inner // groups

    def body(y_ref, z_ref, g_ref, o_ref):
        zv = z_ref[...]
        gy = y_ref[...] * (zv * _sigmoid(zv))
        for k in range(groups):
            sl = slice(k * gs, (k + 1) * gs)
            seg = gy[:, sl]
            r = lax.rsqrt(jnp.mean(seg * seg, axis=-1, keepdims=True) + EPS)
            o_ref[:, sl] = (seg * r * g_ref[:, sl]).astype(BF16)

    return pl.pallas_call(
        body, name=name, grid=(T // tr,),
        in_specs=[pl.BlockSpec((tr, inner), _rows), pl.BlockSpec((tr, inner), _rows), pl.BlockSpec((1, inner), _row0)],
        out_specs=pl.BlockSpec((tr, inner), _rows), out_shape=_sds((T, inner), BF16),
        compiler_params=_cp("parallel"))(y, proj, ng)


def _ssd_post_bwd(name, dyn, y, proj, ng, inner, groups):
    T = y.shape[0]
    tr = _tile(T, 128, 8)
    gs = inner // groups

    def body(dyn_ref, y_ref, z_ref, g_ref, dy_ref, dz_ref, dg_ref):
        @pl.when(pl.program_id(0) == 0)
        def _():
            dg_ref[...] = jnp.zeros_like(dg_ref)

        zv = z_ref[...]
        sg = _sigmoid(zv)
        sz = zv * sg
        yv = y_ref[...]
        gy = yv * sz
        dv = dyn_ref[...]
        for k in range(groups):
            sl = slice(k * gs, (k + 1) * gs)
            seg = gy[:, sl]
            r = lax.rsqrt(jnp.mean(seg * seg, axis=-1, keepdims=True) + EPS)
            xh = seg * r
            d = dv[:, sl]
            dg_ref[:, sl] += jnp.sum(d * xh, axis=0, keepdims=True)
            dxh = d * g_ref[:, sl]
            dgy = r * (dxh - xh * jnp.mean(dxh * xh, axis=-1, keepdims=True))
            dy_ref[:, sl] = dgy * sz[:, sl]
            dz_ref[:, sl] = (dgy * yv[:, sl] * (sg[:, sl] * (1.0 + zv[:, sl] * (1.0 - sg[:, sl])))).astype(BF16)

    return pl.pallas_call(
        body, name=name, grid=(T // tr,),
        in_specs=[pl.BlockSpec((tr, inner), _rows), pl.BlockSpec((tr, inner), _rows), pl.BlockSpec((tr, inner), _rows),
                  pl.BlockSpec((1, inner), _row0)],
        out_specs=[pl.BlockSpec((tr, inner), _rows), pl.BlockSpec((tr, inner), _rows), pl.BlockSpec((1, inner), _row0)],
        out_shape=[_sds((T, inner), F32), _sds((T, inner), BF16), _sds((1, inner), F32)],
        compiler_params=_cp("arbitrary"))(dyn, y, proj, ng)


def _place():
    return lax.axis_index("x"), lax.axis_index("y"), lax.axis_index("c")


def _other_chips(x, y):
    return [(1 - x, y), (x, 1 - y), (1 - x, 1 - y)]


def _remote(src, dst, ssem, rsem, dev):
    return pltpu.make_async_remote_copy(src_ref=src, dst_ref=dst, send_sem=ssem, recv_sem=rsem, device_id=dev,
                                        device_id_type=MESH)


def _all_gather_chips(name, shard):
    R, C = shard.shape
    hr = R // 2

    def body(x_ref, o_ref, ssem, rsem, lsem):
        x, y, c = _place()
        k = 2 * x + y
        sib = (x, y, 1 - c)
        chips = _other_chips(x, y)

        def half(blk, cc):
            return o_ref.at[blk, pl.ds(cc * hr, hr), :]

        local = pltpu.make_async_copy(x_ref, o_ref.at[k], lsem)
        local.start()
        first = [_remote(x_ref.at[pl.ds(c * hr, hr), :], half(k, c), ssem.at[r], rsem.at[r], (px, py, c))
                 for r, (px, py) in enumerate(chips)]
        for cp in first:
            cp.start()
        passed = []
        for r, (px, py) in enumerate(chips):
            kj = 2 * px + py
            _remote(half(kj, c), half(kj, c), ssem.at[r], rsem.at[r], (px, py, c)).wait_recv()
            fw = _remote(half(kj, c), half(kj, c), ssem.at[3 + r], rsem.at[3 + r], sib)
            fw.start()
            passed.append(fw)
        for r, (px, py) in enumerate(chips):
            kj = 2 * px + py
            _remote(half(kj, 1 - c), half(kj, 1 - c), ssem.at[3 + r], rsem.at[3 + r], sib).wait_recv()
        for cp in first + passed:
            cp.wait_send()
        local.wait()

    return pl.pallas_call(
        body, name=name, in_specs=[_ANY], out_specs=_ANY, out_shape=_sds((4, R, C), shard.dtype),
        scratch_shapes=[pltpu.SemaphoreType.DMA((6,)), pltpu.SemaphoreType.DMA((6,)), pltpu.SemaphoreType.DMA],
    )(shard)


def _scatter_chips(name, full):
    _, R, C = full.shape

    def body(g_ref, o_ref, ssem, rsem, lsem):
        x, y, c = _place()
        k = 2 * x + y
        chips = _other_chips(x, y)
        local = pltpu.make_async_copy(g_ref.at[k], o_ref.at[k], lsem)
        local.start()
        sends = [_remote(g_ref.at[2 * px + py], o_ref.at[k], ssem.at[r], rsem.at[r], (px, py, c))
                 for r, (px, py) in enumerate(chips)]
        for cp in sends:
            cp.start()
        for r, (px, py) in enumerate(chips):
            kj = 2 * px + py
            _remote(g_ref.at[kj], o_ref.at[kj], ssem.at[r], rsem.at[r], (px, py, c)).wait_recv()
        for cp in sends:
            cp.wait_send()
        local.wait()

    return pl.pallas_call(
        body, name=name, in_specs=[_ANY], out_specs=_ANY, out_shape=_sds(full.shape, full.dtype),
        scratch_shapes=[pltpu.SemaphoreType.DMA((3,)), pltpu.SemaphoreType.DMA((3,)), pltpu.SemaphoreType.DMA],
    )(full)


def _swap_sibling(name, p):
    def body(p_ref, q_ref, ssem, rsem):
        x, y, c = _place()
        cp = _remote(p_ref, q_ref, ssem, rsem, (x, y, 1 - c))
        cp.start()
        cp.wait()

    return pl.pallas_call(
        body, name=name, in_specs=[_ANY], out_specs=_ANY, out_shape=_sds(p.shape, p.dtype),
        scratch_shapes=[pltpu.SemaphoreType.DMA, pltpu.SemaphoreType.DMA])(p)


def _all_gather_devices(name, buf):
    def body(b_ref, o_ref, ssem, rsem, lsem):
        x, y, c = _place()
        me = 4 * x + 2 * y + c
        flips = [(fx, fy, fc) for fx in (0, 1) for fy in (0, 1) for fc in (0, 1) if fx or fy or fc]
        peers = [((1 - x) if fx else x, (1 - y) if fy else y, (1 - c) if fc else c) for fx, fy, fc in flips]
        local = pltpu.make_async_copy(b_ref, o_ref.at[me], lsem)
        local.start()
        sends = [_remote(b_ref, o_ref.at[me], ssem.at[r], rsem.at[r], p) for r, p in enumerate(peers)]
        for cp in sends:
            cp.start()
        for r, (px, py, pc) in enumerate(peers):
            pid = 4 * px + 2 * py + pc
            _remote(b_ref, o_ref.at[pid], ssem.at[r], rsem.at[r], (px, py, pc)).wait_recv()
        for cp in sends:
            cp.wait_send()
        local.wait()

    return pl.pallas_call(
        body, name=name, in_specs=[_ANY], out_specs=_ANY, out_shape=_sds((8,) + buf.shape, buf.dtype),
        scratch_shapes=[pltpu.SemaphoreType.DMA((7,)), pltpu.SemaphoreType.DMA((7,)), pltpu.SemaphoreType.DMA],
    )(buf)


def _sum_blocks(name, parts):
    n, R, C = parts.shape
    tr = _tile(R, 256, 8)

    def body(p_ref, o_ref):
        acc = p_ref[0].astype(F32)
        for j in range(1, n):
            acc = acc + p_ref[j].astype(F32)
        o_ref[...] = acc

    return pl.pallas_call(
        body, name=name, grid=(R // tr,), in_specs=[pl.BlockSpec((n, tr, C), lambda i: (0, i, 0))],
        out_specs=pl.BlockSpec((tr, C), _rows), out_shape=_sds((R, C), F32), compiler_params=_cp("parallel"))(parts)


def _adamw(name, w, m, v, parts):
    L, R, C = w.shape
    np_ = len(parts[0])
    tr = _tile(R, max(8, (VMEM_LIMIT // 2) // (2 * 4 * C * (7 + L * np_))), 8)
    flat = [p for lp in parts for p in lp]
    c1 = 1.0 / (1.0 - ADAM_B1 ** ADAM_STEP)
    c2 = 1.0 / (1.0 - ADAM_B2 ** ADAM_STEP)

    def body(*refs):
        w_ref, m_ref, v_ref = refs[:3]
        p_refs = refs[3:3 + L * np_]
        g_ref, d_ref, nm_ref, nv_ref = refs[3 + L * np_:]
        layer = pl.program_id(0)
        g = jnp.zeros((tr, C), F32)
        for l in range(L):
            gl = p_refs[l * np_][...]
            for j in range(1, np_):
                gl = gl + p_refs[l * np_ + j][...]
            g = jnp.where(layer == l, gl, g) if L > 1 else gl
        nm = ADAM_B1 * m_ref[...] + (1.0 - ADAM_B1) * g
        nv = ADAM_B2 * v_ref[...] + (1.0 - ADAM_B2) * (g * g)
        g_ref[...] = g
        nm_ref[...] = nm
        nv_ref[...] = nv
        d_ref[...] = -ADAM_LR * ((nm * c1) / (jnp.sqrt(nv * c2) + ADAM_EPS) + ADAM_WD * w_ref[...])

    stacked = pl.BlockSpec((None, tr, C), lambda l, i: (l, i, 0))
    part_specs = [pl.BlockSpec((tr, C), (lambda l, i, ll=ll: (jnp.where(l == ll, i, 0), 0)))
                  for ll in range(L) for _ in range(np_)]
    return pl.pallas_call(
        body, name=name, grid=(L, R // tr), in_specs=[stacked] * 3 + part_specs, out_specs=[stacked] * 4,
        out_shape=[_sds(w.shape, F32)] * 4, compiler_params=_cp("arbitrary", "arbitrary"))(w, m, v, *flat)


_PACK_ROWS = 16


def _pack(arrs):
    pieces = []
    for a in arrs:
        f = a.reshape(-1).astype(F32)
        unit = _PACK_ROWS * LANES
        pad = (-f.shape[0]) % unit
        pieces.append(jnp.pad(f, (0, pad)))
    return jnp.concatenate(pieces).reshape(-1, LANES)


def _unpack(buf, shapes):
    flat = buf.reshape(-1)
    out, off = [], 0
    unit = _PACK_ROWS * LANES
    for s in shapes:
        n = 1
        for d in s:
            n *= d
        out.append(flat[off:off + n].reshape(s))
        off += n + ((-n) % unit)
    return out


def _gather_weight(name, w, l):
    return _all_gather_chips(name, w[l].astype(BF16))


def _reduce_to_owner(tag, dfull):
    recv = _scatter_chips("rs_" + tag, dfull)
    mine = _sum_blocks("sum4_" + tag, recv)
    theirs = _swap_sibling("swap_" + tag, mine)
    return mine, theirs


def kernel(x, ln_ffn_pre, ffn_pre_w_in, ffn_pre_w_out, ln_mix, ln_ffn_post, ffn_post_w_in, ffn_post_w_out, gm_w_in, gm_v_norm, gm_w_s, gm_b_s, gm_w_out, ssm_w_in, ssm_conv_w, ssm_conv_b, ssm_dt_bias, ssm_a_log, ssm_d, ssm_norm, ssm_w_out, ln_final, loss_target, m_ln_ffn_pre, m_ffn_pre_w_in, m_ffn_pre_w_out, m_ln_mix, m_ln_ffn_post, m_ffn_post_w_in, m_ffn_post_w_out, m_gm_w_in, m_gm_v_norm, m_gm_w_s, m_gm_b_s, m_gm_w_out, m_ssm_w_in, m_ssm_conv_w, m_ssm_conv_b, m_ssm_dt_bias, m_ssm_a_log, m_ssm_d, m_ssm_norm, m_ssm_w_out, m_ln_final, v_ln_ffn_pre, v_ffn_pre_w_in, v_ffn_pre_w_out, v_ln_mix, v_ln_ffn_post, v_ffn_post_w_in, v_ffn_post_w_out, v_gm_w_in, v_gm_v_norm, v_gm_w_s, v_gm_b_s, v_gm_w_out, v_ssm_w_in, v_ssm_conv_w, v_ssm_conv_b, v_ssm_dt_bias, v_ssm_a_log, v_ssm_d, v_ssm_norm, v_ssm_w_out, v_ln_final):
    T, D = x.shape[1], x.shape[2]
    depth = ln_ffn_pre.shape[0]
    n_gm, n_ssm = gm_w_in.shape[0], ssm_w_in.shape[0]
    F = ffn_pre_w_out.shape[1] * 4
    GI = gm_w_out.shape[1] * 4
    GG = gm_w_s.shape[1]
    inner = ssm_w_out.shape[1] * 4
    heads = ssm_dt_bias.shape[1]
    cd = ssm_conv_w.shape[1] * 4
    groups = (cd - inner) // (2 * SSM_STATE)
    pshard = ssm_w_in.shape[2]
    pw = inner + cd + LANES
    dtcol = (inner + cd) // LANES
    kchip = 2 * lax.axis_index("x") + lax.axis_index("y")

    small_shard = _pack([jnp.swapaxes(ssm_conv_w, 1, 2), ssm_conv_b, ssm_norm])
    small_all = _all_gather_chips("ag_small", small_shard)
    cq, iq = cd // 4, inner // 4
    parts = [_unpack(small_all[k], [(n_ssm, SSM_CONV, cq), (n_ssm, cq), (n_ssm, iq)]) for k in range(4)]
    conv_wt = jnp.concatenate([p[0] for p in parts], axis=2)
    conv_b = jnp.concatenate([p[1] for p in parts], axis=1)
    norm_g = jnp.concatenate([p[2] for p in parts], axis=1)

    def pad_lanes(v):
        return jnp.pad(v, (0, LANES - v.shape[0]))[None, :]

    xc = x[0]
    saved = []
    wts = []

    def ffn_fwd(tag, xin, g, w_in, w_out, l):
        wi = _gather_weight("ag_ffn_in", w_in, l)
        wo = _gather_weight("ag_ffn_out", w_out, l).reshape(F, D)
        h = _rms_fwd("rms_fwd", xin, g[l][None, :])
        a = _matmul("ffn_in", h, wi, "nn", BF16, out_blocks=2)
        s = _swiglu_fwd("swiglu_fwd", a)
        xo = _matmul("ffn_out", s, wo, "nn", F32, res=xin, scale=0.5)
        return xo, (xin, h, a, s, wi, wo)

    for i in range(depth):
        xc, sv_pre = ffn_fwd("pre", xc, ln_ffn_pre, ffn_pre_w_in, ffn_pre_w_out, i)
        j = i // 2
        h = _rms_fwd("rms_fwd", xc, ln_mix[i][None, :])
        if i % 2 == 0:
            wi = _gather_weight("ag_gm_in", gm_w_in, j)
            wo = _gather_weight("ag_mix_out", gm_w_out, j).reshape(GI, D)
            zpre = _matmul("gm_in", h, wi, "nn", BF16, out_blocks=2)
            u, vn = _gm_act_fwd("gm_act_fwd", zpre, gm_v_norm[j][None, :])
            bt = jnp.pad(gm_b_s[j].T, ((0, 0), (0, LANES - GG)))
            gated = _gm_spatial_fwd("gm_spatial_fwd", u, vn, gm_w_s[j], bt)
            xn = _matmul("mix_out", gated, wo, "nn", F32, res=xc, scale=1.0)
            sv_mix = (xc, h, zpre, u, vn, bt, gated, wi, wo)
        else:
            wg = _gather_weight("ag_ssm_in", ssm_w_in, j)
            wo = _gather_weight("ag_mix_out", ssm_w_out, j).reshape(inner, D)
            wp = jnp.pad(jnp.swapaxes(wg, 0, 1).reshape(D, 4 * pshard), ((0, 0), (0, pw - 4 * pshard)))
            proj = _matmul("ssm_in", h, wp, "nn", F32)
            wt, cb_ = conv_wt[j], conv_b[j][None, :]
            act = _ssd_conv_fwd("ssd_conv_fwd", proj, wt, cb_, inner, cd)
            bias, alog = pad_lanes(ssm_dt_bias[j]), pad_lanes(ssm_a_log[j])
            dtp, acum = _ssd_dt_fwd("ssd_dt_fwd", proj, bias, alog, dtcol, heads)
            dexp = jnp.repeat(ssm_d[j], SSM_HEAD_DIM)[None, :]
            ycore, sprev = _ssd_core_fwd("ssd_core_fwd", act, dtp, acum, dexp, inner, groups)
            ng = norm_g[j][None, :]
            yn = _ssd_post_fwd("ssd_post_fwd", ycore, proj, ng, inner, groups)
            xn = _matmul("mix_out", yn, wo, "nn", F32, res=xc, scale=1.0)
            sv_mix = (xc, h, proj, act, dtp, acum, dexp, ycore, sprev, yn, wt, cb_, bias, alog, ng, wp, wo)
        xc = xn
        xc, sv_post = ffn_fwd("post", xc, ln_ffn_post, ffn_post_w_in, ffn_post_w_out, i)
        saved.append((sv_pre, sv_mix, sv_post))

    loss_tile, dx, dxb, dg_final = _loss_head("loss_head", xc, ln_final[None, :], loss_target[0])
    loss = lax.psum(loss_tile[0, 0], ("x", "y", "c"))

    def ffn_bwd(dx, dxb, sv, g, l):
        xin, h, a, s, wi, wo = sv
        ds = _matmul("ffn_ds", dxb, wo, "nt", BF16, scale=0.5)
        dwo = _matmul("ffn_dwo", s, dxb, "tn", BF16, scale=0.5)
        da = _swiglu_bwd("swiglu_bwd", a, ds)
        dh = _matmul("ffn_dh", da, wi, "nt", F32)
        dwi = _matmul("ffn_dwi", h, da, "tn", BF16, out_blocks=4)
        dx2, dxb2, dg = _rms_bwd("rms_bwd", xin, g[l][None, :], dh, dx)
        return dx2, dxb2, dg, dwi, dwo.reshape(4, F // 4, D)

    gl = {n: [None] * depth for n in ("pre", "mix", "post")}
    big = {n: [None] * depth for n in ("pre_in", "pre_out", "post_in", "post_out")}
    gm_g = {n: [None] * n_gm for n in ("in", "out", "vnorm", "ws", "bs")}
    ssm_g = {n: [None] * n_ssm for n in ("in", "out", "convw", "convb", "dtb", "alog", "d", "norm")}

    for i in reversed(range(depth)):
        sv_pre, sv_mix, sv_post = saved[i]
        j = i // 2
        dx, dxb, gl["post"][i], dwi, dwo = ffn_bwd(dx, dxb, sv_post, ln_ffn_post, i)
        big["post_in"][i] = _reduce_to_owner("ffn_in", dwi)
        big["post_out"][i] = _reduce_to_owner("ffn_out", dwo)
        if i % 2 == 0:
            xin, h, zpre, u, vn, bt, gated, wi, wo = sv_mix
            dgated = _matmul("mix_dy", dxb, wo, "nt", BF16)
            dwo = _matmul("mix_dwo", gated, dxb, "tn", BF16)
            du, dvn, dws, dbt = _gm_spatial_bwd("gm_spatial_bwd", dgated, u, vn, gm_w_s[j], bt)
            dzpre, dvnorm = _gm_act_bwd("gm_act_bwd", zpre, du, dvn, gm_v_norm[j][None, :])
            dh = _matmul("gm_dh", dzpre, wi, "nt", F32)
            dwi = _matmul("gm_dwi", h, dzpre, "tn", BF16, out_blocks=4)
            gm_g["in"][j] = _reduce_to_owner("gm_in", dwi)
            gm_g["out"][j] = _reduce_to_owner("mix_out", dwo.reshape(4, GI // 4, D))
            gm_g["vnorm"][j], gm_g["ws"][j], gm_g["bs"][j] = dvnorm[0], dws, dbt.T[:GG]
        else:
            xin, h, proj, act, dtp, acum, dexp, ycore, sprev, yn, wt, cb_, bias, alog, ng, wp, wo = sv_mix
            dyn = _matmul("ssm_dy", dxb, wo, "nt", F32)
            dwo = _matmul("mix_dwo", yn, dxb, "tn", BF16)
            dyc, dz, dnorm = _ssd_post_bwd("ssd_post_bwd", dyn, ycore, proj, ng, inner, groups)
            dxs, db_, dc_, ddtp_g, dac_g, dd_g = _ssd_core_bwd("ssd_core_bwd", dyc, act, dtp, acum, dexp, sprev,
                                                               inner, groups)
            ddt, dbias, dalog, dds = _ssd_dt_bwd("ssd_dt_bwd", ddtp_g, dac_g, dd_g, proj, bias, alog, dtp, dtcol, heads)
            dact = jnp.concatenate([dxs, db_, dc_], axis=1)
            dxbc, dwt, dcb = _ssd_conv_bwd("ssd_conv_bwd", dact, proj, wt, cb_, inner, cd)
            dproj = jnp.concatenate([dz, dxbc, ddt], axis=1)
            dh = _matmul("ssm_dh", dproj, wp, "nt", F32)
            dwp = _matmul("ssm_dwi", h, dproj, "tn", BF16)
            dwi = jnp.swapaxes(dwp[:, :4 * pshard].reshape(D, 4, pshard), 0, 1)
            ssm_g["in"][j] = _reduce_to_owner("ssm_in", dwi)
            ssm_g["out"][j] = _reduce_to_owner("mix_out", dwo.reshape(4, inner // 4, D))
            ssm_g["convw"][j], ssm_g["convb"][j] = dwt.T, dcb[0]
            ssm_g["dtb"][j], ssm_g["alog"][j], ssm_g["d"][j] = dbias[0, :heads], dalog[0, :heads], dds[0, :heads]
            ssm_g["norm"][j] = dnorm[0]
        dx, dxb, gl["mix"][i] = _rms_bwd("rms_bwd", xin, ln_mix[i][None, :], dh, dx)
        dx, dxb, gl["pre"][i], dwi, dwo = ffn_bwd(dx, dxb, sv_pre, ln_ffn_pre, i)
        big["pre_in"][i] = _reduce_to_owner("ffn_in", dwi)
        big["pre_out"][i] = _reduce_to_owner("ffn_out", dwo)

    small_full = [
        jnp.concatenate(gl["pre"], 0), jnp.concatenate(gl["mix"], 0), jnp.concatenate(gl["post"], 0),
        jnp.stack(gm_g["vnorm"]), jnp.stack(gm_g["ws"]), jnp.stack(gm_g["bs"]),
        jnp.stack(ssm_g["convw"]), jnp.stack(ssm_g["convb"]), jnp.stack(ssm_g["dtb"]), jnp.stack(ssm_g["alog"]),
        jnp.stack(ssm_g["d"]), jnp.stack(ssm_g["norm"]), dg_final[0],
    ]
    full_shapes = [tuple(a.shape) for a in small_full]
    gathered = _all_gather_devices("ag8_small", _pack(small_full))
    summed = _sum_blocks("sum8_small", gathered)
    (g_pre, g_mix, g_post, g_vn, g_ws, g_bs, g_cw, g_cb, g_dtb, g_al, g_d, g_nm, g_fin) = _unpack(summed, full_shapes)
    g_cw = lax.dynamic_slice_in_dim(g_cw, kchip * cq, cq, axis=1)
    g_cb = lax.dynamic_slice_in_dim(g_cb, kchip * cq, cq, axis=1)
    g_nm = lax.dynamic_slice_in_dim(g_nm, kchip * iq, iq, axis=1)
    sm_g = [g_pre, g_mix, g_post, g_vn, g_ws, g_bs, g_cw, g_cb, g_dtb, g_al, g_d, g_nm, g_fin]
    sm_w = [ln_ffn_pre, ln_mix, ln_ffn_post, gm_v_norm, gm_w_s, gm_b_s, ssm_conv_w, ssm_conv_b, ssm_dt_bias,
            ssm_a_log, ssm_d, ssm_norm, ln_final]
    sm_m = [m_ln_ffn_pre, m_ln_mix, m_ln_ffn_post, m_gm_v_norm, m_gm_w_s, m_gm_b_s, m_ssm_conv_w, m_ssm_conv_b,
            m_ssm_dt_bias, m_ssm_a_log, m_ssm_d, m_ssm_norm, m_ln_final]
    sm_v = [v_ln_ffn_pre, v_ln_mix, v_ln_ffn_post, v_gm_v_norm, v_gm_w_s, v_gm_b_s, v_ssm_conv_w, v_ssm_conv_b,
            v_ssm_dt_bias, v_ssm_a_log, v_ssm_d, v_ssm_norm, v_ln_final]
    sm_shapes = [tuple(a.shape) for a in sm_w]
    pk = [_pack(lst)[None] for lst in (sm_w, sm_m, sm_v)]
    sg_, sd_, snm_, snv_ = _adamw("adamw_small", pk[0], pk[1], pk[2], [[_pack(sm_g)]])
    small_out = [_unpack(t[0], sm_shapes) for t in (sg_, sd_, snm_, snv_)]
    small_names = ["ln_ffn_pre", "ln_mix", "ln_ffn_post", "gm_v_norm", "gm_w_s", "gm_b_s", "ssm_conv_w", "ssm_conv_b",
                   "ssm_dt_bias", "ssm_a_log", "ssm_d", "ssm_norm", "ln_final"]

    def big_update(tag, w, m, v, parts):
        L = w.shape[0]
        shp = w.shape
        w2, m2, v2 = (t.reshape(L, -1, shp[-1]) for t in (w, m, v))
        outs = _adamw("adamw_" + tag, w2, m2, v2, [list(p) for p in parts])
        return [o.reshape(shp) for o in outs]

    big_out = {
        "ffn_pre_w_in": big_update("ffn_in", ffn_pre_w_in, m_ffn_pre_w_in, v_ffn_pre_w_in, big["pre_in"]),
        "ffn_pre_w_out": big_update("ffn_out", ffn_pre_w_out, m_ffn_pre_w_out, v_ffn_pre_w_out, big["pre_out"]),
        "ffn_post_w_in": big_update("ffn_in", ffn_post_w_in, m_ffn_post_w_in, v_ffn_post_w_in, big["post_in"]),
        "ffn_post_w_out": big_update("ffn_out", ffn_post_w_out, m_ffn_post_w_out, v_ffn_post_w_out, big["post_out"]),
        "gm_w_in": big_update("gm_in", gm_w_in, m_gm_w_in, v_gm_w_in, gm_g["in"]),
        "gm_w_out": big_update("mix_out", gm_w_out, m_gm_w_out, v_gm_w_out, gm_g["out"]),
        "ssm_w_in": big_update("ssm_in", ssm_w_in, m_ssm_w_in, v_ssm_w_in, ssm_g["in"]),
        "ssm_w_out": big_update("mix_out", ssm_w_out, m_ssm_w_out, v_ssm_w_out, ssm_g["out"]),
    }

    order = ["ln_ffn_pre", "ffn_pre_w_in", "ffn_pre_w_out", "ln_mix", "ln_ffn_post", "ffn_post_w_in", "ffn_post_w_out",
             "gm_w_in", "gm_v_norm", "gm_w_s", "gm_b_s", "gm_w_out", "ssm_w_in", "ssm_conv_w", "ssm_conv_b",
             "ssm_dt_bias", "ssm_a_log", "ssm_d", "ssm_norm", "ssm_w_out", "ln_final"]

    def pick(kind, n):
        if n in big_out:
            return big_out[n][kind]
        return small_out[kind][small_names.index(n)]

    outs = [loss, dx[None]]
    for kind in range(4):
        outs.extend(pick(kind, n) for n in order)
    return tuple(outs)
```

```python
import jax
import jax.numpy as jnp
from jax import lax
from jax.experimental import pallas as pl
from jax.experimental.pallas import tpu as pltpu

F32 = jnp.float32
BF16 = jnp.bfloat16
HIGHEST = lax.Precision.HIGHEST
MESH = pl.DeviceIdType.MESH

EPS = 1e-6
ADAM_LR, ADAM_B1, ADAM_B2, ADAM_EPS, ADAM_WD, ADAM_STEP = 0.001, 0.9, 0.999, 1e-08, 0.01, 10

LANES = 128
CHUNK = 128
SSM_STATE = 128
SSM_HEAD_DIM = 64
SSM_HPG = 8
SSM_CONV = 4
VMEM_LIMIT = 56 * 1024 * 1024
MM_TILE = 1408

_ANY = pl.BlockSpec(memory_space=pl.ANY)


def _cp(*sem):
    return pltpu.CompilerParams(dimension_semantics=sem if sem else None, vmem_limit_bytes=VMEM_LIMIT)


def _tile(n, target, mult=LANES):
    best = None
    t = mult
    while t <= min(n, target):
        if n % t == 0:
            best = t
        t += mult
    return n if best is None else best


def _gcd(*v):
    import math
    g = 0
    for a in v:
        g = math.gcd(g, a)
    return g


def _sds(shape, dtype):
    return jax.ShapeDtypeStruct(tuple(shape), dtype)


def _ldims(shape):
    return tuple(shape) if len(shape) == 2 else (shape[1], shape[0] * shape[2])


def _colblock(shape):
    return None if len(shape) == 2 else shape[2]


def _mspec(shape, tr, tc, rc):
    if len(shape) == 2:
        return pl.BlockSpec((tr, tc), rc)
    per = shape[2] // tc

    def im(i, j, k):
        r, c = rc(i, j, k)
        return (c // per, r, c % per)

    return pl.BlockSpec((None, tr, tc), im)


def _matmul(name, a, b, mode, out_dtype, out_blocks=None, res=None, scale=1.0):
    la, lb = _ldims(a.shape), _ldims(b.shape)
    if mode == "nn":
        (M, K), (K2, N) = la, lb
    elif mode == "nt":
        (M, K), (N, K2) = la, lb
    else:
        (K, M), (K2, N) = la, lb
    assert K == K2, (name, a.shape, b.shape, mode)
    ca, cb = _colblock(a.shape), _colblock(b.shape)
    out_shape = (M, N) if out_blocks is None else (out_blocks, M, N // out_blocks)
    co = _colblock(out_shape)
    m_c, n_c, k_c = [M], [N], [K]
    if ca is not None:
        (m_c if mode == "tn" else k_c).append(ca)
    if cb is not None:
        (k_c if mode == "nt" else n_c).append(cb)
    if co is not None:
        n_c.append(co)
    tm, tn, tk = _tile(_gcd(*m_c), MM_TILE), _tile(_gcd(*n_c), MM_TILE), _tile(_gcd(*k_c), MM_TILE)
    nk = K // tk
    if mode == "tn":
        a_spec = _mspec(a.shape, tk, tm, lambda i, j, k: (k, i))
        dims = (((0,), (0,)), ((), ()))
    else:
        a_spec = _mspec(a.shape, tm, tk, lambda i, j, k: (i, k))
        dims = (((1,), (1,)), ((), ())) if mode == "nt" else (((1,), (0,)), ((), ()))
    if mode == "nt":
        b_spec = _mspec(b.shape, tn, tk, lambda i, j, k: (j, k))
    else:
        b_spec = _mspec(b.shape, tk, tn, lambda i, j, k: (k, j))
    o_spec = _mspec(out_shape, tm, tn, lambda i, j, k: (i, j))
    in_specs, args = [a_spec, b_spec], [a, b]
    if res is not None:
        in_specs.append(pl.BlockSpec((tm, tn), lambda i, j, k: (i, j)))
        args.append(res)

    def body(*refs):
        a_ref, b_ref = refs[0], refs[1]
        res_ref = refs[2] if res is not None else None
        o_ref, acc_ref = refs[-2], refs[-1]
        kk = pl.program_id(2)

        @pl.when(kk == 0)
        def _():
            acc_ref[...] = jnp.zeros_like(acc_ref)

        acc_ref[...] += lax.dot_general(a_ref[...].astype(BF16), b_ref[...].astype(BF16), dims,
                                        preferred_element_type=F32)

        @pl.when(kk == nk - 1)
        def _():
            r = acc_ref[...]
            if scale != 1.0:
                r = r * scale
            if res_ref is not None:
                r = res_ref[...] + r
            o_ref[...] = r.astype(o_ref.dtype)

    return pl.pallas_call(
        body, name=name, grid=(M // tm, N // tn, nk), in_specs=in_specs, out_specs=o_spec,
        out_shape=_sds(out_shape, out_dtype), scratch_shapes=[pltpu.VMEM((tm, tn), F32)],
        compiler_params=_cp("parallel", "parallel", "arbitrary"))(*args)


def _rows(i):
    return (i, 0)


def _row0(i):
    return (0, 0)


def _rms_fwd(name, x, g):
    T, D = x.shape
    tr = _tile(T, 256, 8)

    def body(x_ref, g_ref, o_ref):
        xv = x_ref[...]
        r = lax.rsqrt(jnp.mean(xv * xv, axis=-1, keepdims=True) + EPS)
        o_ref[...] = (xv * r * g_ref[...]).astype(o_ref.dtype)

    return pl.pallas_call(
        body, name=name, grid=(T // tr,),
        in_specs=[pl.BlockSpec((tr, D), _rows), pl.BlockSpec((1, D), _row0)],
        out_specs=pl.BlockSpec((tr, D), _rows), out_shape=_sds((T, D), BF16),
        compiler_params=_cp("parallel"))(x, g)


def _rms_bwd(name, x, g, dh, dres):
    T, D = x.shape
    tr = _tile(T, 256, 8)

    def body(x_ref, g_ref, dh_ref, dres_ref, dx_ref, dxb_ref, dg_ref):
        xv = x_ref[...]
        r = lax.rsqrt(jnp.mean(xv * xv, axis=-1, keepdims=True) + EPS)
        xh = xv * r
        dhv = dh_ref[...]
        dxh = dhv * g_ref[...]
        dx = dres_ref[...] + r * (dxh - xh * jnp.mean(dxh * xh, axis=-1, keepdims=True))
        dx_ref[...] = dx
        dxb_ref[...] = dx.astype(BF16)

        @pl.when(pl.program_id(0) == 0)
        def _():
            dg_ref[...] = jnp.zeros_like(dg_ref)

        dg_ref[...] += jnp.sum(dhv * xh, axis=0, keepdims=True)

    return pl.pallas_call(
        body, name=name, grid=(T // tr,),
        in_specs=[pl.BlockSpec((tr, D), _rows), pl.BlockSpec((1, D), _row0),
                  pl.BlockSpec((tr, D), _rows), pl.BlockSpec((tr, D), _rows)],
        out_specs=[pl.BlockSpec((tr, D), _rows), pl.BlockSpec((tr, D), _rows), pl.BlockSpec((1, D), _row0)],
        out_shape=[_sds((T, D), F32), _sds((T, D), BF16), _sds((1, D), F32)],
        compiler_params=_cp("arbitrary"))(x, g, dh, dres)


def _sigmoid(v):
    return 1.0 / (1.0 + jnp.exp(-v))


def _swiglu_fwd(name, a):
    _, T, F = a.shape
    tr, tc = _tile(T, 512, 8), _tile(F, MM_TILE)

    def body(a_ref, o_ref):
        gate = a_ref[0].astype(F32)
        up = a_ref[1].astype(F32)
        o_ref[...] = (gate * _sigmoid(gate) * up).astype(o_ref.dtype)

    return pl.pallas_call(
        body, name=name, grid=(T // tr, F // tc),
        in_specs=[pl.BlockSpec((2, tr, tc), lambda i, j: (0, i, j))],
        out_specs=pl.BlockSpec((tr, tc), lambda i, j: (i, j)), out_shape=_sds((T, F), BF16),
        compiler_params=_cp("parallel", "parallel"))(a)


def _swiglu_bwd(name, a, ds):
    _, T, F = a.shape
    tr, tc = _tile(T, 512, 8), _tile(F, MM_TILE)

    def body(a_ref, ds_ref, o_ref):
        gate = a_ref[0].astype(F32)
        up = a_ref[1].astype(F32)
        dsv = ds_ref[...].astype(F32)
        sg = _sigmoid(gate)
        o_ref[0] = (dsv * up * sg * (1.0 + gate * (1.0 - sg))).astype(o_ref.dtype)
        o_ref[1] = (dsv * gate * sg).astype(o_ref.dtype)

    return pl.pallas_call(
        body, name=name, grid=(T // tr, F // tc),
        in_specs=[pl.BlockSpec((2, tr, tc), lambda i, j: (0, i, j)), pl.BlockSpec((tr, tc), lambda i, j: (i, j))],
        out_specs=pl.BlockSpec((2, tr, tc), lambda i, j: (0, i, j)), out_shape=_sds((2, T, F), BF16),
        compiler_params=_cp("parallel", "parallel"))(a, ds)


def _loss_head(name, x, g, tgt):
    T, D = x.shape
    tr = _tile(T, 256, 8)

    def body(x_ref, g_ref, t_ref, loss_ref, dx_ref, dxb_ref, dg_ref):
        xv = x_ref[...]
        gv = g_ref[...]
        r = lax.rsqrt(jnp.mean(xv * xv, axis=-1, keepdims=True) + EPS)
        xh = xv * r
        err = xh * gv - t_ref[...]
        dy = err * (1.0 / D)
        dxh = dy * gv
        dx = r * (dxh - xh * jnp.mean(dxh * xh, axis=-1, keepdims=True))
        dx_ref[...] = dx
        dxb_ref[...] = dx.astype(BF16)

        @pl.when(pl.program_id(0) == 0)
        def _():
            dg_ref[...] = jnp.zeros_like(dg_ref)
            loss_ref[...] = jnp.zeros_like(loss_ref)

        dg_ref[...] += jnp.sum(dy * xh, axis=0, keepdims=True)
        part = jnp.sum(jnp.sum(err * err, axis=-1, keepdims=True), axis=0, keepdims=True) * (0.5 / D)
        loss_ref[...] += part

    return pl.pallas_call(
        body, name=name, grid=(T // tr,),
        in_specs=[pl.BlockSpec((tr, D), _rows), pl.BlockSpec((1, D), _row0), pl.BlockSpec((tr, D), _rows)],
        out_specs=[pl.BlockSpec((8, LANES), _row0), pl.BlockSpec((tr, D), _rows), pl.BlockSpec((tr, D), _rows),
                   pl.BlockSpec((1, D), _row0)],
        out_shape=[_sds((8, LANES), F32), _sds((T, D), F32), _sds((T, D), BF16), _sds((1, D), F32)],
        compiler_params=_cp("arbitrary"))(x, g, tgt)


_SQRT_HALF = 0.7071067811865476
_INV_SQRT_2PI = 0.3989422804014327


def _gelu(v):
    return 0.5 * v * (1.0 + lax.erf(v * _SQRT_HALF))


def _gelu_grad(v):
    return 0.5 * (1.0 + lax.erf(v * _SQRT_HALF)) + v * _INV_SQRT_2PI * jnp.exp(-0.5 * v * v)


def _group_expand(rows, width, gd):
    gi = lax.broadcasted_iota(jnp.int32, (rows, width), 0)
    fi = lax.broadcasted_iota(jnp.int32, (rows, width), 1)
    return ((fi >= gi * gd) & (fi < (gi + 1) * gd)).astype(F32)


def _gm_act_fwd(name, zpre, vnorm):
    _, T, GI = zpre.shape
    tr = _tile(T, 128, 8)

    def body(z_ref, g_ref, u_ref, v_ref):
        u_ref[...] = _gelu(z_ref[0].astype(F32)).astype(BF16)
        zv = _gelu(z_ref[1].astype(F32))
        r = lax.rsqrt(jnp.mean(zv * zv, axis=-1, keepdims=True) + EPS)
        v_ref[...] = (zv * r * g_ref[...]).astype(BF16)

    return pl.pallas_call(
        body, name=name, grid=(T // tr,),
        in_specs=[pl.BlockSpec((2, tr, GI), lambda i: (0, i, 0)), pl.BlockSpec((1, GI), _row0)],
        out_specs=[pl.BlockSpec((tr, GI), _rows), pl.BlockSpec((tr, GI), _rows)],
        out_shape=[_sds((T, GI), BF16), _sds((T, GI), BF16)], compiler_params=_cp("parallel"))(zpre, vnorm)


def _gm_act_bwd(name, zpre, du, dvn, vnorm):
    _, T, GI = zpre.shape
    tr = _tile(T, 128, 8)

    def body(z_ref, du_ref, dvn_ref, g_ref, dz_ref, dg_ref):
        xu = z_ref[0].astype(F32)
        xv = z_ref[1].astype(F32)
        zv = _gelu(xv)
        r = lax.rsqrt(jnp.mean(zv * zv, axis=-1, keepdims=True) + EPS)
        xh = zv * r
        dv = dvn_ref[...]

        @pl.when(pl.program_id(0) == 0)
        def _():
            dg_ref[...] = jnp.zeros_like(dg_ref)

        dg_ref[...] += jnp.sum(dv * xh, axis=0, keepdims=True)
        dxh = dv * g_ref[...]
        dzv = r * (dxh - xh * jnp.mean(dxh * xh, axis=-1, keepdims=True))
        dz_ref[0] = (du_ref[...] * _gelu_grad(xu)).astype(BF16)
        dz_ref[1] = (dzv * _gelu_grad(xv)).astype(BF16)

    return pl.pallas_call(
        body, name=name, grid=(T // tr,),
        in_specs=[pl.BlockSpec((2, tr, GI), lambda i: (0, i, 0)), pl.BlockSpec((tr, GI), _rows),
                  pl.BlockSpec((tr, GI), _rows), pl.BlockSpec((1, GI), _row0)],
        out_specs=[pl.BlockSpec((2, tr, GI), lambda i: (0, i, 0)), pl.BlockSpec((1, GI), _row0)],
        out_shape=[_sds((2, T, GI), BF16), _sds((1, GI), F32)], compiler_params=_cp("arbitrary"))(zpre, du, dvn, vnorm)


def _causal_mask():
    r = lax.broadcasted_iota(jnp.int32, (CHUNK, CHUNK), 0)
    c = lax.broadcasted_iota(jnp.int32, (CHUNK, CHUNK), 1)
    return r >= c


def _gm_spatial_fwd(name, u, vn, ws, bt):
    T, GI = u.shape
    G = ws.shape[0]
    gd = GI // G

    def body(u_ref, v_ref, ws_ref, bt_ref, o_ref, bias_scr):
        @pl.when(pl.program_id(0) == 0)
        def _():
            bias_scr[...] = jnp.dot(bt_ref[...], _group_expand(LANES, GI, gd), precision=HIGHEST,
                                    preferred_element_type=F32)

        causal = _causal_mask()
        for g in range(G):
            sl = slice(g * gd, (g + 1) * gd)
            wc = jnp.where(causal, ws_ref[g], 0.0).astype(BF16)
            mixed = jnp.dot(wc, v_ref[:, sl], preferred_element_type=F32) + bias_scr[:, sl]
            o_ref[:, sl] = (u_ref[:, sl].astype(F32) * mixed).astype(o_ref.dtype)

    return pl.pallas_call(
        body, name=name, grid=(T // CHUNK,),
        in_specs=[pl.BlockSpec((CHUNK, GI), _rows), pl.BlockSpec((CHUNK, GI), _rows),
                  pl.BlockSpec((G, CHUNK, CHUNK), lambda i: (0, 0, 0)), pl.BlockSpec((CHUNK, LANES), _row0)],
        out_specs=pl.BlockSpec((CHUNK, GI), _rows), out_shape=_sds((T, GI), BF16),
        scratch_shapes=[pltpu.VMEM((CHUNK, GI), F32)], compiler_params=_cp("arbitrary"))(u, vn, ws, bt)


def _gm_spatial_bwd(name, dgated, u, vn, ws, bt):
    T, GI = u.shape
    G = ws.shape[0]
    gd = GI // G
    nc = T // CHUNK

    def body(dg_ref, u_ref, v_ref, ws_ref, bt_ref, du_ref, dv_ref, dws_ref, dbt_ref, bias_scr, dm_scr):
        step = pl.program_id(0)

        @pl.when(step == 0)
        def _():
            bias_scr[...] = jnp.dot(bt_ref[...], _group_expand(LANES, GI, gd), precision=HIGHEST,
                                    preferred_element_type=F32)
            dm_scr[...] = jnp.zeros_like(dm_scr)
            dws_ref[...] = jnp.zeros_like(dws_ref)

        causal = _causal_mask()
        for g in range(G):
            sl = slice(g * gd, (g + 1) * gd)
            wc = jnp.where(causal, ws_ref[g], 0.0).astype(BF16)
            vv = v_ref[:, sl]
            dgv = dg_ref[:, sl].astype(F32)
            mixed = jnp.dot(wc, vv, preferred_element_type=F32) + bias_scr[:, sl]
            du_ref[:, sl] = dgv * mixed
            dm = dgv * u_ref[:, sl].astype(F32)
            dmb = dm.astype(BF16)
            dv_ref[:, sl] = lax.dot_general(wc, dmb, (((0,), (0,)), ((), ())), preferred_element_type=F32)
            dw = lax.dot_general(dmb, vv, (((1,), (1,)), ((), ())), preferred_element_type=F32)
            dws_ref[g] += jnp.where(causal, dw, 0.0)
            dm_scr[:, sl] += dm

        @pl.when(step == nc - 1)
        def _():
            dbt_ref[...] = lax.dot_general(dm_scr[...], _group_expand(LANES, GI, gd), (((1,), (1,)), ((), ())),
                                           precision=HIGHEST, preferred_element_type=F32)

    return pl.pallas_call(
        body, name=name, grid=(nc,),
        in_specs=[pl.BlockSpec((CHUNK, GI), _rows), pl.BlockSpec((CHUNK, GI), _rows), pl.BlockSpec((CHUNK, GI), _rows),
                  pl.BlockSpec((G, CHUNK, CHUNK), lambda i: (0, 0, 0)), pl.BlockSpec((CHUNK, LANES), _row0)],
        out_specs=[pl.BlockSpec((CHUNK, GI), _rows), pl.BlockSpec((CHUNK, GI), _rows),
                   pl.BlockSpec((G, CHUNK, CHUNK), lambda i: (0, 0, 0)), pl.BlockSpec((CHUNK, LANES), _row0)],
        out_shape=[_sds((T, GI), F32), _sds((T, GI), F32), _sds((G, CHUNK, CHUNK), F32), _sds((CHUNK, LANES), F32)],
        scratch_shapes=[pltpu.VMEM((CHUNK, GI), F32), pltpu.VMEM((CHUNK, GI), F32)],
        compiler_params=_cp("arbitrary"))(dgated, u, vn, ws, bt)


def _conv_taps(xv, w_ref, b_ref):
    rows = lax.broadcasted_iota(jnp.int32, xv.shape, 0)
    acc = xv * w_ref[pl.ds(SSM_CONV - 1, 1), :] + b_ref[...]
    for k in range(1, SSM_CONV):
        sh = jnp.where(rows >= k, pltpu.roll(xv, k, 0), 0.0)
        acc = acc + sh * w_ref[pl.ds(SSM_CONV - 1 - k, 1), :]
    return acc


def _ssd_conv_fwd(name, proj, wt, b, inner, cd):
    T = proj.shape[0]
    tc = _tile(_gcd(inner, cd), 512)
    off = inner // tc

    def body(x_ref, w_ref, b_ref, o_ref):
        pre = _conv_taps(x_ref[...], w_ref, b_ref)
        o_ref[...] = pre * _sigmoid(pre)

    return pl.pallas_call(
        body, name=name, grid=(cd // tc,),
        in_specs=[pl.BlockSpec((T, tc), lambda j: (0, off + j)), pl.BlockSpec((SSM_CONV, tc), lambda j: (0, j)),
                  pl.BlockSpec((1, tc), lambda j: (0, j))],
        out_specs=pl.BlockSpec((T, tc), lambda j: (0, j)), out_shape=_sds((T, cd), F32),
        compiler_params=_cp("parallel"))(proj, wt, b)


def _ssd_conv_bwd(name, dact, proj, wt, b, inner, cd):
    T = proj.shape[0]
    tc = _tile(_gcd(inner, cd), 512)
    off = inner // tc

    def body(da_ref, x_ref, w_ref, b_ref, dx_ref, dw_ref, db_ref):
        xv = x_ref[...]
        pre = _conv_taps(xv, w_ref, b_ref)
        sg = _sigmoid(pre)
        dpre = da_ref[...] * sg * (1.0 + pre * (1.0 - sg))
        rows = lax.broadcasted_iota(jnp.int32, xv.shape, 0)
        db_ref[...] = jnp.sum(dpre, axis=0, keepdims=True)
        dx = dpre * w_ref[pl.ds(SSM_CONV - 1, 1), :]
        dw_ref[pl.ds(SSM_CONV - 1, 1), :] = jnp.sum(dpre * xv, axis=0, keepdims=True)
        for k in range(1, SSM_CONV):
            sh = jnp.where(rows >= k, pltpu.roll(xv, k, 0), 0.0)
            dw_ref[pl.ds(SSM_CONV - 1 - k, 1), :] = jnp.sum(dpre * sh, axis=0, keepdims=True)
            fw = jnp.where(rows < T - k, pltpu.roll(dpre, T - k, 0), 0.0)
            dx = dx + fw * w_ref[pl.ds(SSM_CONV - 1 - k, 1), :]
        dx_ref[...] = dx.astype(BF16)

    return pl.pallas_call(
        body, name=name, grid=(cd // tc,),
        in_specs=[pl.BlockSpec((T, tc), lambda j: (0, j)), pl.BlockSpec((T, tc), lambda j: (0, off + j)),
                  pl.BlockSpec((SSM_CONV, tc), lambda j: (0, j)), pl.BlockSpec((1, tc), lambda j: (0, j))],
        out_specs=[pl.BlockSpec((T, tc), lambda j: (0, j)), pl.BlockSpec((SSM_CONV, tc), lambda j: (0, j)),
                   pl.BlockSpec((1, tc), lambda j: (0, j))],
        out_shape=[_sds((T, cd), BF16), _sds((SSM_CONV, cd), F32), _sds((1, cd), F32)],
        compiler_params=_cp("parallel"))(dact, proj, wt, b)


def _softplus(v):
    return jnp.maximum(v, 0.0) + jnp.log(1.0 + jnp.exp(-jnp.abs(v)))


def _tri(lower):
    r = lax.broadcasted_iota(jnp.int32, (CHUNK, CHUNK), 0)
    c = lax.broadcasted_iota(jnp.int32, (CHUNK, CHUNK), 1)
    return ((c <= r) if lower else (c >= r)).astype(F32)


def _ssd_dt_fwd(name, proj, bias, alog, dtcol, heads):
    T = proj.shape[0]

    def body(dt_ref, b_ref, al_ref, dtp_ref, ac_ref):
        live = lax.broadcasted_iota(jnp.int32, (CHUNK, LANES), 1) < heads
        dtp = jnp.where(live, _softplus(dt_ref[...] + b_ref[...]), 0.0)
        da = dtp * (-jnp.exp(al_ref[...]))
        dtp_ref[...] = dtp
        ac_ref[...] = jnp.dot(_tri(True), da, precision=HIGHEST, preferred_element_type=F32)

    return pl.pallas_call(
        body, name=name, grid=(T // CHUNK,),
        in_specs=[pl.BlockSpec((CHUNK, LANES), lambda i: (i, dtcol)), pl.BlockSpec((1, LANES), _row0),
                  pl.BlockSpec((1, LANES), _row0)],
        out_specs=[pl.BlockSpec((CHUNK, LANES), _rows), pl.BlockSpec((CHUNK, LANES), _rows)],
        out_shape=[_sds((T, LANES), F32), _sds((T, LANES), F32)], compiler_params=_cp("parallel"))(proj, bias, alog)


def _ssd_dt_bwd(name, ddtp_g, dacum_g, dd_g, proj, bias, alog, dtp, dtcol, heads):
    T = proj.shape[0]
    G = ddtp_g.shape[0]

    def body(ddtp_ref, dac_ref, dd_ref, dt_ref, b_ref, al_ref, dtp_ref, ddt_ref, db_ref, dal_ref, dds_ref, da_scr):
        step = pl.program_id(0)

        @pl.when(step == 0)
        def _():
            db_ref[...] = jnp.zeros_like(db_ref)
            da_scr[...] = jnp.zeros_like(da_scr)
            dds_ref[...] = jnp.sum(dd_ref[...], axis=0)

        live = lax.broadcasted_iota(jnp.int32, (CHUNK, LANES), 1) < heads
        a = -jnp.exp(al_ref[...])
        dac = jnp.sum(dac_ref[...], axis=0)
        dda = jnp.dot(_tri(False), dac, precision=HIGHEST, preferred_element_type=F32)
        dtp_v = dtp_ref[...]
        ddtp = jnp.sum(ddtp_ref[...], axis=0) + dda * a
        da_scr[...] += jnp.sum(dda * dtp_v, axis=0, keepdims=True)
        ddt = jnp.where(live, ddtp * _sigmoid(dt_ref[...] + b_ref[...]), 0.0)
        ddt_ref[...] = ddt.astype(BF16)
        db_ref[...] += jnp.sum(ddt, axis=0, keepdims=True)
        dal_ref[...] = da_scr[...] * a

    return pl.pallas_call(
        body, name=name, grid=(T // CHUNK,),
        in_specs=[pl.BlockSpec((G, CHUNK, LANES), lambda i: (0, i, 0)), pl.BlockSpec((G, CHUNK, LANES), lambda i: (0, i, 0)),
                  pl.BlockSpec((G, 8, LANES), lambda i: (0, 0, 0)),
                  pl.BlockSpec((CHUNK, LANES), lambda i: (i, dtcol)), pl.BlockSpec((1, LANES), _row0),
                  pl.BlockSpec((1, LANES), _row0), pl.BlockSpec((CHUNK, LANES), _rows)],
        out_specs=[pl.BlockSpec((CHUNK, LANES), _rows), pl.BlockSpec((8, LANES), _row0), pl.BlockSpec((8, LANES), _row0),
                   pl.BlockSpec((8, LANES), _row0)],
        out_shape=[_sds((T, LANES), BF16), _sds((8, LANES), F32), _sds((8, LANES), F32), _sds((8, LANES), F32)],
        scratch_shapes=[pltpu.VMEM((8, LANES), F32)],
        compiler_params=_cp("arbitrary"))(ddtp_g, dacum_g, dd_g, proj, bias, alog, dtp)


def _head_expand(g, gw):
    hi = lax.broadcasted_iota(jnp.int32, (LANES, gw), 0) - g * SSM_HPG
    fi = lax.broadcasted_iota(jnp.int32, (LANES, gw), 1)
    return ((fi >= hi * SSM_HEAD_DIM) & (fi < (hi + 1) * SSM_HEAD_DIM)).astype(F32)


def _dot(a, b, dims, exact=False):
    if exact:
        return lax.dot_general(a, b, (dims, ((), ())), precision=HIGHEST, preferred_element_type=F32)
    return lax.dot_general(a.astype(BF16), b.astype(BF16), (dims, ((), ())), preferred_element_type=F32)


_NN = ((1,), (0,))
_NT = ((1,), (1,))
_TN = ((0,), (0,))


def _pair_decay(g, q, acum, acum_t_ref, causal):
    lane = lax.broadcasted_iota(jnp.int32, (CHUNK, LANES), 1)
    out = []
    for e in range(2):
        h = g * SSM_HPG + 2 * q + e
        acol = jnp.sum(jnp.where(lane == h, acum, 0.0), axis=1, keepdims=True)
        arow = acum_t_ref[pl.ds(h, 1), :]
        out.append(jnp.exp(jnp.where(causal, acol - arow, -1e30)))
    return out


def _ssd_core_fwd(name, act, dtp, acum, dexp, inner, groups):
    T = act.shape[0]
    nc = T // CHUNK
    gw = SSM_HPG * SSM_HEAD_DIM
    npair = gw // LANES
    bcol, ccol = inner // SSM_STATE, inner // SSM_STATE + groups

    def body(x_ref, b_ref, c_ref, dtp_ref, ac_ref, d_ref, y_ref, sp_ref, st_scr, act_scr, ae_scr):
        g = pl.program_id(0)

        @pl.when(pl.program_id(1) == 0)
        def _():
            st_scr[...] = jnp.zeros_like(st_scr)

        st = st_scr[...]
        sp_ref[...] = st
        e = _head_expand(g, gw)
        acum = ac_ref[...]
        ae = _dot(acum, e, _NN, exact=True)
        dte = _dot(dtp_ref[...], e, _NN, exact=True)
        ae_scr[...] = ae
        act_scr[...] = acum.T
        xv = x_ref[...]
        xdt = xv * dte
        bm, cm = b_ref[...], c_ref[...]
        cb = _dot(cm, bm, _NT)
        causal = _causal_mask()
        lane = lax.broadcasted_iota(jnp.int32, (CHUNK, LANES), 1)
        yoff = _dot(cm, st, _NN) * jnp.exp(ae)
        skip = xv * d_ref[...]
        for q in range(npair):
            sl = slice(q * LANES, (q + 1) * LANES)
            dec = _pair_decay(g, q, acum, act_scr, causal)
            x2 = xdt[:, sl]
            xa = jnp.where(lane < SSM_HEAD_DIM, x2, 0.0)
            yd = _dot(dec[0] * cb, xa, _NN) + _dot(dec[1] * cb, x2 - xa, _NN)
            y_ref[:, sl] = yd + yoff[:, sl] + skip[:, sl]
        alast = ae_scr[pl.ds(CHUNK - 1, 1), :]
        z = xdt * jnp.exp(alast - ae)
        st_scr[...] = st * jnp.exp(alast) + _dot(bm, z, _TN)

    return pl.pallas_call(
        body, name=name, grid=(groups, nc),
        in_specs=[pl.BlockSpec((CHUNK, gw), lambda g, c: (c, g)),
                  pl.BlockSpec((CHUNK, SSM_STATE), lambda g, c: (c, bcol + g)),
                  pl.BlockSpec((CHUNK, SSM_STATE), lambda g, c: (c, ccol + g)),
                  pl.BlockSpec((CHUNK, LANES), lambda g, c: (c, 0)), pl.BlockSpec((CHUNK, LANES), lambda g, c: (c, 0)),
                  pl.BlockSpec((1, gw), lambda g, c: (0, g))],
        out_specs=[pl.BlockSpec((CHUNK, gw), lambda g, c: (c, g)),
                   pl.BlockSpec((None, SSM_STATE, gw), lambda g, c: (c, 0, g))],
        out_shape=[_sds((T, inner), F32), _sds((nc, SSM_STATE, inner), F32)],
        scratch_shapes=[pltpu.VMEM((SSM_STATE, gw), F32), pltpu.VMEM((CHUNK, LANES), F32), pltpu.VMEM((CHUNK, gw), F32)],
        compiler_params=_cp("arbitrary", "arbitrary"))(act, act, act, dtp, acum, dexp)


def _ssd_core_bwd(name, dy, act, dtp, acum, dexp, sprev, inner, groups):
    T = act.shape[0]
    nc = T // CHUNK
    gw = SSM_HPG * SSM_HEAD_DIM
    npair = gw // LANES
    bcol, ccol = inner // SSM_STATE, inner // SSM_STATE + groups

    def rc(g, c):
        return nc - 1 - c

    def body(dy_ref, x_ref, b_ref, c_ref, dtp_ref, ac_ref, d_ref, sp_ref,
             dx_ref, db_ref, dc_ref, ddtp_ref, dac_ref, dd_ref,
             dst_scr, act_scr, ae_scr, dxdt_scr, dd_scr, dact_scr):
        g = pl.program_id(0)
        step = pl.program_id(1)

        @pl.when(step == 0)
        def _():
            dst_scr[...] = jnp.zeros_like(dst_scr)
            dd_scr[...] = jnp.zeros_like(dd_scr)

        dst = dst_scr[...]
        sp = sp_ref[...]
        e = _head_expand(g, gw)
        acum = ac_ref[...]
        ae = _dot(acum, e, _NN, exact=True)
        dte = _dot(dtp_ref[...], e, _NN, exact=True)
        ae_scr[...] = ae
        act_scr[...] = acum.T
        alast = ae_scr[pl.ds(CHUNK - 1, 1), :]
        xv = x_ref[...]
        xdt = xv * dte
        bm, cm = b_ref[...], c_ref[...]
        dyv = dy_ref[...]
        cb = _dot(cm, bm, _NT)
        causal = _causal_mask()
        lane = lax.broadcasted_iota(jnp.int32, (CHUNK, LANES), 1)
        ea = jnp.exp(ae)
        cde = jnp.exp(alast)
        w = jnp.exp(alast - ae)
        z = xdt * w

        dd_scr[...] += jnp.sum(dyv * xv, axis=0, keepdims=True)
        qm = _dot(cm, sp, _NN)
        dq = dyv * ea
        dae = dq * qm
        dc = _dot(dq, sp, _NT)
        dsp = _dot(cm, dq, _TN) + dst * cde
        dal = jnp.sum(dst * sp, axis=0, keepdims=True) * cde
        db = _dot(z, dst, _NT)
        dz = _dot(bm, dst, _NN)
        gw_ = dz * z
        dae = dae - gw_
        dal = dal + jnp.sum(gw_, axis=0, keepdims=True)
        dxdt_scr[...] = dz * w
        dcb = jnp.zeros((CHUNK, CHUNK), F32)
        dacol = jnp.zeros((CHUNK, LANES), F32)
        dact_scr[...] = jnp.zeros_like(dact_scr)
        sub = lax.broadcasted_iota(jnp.int32, (CHUNK, LANES), 0)
        for q in range(npair):
            sl = slice(q * LANES, (q + 1) * LANES)
            dec = _pair_decay(g, q, acum, act_scr, causal)
            x2, dy2 = xdt[:, sl], dyv[:, sl]
            xs_ = (jnp.where(lane < SSM_HEAD_DIM, x2, 0.0),)
            xs_ = xs_ + (x2 - xs_[0],)
            dys = (jnp.where(lane < SSM_HEAD_DIM, dy2, 0.0),)
            dys = dys + (dy2 - dys[0],)
            dx2 = jnp.zeros((CHUNK, LANES), F32)
            for hh in range(2):
                h = g * SSM_HPG + 2 * q + hh
                m = dec[hh] * cb
                dm = _dot(dys[hh], xs_[hh], _NT)
                dx2 = dx2 + _dot(m, dys[hh], _TN)
                dcb = dcb + dm * dec[hh]
                r = dm * m
                dacol = dacol + jnp.where(lane == h, jnp.sum(r, axis=1, keepdims=True), 0.0)
                dact_scr[...] -= jnp.where(sub == h, jnp.sum(r, axis=0, keepdims=True), 0.0)
            dxdt_scr[:, sl] += dx2
        dc = dc + _dot(dcb, bm, _NN)
        db = db + _dot(dcb, cm, _TN)
        dxdt = dxdt_scr[...]
        dx_ref[...] = dyv * d_ref[...] + dxdt * dte
        db_ref[...] = db
        dc_ref[...] = dc
        ddtp_ref[...] = _dot(dxdt * xv, e, _NT, exact=True)
        dal_h = _dot(jnp.broadcast_to(dal, (8, gw)), e, _NT, exact=True)
        dal_row = jnp.max(dal_h, axis=0, keepdims=True)
        dac = _dot(dae, e, _NT, exact=True) + dacol + dact_scr[...].T
        dac_ref[...] = dac + jnp.where(sub == CHUNK - 1, dal_row, 0.0)
        dst_scr[...] = dsp

        @pl.when(step == nc - 1)
        def _():
            dd_ref[...] = _dot(jnp.broadcast_to(dd_scr[...], (8, gw)), e, _NT, exact=True)

    return pl.pallas_call(
        body, name=name, grid=(groups, nc),
        in_specs=[pl.BlockSpec((CHUNK, gw), lambda g, c: (rc(g, c), g)),
                  pl.BlockSpec((CHUNK, gw), lambda g, c: (rc(g, c), g)),
                  pl.BlockSpec((CHUNK, SSM_STATE), lambda g, c: (rc(g, c), bcol + g)),
                  pl.BlockSpec((CHUNK, SSM_STATE), lambda g, c: (rc(g, c), ccol + g)),
                  pl.BlockSpec((CHUNK, LANES), lambda g, c: (rc(g, c), 0)),
                  pl.BlockSpec((CHUNK, LANES), lambda g, c: (rc(g, c), 0)),
                  pl.BlockSpec((1, gw), lambda g, c: (0, g)),
                  pl.BlockSpec((None, SSM_STATE, gw), lambda g, c: (rc(g, c), 0, g))],
        out_specs=[pl.BlockSpec((CHUNK, gw), lambda g, c: (rc(g, c), g)),
                   pl.BlockSpec((CHUNK, SSM_STATE), lambda g, c: (rc(g, c), g)),
                   pl.BlockSpec((CHUNK, SSM_STATE), lambda g, c: (rc(g, c), g)),
                   pl.BlockSpec((None, CHUNK, LANES), lambda g, c: (g, rc(g, c), 0)),
                   pl.BlockSpec((None, CHUNK, LANES), lambda g, c: (g, rc(g, c), 0)),
                   pl.BlockSpec((None, 8, LANES), lambda g, c: (g, 0, 0))],
        out_shape=[_sds((T, inner), F32), _sds((T, groups * SSM_STATE), F32), _sds((T, groups * SSM_STATE), F32),
                   _sds((groups, T, LANES), F32), _sds((groups, T, LANES), F32), _sds((groups, 8, LANES), F32)],
        scratch_shapes=[pltpu.VMEM((SSM_STATE, gw), F32), pltpu.VMEM((CHUNK, LANES), F32), pltpu.VMEM((CHUNK, gw), F32),
                        pltpu.VMEM((CHUNK, gw), F32), pltpu.VMEM((1, gw), F32), pltpu.VMEM((CHUNK, LANES), F32)],
        compiler_params=_cp("arbitrary", "arbitrary"))(dy, act, act, act, dtp, acum, dexp, sprev)


def _ssd_post_fwd(name, y, proj, ng, inner, groups):
    T = y.shape[0]
    tr = _tile(T, 128, 8)
    gs = inner // groups

    def body(y_ref, z_ref, g_ref, o_ref):
        zv = z_ref[...]
        gy = y_ref[...] * (zv * _sigmoid(zv))
        for k in range(groups):
            sl = slice(k * gs, (k + 1) * gs)
            seg = gy[:, sl]
            r = lax.rsqrt(jnp.mean(seg * seg, axis=-1, keepdims=True) + EPS)
            o_ref[:, sl] = (seg * r * g_ref[:, sl]).astype(BF16)

    return pl.pallas_call(
        body, name=name, grid=(T // tr,),
        in_specs=[pl.BlockSpec((tr, inner), _rows), pl.BlockSpec((tr, inner), _rows), pl.BlockSpec((1, inner), _row0)],
        out_specs=pl.BlockSpec((tr, inner), _rows), out_shape=_sds((T, inner), BF16),
        compiler_params=_cp("parallel"))(y, proj, ng)


def _ssd_post_bwd(name, dyn, y, proj, ng, inner, groups):
    T = y.shape[0]
    tr = _tile(T, 128, 8)
    gs = inner // groups

    def body(dyn_ref, y_ref, z_ref, g_ref, dy_ref, dz_ref, dg_ref):
        @pl.when(pl.program_id(0) == 0)
        def _():
            dg_ref[...] = jnp.zeros_like(dg_ref)

        zv = z_ref[...]
        sg = _sigmoid(zv)
        sz = zv * sg
        yv = y_ref[...]
        gy = yv * sz
        dv = dyn_ref[...]
        for k in range(groups):
            sl = slice(k * gs, (k + 1) * gs)
            seg = gy[:, sl]
            r = lax.rsqrt(jnp.mean(seg * seg, axis=-1, keepdims=True) + EPS)
            xh = seg * r
            d = dv[:, sl]
            dg_ref[:, sl] += jnp.sum(d * xh, axis=0, keepdims=True)
            dxh = d * g_ref[:, sl]
            dgy = r * (dxh - xh * jnp.mean(dxh * xh, axis=-1, keepdims=True))
            dy_ref[:, sl] = dgy * sz[:, sl]
            dz_ref[:, sl] = (dgy * yv[:, sl] * (sg[:, sl] * (1.0 + zv[:, sl] * (1.0 - sg[:, sl])))).astype(BF16)

    return pl.pallas_call(
        body, name=name, grid=(T // tr,),
        in_specs=[pl.BlockSpec((tr, inner), _rows), pl.BlockSpec((tr, inner), _rows), pl.BlockSpec((tr, inner), _rows),
                  pl.BlockSpec((1, inner), _row0)],
        out_specs=[pl.BlockSpec((tr, inner), _rows), pl.BlockSpec((tr, inner), _rows), pl.BlockSpec((1, inner), _row0)],
        out_shape=[_sds((T, inner), F32), _sds((T, inner), BF16), _sds((1, inner), F32)],
        compiler_params=_cp("arbitrary"))(dyn, y, proj, ng)


def _place():
    return lax.axis_index("x"), lax.axis_index("y"), lax.axis_index("c")


def _other_chips(x, y):
    return [(1 - x, y), (x, 1 - y), (1 - x, 1 - y)]


def _remote(src, dst, ssem, rsem, dev):
    return pltpu.make_async_remote_copy(src_ref=src, dst_ref=dst, send_sem=ssem, recv_sem=rsem, device_id=dev,
                                        device_id_type=MESH)


def _all_gather_chips(name, shard):
    R, C = shard.shape
    hr = R // 2

    def body(x_ref, o_ref, ssem, rsem, lsem):
        x, y, c = _place()
        k = 2 * x + y
        sib = (x, y, 1 - c)
        chips = _other_chips(x, y)

        def half(blk, cc):
            return o_ref.at[blk, pl.ds(cc * hr, hr), :]

        local = pltpu.make_async_copy(x_ref, o_ref.at[k], lsem)
        local.start()
        first = [_remote(x_ref.at[pl.ds(c * hr, hr), :], half(k, c), ssem.at[r], rsem.at[r], (px, py, c))
                 for r, (px, py) in enumerate(chips)]
        for cp in first:
            cp.start()
        passed = []
        for r, (px, py) in enumerate(chips):
            kj = 2 * px + py
            _remote(half(kj, c), half(kj, c), ssem.at[r], rsem.at[r], (px, py, c)).wait_recv()
            fw = _remote(half(kj, c), half(kj, c), ssem.at[3 + r], rsem.at[3 + r], sib)
            fw.start()
            passed.append(fw)
        for r, (px, py) in enumerate(chips):
            kj = 2 * px + py
            _remote(half(kj, 1 - c), half(kj, 1 - c), ssem.at[3 + r], rsem.at[3 + r], sib).wait_recv()
        for cp in first + passed:
            cp.wait_send()
        local.wait()

    return pl.pallas_call(
        body, name=name, in_specs=[_ANY], out_specs=_ANY, out_shape=_sds((4, R, C), shard.dtype),
        scratch_shapes=[pltpu.SemaphoreType.DMA((6,)), pltpu.SemaphoreType.DMA((6,)), pltpu.SemaphoreType.DMA],
    )(shard)


_HBM = pl.BlockSpec(memory_space=pltpu.HBM)
_SEM = pl.BlockSpec(memory_space=pltpu.SEMAPHORE)
_EFFECT = pltpu.SideEffectType.DATAFLOW_SIDE_EFFECTING


def _hbm(a):
    return pltpu.with_memory_space_constraint(a, pltpu.HBM)


def _gather_copies(x_refs, land_refs, ssem, rsem):
    x, y, c = _place()
    k = 2 * x + y
    sends, arrivals = [], []
    for a, (xr, lr) in enumerate(zip(x_refs, land_refs)):
        hr = xr.shape[0] // 2
        src = xr.at[pl.ds(c * hr, hr), :]
        for r, (px, py) in enumerate(_other_chips(x, y)):
            i = 3 * a + r
            sends.append(_remote(src, lr.at[k, pl.ds(c * hr, hr), :], ssem.at[i], rsem.at[i], (px, py, c)))
            arrivals.append(_remote(src, lr.at[2 * px + py, pl.ds(c * hr, hr), :], ssem.at[i], rsem.at[i], (px, py, c)))
    return sends, arrivals


def _scatter_copies(x_refs, land_refs, ssem, rsem):
    x, y, c = _place()
    k = 2 * x + y
    sends, arrivals = [], []
    for a, (xr, lr) in enumerate(zip(x_refs, land_refs)):
        for r, (px, py) in enumerate(_other_chips(x, y)):
            i = 3 * a + r
            kj = 2 * px + py
            sends.append(_remote(xr.at[kj], lr.at[k], ssem.at[i], rsem.at[i], (px, py, c)))
            arrivals.append(_remote(xr.at[kj], lr.at[kj], ssem.at[i], rsem.at[i], (px, py, c)))
    return sends, arrivals


def _split_start(name, srcs, land_shapes, copies_fn):
    n = len(srcs)

    def body(*refs):
        sends, _ = copies_fn(refs[:n], refs[n:2 * n], refs[2 * n], refs[2 * n + 1])
        for cp in sends:
            cp.start()
        token = refs[-1]
        token[...] = jnp.zeros_like(token)

    lands = [_hbm(lax.empty(shp, srcs[0].dtype)) for shp in land_shapes]
    outs = pl.pallas_call(
        body, name=name, in_specs=[_HBM] * (2 * n),
        out_specs=(_SEM, _SEM) + (_HBM,) * (2 * n) + (pl.BlockSpec(memory_space=pltpu.VMEM),),
        out_shape=(pltpu.SemaphoreType.DMA((3 * n,)), pltpu.SemaphoreType.DMA((3 * n,)))
        + tuple(pltpu.HBM(s.shape, s.dtype) for s in srcs) + tuple(pltpu.HBM(l.shape, l.dtype) for l in lands)
        + (_sds((8, LANES), F32),),
        input_output_aliases={i: 2 + i for i in range(2 * n)},
        compiler_params=pltpu.CompilerParams(has_side_effects=_EFFECT),
    )(*[_hbm(s) for s in srcs], *lands)
    return outs[0], outs[1], list(outs[2:2 + n]), list(outs[2 + n:2 + 2 * n]), outs[-1]


def _split_wait(name, started, copies_fn, after):
    ssem, rsem, srcs, lands, _ = started
    n = len(srcs)

    def body(*refs):
        sends, arrivals = copies_fn(refs[:n], refs[n:2 * n], refs[2 * n], refs[2 * n + 1])
        for cp in sends:
            cp.wait_send()
        for cp in arrivals:
            cp.wait_recv()

    outs = pl.pallas_call(
        body, name=name, in_specs=[_HBM] * (2 * n) + [_SEM, _SEM, _ANY], out_specs=(_HBM,) * (2 * n),
        out_shape=tuple(pltpu.HBM(t.shape, t.dtype) for t in srcs + lands),
        input_output_aliases={i: i for i in range(2 * n)},
        compiler_params=pltpu.CompilerParams(has_side_effects=_EFFECT),
    )(*srcs, *lands, ssem, rsem, after)
    return list(outs[:n]), list(outs[n:])


def _share_sibling(name, shards, lands):
    n = len(shards)

    def body(*refs):
        x_refs, l_in, l_out = refs[:n], refs[n:2 * n], refs[2 * n:3 * n]
        ssem, rsem, lsem = refs[3 * n:]
        x, y, c = _place()
        k = 2 * x + y
        sib = (x, y, 1 - c)
        copies, locals_ = [], []
        for a in range(n):
            hr = x_refs[a].shape[0] // 2
            loc = pltpu.make_async_copy(x_refs[a], l_out[a].at[k], lsem.at[a])
            loc.start()
            locals_.append(loc)
            for r, (px, py) in enumerate(_other_chips(x, y)):
                blk = l_out[a].at[2 * px + py, pl.ds(c * hr, hr), :]
                cp = _remote(l_in[a].at[2 * px + py, pl.ds(c * hr, hr), :], blk, ssem.at[3 * a + r], rsem.at[3 * a + r], sib)
                cp.start()
                copies.append(cp)
        for a in range(n):
            hr = x_refs[a].shape[0] // 2
            for r, (px, py) in enumerate(_other_chips(x, y)):
                blk = l_out[a].at[2 * px + py, pl.ds((1 - c) * hr, hr), :]
                _remote(blk, blk, ssem.at[3 * a + r], rsem.at[3 * a + r], sib).wait_recv()
        for cp in copies:
            cp.wait_send()
        for loc in locals_:
            loc.wait()

    outs = pl.pallas_call(
        body, name=name, in_specs=[_ANY] * (2 * n), out_specs=[_ANY] * n,
        out_shape=[_sds(l.shape, l.dtype) for l in lands], input_output_aliases={n + i: i for i in range(n)},
        scratch_shapes=[pltpu.SemaphoreType.DMA((3 * n,)), pltpu.SemaphoreType.DMA((3 * n,)),
                        pltpu.SemaphoreType.DMA((n,))],
    )(*shards, *lands)
    return list(outs)


def _swap_halves(name, fulls):
    n = len(fulls)

    def body(*refs):
        g_refs, o_refs = refs[:n], refs[n:2 * n]
        ssem, rsem = refs[2 * n:]
        x, y, c = _place()
        copies = []
        for a in range(n):
            hr = g_refs[a].shape[1] // 2
            cp = _remote(g_refs[a].at[:, pl.ds((1 - c) * hr, hr), :], o_refs[a], ssem.at[a], rsem.at[a], (x, y, 1 - c))
            cp.start()
            copies.append(cp)
        for cp in copies:
            cp.wait()

    outs = pl.pallas_call(
        body, name=name, in_specs=[_ANY] * n, out_specs=[_ANY] * n,
        out_shape=[_sds((f.shape[0], f.shape[1] // 2, f.shape[2]), f.dtype) for f in fulls],
        scratch_shapes=[pltpu.SemaphoreType.DMA((n,)), pltpu.SemaphoreType.DMA((n,))],
    )(*fulls)
    return list(outs)


def _add_halves(name, full, other):
    nb, R, C = full.shape
    hr = R // 2
    tr = _tile(hr, 256, 16)
    nh = hr // tr

    def body(f_ref, o_ref, s_ref):
        s_ref[...] = (f_ref[...].astype(F32) + o_ref[...].astype(F32)).astype(s_ref.dtype)

    return pl.pallas_call(
        body, name=name, grid=(nb, nh),
        in_specs=[pl.BlockSpec((None, tr, C), lambda b, i: (b, lax.axis_index("c") * nh + i, 0)),
                  pl.BlockSpec((None, tr, C), lambda b, i: (b, i, 0))],
        out_specs=pl.BlockSpec((None, tr, C), lambda b, i: (b, i, 0)), out_shape=_sds((nb, hr, C), full.dtype),
        compiler_params=_cp("parallel", "parallel"))(full, other)


def _sum_owner(name, land, mine):
    nb, hr, C = land.shape
    tr = _tile(hr, 128, 16)

    def body(l_ref, m_ref, o_ref):
        k = 2 * lax.axis_index("x") + lax.axis_index("y")
        own = m_ref[...].astype(F32)
        acc = jnp.where(k == 0, own, l_ref[0].astype(F32))
        for j in range(1, nb):
            acc = acc + jnp.where(k == j, own, l_ref[j].astype(F32))
        o_ref[...] = acc

    return pl.pallas_call(
        body, name=name, grid=(hr // tr,),
        in_specs=[pl.BlockSpec((nb, tr, C), lambda i: (0, i, 0)),
                  pl.BlockSpec((None, tr, C), lambda i: (2 * lax.axis_index("x") + lax.axis_index("y"), i, 0))],
        out_specs=pl.BlockSpec((tr, C), _rows), out_shape=_sds((hr, C), F32), compiler_params=_cp("parallel"))(land, mine)


def _join_halves(name, halves):
    n = len(halves)

    def body(*refs):
        p_refs, o_refs = refs[:n], refs[n:2 * n]
        ssem, rsem, lsem = refs[2 * n:]
        x, y, c = _place()
        copies = []
        for a in range(n):
            loc = pltpu.make_async_copy(p_refs[a], o_refs[a].at[c], lsem.at[a])
            loc.start()
            cp = _remote(p_refs[a], o_refs[a].at[c], ssem.at[a], rsem.at[a], (x, y, 1 - c))
            cp.start()
            copies.append((loc, cp))
        for a, (loc, cp) in enumerate(copies):
            _remote(p_refs[a], o_refs[a].at[1 - c], ssem.at[a], rsem.at[a], (x, y, 1 - c)).wait_recv()
            cp.wait_send()
            loc.wait()

    outs = pl.pallas_call(
        body, name=name, in_specs=[_ANY] * n, out_specs=[_ANY] * n,
        out_shape=[_sds((2,) + h.shape, h.dtype) for h in halves],
        scratch_shapes=[pltpu.SemaphoreType.DMA((n,)), pltpu.SemaphoreType.DMA((n,)), pltpu.SemaphoreType.DMA((n,))],
    )(*halves)
    return list(outs)


def _all_gather_devices(name, buf):
    def body(b_ref, o_ref, ssem, rsem, lsem):
        x, y, c = _place()
        me = 4 * x + 2 * y + c
        flips = [(fx, fy, fc) for fx in (0, 1) for fy in (0, 1) for fc in (0, 1) if fx or fy or fc]
        peers = [((1 - x) if fx else x, (1 - y) if fy else y, (1 - c) if fc else c) for fx, fy, fc in flips]
        local = pltpu.make_async_copy(b_ref, o_ref.at[me], lsem)
        local.start()
        sends = [_remote(b_ref, o_ref.at[me], ssem.at[r], rsem.at[r], p) for r, p in enumerate(peers)]
        for cp in sends:
            cp.start()
        for r, (px, py, pc) in enumerate(peers):
            pid = 4 * px + 2 * py + pc
            _remote(b_ref, o_ref.at[pid], ssem.at[r], rsem.at[r], (px, py, pc)).wait_recv()
        for cp in sends:
            cp.wait_send()
        local.wait()

    return pl.pallas_call(
        body, name=name, in_specs=[_ANY], out_specs=_ANY, out_shape=_sds((8,) + buf.shape, buf.dtype),
        scratch_shapes=[pltpu.SemaphoreType.DMA((7,)), pltpu.SemaphoreType.DMA((7,)), pltpu.SemaphoreType.DMA],
    )(buf)


def _sum_blocks(name, parts):
    n, R, C = parts.shape
    tr = _tile(R, 256, 8)

    def body(p_ref, o_ref):
        acc = p_ref[0].astype(F32)
        for j in range(1, n):
            acc = acc + p_ref[j].astype(F32)
        o_ref[...] = acc

    return pl.pallas_call(
        body, name=name, grid=(R // tr,), in_specs=[pl.BlockSpec((n, tr, C), lambda i: (0, i, 0))],
        out_specs=pl.BlockSpec((tr, C), _rows), out_shape=_sds((R, C), F32), compiler_params=_cp("parallel"))(parts)


def _adamw(name, w, m, v, parts):
    L, R, C = w.shape
    np_ = len(parts[0])
    tr = _tile(R, max(8, (VMEM_LIMIT // 2) // (2 * 4 * C * (7 + L * np_))), 8)
    flat = [p for lp in parts for p in lp]
    c1 = 1.0 / (1.0 - ADAM_B1 ** ADAM_STEP)
    c2 = 1.0 / (1.0 - ADAM_B2 ** ADAM_STEP)

    def body(*refs):
        w_ref, m_ref, v_ref = refs[:3]
        p_refs = refs[3:3 + L * np_]
        g_ref, d_ref, nm_ref, nv_ref = refs[3 + L * np_:]
        layer = pl.program_id(0)
        g = jnp.zeros((tr, C), F32)
        for l in range(L):
            gl = p_refs[l * np_][...]
            for j in range(1, np_):
                gl = gl + p_refs[l * np_ + j][...]
            g = jnp.where(layer == l, gl, g) if L > 1 else gl
        nm = ADAM_B1 * m_ref[...] + (1.0 - ADAM_B1) * g
        nv = ADAM_B2 * v_ref[...] + (1.0 - ADAM_B2) * (g * g)
        g_ref[...] = g
        nm_ref[...] = nm
        nv_ref[...] = nv
        d_ref[...] = -ADAM_LR * ((nm * c1) / (jnp.sqrt(nv * c2) + ADAM_EPS) + ADAM_WD * w_ref[...])

    stacked = pl.BlockSpec((None, tr, C), lambda l, i: (l, i, 0))
    part_specs = [pl.BlockSpec((tr, C), (lambda l, i, ll=ll: (jnp.where(l == ll, i, 0), 0)))
                  for ll in range(L) for _ in range(np_)]
    return pl.pallas_call(
        body, name=name, grid=(L, R // tr), in_specs=[stacked] * 3 + part_specs, out_specs=[stacked] * 4,
        out_shape=[_sds(w.shape, F32)] * 4, compiler_params=_cp("arbitrary", "arbitrary"))(w, m, v, *flat)


_PACK_ROWS = 16


def _pack(arrs):
    pieces = []
    for a in arrs:
        f = a.reshape(-1).astype(F32)
        unit = _PACK_ROWS * LANES
        pad = (-f.shape[0]) % unit
        pieces.append(jnp.pad(f, (0, pad)))
    return jnp.concatenate(pieces).reshape(-1, LANES)


def _unpack(buf, shapes):
    flat = buf.reshape(-1)
    out, off = [], 0
    unit = _PACK_ROWS * LANES
    for s in shapes:
        n = 1
        for d in s:
            n *= d
        out.append(flat[off:off + n].reshape(s))
        off += n + ((-n) % unit)
    return out


def _reduce_start(tag, fulls):
    others = _swap_halves("rs_swap", fulls)
    sums = [_add_halves("rs_add", f, o) for f, o in zip(fulls, others)]
    return _split_start("rs_start_" + tag, sums, [t.shape for t in sums], _scatter_copies)


def _reduce_finish(tag, started, after):
    sums, lands = _split_wait("rs_wait_" + tag, started, _scatter_copies, after)
    halves = [_sum_owner("rs_sum", l, s) for l, s in zip(lands, sums)]
    return [j.reshape(2 * j.shape[1], j.shape[2]) for j in _join_halves("rs_join", halves)]


def kernel(x, ln_ffn_pre, ffn_pre_w_in, ffn_pre_w_out, ln_mix, ln_ffn_post, ffn_post_w_in, ffn_post_w_out, gm_w_in, gm_v_norm, gm_w_s, gm_b_s, gm_w_out, ssm_w_in, ssm_conv_w, ssm_conv_b, ssm_dt_bias, ssm_a_log, ssm_d, ssm_norm, ssm_w_out, ln_final, loss_target, m_ln_ffn_pre, m_ffn_pre_w_in, m_ffn_pre_w_out, m_ln_mix, m_ln_ffn_post, m_ffn_post_w_in, m_ffn_post_w_out, m_gm_w_in, m_gm_v_norm, m_gm_w_s, m_gm_b_s, m_gm_w_out, m_ssm_w_in, m_ssm_conv_w, m_ssm_conv_b, m_ssm_dt_bias, m_ssm_a_log, m_ssm_d, m_ssm_norm, m_ssm_w_out, m_ln_final, v_ln_ffn_pre, v_ffn_pre_w_in, v_ffn_pre_w_out, v_ln_mix, v_ln_ffn_post, v_ffn_post_w_in, v_ffn_post_w_out, v_gm_w_in, v_gm_v_norm, v_gm_w_s, v_gm_b_s, v_gm_w_out, v_ssm_w_in, v_ssm_conv_w, v_ssm_conv_b, v_ssm_dt_bias, v_ssm_a_log, v_ssm_d, v_ssm_norm, v_ssm_w_out, v_ln_final):
    T, D = x.shape[1], x.shape[2]
    depth = ln_ffn_pre.shape[0]
    n_gm, n_ssm = gm_w_in.shape[0], ssm_w_in.shape[0]
    F = ffn_pre_w_out.shape[1] * 4
    GI = gm_w_out.shape[1] * 4
    GG = gm_w_s.shape[1]
    inner = ssm_w_out.shape[1] * 4
    heads = ssm_dt_bias.shape[1]
    cd = ssm_conv_w.shape[1] * 4
    groups = (cd - inner) // (2 * SSM_STATE)
    pshard = ssm_w_in.shape[2]
    pw = inner + cd + LANES
    dtcol = (inner + cd) // LANES
    kchip = 2 * lax.axis_index("x") + lax.axis_index("y")

    small_shard = _pack([jnp.swapaxes(ssm_conv_w, 1, 2), ssm_conv_b, ssm_norm])
    small_all = _all_gather_chips("ag_small", small_shard)
    cq, iq = cd // 4, inner // 4
    parts = [_unpack(small_all[k], [(n_ssm, SSM_CONV, cq), (n_ssm, cq), (n_ssm, iq)]) for k in range(4)]
    conv_wt = jnp.concatenate([p[0] for p in parts], axis=2)
    conv_b = jnp.concatenate([p[1] for p in parts], axis=1)
    norm_g = jnp.concatenate([p[2] for p in parts], axis=1)

    def pad_lanes(v):
        return jnp.pad(v, (0, LANES - v.shape[0]))[None, :]

    xc = x[0]
    saved = []

    def shards_of(i, kind):
        if kind == "pre":
            return [ffn_pre_w_in[i], ffn_pre_w_out[i]]
        if kind == "post":
            return [ffn_post_w_in[i], ffn_post_w_out[i]]
        if i % 2 == 0:
            return [gm_w_in[i // 2], gm_w_out[i // 2]]
        return [ssm_w_in[i // 2], ssm_w_out[i // 2]]

    subs = [(i, kind) for i in range(depth) for kind in ("pre", "mix", "post")]
    nsub = len(subs)
    gathers = [None] * nsub
    ahead = 2

    def gather_start(s):
        shards = [w.astype(BF16) for w in shards_of(*subs[s])]
        gathers[s] = _split_start("ag_start_%d" % s, shards, [(4,) + t.shape for t in shards], _gather_copies)

    def gathered(s, after):
        shards, lands = _split_wait("ag_wait_%d" % s, gathers[s], _gather_copies, after)
        full = _share_sibling("ag_share", shards, lands)
        if s + ahead < nsub:
            gather_start(s + ahead)
            return full, gathers[s + ahead][4][0:1, 0:1]
        return full, jnp.zeros((1, 1), F32)

    for s in range(min(ahead, nsub)):
        gather_start(s)

    def ffn_fwd(s, xin, g, l):
        (wi, wo), tok = gathered(s, xin)
        wo = wo.reshape(F, D)
        h = _rms_fwd("rms_fwd", xin, g[l][None, :] + tok)
        a = _matmul("ffn_in", h, wi, "nn", BF16, out_blocks=2)
        s_ = _swiglu_fwd("swiglu_fwd", a)
        xo = _matmul("ffn_out", s_, wo, "nn", F32, res=xin, scale=0.5)
        return xo, (xin, h, a, s_, wi, wo)

    for i in range(depth):
        xc, sv_pre = ffn_fwd(3 * i, xc, ln_ffn_pre, i)
        j = i // 2
        (wi, wo), tok = gathered(3 * i + 1, xc)
        h = _rms_fwd("rms_fwd", xc, ln_mix[i][None, :] + tok)
        if i % 2 == 0:
            wo = wo.reshape(GI, D)
            zpre = _matmul("gm_in", h, wi, "nn", BF16, out_blocks=2)
            u, vn = _gm_act_fwd("gm_act_fwd", zpre, gm_v_norm[j][None, :])
            bt = jnp.pad(gm_b_s[j].T, ((0, 0), (0, LANES - GG)))
            gated = _gm_spatial_fwd("gm_spatial_fwd", u, vn, gm_w_s[j], bt)
            xn = _matmul("mix_out", gated, wo, "nn", F32, res=xc, scale=1.0)
            sv_mix = (xc, h, zpre, u, vn, bt, gated, wi, wo)
        else:
            wg, wo = wi, wo.reshape(inner, D)
            wp = jnp.pad(jnp.swapaxes(wg, 0, 1).reshape(D, 4 * pshard), ((0, 0), (0, pw - 4 * pshard)))
            proj = _matmul("ssm_in", h, wp, "nn", F32)
            wt, cb_ = conv_wt[j], conv_b[j][None, :]
            act = _ssd_conv_fwd("ssd_conv_fwd", proj, wt, cb_, inner, cd)
            bias, alog = pad_lanes(ssm_dt_bias[j]), pad_lanes(ssm_a_log[j])
            dtp, acum = _ssd_dt_fwd("ssd_dt_fwd", proj, bias, alog, dtcol, heads)
            dexp = jnp.repeat(ssm_d[j], SSM_HEAD_DIM)[None, :]
            ycore, sprev = _ssd_core_fwd("ssd_core_fwd", act, dtp, acum, dexp, inner, groups)
            ng = norm_g[j][None, :]
            yn = _ssd_post_fwd("ssd_post_fwd", ycore, proj, ng, inner, groups)
            xn = _matmul("mix_out", yn, wo, "nn", F32, res=xc, scale=1.0)
            sv_mix = (xc, h, proj, act, dtp, acum, dexp, ycore, sprev, yn, wt, cb_, bias, alog, ng, wp, wo)
        xc = xn
        xc, sv_post = ffn_fwd(3 * i + 2, xc, ln_ffn_post, i)
        saved.append((sv_pre, sv_mix, sv_post))

    loss_tile, dx, dxb, dg_final = _loss_head("loss_head", xc, ln_final[None, :], loss_target[0])
    loss = lax.psum(loss_tile[0, 0], ("x", "y", "c"))

    wgrad = [None] * nsub
    in_flight = []

    def reduce_start(s, fulls):
        st = _reduce_start(str(s), fulls)
        in_flight.append((s, st))
        return st[4][0:1, 0:1]

    def reduce_finish_older(after, keep=1):
        while len(in_flight) > keep:
            s, st = in_flight.pop(0)
            wgrad[s] = _reduce_finish(str(s), st, after)

    def ffn_bwd(s, dx, dxb, sv, g, l):
        xin, h, a, s_, wi, wo = sv
        ds = _matmul("ffn_ds", dxb, wo, "nt", BF16, scale=0.5)
        dwo = _matmul("ffn_dwo", s_, dxb, "tn", BF16, scale=0.5)
        da = _swiglu_bwd("swiglu_bwd", a, ds)
        dh = _matmul("ffn_dh", da, wi, "nt", F32)
        dwi = _matmul("ffn_dwi", h, da, "tn", BF16, out_blocks=4)
        tok = reduce_start(s, [dwi, dwo.reshape(4, F // 4, D)])
        dx2, dxb2, dg = _rms_bwd("rms_bwd", xin, g[l][None, :] + tok, dh, dx)
        reduce_finish_older(dx2)
        return dx2, dxb2, dg

    gl = {n: [None] * depth for n in ("pre", "mix", "post")}
    gm_g = {n: [None] * n_gm for n in ("vnorm", "ws", "bs")}
    ssm_g = {n: [None] * n_ssm for n in ("convw", "convb", "dtb", "alog", "d", "norm")}

    for i in reversed(range(depth)):
        sv_pre, sv_mix, sv_post = saved[i]
        j = i // 2
        dx, dxb, gl["post"][i] = ffn_bwd(3 * i + 2, dx, dxb, sv_post, ln_ffn_post, i)
        if i % 2 == 0:
            xin, h, zpre, u, vn, bt, gated, wi, wo = sv_mix
            dgated = _matmul("mix_dy", dxb, wo, "nt", BF16)
            dwo = _matmul("mix_dwo", gated, dxb, "tn", BF16)
            du, dvn, dws, dbt = _gm_spatial_bwd("gm_spatial_bwd", dgated, u, vn, gm_w_s[j], bt)
            dzpre, dvnorm = _gm_act_bwd("gm_act_bwd", zpre, du, dvn, gm_v_norm[j][None, :])
            dh = _matmul("gm_dh", dzpre, wi, "nt", F32)
            dwi = _matmul("gm_dwi", h, dzpre, "tn", BF16, out_blocks=4)
            tok = reduce_start(3 * i + 1, [dwi, dwo.reshape(4, GI // 4, D)])
            gm_g["vnorm"][j], gm_g["ws"][j], gm_g["bs"][j] = dvnorm[0], dws, dbt.T[:GG]
        else:
            xin, h, proj, act, dtp, acum, dexp, ycore, sprev, yn, wt, cb_, bias, alog, ng, wp, wo = sv_mix
            dyn = _matmul("ssm_dy", dxb, wo, "nt", F32)
            dwo = _matmul("mix_dwo", yn, dxb, "tn", BF16)
            dyc, dz, dnorm = _ssd_post_bwd("ssd_post_bwd", dyn, ycore, proj, ng, inner, groups)
            dxs, db_, dc_, ddtp_g, dac_g, dd_g = _ssd_core_bwd("ssd_core_bwd", dyc, act, dtp, acum, dexp, sprev,
                                                               inner, groups)
            ddt, dbias, dalog, dds = _ssd_dt_bwd("ssd_dt_bwd", ddtp_g, dac_g, dd_g, proj, bias, alog, dtp, dtcol, heads)
            dact = jnp.concatenate([dxs, db_, dc_], axis=1)
            dxbc, dwt, dcb = _ssd_conv_bwd("ssd_conv_bwd", dact, proj, wt, cb_, inner, cd)
            dproj = jnp.concatenate([dz, dxbc, ddt], axis=1)
            dh = _matmul("ssm_dh", dproj, wp, "nt", F32)
            dwp = _matmul("ssm_dwi", h, dproj, "tn", BF16)
            dwi = jnp.swapaxes(dwp[:, :4 * pshard].reshape(D, 4, pshard), 0, 1)
            tok = reduce_start(3 * i + 1, [dwi, dwo.reshape(4, inner // 4, D)])
            ssm_g["convw"][j], ssm_g["convb"][j] = dwt.T, dcb[0]
            ssm_g["dtb"][j], ssm_g["alog"][j], ssm_g["d"][j] = dbias[0, :heads], dalog[0, :heads], dds[0, :heads]
            ssm_g["norm"][j] = dnorm[0]
        dx, dxb, gl["mix"][i] = _rms_bwd("rms_bwd", xin, ln_mix[i][None, :] + tok, dh, dx)
        reduce_finish_older(dx)
        dx, dxb, gl["pre"][i] = ffn_bwd(3 * i, dx, dxb, sv_pre, ln_ffn_pre, i)

    small_full = [
        jnp.concatenate(gl["pre"], 0), jnp.concatenate(gl["mix"], 0), jnp.concatenate(gl["post"], 0),
        jnp.stack(gm_g["vnorm"]), jnp.stack(gm_g["ws"]), jnp.stack(gm_g["bs"]),
        jnp.stack(ssm_g["convw"]), jnp.stack(ssm_g["convb"]), jnp.stack(ssm_g["dtb"]), jnp.stack(ssm_g["alog"]),
        jnp.stack(ssm_g["d"]), jnp.stack(ssm_g["norm"]), dg_final[0],
    ]
    full_shapes = [tuple(a.shape) for a in small_full]
    gathered = _all_gather_devices("ag8_small", _pack(small_full))
    summed = _sum_blocks("sum8_small", gathered)
    (g_pre, g_mix, g_post, g_vn, g_ws, g_bs, g_cw, g_cb, g_dtb, g_al, g_d, g_nm, g_fin) = _unpack(summed, full_shapes)
    g_cw = lax.dynamic_slice_in_dim(g_cw, kchip * cq, cq, axis=1)
    g_cb = lax.dynamic_slice_in_dim(g_cb, kchip * cq, cq, axis=1)
    g_nm = lax.dynamic_slice_in_dim(g_nm, kchip * iq, iq, axis=1)
    sm_g = [g_pre, g_mix, g_post, g_vn, g_ws, g_bs, g_cw, g_cb, g_dtb, g_al, g_d, g_nm, g_fin]
    sm_w = [ln_ffn_pre, ln_mix, ln_ffn_post, gm_v_norm, gm_w_s, gm_b_s, ssm_conv_w, ssm_conv_b, ssm_dt_bias,
            ssm_a_log, ssm_d, ssm_norm, ln_final]
    sm_m = [m_ln_ffn_pre, m_ln_mix, m_ln_ffn_post, m_gm_v_norm, m_gm_w_s, m_gm_b_s, m_ssm_conv_w, m_ssm_conv_b,
            m_ssm_dt_bias, m_ssm_a_log, m_ssm_d, m_ssm_norm, m_ln_final]
    sm_v = [v_ln_ffn_pre, v_ln_mix, v_ln_ffn_post, v_gm_v_norm, v_gm_w_s, v_gm_b_s, v_ssm_conv_w, v_ssm_conv_b,
            v_ssm_dt_bias, v_ssm_a_log, v_ssm_d, v_ssm_norm, v_ln_final]
    sm_shapes = [tuple(a.shape) for a in sm_w]
    pk = [_pack(lst)[None] for lst in (sm_w, sm_m, sm_v)]
    sg_, sd_, snm_, snv_ = _adamw("adamw_small", pk[0], pk[1], pk[2], [[_pack(sm_g)]])
    small_out = [_unpack(t[0], sm_shapes) for t in (sg_, sd_, snm_, snv_)]
    small_names = ["ln_ffn_pre", "ln_mix", "ln_ffn_post", "gm_v_norm", "gm_w_s", "gm_b_s", "ssm_conv_w", "ssm_conv_b",
                   "ssm_dt_bias", "ssm_a_log", "ssm_d", "ssm_norm", "ln_final"]

    def big_update(tag, w, m, v, sub_ids, which):
        L = w.shape[0]
        shp = w.shape
        w2, m2, v2 = (t.reshape(L, -1, shp[-1]) for t in (w, m, v))
        outs = _adamw("adamw_" + tag, w2, m2, v2, [[wgrad[s][which]] for s in sub_ids])
        return [o.reshape(shp) for o in outs]

    pre_ids = [3 * i for i in range(depth)]
    post_ids = [3 * i + 2 for i in range(depth)]
    gm_ids = [3 * i + 1 for i in range(depth) if i % 2 == 0]
    ssm_ids = [3 * i + 1 for i in range(depth) if i % 2 == 1]
    big_out = {
        "ffn_post_w_in": big_update("ffn_in", ffn_post_w_in, m_ffn_post_w_in, v_ffn_post_w_in, post_ids, 0),
        "ffn_post_w_out": big_update("ffn_out", ffn_post_w_out, m_ffn_post_w_out, v_ffn_post_w_out, post_ids, 1),
        "gm_w_in": big_update("gm_in", gm_w_in, m_gm_w_in, v_gm_w_in, gm_ids, 0),
        "gm_w_out": big_update("mix_out", gm_w_out, m_gm_w_out, v_gm_w_out, gm_ids, 1),
        "ssm_w_in": big_update("ssm_in", ssm_w_in, m_ssm_w_in, v_ssm_w_in, ssm_ids, 0),
        "ssm_w_out": big_update("mix_out", ssm_w_out, m_ssm_w_out, v_ssm_w_out, ssm_ids, 1),
    }
    reduce_finish_older(big_out["ssm_w_out"][1], keep=0)
    big_out["ffn_pre_w_in"] = big_update("ffn_in", ffn_pre_w_in, m_ffn_pre_w_in, v_ffn_pre_w_in, pre_ids, 0)
    big_out["ffn_pre_w_out"] = big_update("ffn_out", ffn_pre_w_out, m_ffn_pre_w_out, v_ffn_pre_w_out, pre_ids, 1)

    order = ["ln_ffn_pre", "ffn_pre_w_in", "ffn_pre_w_out", "ln_mix", "ln_ffn_post", "ffn_post_w_in", "ffn_post_w_out",
             "gm_w_in", "gm_v_norm", "gm_w_s", "gm_b_s", "gm_w_out", "ssm_w_in", "ssm_conv_w", "ssm_conv_b",
             "ssm_dt_bias", "ssm_a_log", "ssm_d", "ssm_norm", "ssm_w_out", "ln_final"]

    def pick(kind, n):
        if n in big_out:
            return big_out[n][kind]
        return small_out[kind][small_names.index(n)]

    outs = [loss, dx[None]]
    for kind in range(4):
        outs.extend(pick(kind, n) for n in order)
    return tuple(outs)
```

```python
import jax
import jax.numpy as jnp
from jax import lax
from jax.experimental import pallas as pl
from jax.experimental.pallas import tpu as pltpu

F32 = jnp.float32
BF16 = jnp.bfloat16
HIGHEST = lax.Precision.HIGHEST
MESH = pl.DeviceIdType.MESH

EPS = 1e-6
ADAM_LR, ADAM_B1, ADAM_B2, ADAM_EPS, ADAM_WD, ADAM_STEP = 0.001, 0.9, 0.999, 1e-08, 0.01, 10

LANES = 128
CHUNK = 128
SSM_STATE = 128
SSM_HEAD_DIM = 64
SSM_HPG = 8
SSM_CONV = 4
VMEM_LIMIT = 56 * 1024 * 1024
MM_TILE = 1408

_ANY = pl.BlockSpec(memory_space=pl.ANY)


def _cp(*sem):
    return pltpu.CompilerParams(dimension_semantics=sem if sem else None, vmem_limit_bytes=VMEM_LIMIT)


def _tile(n, target, mult=LANES):
    best = None
    t = mult
    while t <= min(n, target):
        if n % t == 0:
            best = t
        t += mult
    return n if best is None else best


def _gcd(*v):
    import math
    g = 0
    for a in v:
        g = math.gcd(g, a)
    return g


def _sds(shape, dtype):
    return jax.ShapeDtypeStruct(tuple(shape), dtype)


def _ldims(shape):
    return tuple(shape) if len(shape) == 2 else (shape[1], shape[0] * shape[2])


def _colblock(shape):
    return None if len(shape) == 2 else shape[2]


def _mspec(shape, tr, tc, rc):
    if len(shape) == 2:
        return pl.BlockSpec((tr, tc), rc)
    per = shape[2] // tc

    def im(i, j, k):
        r, c = rc(i, j, k)
        return (c // per, r, c % per)

    return pl.BlockSpec((None, tr, tc), im)


def _matmul(name, a, b, mode, out_dtype, out_blocks=None, res=None, scale=1.0):
    la, lb = _ldims(a.shape), _ldims(b.shape)
    if mode == "nn":
        (M, K), (K2, N) = la, lb
    elif mode == "nt":
        (M, K), (N, K2) = la, lb
    else:
        (K, M), (K2, N) = la, lb
    assert K == K2, (name, a.shape, b.shape, mode)
    ca, cb = _colblock(a.shape), _colblock(b.shape)
    out_shape = (M, N) if out_blocks is None else (out_blocks, M, N // out_blocks)
    co = _colblock(out_shape)
    m_c, n_c, k_c = [M], [N], [K]
    if ca is not None:
        (m_c if mode == "tn" else k_c).append(ca)
    if cb is not None:
        (k_c if mode == "nt" else n_c).append(cb)
    if co is not None:
        n_c.append(co)
    tm, tn, tk = _tile(_gcd(*m_c), MM_TILE), _tile(_gcd(*n_c), MM_TILE), _tile(_gcd(*k_c), MM_TILE)
    nk = K // tk
    if mode == "tn":
        a_spec = _mspec(a.shape, tk, tm, lambda i, j, k: (k, i))
        dims = (((0,), (0,)), ((), ()))
    else:
        a_spec = _mspec(a.shape, tm, tk, lambda i, j, k: (i, k))
        dims = (((1,), (1,)), ((), ())) if mode == "nt" else (((1,), (0,)), ((), ()))
    if mode == "nt":
        b_spec = _mspec(b.shape, tn, tk, lambda i, j, k: (j, k))
    else:
        b_spec = _mspec(b.shape, tk, tn, lambda i, j, k: (k, j))
    o_spec = _mspec(out_shape, tm, tn, lambda i, j, k: (i, j))
    in_specs, args = [a_spec, b_spec], [a, b]
    if res is not None:
        in_specs.append(pl.BlockSpec((tm, tn), lambda i, j, k: (i, j)))
        args.append(res)

    def body(*refs):
        a_ref, b_ref = refs[0], refs[1]
        res_ref = refs[2] if res is not None else None
        o_ref, acc_ref = refs[-2], refs[-1]
        kk = pl.program_id(2)

        @pl.when(kk == 0)
        def _():
            acc_ref[...] = jnp.zeros_like(acc_ref)

        acc_ref[...] += lax.dot_general(a_ref[...].astype(BF16), b_ref[...].astype(BF16), dims,
                                        preferred_element_type=F32)

        @pl.when(kk == nk - 1)
        def _():
            r = acc_ref[...]
            if scale != 1.0:
                r = r * scale
            if res_ref is not None:
                r = res_ref[...] + r
            o_ref[...] = r.astype(o_ref.dtype)

    return pl.pallas_call(
        body, name=name, grid=(M // tm, N // tn, nk), in_specs=in_specs, out_specs=o_spec,
        out_shape=_sds(out_shape, out_dtype), scratch_shapes=[pltpu.VMEM((tm, tn), F32)],
        compiler_params=_cp("parallel", "parallel", "arbitrary"))(*args)


def _rows(i):
    return (i, 0)


def _row0(i):
    return (0, 0)


def _rms_fwd(name, x, g):
    T, D = x.shape
    tr = _tile(T, 256, 8)

    def body(x_ref, g_ref, o_ref):
        xv = x_ref[...]
        r = lax.rsqrt(jnp.mean(xv * xv, axis=-1, keepdims=True) + EPS)
        o_ref[...] = (xv * r * g_ref[...]).astype(o_ref.dtype)

    return pl.pallas_call(
        body, name=name, grid=(T // tr,),
        in_specs=[pl.BlockSpec((tr, D), _rows), pl.BlockSpec((1, D), _row0)],
        out_specs=pl.BlockSpec((tr, D), _rows), out_shape=_sds((T, D), BF16),
        compiler_params=_cp("parallel"))(x, g)


def _rms_bwd(name, x, g, dh, dres):
    T, D = x.shape
    tr = _tile(T, 256, 8)

    def body(x_ref, g_ref, dh_ref, dres_ref, dx_ref, dxb_ref, dg_ref):
        xv = x_ref[...]
        r = lax.rsqrt(jnp.mean(xv * xv, axis=-1, keepdims=True) + EPS)
        xh = xv * r
        dhv = dh_ref[...]
        dxh = dhv * g_ref[...]
        dx = dres_ref[...] + r * (dxh - xh * jnp.mean(dxh * xh, axis=-1, keepdims=True))
        dx_ref[...] = dx
        dxb_ref[...] = dx.astype(BF16)

        @pl.when(pl.program_id(0) == 0)
        def _():
            dg_ref[...] = jnp.zeros_like(dg_ref)

        dg_ref[...] += jnp.sum(dhv * xh, axis=0, keepdims=True)

    return pl.pallas_call(
        body, name=name, grid=(T // tr,),
        in_specs=[pl.BlockSpec((tr, D), _rows), pl.BlockSpec((1, D), _row0),
                  pl.BlockSpec((tr, D), _rows), pl.BlockSpec((tr, D), _rows)],
        out_specs=[pl.BlockSpec((tr, D), _rows), pl.BlockSpec((tr, D), _rows), pl.BlockSpec((1, D), _row0)],
        out_shape=[_sds((T, D), F32), _sds((T, D), BF16), _sds((1, D), F32)],
        compiler_params=_cp("arbitrary"))(x, g, dh, dres)


def _sigmoid(v):
    return 1.0 / (1.0 + jnp.exp(-v))


def _swiglu_fwd(name, a):
    _, T, F = a.shape
    tr, tc = _tile(T, 512, 8), _tile(F, MM_TILE)

    def body(a_ref, o_ref):
        gate = a_ref[0].astype(F32)
        up = a_ref[1].astype(F32)
        o_ref[...] = (gate * _sigmoid(gate) * up).astype(o_ref.dtype)

    return pl.pallas_call(
        body, name=name, grid=(T // tr, F // tc),
        in_specs=[pl.BlockSpec((2, tr, tc), lambda i, j: (0, i, j))],
        out_specs=pl.BlockSpec((tr, tc), lambda i, j: (i, j)), out_shape=_sds((T, F), BF16),
        compiler_params=_cp("parallel", "parallel"))(a)


def _swiglu_bwd(name, a, ds):
    _, T, F = a.shape
    tr, tc = _tile(T, 512, 8), _tile(F, MM_TILE)

    def body(a_ref, ds_ref, o_ref):
        gate = a_ref[0].astype(F32)
        up = a_ref[1].astype(F32)
        dsv = ds_ref[...].astype(F32)
        sg = _sigmoid(gate)
        o_ref[0] = (dsv * up * sg * (1.0 + gate * (1.0 - sg))).astype(o_ref.dtype)
        o_ref[1] = (dsv * gate * sg).astype(o_ref.dtype)

    return pl.pallas_call(
        body, name=name, grid=(T // tr, F // tc),
        in_specs=[pl.BlockSpec((2, tr, tc), lambda i, j: (0, i, j)), pl.BlockSpec((tr, tc), lambda i, j: (i, j))],
        out_specs=pl.BlockSpec((2, tr, tc), lambda i, j: (0, i, j)), out_shape=_sds((2, T, F), BF16),
        compiler_params=_cp("parallel", "parallel"))(a, ds)


def _loss_head(name, x, g, tgt):
    T, D = x.shape
    tr = _tile(T, 256, 8)

    def body(x_ref, g_ref, t_ref, loss_ref, dx_ref, dxb_ref, dg_ref):
        xv = x_ref[...]
        gv = g_ref[...]
        r = lax.rsqrt(jnp.mean(xv * xv, axis=-1, keepdims=True) + EPS)
        xh = xv * r
        err = xh * gv - t_ref[...]
        dy = err * (1.0 / D)
        dxh = dy * gv
        dx = r * (dxh - xh * jnp.mean(dxh * xh, axis=-1, keepdims=True))
        dx_ref[...] = dx
        dxb_ref[...] = dx.astype(BF16)

        @pl.when(pl.program_id(0) == 0)
        def _():
            dg_ref[...] = jnp.zeros_like(dg_ref)
            loss_ref[...] = jnp.zeros_like(loss_ref)

        dg_ref[...] += jnp.sum(dy * xh, axis=0, keepdims=True)
        part = jnp.sum(jnp.sum(err * err, axis=-1, keepdims=True), axis=0, keepdims=True) * (0.5 / D)
        loss_ref[...] += part

    return pl.pallas_call(
        body, name=name, grid=(T // tr,),
        in_specs=[pl.BlockSpec((tr, D), _rows), pl.BlockSpec((1, D), _row0), pl.BlockSpec((tr, D), _rows)],
        out_specs=[pl.BlockSpec((8, LANES), _row0), pl.BlockSpec((tr, D), _rows), pl.BlockSpec((tr, D), _rows),
                   pl.BlockSpec((1, D), _row0)],
        out_shape=[_sds((8, LANES), F32), _sds((T, D), F32), _sds((T, D), BF16), _sds((1, D), F32)],
        compiler_params=_cp("arbitrary"))(x, g, tgt)


_SQRT_HALF = 0.7071067811865476
_INV_SQRT_2PI = 0.3989422804014327


def _gelu(v):
    return 0.5 * v * (1.0 + lax.erf(v * _SQRT_HALF))


def _gelu_grad(v):
    return 0.5 * (1.0 + lax.erf(v * _SQRT_HALF)) + v * _INV_SQRT_2PI * jnp.exp(-0.5 * v * v)


def _group_expand(rows, width, gd):
    gi = lax.broadcasted_iota(jnp.int32, (rows, width), 0)
    fi = lax.broadcasted_iota(jnp.int32, (rows, width), 1)
    return ((fi >= gi * gd) & (fi < (gi + 1) * gd)).astype(F32)


def _gm_act_fwd(name, zpre, vnorm):
    _, T, GI = zpre.shape
    tr = _tile(T, 128, 8)

    def body(z_ref, g_ref, u_ref, v_ref):
        u_ref[...] = _gelu(z_ref[0].astype(F32)).astype(BF16)
        zv = _gelu(z_ref[1].astype(F32))
        r = lax.rsqrt(jnp.mean(zv * zv, axis=-1, keepdims=True) + EPS)
        v_ref[...] = (zv * r * g_ref[...]).astype(BF16)

    return pl.pallas_call(
        body, name=name, grid=(T // tr,),
        in_specs=[pl.BlockSpec((2, tr, GI), lambda i: (0, i, 0)), pl.BlockSpec((1, GI), _row0)],
        out_specs=[pl.BlockSpec((tr, GI), _rows), pl.BlockSpec((tr, GI), _rows)],
        out_shape=[_sds((T, GI), BF16), _sds((T, GI), BF16)], compiler_params=_cp("parallel"))(zpre, vnorm)


def _gm_act_bwd(name, zpre, du, dvn, vnorm):
    _, T, GI = zpre.shape
    tr = _tile(T, 128, 8)

    def body(z_ref, du_ref, dvn_ref, g_ref, dz_ref, dg_ref):
        xu = z_ref[0].astype(F32)
        xv = z_ref[1].astype(F32)
        zv = _gelu(xv)
        r = lax.rsqrt(jnp.mean(zv * zv, axis=-1, keepdims=True) + EPS)
        xh = zv * r
        dv = dvn_ref[...]

        @pl.when(pl.program_id(0) == 0)
        def _():
            dg_ref[...] = jnp.zeros_like(dg_ref)

        dg_ref[...] += jnp.sum(dv * xh, axis=0, keepdims=True)
        dxh = dv * g_ref[...]
        dzv = r * (dxh - xh * jnp.mean(dxh * xh, axis=-1, keepdims=True))
        dz_ref[0] = (du_ref[...] * _gelu_grad(xu)).astype(BF16)
        dz_ref[1] = (dzv * _gelu_grad(xv)).astype(BF16)

    return pl.pallas_call(
        body, name=name, grid=(T // tr,),
        in_specs=[pl.BlockSpec((2, tr, GI), lambda i: (0, i, 0)), pl.BlockSpec((tr, GI), _rows),
                  pl.BlockSpec((tr, GI), _rows), pl.BlockSpec((1, GI), _row0)],
        out_specs=[pl.BlockSpec((2, tr, GI), lambda i: (0, i, 0)), pl.BlockSpec((1, GI), _row0)],
        out_shape=[_sds((2, T, GI), BF16), _sds((1, GI), F32)], compiler_params=_cp("arbitrary"))(zpre, du, dvn, vnorm)


def _causal_mask():
    r = lax.broadcasted_iota(jnp.int32, (CHUNK, CHUNK), 0)
    c = lax.broadcasted_iota(jnp.int32, (CHUNK, CHUNK), 1)
    return r >= c


def _gm_spatial_fwd(name, u, vn, ws, bt):
    T, GI = u.shape
    G = ws.shape[0]
    gd = GI // G

    def body(u_ref, v_ref, ws_ref, bt_ref, o_ref, bias_scr):
        @pl.when(pl.program_id(0) == 0)
        def _():
            bias_scr[...] = jnp.dot(bt_ref[...], _group_expand(LANES, GI, gd), precision=HIGHEST,
                                    preferred_element_type=F32)

        causal = _causal_mask()
        for g in range(G):
            sl = slice(g * gd, (g + 1) * gd)
            wc = jnp.where(causal, ws_ref[g], 0.0).astype(BF16)
            mixed = jnp.dot(wc, v_ref[:, sl], preferred_element_type=F32) + bias_scr[:, sl]
            o_ref[:, sl] = (u_ref[:, sl].astype(F32) * mixed).astype(o_ref.dtype)

    return pl.pallas_call(
        body, name=name, grid=(T // CHUNK,),
        in_specs=[pl.BlockSpec((CHUNK, GI), _rows), pl.BlockSpec((CHUNK, GI), _rows),
                  pl.BlockSpec((G, CHUNK, CHUNK), lambda i: (0, 0, 0)), pl.BlockSpec((CHUNK, LANES), _row0)],
        out_specs=pl.BlockSpec((CHUNK, GI), _rows), out_shape=_sds((T, GI), BF16),
        scratch_shapes=[pltpu.VMEM((CHUNK, GI), F32)], compiler_params=_cp("arbitrary"))(u, vn, ws, bt)


def _gm_spatial_bwd(name, dgated, u, vn, ws, bt):
    T, GI = u.shape
    G = ws.shape[0]
    gd = GI // G
    nc = T // CHUNK

    def body(dg_ref, u_ref, v_ref, ws_ref, bt_ref, du_ref, dv_ref, dws_ref, dbt_ref, bias_scr, dm_scr):
        step = pl.program_id(0)

        @pl.when(step == 0)
        def _():
            bias_scr[...] = jnp.dot(bt_ref[...], _group_expand(LANES, GI, gd), precision=HIGHEST,
                                    preferred_element_type=F32)
            dm_scr[...] = jnp.zeros_like(dm_scr)
            dws_ref[...] = jnp.zeros_like(dws_ref)

        causal = _causal_mask()
        for g in range(G):
            sl = slice(g * gd, (g + 1) * gd)
            wc = jnp.where(causal, ws_ref[g], 0.0).astype(BF16)
            vv = v_ref[:, sl]
            dgv = dg_ref[:, sl].astype(F32)
            mixed = jnp.dot(wc, vv, preferred_element_type=F32) + bias_scr[:, sl]
            du_ref[:, sl] = dgv * mixed
            dm = dgv * u_ref[:, sl].astype(F32)
            dmb = dm.astype(BF16)
            dv_ref[:, sl] = lax.dot_general(wc, dmb, (((0,), (0,)), ((), ())), preferred_element_type=F32)
            dw = lax.dot_general(dmb, vv, (((1,), (1,)), ((), ())), preferred_element_type=F32)
            dws_ref[g] += jnp.where(causal, dw, 0.0)
            dm_scr[:, sl] += dm

        @pl.when(step == nc - 1)
        def _():
            dbt_ref[...] = lax.dot_general(dm_scr[...], _group_expand(LANES, GI, gd), (((1,), (1,)), ((), ())),
                                           precision=HIGHEST, preferred_element_type=F32)

    return pl.pallas_call(
        body, name=name, grid=(nc,),
        in_specs=[pl.BlockSpec((CHUNK, GI), _rows), pl.BlockSpec((CHUNK, GI), _rows), pl.BlockSpec((CHUNK, GI), _rows),
                  pl.BlockSpec((G, CHUNK, CHUNK), lambda i: (0, 0, 0)), pl.BlockSpec((CHUNK, LANES), _row0)],
        out_specs=[pl.BlockSpec((CHUNK, GI), _rows), pl.BlockSpec((CHUNK, GI), _rows),
                   pl.BlockSpec((G, CHUNK, CHUNK), lambda i: (0, 0, 0)), pl.BlockSpec((CHUNK, LANES), _row0)],
        out_shape=[_sds((T, GI), F32), _sds((T, GI), F32), _sds((G, CHUNK, CHUNK), F32), _sds((CHUNK, LANES), F32)],
        scratch_shapes=[pltpu.VMEM((CHUNK, GI), F32), pltpu.VMEM((CHUNK, GI), F32)],
        compiler_params=_cp("arbitrary"))(dgated, u, vn, ws, bt)


def _conv_taps(xv, w_ref, b_ref):
    rows = lax.broadcasted_iota(jnp.int32, xv.shape, 0)
    acc = xv * w_ref[pl.ds(SSM_CONV - 1, 1), :] + b_ref[...]
    for k in range(1, SSM_CONV):
        sh = jnp.where(rows >= k, pltpu.roll(xv, k, 0), 0.0)
        acc = acc + sh * w_ref[pl.ds(SSM_CONV - 1 - k, 1), :]
    return acc


def _ssd_conv_fwd(name, proj, wt, b, inner, cd):
    T = proj.shape[0]
    tc = _tile(_gcd(inner, cd), 512)
    off = inner // tc

    def body(x_ref, w_ref, b_ref, o_ref):
        pre = _conv_taps(x_ref[...], w_ref, b_ref)
        o_ref[...] = pre * _sigmoid(pre)

    return pl.pallas_call(
        body, name=name, grid=(cd // tc,),
        in_specs=[pl.BlockSpec((T, tc), lambda j: (0, off + j)), pl.BlockSpec((SSM_CONV, tc), lambda j: (0, j)),
                  pl.BlockSpec((1, tc), lambda j: (0, j))],
        out_specs=pl.BlockSpec((T, tc), lambda j: (0, j)), out_shape=_sds((T, cd), F32),
        compiler_params=_cp("parallel"))(proj, wt, b)


def _ssd_conv_bwd(name, dact, proj, wt, b, inner, cd):
    T = proj.shape[0]
    tc = _tile(_gcd(inner, cd), 512)
    off = inner // tc

    def body(da_ref, x_ref, w_ref, b_ref, dx_ref, dw_ref, db_ref):
        xv = x_ref[...]
        pre = _conv_taps(xv, w_ref, b_ref)
        sg = _sigmoid(pre)
        dpre = da_ref[...] * sg * (1.0 + pre * (1.0 - sg))
        rows = lax.broadcasted_iota(jnp.int32, xv.shape, 0)
        db_ref[...] = jnp.sum(dpre, axis=0, keepdims=True)
        dx = dpre * w_ref[pl.ds(SSM_CONV - 1, 1), :]
        dw_ref[pl.ds(SSM_CONV - 1, 1), :] = jnp.sum(dpre * xv, axis=0, keepdims=True)
        for k in range(1, SSM_CONV):
            sh = jnp.where(rows >= k, pltpu.roll(xv, k, 0), 0.0)
            dw_ref[pl.ds(SSM_CONV - 1 - k, 1), :] = jnp.sum(dpre * sh, axis=0, keepdims=True)
            fw = jnp.where(rows < T - k, pltpu.roll(dpre, T - k, 0), 0.0)
            dx = dx + fw * w_ref[pl.ds(SSM_CONV - 1 - k, 1), :]
        dx_ref[...] = dx.astype(BF16)

    return pl.pallas_call(
        body, name=name, grid=(cd // tc,),
        in_specs=[pl.BlockSpec((T, tc), lambda j: (0, j)), pl.BlockSpec((T, tc), lambda j: (0, off + j)),
                  pl.BlockSpec((SSM_CONV, tc), lambda j: (0, j)), pl.BlockSpec((1, tc), lambda j: (0, j))],
        out_specs=[pl.BlockSpec((T, tc), lambda j: (0, j)), pl.BlockSpec((SSM_CONV, tc), lambda j: (0, j)),
                   pl.BlockSpec((1, tc), lambda j: (0, j))],
        out_shape=[_sds((T, cd), BF16), _sds((SSM_CONV, cd), F32), _sds((1, cd), F32)],
        compiler_params=_cp("parallel"))(dact, proj, wt, b)


def _softplus(v):
    return jnp.maximum(v, 0.0) + jnp.log(1.0 + jnp.exp(-jnp.abs(v)))


def _tri(lower):
    r = lax.broadcasted_iota(jnp.int32, (CHUNK, CHUNK), 0)
    c = lax.broadcasted_iota(jnp.int32, (CHUNK, CHUNK), 1)
    return ((c <= r) if lower else (c >= r)).astype(F32)


def _ssd_dt_fwd(name, proj, bias, alog, dtcol, heads):
    T = proj.shape[0]

    def body(dt_ref, b_ref, al_ref, dtp_ref, ac_ref):
        live = lax.broadcasted_iota(jnp.int32, (CHUNK, LANES), 1) < heads
        dtp = jnp.where(live, _softplus(dt_ref[...] + b_ref[...]), 0.0)
        da = dtp * (-jnp.exp(al_ref[...]))
        dtp_ref[...] = dtp
        ac_ref[...] = jnp.dot(_tri(True), da, precision=HIGHEST, preferred_element_type=F32)

    return pl.pallas_call(
        body, name=name, grid=(T // CHUNK,),
        in_specs=[pl.BlockSpec((CHUNK, LANES), lambda i: (i, dtcol)), pl.BlockSpec((1, LANES), _row0),
                  pl.BlockSpec((1, LANES), _row0)],
        out_specs=[pl.BlockSpec((CHUNK, LANES), _rows), pl.BlockSpec((CHUNK, LANES), _rows)],
        out_shape=[_sds((T, LANES), F32), _sds((T, LANES), F32)], compiler_params=_cp("parallel"))(proj, bias, alog)


def _ssd_dt_bwd(name, ddtp_g, dacum_g, dd_g, proj, bias, alog, dtp, dtcol, heads):
    T = proj.shape[0]
    G = ddtp_g.shape[0]

    def body(ddtp_ref, dac_ref, dd_ref, dt_ref, b_ref, al_ref, dtp_ref, ddt_ref, db_ref, dal_ref, dds_ref, da_scr):
        step = pl.program_id(0)

        @pl.when(step == 0)
        def _():
            db_ref[...] = jnp.zeros_like(db_ref)
            da_scr[...] = jnp.zeros_like(da_scr)
            dds_ref[...] = jnp.sum(dd_ref[...], axis=0)

        live = lax.broadcasted_iota(jnp.int32, (CHUNK, LANES), 1) < heads
        a = -jnp.exp(al_ref[...])
        dac = jnp.sum(dac_ref[...], axis=0)
        dda = jnp.dot(_tri(False), dac, precision=HIGHEST, preferred_element_type=F32)
        dtp_v = dtp_ref[...]
        ddtp = jnp.sum(ddtp_ref[...], axis=0) + dda * a
        da_scr[...] += jnp.sum(dda * dtp_v, axis=0, keepdims=True)
        ddt = jnp.where(live, ddtp * _sigmoid(dt_ref[...] + b_ref[...]), 0.0)
        ddt_ref[...] = ddt.astype(BF16)
        db_ref[...] += jnp.sum(ddt, axis=0, keepdims=True)
        dal_ref[...] = da_scr[...] * a

    return pl.pallas_call(
        body, name=name, grid=(T // CHUNK,),
        in_specs=[pl.BlockSpec((G, CHUNK, LANES), lambda i: (0, i, 0)), pl.BlockSpec((G, CHUNK, LANES), lambda i: (0, i, 0)),
                  pl.BlockSpec((G, 8, LANES), lambda i: (0, 0, 0)),
                  pl.BlockSpec((CHUNK, LANES), lambda i: (i, dtcol)), pl.BlockSpec((1, LANES), _row0),
                  pl.BlockSpec((1, LANES), _row0), pl.BlockSpec((CHUNK, LANES), _rows)],
        out_specs=[pl.BlockSpec((CHUNK, LANES), _rows), pl.BlockSpec((8, LANES), _row0), pl.BlockSpec((8, LANES), _row0),
                   pl.BlockSpec((8, LANES), _row0)],
        out_shape=[_sds((T, LANES), BF16), _sds((8, LANES), F32), _sds((8, LANES), F32), _sds((8, LANES), F32)],
        scratch_shapes=[pltpu.VMEM((8, LANES), F32)],
        compiler_params=_cp("arbitrary"))(ddtp_g, dacum_g, dd_g, proj, bias, alog, dtp)


def _head_expand(g, gw):
    hi = lax.broadcasted_iota(jnp.int32, (LANES, gw), 0) - g * SSM_HPG
    fi = lax.broadcasted_iota(jnp.int32, (LANES, gw), 1)
    return ((fi >= hi * SSM_HEAD_DIM) & (fi < (hi + 1) * SSM_HEAD_DIM)).astype(F32)


def _dot(a, b, dims, exact=False):
    if exact:
        return lax.dot_general(a, b, (dims, ((), ())), precision=HIGHEST, preferred_element_type=F32)
    return lax.dot_general(a.astype(BF16), b.astype(BF16), (dims, ((), ())), preferred_element_type=F32)


_NN = ((1,), (0,))
_NT = ((1,), (1,))
_TN = ((0,), (0,))


def _pair_decay(g, q, acum, acum_t_ref, causal):
    lane = lax.broadcasted_iota(jnp.int32, (CHUNK, LANES), 1)
    out = []
    for e in range(2):
        h = g * SSM_HPG + 2 * q + e
        acol = jnp.sum(jnp.where(lane == h, acum, 0.0), axis=1, keepdims=True)
        arow = acum_t_ref[pl.ds(h, 1), :]
        out.append(jnp.exp(jnp.where(causal, acol - arow, -1e30)))
    return out


def _ssd_core_fwd(name, act, dtp, acum, dexp, inner, groups):
    T = act.shape[0]
    nc = T // CHUNK
    gw = SSM_HPG * SSM_HEAD_DIM
    npair = gw // LANES
    bcol, ccol = inner // SSM_STATE, inner // SSM_STATE + groups

    def body(x_ref, b_ref, c_ref, dtp_ref, ac_ref, d_ref, y_ref, sp_ref, st_scr, act_scr, ae_scr):
        g = pl.program_id(0)

        @pl.when(pl.program_id(1) == 0)
        def _():
            st_scr[...] = jnp.zeros_like(st_scr)

        st = st_scr[...]
        sp_ref[...] = st
        e = _head_expand(g, gw)
        acum = ac_ref[...]
        ae = _dot(acum, e, _NN, exact=True)
        dte = _dot(dtp_ref[...], e, _NN, exact=True)
        ae_scr[...] = ae
        act_scr[...] = acum.T
        xv = x_ref[...]
        xdt = xv * dte
        bm, cm = b_ref[...], c_ref[...]
        cb = _dot(cm, bm, _NT)
        causal = _causal_mask()
        lane = lax.broadcasted_iota(jnp.int32, (CHUNK, LANES), 1)
        yoff = _dot(cm, st, _NN) * jnp.exp(ae)
        skip = xv * d_ref[...]
        for q in range(npair):
            sl = slice(q * LANES, (q + 1) * LANES)
            dec = _pair_decay(g, q, acum, act_scr, causal)
            x2 = xdt[:, sl]
            xa = jnp.where(lane < SSM_HEAD_DIM, x2, 0.0)
            yd = _dot(dec[0] * cb, xa, _NN) + _dot(dec[1] * cb, x2 - xa, _NN)
            y_ref[:, sl] = yd + yoff[:, sl] + skip[:, sl]
        alast = ae_scr[pl.ds(CHUNK - 1, 1), :]
        z = xdt * jnp.exp(alast - ae)
        st_scr[...] = st * jnp.exp(alast) + _dot(bm, z, _TN)

    return pl.pallas_call(
        body, name=name, grid=(groups, nc),
        in_specs=[pl.BlockSpec((CHUNK, gw), lambda g, c: (c, g)),
                  pl.BlockSpec((CHUNK, SSM_STATE), lambda g, c: (c, bcol + g)),
                  pl.BlockSpec((CHUNK, SSM_STATE), lambda g, c: (c, ccol + g)),
                  pl.BlockSpec((CHUNK, LANES), lambda g, c: (c, 0)), pl.BlockSpec((CHUNK, LANES), lambda g, c: (c, 0)),
                  pl.BlockSpec((1, gw), lambda g, c: (0, g))],
        out_specs=[pl.BlockSpec((CHUNK, gw), lambda g, c: (c, g)),
                   pl.BlockSpec((None, SSM_STATE, gw), lambda g, c: (c, 0, g))],
        out_shape=[_sds((T, inner), F32), _sds((nc, SSM_STATE, inner), F32)],
        scratch_shapes=[pltpu.VMEM((SSM_STATE, gw), F32), pltpu.VMEM((CHUNK, LANES), F32), pltpu.VMEM((CHUNK, gw), F32)],
        compiler_params=_cp("arbitrary", "arbitrary"))(act, act, act, dtp, acum, dexp)


def _ssd_core_bwd(name, dy, act, dtp, acum, dexp, sprev, inner, groups):
    T = act.shape[0]
    nc = T // CHUNK
    gw = SSM_HPG * SSM_HEAD_DIM
    npair = gw // LANES
    bcol, ccol = inner // SSM_STATE, inner // SSM_STATE + groups

    def rc(g, c):
        return nc - 1 - c

    def body(dy_ref, x_ref, b_ref, c_ref, dtp_ref, ac_ref, d_ref, sp_ref,
             dx_ref, db_ref, dc_ref, ddtp_ref, dac_ref, dd_ref,
             dst_scr, act_scr, ae_scr, dxdt_scr, dd_scr, dact_scr):
        g = pl.program_id(0)
        step = pl.program_id(1)

        @pl.when(step == 0)
        def _():
            dst_scr[...] = jnp.zeros_like(dst_scr)
            dd_scr[...] = jnp.zeros_like(dd_scr)

        dst = dst_scr[...]
        sp = sp_ref[...]
        e = _head_expand(g, gw)
        acum = ac_ref[...]
        ae = _dot(acum, e, _NN, exact=True)
        dte = _dot(dtp_ref[...], e, _NN, exact=True)
        ae_scr[...] = ae
        act_scr[...] = acum.T
        alast = ae_scr[pl.ds(CHUNK - 1, 1), :]
        xv = x_ref[...]
        xdt = xv * dte
        bm, cm = b_ref[...], c_ref[...]
        dyv = dy_ref[...]
        cb = _dot(cm, bm, _NT)
        causal = _causal_mask()
        lane = lax.broadcasted_iota(jnp.int32, (CHUNK, LANES), 1)
        ea = jnp.exp(ae)
        cde = jnp.exp(alast)
        w = jnp.exp(alast - ae)
        z = xdt * w

        dd_scr[...] += jnp.sum(dyv * xv, axis=0, keepdims=True)
        qm = _dot(cm, sp, _NN)
        dq = dyv * ea
        dae = dq * qm
        dc = _dot(dq, sp, _NT)
        dsp = _dot(cm, dq, _TN) + dst * cde
        dal = jnp.sum(dst * sp, axis=0, keepdims=True) * cde
        db = _dot(z, dst, _NT)
        dz = _dot(bm, dst, _NN)
        gw_ = dz * z
        dae = dae - gw_
        dal = dal + jnp.sum(gw_, axis=0, keepdims=True)
        dxdt_scr[...] = dz * w
        dcb = jnp.zeros((CHUNK, CHUNK), F32)
        dacol = jnp.zeros((CHUNK, LANES), F32)
        dact_scr[...] = jnp.zeros_like(dact_scr)
        sub = lax.broadcasted_iota(jnp.int32, (CHUNK, LANES), 0)
        for q in range(npair):
            sl = slice(q * LANES, (q + 1) * LANES)
            dec = _pair_decay(g, q, acum, act_scr, causal)
            x2, dy2 = xdt[:, sl], dyv[:, sl]
            xs_ = (jnp.where(lane < SSM_HEAD_DIM, x2, 0.0),)
            xs_ = xs_ + (x2 - xs_[0],)
            dys = (jnp.where(lane < SSM_HEAD_DIM, dy2, 0.0),)
            dys = dys + (dy2 - dys[0],)
            dx2 = jnp.zeros((CHUNK, LANES), F32)
            for hh in range(2):
                h = g * SSM_HPG + 2 * q + hh
                m = dec[hh] * cb
                dm = _dot(dys[hh], xs_[hh], _NT)
                dx2 = dx2 + _dot(m, dys[hh], _TN)
                dcb = dcb + dm * dec[hh]
                r = dm * m
                dacol = dacol + jnp.where(lane == h, jnp.sum(r, axis=1, keepdims=True), 0.0)
                dact_scr[...] -= jnp.where(sub == h, jnp.sum(r, axis=0, keepdims=True), 0.0)
            dxdt_scr[:, sl] += dx2
        dc = dc + _dot(dcb, bm, _NN)
        db = db + _dot(dcb, cm, _TN)
        dxdt = dxdt_scr[...]
        dx_ref[...] = dyv * d_ref[...] + dxdt * dte
        db_ref[...] = db
        dc_ref[...] = dc
        ddtp_ref[...] = _dot(dxdt * xv, e, _NT, exact=True)
        dal_h = _dot(jnp.broadcast_to(dal, (8, gw)), e, _NT, exact=True)
        dal_row = jnp.max(dal_h, axis=0, keepdims=True)
        dac = _dot(dae, e, _NT, exact=True) + dacol + dact_scr[...].T
        dac_ref[...] = dac + jnp.where(sub == CHUNK - 1, dal_row, 0.0)
        dst_scr[...] = dsp

        @pl.when(step == nc - 1)
        def _():
            dd_ref[...] = _dot(jnp.broadcast_to(dd_scr[...], (8, gw)), e, _NT, exact=True)

    return pl.pallas_call(
        body, name=name, grid=(groups, nc),
        in_specs=[pl.BlockSpec((CHUNK, gw), lambda g, c: (rc(g, c), g)),
                  pl.BlockSpec((CHUNK, gw), lambda g, c: (rc(g, c), g)),
                  pl.BlockSpec((CHUNK, SSM_STATE), lambda g, c: (rc(g, c), bcol + g)),
                  pl.BlockSpec((CHUNK, SSM_STATE), lambda g, c: (rc(g, c), ccol + g)),
                  pl.BlockSpec((CHUNK, LANES), lambda g, c: (rc(g, c), 0)),
                  pl.BlockSpec((CHUNK, LANES), lambda g, c: (rc(g, c), 0)),
                  pl.BlockSpec((1, gw), lambda g, c: (0, g)),
                  pl.BlockSpec((None, SSM_STATE, gw), lambda g, c: (rc(g, c), 0, g))],
        out_specs=[pl.BlockSpec((CHUNK, gw), lambda g, c: (rc(g, c), g)),
                   pl.BlockSpec((CHUNK, SSM_STATE), lambda g, c: (rc(g, c), g)),
                   pl.BlockSpec((CHUNK, SSM_STATE), lambda g, c: (rc(g, c), g)),
                   pl.BlockSpec((None, CHUNK, LANES), lambda g, c: (g, rc(g, c), 0)),
                   pl.BlockSpec((None, CHUNK, LANES), lambda g, c: (g, rc(g, c), 0)),
                   pl.BlockSpec((None, 8, LANES), lambda g, c: (g, 0, 0))],
        out_shape=[_sds((T, inner), F32), _sds((T, groups * SSM_STATE), F32), _sds((T, groups * SSM_STATE), F32),
                   _sds((groups, T, LANES), F32), _sds((groups, T, LANES), F32), _sds((groups, 8, LANES), F32)],
        scratch_shapes=[pltpu.VMEM((SSM_STATE, gw), F32), pltpu.VMEM((CHUNK, LANES), F32), pltpu.VMEM((CHUNK, gw), F32),
                        pltpu.VMEM((CHUNK, gw), F32), pltpu.VMEM((1, gw), F32), pltpu.VMEM((CHUNK, LANES), F32)],
        compiler_params=_cp("arbitrary", "arbitrary"))(dy, act, act, act, dtp, acum, dexp, sprev)


def _ssd_post_fwd(name, y, proj, ng, inner, groups):
    T = y.shape[0]
    tr = _tile(T, 128, 8)
    gs = inner // groups

    def body(y_ref, z_ref, g_ref, o_ref):
        zv = z_ref[...]
        gy = y_ref[...] * (zv * _sigmoid(zv))
        for k in range(groups):
            sl = slice(k * gs, (k + 1) * gs)
            seg = gy[:, sl]
            r = lax.rsqrt(jnp.mean(seg * seg, axis=-1, keepdims=True) + EPS)
            o_ref[:, sl] = (seg * r * g_ref[:, sl]).astype(BF16)

    return pl.pallas_call(
        body, name=name, grid=(T // tr,),
        in_specs=[pl.BlockSpec((tr, inner), _rows), pl.BlockSpec((tr, inner), _rows), pl.BlockSpec((1, inner), _row0)],
        out_specs=pl.BlockSpec((tr, inner), _rows), out_shape=_sds((T, inner), BF16),
        compiler_params=_cp("parallel"))(y, proj, ng)


def _ssd_post_bwd(name, dyn, y, proj, ng, inner, groups):
    T = y.shape[0]
    tr = _tile(T, 128, 8)
    gs = inner // groups

    def body(dyn_ref, y_ref, z_ref, g_ref, dy_ref, dz_ref, dg_ref):
        @pl.when(pl.program_id(0) == 0)
        def _():
            dg_ref[...] = jnp.zeros_like(dg_ref)

        zv = z_ref[...]
        sg = _sigmoid(zv)
        sz = zv * sg
        yv = y_ref[...]
        gy = yv * sz
        dv = dyn_ref[...]
        for k in range(groups):
            sl = slice(k * gs, (k + 1) * gs)
            seg = gy[:, sl]
            r = lax.rsqrt(jnp.mean(seg * seg, axis=-1, keepdims=True) + EPS)
            xh = seg * r
            d = dv[:, sl]
            dg_ref[:, sl] += jnp.sum(d * xh, axis=0, keepdims=True)
            dxh = d * g_ref[:, sl]
            dgy = r * (dxh - xh * jnp.mean(dxh * xh, axis=-1, keepdims=True))
            dy_ref[:, sl] = dgy * sz[:, sl]
            dz_ref[:, sl] = (dgy * yv[:, sl] * (sg[:, sl] * (1.0 + zv[:, sl] * (1.0 - sg[:, sl])))).astype(BF16)

    return pl.pallas_call(
        body, name=name, grid=(T // tr,),
        in_specs=[pl.BlockSpec((tr, inner), _rows), pl.BlockSpec((tr, inner), _rows), pl.BlockSpec((tr, inner), _rows),
                  pl.BlockSpec((1, inner), _row0)],
        out_specs=[pl.BlockSpec((tr, inner), _rows), pl.BlockSpec((tr, inner), _rows), pl.BlockSpec((1, inner), _row0)],
        out_shape=[_sds((T, inner), F32), _sds((T, inner), BF16), _sds((1, inner), F32)],
        compiler_params=_cp("arbitrary"))(dyn, y, proj, ng)


def _place():
    return lax.axis_index("x"), lax.axis_index("y"), lax.axis_index("c")


def _other_chips(x, y):
    return [(1 - x, y), (x, 1 - y), (1 - x, 1 - y)]


def _remote(src, dst, ssem, rsem, dev):
    return pltpu.make_async_remote_copy(src_ref=src, dst_ref=dst, send_sem=ssem, recv_sem=rsem, device_id=dev,
                                        device_id_type=MESH)


def _all_gather_chips(name, shard):
    R, C = shard.shape
    hr = R // 2

    def body(x_ref, o_ref, ssem, rsem, lsem):
        x, y, c = _place()
        k = 2 * x + y
        sib = (x, y, 1 - c)
        chips = _other_chips(x, y)

        def half(blk, cc):
            return o_ref.at[blk, pl.ds(cc * hr, hr), :]

        local = pltpu.make_async_copy(x_ref, o_ref.at[k], lsem)
        local.start()
        first = [_remote(x_ref.at[pl.ds(c * hr, hr), :], half(k, c), ssem.at[r], rsem.at[r], (px, py, c))
                 for r, (px, py) in enumerate(chips)]
        for cp in first:
            cp.start()
        passed = []
        for r, (px, py) in enumerate(chips):
            kj = 2 * px + py
            _remote(half(kj, c), half(kj, c), ssem.at[r], rsem.at[r], (px, py, c)).wait_recv()
            fw = _remote(half(kj, c), half(kj, c), ssem.at[3 + r], rsem.at[3 + r], sib)
            fw.start()
            passed.append(fw)
        for r, (px, py) in enumerate(chips):
            kj = 2 * px + py
            _remote(half(kj, 1 - c), half(kj, 1 - c), ssem.at[3 + r], rsem.at[3 + r], sib).wait_recv()
        for cp in first + passed:
            cp.wait_send()
        local.wait()

    return pl.pallas_call(
        body, name=name, in_specs=[_ANY], out_specs=_ANY, out_shape=_sds((4, R, C), shard.dtype),
        scratch_shapes=[pltpu.SemaphoreType.DMA((6,)), pltpu.SemaphoreType.DMA((6,)), pltpu.SemaphoreType.DMA],
    )(shard)


_HBM = pl.BlockSpec(memory_space=pltpu.HBM)
_SEM = pl.BlockSpec(memory_space=pltpu.SEMAPHORE)
_EFFECT = pltpu.SideEffectType.DATAFLOW_SIDE_EFFECTING


def _hbm(a):
    return pltpu.with_memory_space_constraint(a, pltpu.HBM)


NORTH = 1


def _gather_copies(x_refs, land_refs, ssem, rsem):
    x, y, _ = _place()
    k = 2 * x + y
    sends, arrivals = [], []
    for a, (xr, lr) in enumerate(zip(x_refs, land_refs)):
        for r, (px, py) in enumerate(_other_chips(x, y)):
            i = 3 * a + r
            sends.append(_remote(xr, lr.at[k], ssem.at[i], rsem.at[i], (px, py, NORTH)))
            arrivals.append(_remote(xr, lr.at[2 * px + py], ssem.at[i], rsem.at[i], (px, py, NORTH)))
    return sends, arrivals


def _scatter_copies(x_refs, land_refs, ssem, rsem):
    x, y, c = _place()
    k = 2 * x + y
    sends, arrivals = [], []
    for a, (xr, lr) in enumerate(zip(x_refs, land_refs)):
        for r, (px, py) in enumerate(_other_chips(x, y)):
            i = 3 * a + r
            kj = 2 * px + py
            sends.append(_remote(xr.at[kj], lr.at[k], ssem.at[i], rsem.at[i], (px, py, c)))
            arrivals.append(_remote(xr.at[kj], lr.at[kj], ssem.at[i], rsem.at[i], (px, py, c)))
    return sends, arrivals


def _on_cores(north_only, fn):
    if north_only:
        pl.when(lax.axis_index("c") == NORTH)(fn)
    else:
        fn()


def _split_start(name, srcs, land_shapes, copies_fn, north_only=False):
    n = len(srcs)

    def body(*refs):
        def go():
            sends, _ = copies_fn(refs[:n], refs[n:2 * n], refs[2 * n], refs[2 * n + 1])
            for cp in sends:
                cp.start()

        _on_cores(north_only, go)
        token = refs[-1]
        token[...] = jnp.zeros_like(token)

    lands = [_hbm(lax.empty(shp, srcs[0].dtype)) for shp in land_shapes]
    outs = pl.pallas_call(
        body, name=name, in_specs=[_HBM] * (2 * n),
        out_specs=(_SEM, _SEM) + (_HBM,) * (2 * n) + (pl.BlockSpec(memory_space=pltpu.VMEM),),
        out_shape=(pltpu.SemaphoreType.DMA((3 * n,)), pltpu.SemaphoreType.DMA((3 * n,)))
        + tuple(pltpu.HBM(s.shape, s.dtype) for s in srcs) + tuple(pltpu.HBM(l.shape, l.dtype) for l in lands)
        + (_sds((8, LANES), F32),),
        input_output_aliases={i: 2 + i for i in range(2 * n)},
        compiler_params=pltpu.CompilerParams(has_side_effects=_EFFECT),
    )(*[_hbm(s) for s in srcs], *lands)
    return outs[0], outs[1], list(outs[2:2 + n]), list(outs[2 + n:2 + 2 * n]), outs[-1]


def _split_wait(name, started, copies_fn, after, north_only=False):
    ssem, rsem, srcs, lands, _ = started
    n = len(srcs)

    def body(*refs):
        def go():
            sends, arrivals = copies_fn(refs[:n], refs[n:2 * n], refs[2 * n], refs[2 * n + 1])
            for cp in sends:
                cp.wait_send()
            for cp in arrivals:
                cp.wait_recv()

        _on_cores(north_only, go)

    outs = pl.pallas_call(
        body, name=name, in_specs=[_HBM] * (2 * n) + [_SEM, _SEM, _ANY], out_specs=(_HBM,) * (2 * n),
        out_shape=tuple(pltpu.HBM(t.shape, t.dtype) for t in srcs + lands),
        input_output_aliases={i: i for i in range(2 * n)},
        compiler_params=pltpu.CompilerParams(has_side_effects=_EFFECT),
    )(*srcs, *lands, ssem, rsem, after)
    return list(outs[:n]), list(outs[n:])


def _share_sibling(name, shards, lands):
    n = len(shards)

    def body(*refs):
        x_refs, l_in, l_out = refs[:n], refs[n:2 * n], refs[2 * n:3 * n]
        ssem, rsem, lsem = refs[3 * n:]
        x, y, c = _place()
        k = 2 * x + y
        locals_ = []
        for a in range(n):
            loc = pltpu.make_async_copy(x_refs[a], l_out[a].at[k], lsem.at[a])
            loc.start()
            locals_.append(loc)

        def hand_on(a, r, px, py):
            blk = 2 * px + py
            return _remote(l_in[a].at[blk], l_out[a].at[blk], ssem.at[3 * a + r], rsem.at[3 * a + r], (x, y, 1 - NORTH))

        @pl.when(c == NORTH)
        def _():
            copies = [hand_on(a, r, px, py) for a in range(n) for r, (px, py) in enumerate(_other_chips(x, y))]
            for cp in copies:
                cp.start()
            for cp in copies:
                cp.wait_send()

        @pl.when(c != NORTH)
        def _():
            for a in range(n):
                for r, (px, py) in enumerate(_other_chips(x, y)):
                    hand_on(a, r, px, py).wait_recv()

        for loc in locals_:
            loc.wait()

    outs = pl.pallas_call(
        body, name=name, in_specs=[_ANY] * (2 * n), out_specs=[_ANY] * n,
        out_shape=[_sds(l.shape, l.dtype) for l in lands], input_output_aliases={n + i: i for i in range(n)},
        scratch_shapes=[pltpu.SemaphoreType.DMA((3 * n,)), pltpu.SemaphoreType.DMA((3 * n,)),
                        pltpu.SemaphoreType.DMA((n,))],
    )(*shards, *lands)
    return list(outs)


def _swap_halves(name, fulls, after=None):
    n = len(fulls)
    extra = [] if after is None else [after]

    def body(*refs):
        g_refs, o_refs = refs[:n], refs[n + len(extra):2 * n + len(extra)]
        ssem, rsem = refs[2 * n + len(extra):]
        x, y, c = _place()
        copies = []
        for a in range(n):
            hr = g_refs[a].shape[1] // 2
            cp = _remote(g_refs[a].at[:, pl.ds((1 - c) * hr, hr), :], o_refs[a], ssem.at[a], rsem.at[a], (x, y, 1 - c))
            cp.start()
            copies.append(cp)
        for cp in copies:
            cp.wait()

    outs = pl.pallas_call(
        body, name=name, in_specs=[_ANY] * (n + len(extra)), out_specs=[_ANY] * n,
        out_shape=[_sds((f.shape[0], f.shape[1] // 2, f.shape[2]), f.dtype) for f in fulls],
        scratch_shapes=[pltpu.SemaphoreType.DMA((n,)), pltpu.SemaphoreType.DMA((n,))],
    )(*fulls, *extra)
    return list(outs)


def _add_halves(name, full, other):
    nb, R, C = full.shape
    hr = R // 2
    tr = _tile(hr, 256, 16)
    nh = hr // tr

    def body(f_ref, o_ref, s_ref):
        s_ref[...] = (f_ref[...].astype(F32) + o_ref[...].astype(F32)).astype(s_ref.dtype)

    return pl.pallas_call(
        body, name=name, grid=(nb, nh),
        in_specs=[pl.BlockSpec((None, tr, C), lambda b, i: (b, lax.axis_index("c") * nh + i, 0)),
                  pl.BlockSpec((None, tr, C), lambda b, i: (b, i, 0))],
        out_specs=pl.BlockSpec((None, tr, C), lambda b, i: (b, i, 0)), out_shape=_sds((nb, hr, C), full.dtype),
        compiler_params=_cp("parallel", "parallel"))(full, other)


def _sum_owner(name, land, mine):
    nb, hr, C = land.shape
    tr = _tile(hr, 128, 16)

    def body(l_ref, m_ref, o_ref):
        k = 2 * lax.axis_index("x") + lax.axis_index("y")
        own = m_ref[...].astype(F32)
        acc = jnp.where(k == 0, own, l_ref[0].astype(F32))
        for j in range(1, nb):
            acc = acc + jnp.where(k == j, own, l_ref[j].astype(F32))
        o_ref[...] = acc

    return pl.pallas_call(
        body, name=name, grid=(hr // tr,),
        in_specs=[pl.BlockSpec((nb, tr, C), lambda i: (0, i, 0)),
                  pl.BlockSpec((None, tr, C), lambda i: (2 * lax.axis_index("x") + lax.axis_index("y"), i, 0))],
        out_specs=pl.BlockSpec((tr, C), _rows), out_shape=_sds((hr, C), F32), compiler_params=_cp("parallel"))(land, mine)


def _join_halves(name, halves):
    n = len(halves)

    def body(*refs):
        p_refs, o_refs = refs[:n], refs[n:2 * n]
        ssem, rsem, lsem = refs[2 * n:]
        x, y, c = _place()
        copies = []
        for a in range(n):
            loc = pltpu.make_async_copy(p_refs[a], o_refs[a].at[c], lsem.at[a])
            loc.start()
            cp = _remote(p_refs[a], o_refs[a].at[c], ssem.at[a], rsem.at[a], (x, y, 1 - c))
            cp.start()
            copies.append((loc, cp))
        for a, (loc, cp) in enumerate(copies):
            _remote(p_refs[a], o_refs[a].at[1 - c], ssem.at[a], rsem.at[a], (x, y, 1 - c)).wait_recv()
            cp.wait_send()
            loc.wait()

    outs = pl.pallas_call(
        body, name=name, in_specs=[_ANY] * n, out_specs=[_ANY] * n,
        out_shape=[_sds((2,) + h.shape, h.dtype) for h in halves],
        scratch_shapes=[pltpu.SemaphoreType.DMA((n,)), pltpu.SemaphoreType.DMA((n,)), pltpu.SemaphoreType.DMA((n,))],
    )(*halves)
    return list(outs)


def _all_gather_devices(name, buf):
    def body(b_ref, o_ref, ssem, rsem, lsem):
        x, y, c = _place()
        me = 4 * x + 2 * y + c
        flips = [(fx, fy, fc) for fx in (0, 1) for fy in (0, 1) for fc in (0, 1) if fx or fy or fc]
        peers = [((1 - x) if fx else x, (1 - y) if fy else y, (1 - c) if fc else c) for fx, fy, fc in flips]
        local = pltpu.make_async_copy(b_ref, o_ref.at[me], lsem)
        local.start()
        sends = [_remote(b_ref, o_ref.at[me], ssem.at[r], rsem.at[r], p) for r, p in enumerate(peers)]
        for cp in sends:
            cp.start()
        for r, (px, py, pc) in enumerate(peers):
            pid = 4 * px + 2 * py + pc
            _remote(b_ref, o_ref.at[pid], ssem.at[r], rsem.at[r], (px, py, pc)).wait_recv()
        for cp in sends:
            cp.wait_send()
        local.wait()

    return pl.pallas_call(
        body, name=name, in_specs=[_ANY], out_specs=_ANY, out_shape=_sds((8,) + buf.shape, buf.dtype),
        scratch_shapes=[pltpu.SemaphoreType.DMA((7,)), pltpu.SemaphoreType.DMA((7,)), pltpu.SemaphoreType.DMA],
    )(buf)


def _sum_blocks(name, parts):
    n, R, C = parts.shape
    tr = _tile(R, 256, 8)

    def body(p_ref, o_ref):
        acc = p_ref[0].astype(F32)
        for j in range(1, n):
            acc = acc + p_ref[j].astype(F32)
        o_ref[...] = acc

    return pl.pallas_call(
        body, name=name, grid=(R // tr,), in_specs=[pl.BlockSpec((n, tr, C), lambda i: (0, i, 0))],
        out_specs=pl.BlockSpec((tr, C), _rows), out_shape=_sds((R, C), F32), compiler_params=_cp("parallel"))(parts)


def _adamw(name, w, m, v, parts, tok=None):
    L, R, C = w.shape
    np_ = len(parts[0])
    tr = _tile(R, max(8, (VMEM_LIMIT // 2) // (2 * 4 * C * (7 + L * np_))), 8)
    flat = [p for lp in parts for p in lp]
    if tok is not None:
        flat = flat + [tok]
    c1 = 1.0 / (1.0 - ADAM_B1 ** ADAM_STEP)
    c2 = 1.0 / (1.0 - ADAM_B2 ** ADAM_STEP)

    def body(*refs):
        w_ref, m_ref, v_ref = refs[:3]
        p_refs = refs[3:3 + L * np_]
        g_ref, d_ref, nm_ref, nv_ref = refs[-4:]
        layer = pl.program_id(0)
        g = jnp.zeros((tr, C), F32)
        for l in range(L):
            gl = p_refs[l * np_][...]
            for j in range(1, np_):
                gl = gl + p_refs[l * np_ + j][...]
            g = jnp.where(layer == l, gl, g) if L > 1 else gl
        if tok is not None:
            g = g + refs[3 + L * np_][...]
        nm = ADAM_B1 * m_ref[...] + (1.0 - ADAM_B1) * g
        nv = ADAM_B2 * v_ref[...] + (1.0 - ADAM_B2) * (g * g)
        g_ref[...] = g
        nm_ref[...] = nm
        nv_ref[...] = nv
        d_ref[...] = -ADAM_LR * ((nm * c1) / (jnp.sqrt(nv * c2) + ADAM_EPS) + ADAM_WD * w_ref[...])

    stacked = pl.BlockSpec((None, tr, C), lambda l, i: (l, i, 0))
    part_specs = [pl.BlockSpec((tr, C), (lambda l, i, ll=ll: (jnp.where(l == ll, i, 0), 0)))
                  for ll in range(L) for _ in range(np_)]
    if tok is not None:
        part_specs.append(pl.BlockSpec((1, 1), lambda l, i: (0, 0)))
    return pl.pallas_call(
        body, name=name, grid=(L, R // tr), in_specs=[stacked] * 3 + part_specs, out_specs=[stacked] * 4,
        out_shape=[_sds(w.shape, F32)] * 4, compiler_params=_cp("arbitrary", "arbitrary"))(w, m, v, *flat)


_PACK_ROWS = 16


def _pack(arrs):
    pieces = []
    for a in arrs:
        f = a.reshape(-1).astype(F32)
        unit = _PACK_ROWS * LANES
        pad = (-f.shape[0]) % unit
        pieces.append(jnp.pad(f, (0, pad)))
    return jnp.concatenate(pieces).reshape(-1, LANES)


def _unpack(buf, shapes):
    flat = buf.reshape(-1)
    out, off = [], 0
    unit = _PACK_ROWS * LANES
    for s in shapes:
        n = 1
        for d in s:
            n *= d
        out.append(flat[off:off + n].reshape(s))
        off += n + ((-n) % unit)
    return out


def _reduce_start(tag, fulls, after=None):
    others = _swap_halves("rs_swap", fulls, after)
    sums = [_add_halves("rs_add", f, o) for f, o in zip(fulls, others)]
    return _split_start("rs_start_" + tag, sums, [t.shape for t in sums], _scatter_copies)


def _reduce_finish(tag, started, after):
    sums, lands = _split_wait("rs_wait_" + tag, started, _scatter_copies, after)
    halves = [_sum_owner("rs_sum", l, s) for l, s in zip(lands, sums)]
    return [j.reshape(2 * j.shape[1], j.shape[2]) for j in _join_halves("rs_join", halves)]


def kernel(x, ln_ffn_pre, ffn_pre_w_in, ffn_pre_w_out, ln_mix, ln_ffn_post, ffn_post_w_in, ffn_post_w_out, gm_w_in, gm_v_norm, gm_w_s, gm_b_s, gm_w_out, ssm_w_in, ssm_conv_w, ssm_conv_b, ssm_dt_bias, ssm_a_log, ssm_d, ssm_norm, ssm_w_out, ln_final, loss_target, m_ln_ffn_pre, m_ffn_pre_w_in, m_ffn_pre_w_out, m_ln_mix, m_ln_ffn_post, m_ffn_post_w_in, m_ffn_post_w_out, m_gm_w_in, m_gm_v_norm, m_gm_w_s, m_gm_b_s, m_gm_w_out, m_ssm_w_in, m_ssm_conv_w, m_ssm_conv_b, m_ssm_dt_bias, m_ssm_a_log, m_ssm_d, m_ssm_norm, m_ssm_w_out, m_ln_final, v_ln_ffn_pre, v_ffn_pre_w_in, v_ffn_pre_w_out, v_ln_mix, v_ln_ffn_post, v_ffn_post_w_in, v_ffn_post_w_out, v_gm_w_in, v_gm_v_norm, v_gm_w_s, v_gm_b_s, v_gm_w_out, v_ssm_w_in, v_ssm_conv_w, v_ssm_conv_b, v_ssm_dt_bias, v_ssm_a_log, v_ssm_d, v_ssm_norm, v_ssm_w_out, v_ln_final):
    T, D = x.shape[1], x.shape[2]
    depth = ln_ffn_pre.shape[0]
    n_gm, n_ssm = gm_w_in.shape[0], ssm_w_in.shape[0]
    F = ffn_pre_w_out.shape[1] * 4
    GI = gm_w_out.shape[1] * 4
    GG = gm_w_s.shape[1]
    inner = ssm_w_out.shape[1] * 4
    heads = ssm_dt_bias.shape[1]
    cd = ssm_conv_w.shape[1] * 4
    groups = (cd - inner) // (2 * SSM_STATE)
    pshard = ssm_w_in.shape[2]
    pw = inner + cd + LANES
    dtcol = (inner + cd) // LANES
    kchip = 2 * lax.axis_index("x") + lax.axis_index("y")

    small_shard = _pack([jnp.swapaxes(ssm_conv_w, 1, 2), ssm_conv_b, ssm_norm])
    small_all = _all_gather_chips("ag_small", small_shard)
    cq, iq = cd // 4, inner // 4
    parts = [_unpack(small_all[k], [(n_ssm, SSM_CONV, cq), (n_ssm, cq), (n_ssm, iq)]) for k in range(4)]
    conv_wt = jnp.concatenate([p[0] for p in parts], axis=2)
    conv_b = jnp.concatenate([p[1] for p in parts], axis=1)
    norm_g = jnp.concatenate([p[2] for p in parts], axis=1)

    def pad_lanes(v):
        return jnp.pad(v, (0, LANES - v.shape[0]))[None, :]

    xc = x[0]
    saved = []

    def shards_of(i, kind):
        if kind == "pre":
            return [ffn_pre_w_in[i], ffn_pre_w_out[i]]
        if kind == "post":
            return [ffn_post_w_in[i], ffn_post_w_out[i]]
        if i % 2 == 0:
            return [gm_w_in[i // 2], gm_w_out[i // 2]]
        return [ssm_w_in[i // 2], ssm_w_out[i // 2]]

    subs = [(i, kind) for i in range(depth) for kind in ("pre", "mix", "post")]
    nsub = len(subs)
    gathers = [None] * nsub
    ahead = 3

    def gather_start(s):
        shards = [w.astype(BF16) for w in shards_of(*subs[s])]
        gathers[s] = _split_start("ag_start_%d" % s, shards, [(4,) + t.shape for t in shards], _gather_copies,
                                  north_only=True)

    def gathered(s, after):
        shards, lands = _split_wait("ag_wait_%d" % s, gathers[s], _gather_copies, after, north_only=True)
        full = _share_sibling("ag_share", shards, lands)
        if s + ahead < nsub:
            gather_start(s + ahead)
            return full, gathers[s + ahead][4][0:1, 0:1]
        return full, jnp.zeros((1, 1), F32)

    for s in range(min(ahead, nsub)):
        gather_start(s)

    def ffn_fwd(s, xin, g, l):
        (wi, wo), tok = gathered(s, xin)
        wo = wo.reshape(F, D)
        h = _rms_fwd("rms_fwd", xin, g[l][None, :] + tok)
        a = _matmul("ffn_in", h, wi, "nn", BF16, out_blocks=2)
        s_ = _swiglu_fwd("swiglu_fwd", a)
        xo = _matmul("ffn_out", s_, wo, "nn", F32, res=xin, scale=0.5)
        return xo, (xin, h, a, s_, wi, wo)

    for i in range(depth):
        xc, sv_pre = ffn_fwd(3 * i, xc, ln_ffn_pre, i)
        j = i // 2
        (wi, wo), tok = gathered(3 * i + 1, xc)
        h = _rms_fwd("rms_fwd", xc, ln_mix[i][None, :] + tok)
        if i % 2 == 0:
            wo = wo.reshape(GI, D)
            zpre = _matmul("gm_in", h, wi, "nn", BF16, out_blocks=2)
            u, vn = _gm_act_fwd("gm_act_fwd", zpre, gm_v_norm[j][None, :])
            bt = jnp.pad(gm_b_s[j].T, ((0, 0), (0, LANES - GG)))
            gated = _gm_spatial_fwd("gm_spatial_fwd", u, vn, gm_w_s[j], bt)
            xn = _matmul("mix_out", gated, wo, "nn", F32, res=xc, scale=1.0)
            sv_mix = (xc, h, zpre, u, vn, bt, gated, wi, wo)
        else:
            wg, wo = wi, wo.reshape(inner, D)
            wp = jnp.pad(jnp.swapaxes(wg, 0, 1).reshape(D, 4 * pshard), ((0, 0), (0, pw - 4 * pshard)))
            proj = _matmul("ssm_in", h, wp, "nn", F32)
            wt, cb_ = conv_wt[j], conv_b[j][None, :]
            act = _ssd_conv_fwd("ssd_conv_fwd", proj, wt, cb_, inner, cd)
            bias, alog = pad_lanes(ssm_dt_bias[j]), pad_lanes(ssm_a_log[j])
            dtp, acum = _ssd_dt_fwd("ssd_dt_fwd", proj, bias, alog, dtcol, heads)
            dexp = jnp.repeat(ssm_d[j], SSM_HEAD_DIM)[None, :]
            ycore, sprev = _ssd_core_fwd("ssd_core_fwd", act, dtp, acum, dexp, inner, groups)
            ng = norm_g[j][None, :]
            yn = _ssd_post_fwd("ssd_post_fwd", ycore, proj, ng, inner, groups)
            xn = _matmul("mix_out", yn, wo, "nn", F32, res=xc, scale=1.0)
            sv_mix = (xc, h, proj, act, dtp, acum, dexp, ycore, sprev, yn, wt, cb_, bias, alog, ng, wp, wo)
        xc = xn
        xc, sv_post = ffn_fwd(3 * i + 2, xc, ln_ffn_post, i)
        saved.append((sv_pre, sv_mix, sv_post))

    loss_tile, dx, dxb, dg_final = _loss_head("loss_head", xc, ln_final[None, :], loss_target[0])
    loss = lax.psum(loss_tile[0, 0], ("x", "y", "c"))

    wgrad = [None] * nsub
    in_flight = []
    deferred = []

    def reduce_finish(after):
        while in_flight:
            s, st = in_flight.pop(0)
            wgrad[s] = _reduce_finish(str(s), st, after)

    def reduce_start(s, fulls):
        reduce_finish(fulls[0])
        if s == 0:
            deferred.append(fulls)
            return jnp.zeros((1, 1), F32)
        st = _reduce_start(str(s), fulls)
        in_flight.append((s, st))
        return st[4][0:1, 0:1]

    def ffn_bwd(s, dx, dxb, sv, g, l):
        xin, h, a, s_, wi, wo = sv
        ds = _matmul("ffn_ds", dxb, wo, "nt", BF16, scale=0.5)
        dwo = _matmul("ffn_dwo", s_, dxb, "tn", BF16, scale=0.5)
        da = _swiglu_bwd("swiglu_bwd", a, ds)
        dh = _matmul("ffn_dh", da, wi, "nt", F32)
        dwi = _matmul("ffn_dwi", h, da, "tn", BF16, out_blocks=4)
        tok = reduce_start(s, [dwi, dwo.reshape(4, F // 4, D)])
        return _rms_bwd("rms_bwd", xin, g[l][None, :] + tok, dh, dx)

    gl = {n: [None] * depth for n in ("pre", "mix", "post")}
    gm_g = {n: [None] * n_gm for n in ("vnorm", "ws", "bs")}
    ssm_g = {n: [None] * n_ssm for n in ("convw", "convb", "dtb", "alog", "d", "norm")}

    for i in reversed(range(depth)):
        sv_pre, sv_mix, sv_post = saved[i]
        j = i // 2
        dx, dxb, gl["post"][i] = ffn_bwd(3 * i + 2, dx, dxb, sv_post, ln_ffn_post, i)
        if i % 2 == 0:
            xin, h, zpre, u, vn, bt, gated, wi, wo = sv_mix
            dgated = _matmul("mix_dy", dxb, wo, "nt", BF16)
            dwo = _matmul("mix_dwo", gated, dxb, "tn", BF16)
            du, dvn, dws, dbt = _gm_spatial_bwd("gm_spatial_bwd", dgated, u, vn, gm_w_s[j], bt)
            dzpre, dvnorm = _gm_act_bwd("gm_act_bwd", zpre, du, dvn, gm_v_norm[j][None, :])
            dh = _matmul("gm_dh", dzpre, wi, "nt", F32)
            dwi = _matmul("gm_dwi", h, dzpre, "tn", BF16, out_blocks=4)
            tok = reduce_start(3 * i + 1, [dwi, dwo.reshape(4, GI // 4, D)])
            gm_g["vnorm"][j], gm_g["ws"][j], gm_g["bs"][j] = dvnorm[0], dws, dbt.T[:GG]
        else:
            xin, h, proj, act, dtp, acum, dexp, ycore, sprev, yn, wt, cb_, bias, alog, ng, wp, wo = sv_mix
            dyn = _matmul("ssm_dy", dxb, wo, "nt", F32)
            dwo = _matmul("mix_dwo", yn, dxb, "tn", BF16)
            dyc, dz, dnorm = _ssd_post_bwd("ssd_post_bwd", dyn, ycore, proj, ng, inner, groups)
            dxs, db_, dc_, ddtp_g, dac_g, dd_g = _ssd_core_bwd("ssd_core_bwd", dyc, act, dtp, acum, dexp, sprev,
                                                               inner, groups)
            ddt, dbias, dalog, dds = _ssd_dt_bwd("ssd_dt_bwd", ddtp_g, dac_g, dd_g, proj, bias, alog, dtp, dtcol, heads)
            dact = jnp.concatenate([dxs, db_, dc_], axis=1)
            dxbc, dwt, dcb = _ssd_conv_bwd("ssd_conv_bwd", dact, proj, wt, cb_, inner, cd)
            dproj = jnp.concatenate([dz, dxbc, ddt], axis=1)
            dh = _matmul("ssm_dh", dproj, wp, "nt", F32)
            dwp = _matmul("ssm_dwi", h, dproj, "tn", BF16)
            dwi = jnp.swapaxes(dwp[:, :4 * pshard].reshape(D, 4, pshard), 0, 1)
            tok = reduce_start(3 * i + 1, [dwi, dwo.reshape(4, inner // 4, D)])
            ssm_g["convw"][j], ssm_g["convb"][j] = dwt.T, dcb[0]
            ssm_g["dtb"][j], ssm_g["alog"][j], ssm_g["d"][j] = dbias[0, :heads], dalog[0, :heads], dds[0, :heads]
            ssm_g["norm"][j] = dnorm[0]
        dx, dxb, gl["mix"][i] = _rms_bwd("rms_bwd", xin, ln_mix[i][None, :] + tok, dh, dx)
        dx, dxb, gl["pre"][i] = ffn_bwd(3 * i, dx, dxb, sv_pre, ln_ffn_pre, i)

    small_full = [
        jnp.concatenate(gl["pre"], 0), jnp.concatenate(gl["mix"], 0), jnp.concatenate(gl["post"], 0),
        jnp.stack(gm_g["vnorm"]), jnp.stack(gm_g["ws"]), jnp.stack(gm_g["bs"]),
        jnp.stack(ssm_g["convw"]), jnp.stack(ssm_g["convb"]), jnp.stack(ssm_g["dtb"]), jnp.stack(ssm_g["alog"]),
        jnp.stack(ssm_g["d"]), jnp.stack(ssm_g["norm"]), dg_final[0],
    ]
    full_shapes = [tuple(a.shape) for a in small_full]
    gathered = _all_gather_devices("ag8_small", _pack(small_full))
    summed = _sum_blocks("sum8_small", gathered)
    (g_pre, g_mix, g_post, g_vn, g_ws, g_bs, g_cw, g_cb, g_dtb, g_al, g_d, g_nm, g_fin) = _unpack(summed, full_shapes)
    g_cw = lax.dynamic_slice_in_dim(g_cw, kchip * cq, cq, axis=1)
    g_cb = lax.dynamic_slice_in_dim(g_cb, kchip * cq, cq, axis=1)
    g_nm = lax.dynamic_slice_in_dim(g_nm, kchip * iq, iq, axis=1)
    sm_g = [g_pre, g_mix, g_post, g_vn, g_ws, g_bs, g_cw, g_cb, g_dtb, g_al, g_d, g_nm, g_fin]
    sm_w = [ln_ffn_pre, ln_mix, ln_ffn_post, gm_v_norm, gm_w_s, gm_b_s, ssm_conv_w, ssm_conv_b, ssm_dt_bias,
            ssm_a_log, ssm_d, ssm_norm, ln_final]
    sm_m = [m_ln_ffn_pre, m_ln_mix, m_ln_ffn_post, m_gm_v_norm, m_gm_w_s, m_gm_b_s, m_ssm_conv_w, m_ssm_conv_b,
            m_ssm_dt_bias, m_ssm_a_log, m_ssm_d, m_ssm_norm, m_ln_final]
    sm_v = [v_ln_ffn_pre, v_ln_mix, v_ln_ffn_post, v_gm_v_norm, v_gm_w_s, v_gm_b_s, v_ssm_conv_w, v_ssm_conv_b,
            v_ssm_dt_bias, v_ssm_a_log, v_ssm_d, v_ssm_norm, v_ln_final]
    sm_shapes = [tuple(a.shape) for a in sm_w]
    pk = [_pack(lst)[None] for lst in (sm_w, sm_m, sm_v)]
    sg_, sd_, snm_, snv_ = _adamw("adamw_small", pk[0], pk[1], pk[2], [[_pack(sm_g)]])
    small_out = [_unpack(t[0], sm_shapes) for t in (sg_, sd_, snm_, snv_)]
    small_names = ["ln_ffn_pre", "ln_mix", "ln_ffn_post", "gm_v_norm", "gm_w_s", "gm_b_s", "ssm_conv_w", "ssm_conv_b",
                   "ssm_dt_bias", "ssm_a_log", "ssm_d", "ssm_norm", "ln_final"]

    def big_update(tag, w, m, v, sub_ids, which, tok=None):
        L = w.shape[0]
        shp = w.shape
        w2, m2, v2 = (t.reshape(L, -1, shp[-1]) for t in (w, m, v))
        outs = _adamw("adamw_" + tag, w2, m2, v2, [[wgrad[s][which]] for s in sub_ids], tok)
        return [o.reshape(shp) for o in outs]

    last = _reduce_start("0", deferred[0], after=sd_)
    tok = last[4][0:1, 0:1]
    pre_ids = [3 * i for i in range(depth)]
    post_ids = [3 * i + 2 for i in range(depth)]
    gm_ids = [3 * i + 1 for i in range(depth) if i % 2 == 0]
    ssm_ids = [3 * i + 1 for i in range(depth) if i % 2 == 1]
    big_out = {
        "ffn_post_w_in": big_update("ffn_in", ffn_post_w_in, m_ffn_post_w_in, v_ffn_post_w_in, post_ids, 0, tok),
        "ffn_post_w_out": big_update("ffn_out", ffn_post_w_out, m_ffn_post_w_out, v_ffn_post_w_out, post_ids, 1, tok),
        "gm_w_in": big_update("gm_in", gm_w_in, m_gm_w_in, v_gm_w_in, gm_ids, 0, tok),
        "gm_w_out": big_update("mix_out", gm_w_out, m_gm_w_out, v_gm_w_out, gm_ids, 1, tok),
        "ssm_w_in": big_update("ssm_in", ssm_w_in, m_ssm_w_in, v_ssm_w_in, ssm_ids, 0, tok),
        "ssm_w_out": big_update("mix_out", ssm_w_out, m_ssm_w_out, v_ssm_w_out, ssm_ids, 1, tok),
    }
    after = big_out["ffn_post_w_in"][1][0, 0:1, 0:1]
    for n in ("ffn_post_w_out", "gm_w_in", "gm_w_out", "ssm_w_in", "ssm_w_out"):
        after = after + big_out[n][1][0, 0:1, 0:1]
    wgrad[0] = _reduce_finish("0", last, after)
    big_out["ffn_pre_w_in"] = big_update("ffn_in", ffn_pre_w_in, m_ffn_pre_w_in, v_ffn_pre_w_in, pre_ids, 0)
    big_out["ffn_pre_w_out"] = big_update("ffn_out", ffn_pre_w_out, m_ffn_pre_w_out, v_ffn_pre_w_out, pre_ids, 1)

    order = ["ln_ffn_pre", "ffn_pre_w_in", "ffn_pre_w_out", "ln_mix", "ln_ffn_post", "ffn_post_w_in", "ffn_post_w_out",
             "gm_w_in", "gm_v_norm", "gm_w_s", "gm_b_s", "gm_w_out", "ssm_w_in", "ssm_conv_w", "ssm_conv_b",
             "ssm_dt_bias", "ssm_a_log", "ssm_d", "ssm_norm", "ssm_w_out", "ln_final"]

    def pick(kind, n):
        if n in big_out:
            return big_out[n][kind]
        return small_out[kind][small_names.index(n)]

    outs = [loss, dx[None]]
    for kind in range(4):
        outs.extend(pick(kind, n) for n in order)
    return tuple(outs)
```

```python
import jax
import jax.numpy as jnp
from jax import lax
from jax.experimental import pallas as pl
from jax.experimental.pallas import tpu as pltpu

F32 = jnp.float32
BF16 = jnp.bfloat16
HIGHEST = lax.Precision.HIGHEST
MESH = pl.DeviceIdType.MESH

EPS = 1e-6
ADAM_LR, ADAM_B1, ADAM_B2, ADAM_EPS, ADAM_WD, ADAM_STEP = 0.001, 0.9, 0.999, 1e-08, 0.01, 10

LANES = 128
CHUNK = 128
SSM_STATE = 128
SSM_HEAD_DIM = 64
SSM_HPG = 8
SSM_CONV = 4
VMEM_LIMIT = 56 * 1024 * 1024
MM_TILE = 1408

_ANY = pl.BlockSpec(memory_space=pl.ANY)


def _cp(*sem):
    return pltpu.CompilerParams(dimension_semantics=sem if sem else None, vmem_limit_bytes=VMEM_LIMIT)


def _tile(n, target, mult=LANES):
    best = None
    t = mult
    while t <= min(n, target):
        if n % t == 0:
            best = t
        t += mult
    return n if best is None else best


def _gcd(*v):
    import math
    g = 0
    for a in v:
        g = math.gcd(g, a)
    return g


def _sds(shape, dtype):
    return jax.ShapeDtypeStruct(tuple(shape), dtype)


def _ldims(shape):
    return tuple(shape) if len(shape) == 2 else (shape[1], shape[0] * shape[2])


def _colblock(shape):
    return None if len(shape) == 2 else shape[2]


def _mspec(shape, tr, tc, rc):
    if len(shape) == 2:
        return pl.BlockSpec((tr, tc), rc)
    per = shape[2] // tc

    def im(i, j, k):
        r, c = rc(i, j, k)
        return (c // per, r, c % per)

    return pl.BlockSpec((None, tr, tc), im)


def _matmul(name, a, b, mode, out_dtype, out_blocks=None, res=None, scale=1.0):
    la, lb = _ldims(a.shape), _ldims(b.shape)
    if mode == "nn":
        (M, K), (K2, N) = la, lb
    elif mode == "nt":
        (M, K), (N, K2) = la, lb
    else:
        (K, M), (K2, N) = la, lb
    assert K == K2, (name, a.shape, b.shape, mode)
    ca, cb = _colblock(a.shape), _colblock(b.shape)
    out_shape = (M, N) if out_blocks is None else (out_blocks, M, N // out_blocks)
    co = _colblock(out_shape)
    m_c, n_c, k_c = [M], [N], [K]
    if ca is not None:
        (m_c if mode == "tn" else k_c).append(ca)
    if cb is not None:
        (k_c if mode == "nt" else n_c).append(cb)
    if co is not None:
        n_c.append(co)
    tm, tn, tk = _tile(_gcd(*m_c), MM_TILE), _tile(_gcd(*n_c), MM_TILE), _tile(_gcd(*k_c), MM_TILE)
    nk = K // tk
    if mode == "tn":
        a_spec = _mspec(a.shape, tk, tm, lambda i, j, k: (k, i))
        dims = (((0,), (0,)), ((), ()))
    else:
        a_spec = _mspec(a.shape, tm, tk, lambda i, j, k: (i, k))
        dims = (((1,), (1,)), ((), ())) if mode == "nt" else (((1,), (0,)), ((), ()))
    if mode == "nt":
        b_spec = _mspec(b.shape, tn, tk, lambda i, j, k: (j, k))
    else:
        b_spec = _mspec(b.shape, tk, tn, lambda i, j, k: (k, j))
    o_spec = _mspec(out_shape, tm, tn, lambda i, j, k: (i, j))
    in_specs, args = [a_spec, b_spec], [a, b]
    if res is not None:
        in_specs.append(pl.BlockSpec((tm, tn), lambda i, j, k: (i, j)))
        args.append(res)

    def body(*refs):
        a_ref, b_ref = refs[0], refs[1]
        res_ref = refs[2] if res is not None else None
        o_ref, acc_ref = refs[-2], refs[-1]
        kk = pl.program_id(2)

        @pl.when(kk == 0)
        def _():
            acc_ref[...] = jnp.zeros_like(acc_ref)

        acc_ref[...] += lax.dot_general(a_ref[...].astype(BF16), b_ref[...].astype(BF16), dims,
                                        preferred_element_type=F32)

        @pl.when(kk == nk - 1)
        def _():
            r = acc_ref[...]
            if scale != 1.0:
                r = r * scale
            if res_ref is not None:
                r = res_ref[...] + r
            o_ref[...] = r.astype(o_ref.dtype)

    return pl.pallas_call(
        body, name=name, grid=(M // tm, N // tn, nk), in_specs=in_specs, out_specs=o_spec,
        out_shape=_sds(out_shape, out_dtype), scratch_shapes=[pltpu.VMEM((tm, tn), F32)],
        compiler_params=_cp("parallel", "parallel", "arbitrary"))(*args)


def _rows(i):
    return (i, 0)


def _row0(i):
    return (0, 0)


def _rms_fwd(name, x, g):
    T, D = x.shape
    tr = _tile(T, 256, 8)

    def body(x_ref, g_ref, o_ref):
        xv = x_ref[...]
        r = lax.rsqrt(jnp.mean(xv * xv, axis=-1, keepdims=True) + EPS)
        o_ref[...] = (xv * r * g_ref[...]).astype(o_ref.dtype)

    return pl.pallas_call(
        body, name=name, grid=(T // tr,),
        in_specs=[pl.BlockSpec((tr, D), _rows), pl.BlockSpec((1, D), _row0)],
        out_specs=pl.BlockSpec((tr, D), _rows), out_shape=_sds((T, D), BF16),
        compiler_params=_cp("parallel"))(x, g)


def _rms_bwd(name, x, g, dh, dres):
    T, D = x.shape
    tr = _tile(T, 256, 8)

    def body(x_ref, g_ref, dh_ref, dres_ref, dx_ref, dxb_ref, dg_ref):
        xv = x_ref[...]
        r = lax.rsqrt(jnp.mean(xv * xv, axis=-1, keepdims=True) + EPS)
        xh = xv * r
        dhv = dh_ref[...]
        dxh = dhv * g_ref[...]
        dx = dres_ref[...] + r * (dxh - xh * jnp.mean(dxh * xh, axis=-1, keepdims=True))
        dx_ref[...] = dx
        dxb_ref[...] = dx.astype(BF16)

        @pl.when(pl.program_id(0) == 0)
        def _():
            dg_ref[...] = jnp.zeros_like(dg_ref)

        dg_ref[...] += jnp.sum(dhv * xh, axis=0, keepdims=True)

    return pl.pallas_call(
        body, name=name, grid=(T // tr,),
        in_specs=[pl.BlockSpec((tr, D), _rows), pl.BlockSpec((1, D), _row0),
                  pl.BlockSpec((tr, D), _rows), pl.BlockSpec((tr, D), _rows)],
        out_specs=[pl.BlockSpec((tr, D), _rows), pl.BlockSpec((tr, D), _rows), pl.BlockSpec((1, D), _row0)],
        out_shape=[_sds((T, D), F32), _sds((T, D), BF16), _sds((1, D), F32)],
        compiler_params=_cp("arbitrary"))(x, g, dh, dres)


def _sigmoid(v):
    return 1.0 / (1.0 + jnp.exp(-v))


def _swiglu_fwd(name, a):
    _, T, F = a.shape
    tr, tc = _tile(T, 512, 8), _tile(F, MM_TILE)

    def body(a_ref, o_ref):
        gate = a_ref[0].astype(F32)
        up = a_ref[1].astype(F32)
        o_ref[...] = (gate * _sigmoid(gate) * up).astype(o_ref.dtype)

    return pl.pallas_call(
        body, name=name, grid=(T // tr, F // tc),
        in_specs=[pl.BlockSpec((2, tr, tc), lambda i, j: (0, i, j))],
        out_specs=pl.BlockSpec((tr, tc), lambda i, j: (i, j)), out_shape=_sds((T, F), BF16),
        compiler_params=_cp("parallel", "parallel"))(a)


def _swiglu_bwd(name, a, ds):
    _, T, F = a.shape
    tr, tc = _tile(T, 512, 8), _tile(F, MM_TILE)

    def body(a_ref, ds_ref, o_ref):
        gate = a_ref[0].astype(F32)
        up = a_ref[1].astype(F32)
        dsv = ds_ref[...].astype(F32)
        sg = _sigmoid(gate)
        o_ref[0] = (dsv * up * sg * (1.0 + gate * (1.0 - sg))).astype(o_ref.dtype)
        o_ref[1] = (dsv * gate * sg).astype(o_ref.dtype)

    return pl.pallas_call(
        body, name=name, grid=(T // tr, F // tc),
        in_specs=[pl.BlockSpec((2, tr, tc), lambda i, j: (0, i, j)), pl.BlockSpec((tr, tc), lambda i, j: (i, j))],
        out_specs=pl.BlockSpec((2, tr, tc), lambda i, j: (0, i, j)), out_shape=_sds((2, T, F), BF16),
        compiler_params=_cp("parallel", "parallel"))(a, ds)


def _loss_head(name, x, g, tgt):
    T, D = x.shape
    tr = _tile(T, 256, 8)

    def body(x_ref, g_ref, t_ref, loss_ref, dx_ref, dxb_ref, dg_ref):
        xv = x_ref[...]
        gv = g_ref[...]
        r = lax.rsqrt(jnp.mean(xv * xv, axis=-1, keepdims=True) + EPS)
        xh = xv * r
        err = xh * gv - t_ref[...]
        dy = err * (1.0 / D)
        dxh = dy * gv
        dx = r * (dxh - xh * jnp.mean(dxh * xh, axis=-1, keepdims=True))
        dx_ref[...] = dx
        dxb_ref[...] = dx.astype(BF16)

        @pl.when(pl.program_id(0) == 0)
        def _():
            dg_ref[...] = jnp.zeros_like(dg_ref)
            loss_ref[...] = jnp.zeros_like(loss_ref)

        dg_ref[...] += jnp.sum(dy * xh, axis=0, keepdims=True)
        part = jnp.sum(jnp.sum(err * err, axis=-1, keepdims=True), axis=0, keepdims=True) * (0.5 / D)
        loss_ref[...] += part

    return pl.pallas_call(
        body, name=name, grid=(T // tr,),
        in_specs=[pl.BlockSpec((tr, D), _rows), pl.BlockSpec((1, D), _row0), pl.BlockSpec((tr, D), _rows)],
        out_specs=[pl.BlockSpec((8, LANES), _row0), pl.BlockSpec((tr, D), _rows), pl.BlockSpec((tr, D), _rows),
                   pl.BlockSpec((1, D), _row0)],
        out_shape=[_sds((8, LANES), F32), _sds((T, D), F32), _sds((T, D), BF16), _sds((1, D), F32)],
        compiler_params=_cp("arbitrary"))(x, g, tgt)


_SQRT_HALF = 0.7071067811865476
_INV_SQRT_2PI = 0.3989422804014327


def _gelu(v):
    return 0.5 * v * (1.0 + lax.erf(v * _SQRT_HALF))


def _gelu_grad(v):
    return 0.5 * (1.0 + lax.erf(v * _SQRT_HALF)) + v * _INV_SQRT_2PI * jnp.exp(-0.5 * v * v)


def _group_expand(rows, width, gd):
    gi = lax.broadcasted_iota(jnp.int32, (rows, width), 0)
    fi = lax.broadcasted_iota(jnp.int32, (rows, width), 1)
    return ((fi >= gi * gd) & (fi < (gi + 1) * gd)).astype(F32)


def _gm_act_fwd(name, zpre, vnorm):
    _, T, GI = zpre.shape
    tr = _tile(T, 128, 8)

    def body(z_ref, g_ref, u_ref, v_ref):
        u_ref[...] = _gelu(z_ref[0].astype(F32)).astype(BF16)
        zv = _gelu(z_ref[1].astype(F32))
        r = lax.rsqrt(jnp.mean(zv * zv, axis=-1, keepdims=True) + EPS)
        v_ref[...] = (zv * r * g_ref[...]).astype(BF16)

    return pl.pallas_call(
        body, name=name, grid=(T // tr,),
        in_specs=[pl.BlockSpec((2, tr, GI), lambda i: (0, i, 0)), pl.BlockSpec((1, GI), _row0)],
        out_specs=[pl.BlockSpec((tr, GI), _rows), pl.BlockSpec((tr, GI), _rows)],
        out_shape=[_sds((T, GI), BF16), _sds((T, GI), BF16)], compiler_params=_cp("parallel"))(zpre, vnorm)


def _gm_act_bwd(name, zpre, du, dvn, vnorm):
    _, T, GI = zpre.shape
    tr = _tile(T, 128, 8)

    def body(z_ref, du_ref, dvn_ref, g_ref, dz_ref, dg_ref):
        xu = z_ref[0].astype(F32)
        xv = z_ref[1].astype(F32)
        zv = _gelu(xv)
        r = lax.rsqrt(jnp.mean(zv * zv, axis=-1, keepdims=True) + EPS)
        xh = zv * r
        dv = dvn_ref[...]

        @pl.when(pl.program_id(0) == 0)
        def _():
            dg_ref[...] = jnp.zeros_like(dg_ref)

        dg_ref[...] += jnp.sum(dv * xh, axis=0, keepdims=True)
        dxh = dv * g_ref[...]
        dzv = r * (dxh - xh * jnp.mean(dxh * xh, axis=-1, keepdims=True))
        dz_ref[0] = (du_ref[...] * _gelu_grad(xu)).astype(BF16)
        dz_ref[1] = (dzv * _gelu_grad(xv)).astype(BF16)

    return pl.pallas_call(
        body, name=name, grid=(T // tr,),
        in_specs=[pl.BlockSpec((2, tr, GI), lambda i: (0, i, 0)), pl.BlockSpec((tr, GI), _rows),
                  pl.BlockSpec((tr, GI), _rows), pl.BlockSpec((1, GI), _row0)],
        out_specs=[pl.BlockSpec((2, tr, GI), lambda i: (0, i, 0)), pl.BlockSpec((1, GI), _row0)],
        out_shape=[_sds((2, T, GI), BF16), _sds((1, GI), F32)], compiler_params=_cp("arbitrary"))(zpre, du, dvn, vnorm)


def _causal_mask():
    r = lax.broadcasted_iota(jnp.int32, (CHUNK, CHUNK), 0)
    c = lax.broadcasted_iota(jnp.int32, (CHUNK, CHUNK), 1)
    return r >= c


def _gm_spatial_fwd(name, u, vn, ws, bt):
    T, GI = u.shape
    G = ws.shape[0]
    gd = GI // G

    def body(u_ref, v_ref, ws_ref, bt_ref, o_ref, bias_scr):
        @pl.when(pl.program_id(0) == 0)
        def _():
            bias_scr[...] = jnp.dot(bt_ref[...], _group_expand(LANES, GI, gd), precision=HIGHEST,
                                    preferred_element_type=F32)

        causal = _causal_mask()
        for g in range(G):
            sl = slice(g * gd, (g + 1) * gd)
            wc = jnp.where(causal, ws_ref[g], 0.0).astype(BF16)
            mixed = jnp.dot(wc, v_ref[:, sl], preferred_element_type=F32) + bias_scr[:, sl]
            o_ref[:, sl] = (u_ref[:, sl].astype(F32) * mixed).astype(o_ref.dtype)

    return pl.pallas_call(
        body, name=name, grid=(T // CHUNK,),
        in_specs=[pl.BlockSpec((CHUNK, GI), _rows), pl.BlockSpec((CHUNK, GI), _rows),
                  pl.BlockSpec((G, CHUNK, CHUNK), lambda i: (0, 0, 0)), pl.BlockSpec((CHUNK, LANES), _row0)],
        out_specs=pl.BlockSpec((CHUNK, GI), _rows), out_shape=_sds((T, GI), BF16),
        scratch_shapes=[pltpu.VMEM((CHUNK, GI), F32)], compiler_params=_cp("arbitrary"))(u, vn, ws, bt)


def _gm_spatial_bwd(name, dgated, u, vn, ws, bt):
    T, GI = u.shape
    G = ws.shape[0]
    gd = GI // G
    nc = T // CHUNK

    def body(dg_ref, u_ref, v_ref, ws_ref, bt_ref, du_ref, dv_ref, dws_ref, dbt_ref, bias_scr, dm_scr):
        step = pl.program_id(0)

        @pl.when(step == 0)
        def _():
            bias_scr[...] = jnp.dot(bt_ref[...], _group_expand(LANES, GI, gd), precision=HIGHEST,
                                    preferred_element_type=F32)
            dm_scr[...] = jnp.zeros_like(dm_scr)
            dws_ref[...] = jnp.zeros_like(dws_ref)

        causal = _causal_mask()
        for g in range(G):
            sl = slice(g * gd, (g + 1) * gd)
            wc = jnp.where(causal, ws_ref[g], 0.0).astype(BF16)
            vv = v_ref[:, sl]
            dgv = dg_ref[:, sl].astype(F32)
            mixed = jnp.dot(wc, vv, preferred_element_type=F32) + bias_scr[:, sl]
            du_ref[:, sl] = dgv * mixed
            dm = dgv * u_ref[:, sl].astype(F32)
            dmb = dm.astype(BF16)
            dv_ref[:, sl] = lax.dot_general(wc, dmb, (((0,), (0,)), ((), ())), preferred_element_type=F32)
            dw = lax.dot_general(dmb, vv, (((1,), (1,)), ((), ())), preferred_element_type=F32)
            dws_ref[g] += jnp.where(causal, dw, 0.0)
            dm_scr[:, sl] += dm

        @pl.when(step == nc - 1)
        def _():
            dbt_ref[...] = lax.dot_general(dm_scr[...], _group_expand(LANES, GI, gd), (((1,), (1,)), ((), ())),
                                           precision=HIGHEST, preferred_element_type=F32)

    return pl.pallas_call(
        body, name=name, grid=(nc,),
        in_specs=[pl.BlockSpec((CHUNK, GI), _rows), pl.BlockSpec((CHUNK, GI), _rows), pl.BlockSpec((CHUNK, GI), _rows),
                  pl.BlockSpec((G, CHUNK, CHUNK), lambda i: (0, 0, 0)), pl.BlockSpec((CHUNK, LANES), _row0)],
        out_specs=[pl.BlockSpec((CHUNK, GI), _rows), pl.BlockSpec((CHUNK, GI), _rows),
                   pl.BlockSpec((G, CHUNK, CHUNK), lambda i: (0, 0, 0)), pl.BlockSpec((CHUNK, LANES), _row0)],
        out_shape=[_sds((T, GI), F32), _sds((T, GI), F32), _sds((G, CHUNK, CHUNK), F32), _sds((CHUNK, LANES), F32)],
        scratch_shapes=[pltpu.VMEM((CHUNK, GI), F32), pltpu.VMEM((CHUNK, GI), F32)],
        compiler_params=_cp("arbitrary"))(dgated, u, vn, ws, bt)


def _conv_taps(xv, w_ref, b_ref):
    rows = lax.broadcasted_iota(jnp.int32, xv.shape, 0)
    acc = xv * w_ref[pl.ds(SSM_CONV - 1, 1), :] + b_ref[...]
    for k in range(1, SSM_CONV):
        sh = jnp.where(rows >= k, pltpu.roll(xv, k, 0), 0.0)
        acc = acc + sh * w_ref[pl.ds(SSM_CONV - 1 - k, 1), :]
    return acc


def _ssd_conv_fwd(name, proj, wt, b, inner, cd):
    T = proj.shape[0]
    tc = _tile(_gcd(inner, cd), 512)
    off = inner // tc

    def body(x_ref, w_ref, b_ref, o_ref):
        pre = _conv_taps(x_ref[...], w_ref, b_ref)
        o_ref[...] = pre * _sigmoid(pre)

    return pl.pallas_call(
        body, name=name, grid=(cd // tc,),
        in_specs=[pl.BlockSpec((T, tc), lambda j: (0, off + j)), pl.BlockSpec((SSM_CONV, tc), lambda j: (0, j)),
                  pl.BlockSpec((1, tc), lambda j: (0, j))],
        out_specs=pl.BlockSpec((T, tc), lambda j: (0, j)), out_shape=_sds((T, cd), F32),
        compiler_params=_cp("parallel"))(proj, wt, b)


def _ssd_conv_bwd(name, dact, proj, wt, b, inner, cd):
    T = proj.shape[0]
    tc = _tile(_gcd(inner, cd), 512)
    off = inner // tc

    def body(da_ref, x_ref, w_ref, b_ref, dx_ref, dw_ref, db_ref):
        xv = x_ref[...]
        pre = _conv_taps(xv, w_ref, b_ref)
        sg = _sigmoid(pre)
        dpre = da_ref[...] * sg * (1.0 + pre * (1.0 - sg))
        rows = lax.broadcasted_iota(jnp.int32, xv.shape, 0)
        db_ref[...] = jnp.sum(dpre, axis=0, keepdims=True)
        dx = dpre * w_ref[pl.ds(SSM_CONV - 1, 1), :]
        dw_ref[pl.ds(SSM_CONV - 1, 1), :] = jnp.sum(dpre * xv, axis=0, keepdims=True)
        for k in range(1, SSM_CONV):
            sh = jnp.where(rows >= k, pltpu.roll(xv, k, 0), 0.0)
            dw_ref[pl.ds(SSM_CONV - 1 - k, 1), :] = jnp.sum(dpre * sh, axis=0, keepdims=True)
            fw = jnp.where(rows < T - k, pltpu.roll(dpre, T - k, 0), 0.0)
            dx = dx + fw * w_ref[pl.ds(SSM_CONV - 1 - k, 1), :]
        dx_ref[...] = dx.astype(BF16)

    return pl.pallas_call(
        body, name=name, grid=(cd // tc,),
        in_specs=[pl.BlockSpec((T, tc), lambda j: (0, j)), pl.BlockSpec((T, tc), lambda j: (0, off + j)),
                  pl.BlockSpec((SSM_CONV, tc), lambda j: (0, j)), pl.BlockSpec((1, tc), lambda j: (0, j))],
        out_specs=[pl.BlockSpec((T, tc), lambda j: (0, j)), pl.BlockSpec((SSM_CONV, tc), lambda j: (0, j)),
                   pl.BlockSpec((1, tc), lambda j: (0, j))],
        out_shape=[_sds((T, cd), BF16), _sds((SSM_CONV, cd), F32), _sds((1, cd), F32)],
        compiler_params=_cp("parallel"))(dact, proj, wt, b)


def _softplus(v):
    return jnp.maximum(v, 0.0) + jnp.log(1.0 + jnp.exp(-jnp.abs(v)))


def _tri(lower):
    r = lax.broadcasted_iota(jnp.int32, (CHUNK, CHUNK), 0)
    c = lax.broadcasted_iota(jnp.int32, (CHUNK, CHUNK), 1)
    return ((c <= r) if lower else (c >= r)).astype(F32)


def _ssd_dt_fwd(name, proj, bias, alog, dtcol, heads):
    T = proj.shape[0]

    def body(dt_ref, b_ref, al_ref, dtp_ref, ac_ref):
        live = lax.broadcasted_iota(jnp.int32, (CHUNK, LANES), 1) < heads
        dtp = jnp.where(live, _softplus(dt_ref[...] + b_ref[...]), 0.0)
        da = dtp * (-jnp.exp(al_ref[...]))
        dtp_ref[...] = dtp
        ac_ref[...] = jnp.dot(_tri(True), da, precision=HIGHEST, preferred_element_type=F32)

    return pl.pallas_call(
        body, name=name, grid=(T // CHUNK,),
        in_specs=[pl.BlockSpec((CHUNK, LANES), lambda i: (i, dtcol)), pl.BlockSpec((1, LANES), _row0),
                  pl.BlockSpec((1, LANES), _row0)],
        out_specs=[pl.BlockSpec((CHUNK, LANES), _rows), pl.BlockSpec((CHUNK, LANES), _rows)],
        out_shape=[_sds((T, LANES), F32), _sds((T, LANES), F32)], compiler_params=_cp("parallel"))(proj, bias, alog)


def _ssd_dt_bwd(name, ddtp_g, dacum_g, dd_g, proj, bias, alog, dtp, dtcol, heads):
    T = proj.shape[0]
    G = ddtp_g.shape[0]

    def body(ddtp_ref, dac_ref, dd_ref, dt_ref, b_ref, al_ref, dtp_ref, ddt_ref, db_ref, dal_ref, dds_ref, da_scr):
        step = pl.program_id(0)

        @pl.when(step == 0)
        def _():
            db_ref[...] = jnp.zeros_like(db_ref)
            da_scr[...] = jnp.zeros_like(da_scr)
            dds_ref[...] = jnp.sum(dd_ref[...], axis=0)

        live = lax.broadcasted_iota(jnp.int32, (CHUNK, LANES), 1) < heads
        a = -jnp.exp(al_ref[...])
        dac = jnp.sum(dac_ref[...], axis=0)
        dda = jnp.dot(_tri(False), dac, precision=HIGHEST, preferred_element_type=F32)
        dtp_v = dtp_ref[...]
        ddtp = jnp.sum(ddtp_ref[...], axis=0) + dda * a
        da_scr[...] += jnp.sum(dda * dtp_v, axis=0, keepdims=True)
        ddt = jnp.where(live, ddtp * _sigmoid(dt_ref[...] + b_ref[...]), 0.0)
        ddt_ref[...] = ddt.astype(BF16)
        db_ref[...] += jnp.sum(ddt, axis=0, keepdims=True)
        dal_ref[...] = da_scr[...] * a

    return pl.pallas_call(
        body, name=name, grid=(T // CHUNK,),
        in_specs=[pl.BlockSpec((G, CHUNK, LANES), lambda i: (0, i, 0)), pl.BlockSpec((G, CHUNK, LANES), lambda i: (0, i, 0)),
                  pl.BlockSpec((G, 8, LANES), lambda i: (0, 0, 0)),
                  pl.BlockSpec((CHUNK, LANES), lambda i: (i, dtcol)), pl.BlockSpec((1, LANES), _row0),
                  pl.BlockSpec((1, LANES), _row0), pl.BlockSpec((CHUNK, LANES), _rows)],
        out_specs=[pl.BlockSpec((CHUNK, LANES), _rows), pl.BlockSpec((8, LANES), _row0), pl.BlockSpec((8, LANES), _row0),
                   pl.BlockSpec((8, LANES), _row0)],
        out_shape=[_sds((T, LANES), BF16), _sds((8, LANES), F32), _sds((8, LANES), F32), _sds((8, LANES), F32)],
        scratch_shapes=[pltpu.VMEM((8, LANES), F32)],
        compiler_params=_cp("arbitrary"))(ddtp_g, dacum_g, dd_g, proj, bias, alog, dtp)


def _head_expand(g, gw):
    hi = lax.broadcasted_iota(jnp.int32, (LANES, gw), 0) - g * SSM_HPG
    fi = lax.broadcasted_iota(jnp.int32, (LANES, gw), 1)
    return ((fi >= hi * SSM_HEAD_DIM) & (fi < (hi + 1) * SSM_HEAD_DIM)).astype(F32)


def _dot(a, b, dims, exact=False):
    if exact:
        return lax.dot_general(a, b, (dims, ((), ())), precision=HIGHEST, preferred_element_type=F32)
    return lax.dot_general(a.astype(BF16), b.astype(BF16), (dims, ((), ())), preferred_element_type=F32)


_NN = ((1,), (0,))
_NT = ((1,), (1,))
_TN = ((0,), (0,))


def _pair_decay(g, q, acum, acum_t_ref, causal):
    lane = lax.broadcasted_iota(jnp.int32, (CHUNK, LANES), 1)
    out = []
    for e in range(2):
        h = g * SSM_HPG + 2 * q + e
        acol = jnp.sum(jnp.where(lane == h, acum, 0.0), axis=1, keepdims=True)
        arow = acum_t_ref[pl.ds(h, 1), :]
        out.append(jnp.exp(jnp.where(causal, acol - arow, -1e30)))
    return out


def _ssd_core_fwd(name, act, dtp, acum, dexp, inner, groups):
    T = act.shape[0]
    nc = T // CHUNK
    gw = SSM_HPG * SSM_HEAD_DIM
    npair = gw // LANES
    bcol, ccol = inner // SSM_STATE, inner // SSM_STATE + groups

    def body(x_ref, b_ref, c_ref, dtp_ref, ac_ref, d_ref, y_ref, sp_ref, st_scr, act_scr, ae_scr):
        g = pl.program_id(0)

        @pl.when(pl.program_id(1) == 0)
        def _():
            st_scr[...] = jnp.zeros_like(st_scr)

        st = st_scr[...]
        sp_ref[...] = st
        e = _head_expand(g, gw)
        acum = ac_ref[...]
        ae = _dot(acum, e, _NN, exact=True)
        dte = _dot(dtp_ref[...], e, _NN, exact=True)
        ae_scr[...] = ae
        act_scr[...] = acum.T
        xv = x_ref[...]
        xdt = xv * dte
        bm, cm = b_ref[...], c_ref[...]
        cb = _dot(cm, bm, _NT)
        causal = _causal_mask()
        lane = lax.broadcasted_iota(jnp.int32, (CHUNK, LANES), 1)
        yoff = _dot(cm, st, _NN) * jnp.exp(ae)
        skip = xv * d_ref[...]
        for q in range(npair):
            sl = slice(q * LANES, (q + 1) * LANES)
            dec = _pair_decay(g, q, acum, act_scr, causal)
            x2 = xdt[:, sl]
            xa = jnp.where(lane < SSM_HEAD_DIM, x2, 0.0)
            yd = _dot(dec[0] * cb, xa, _NN) + _dot(dec[1] * cb, x2 - xa, _NN)
            y_ref[:, sl] = yd + yoff[:, sl] + skip[:, sl]
        alast = ae_scr[pl.ds(CHUNK - 1, 1), :]
        z = xdt * jnp.exp(alast - ae)
        st_scr[...] = st * jnp.exp(alast) + _dot(bm, z, _TN)

    return pl.pallas_call(
        body, name=name, grid=(groups, nc),
        in_specs=[pl.BlockSpec((CHUNK, gw), lambda g, c: (c, g)),
                  pl.BlockSpec((CHUNK, SSM_STATE), lambda g, c: (c, bcol + g)),
                  pl.BlockSpec((CHUNK, SSM_STATE), lambda g, c: (c, ccol + g)),
                  pl.BlockSpec((CHUNK, LANES), lambda g, c: (c, 0)), pl.BlockSpec((CHUNK, LANES), lambda g, c: (c, 0)),
                  pl.BlockSpec((1, gw), lambda g, c: (0, g))],
        out_specs=[pl.BlockSpec((CHUNK, gw), lambda g, c: (c, g)),
                   pl.BlockSpec((None, SSM_STATE, gw), lambda g, c: (c, 0, g))],
        out_shape=[_sds((T, inner), F32), _sds((nc, SSM_STATE, inner), F32)],
        scratch_shapes=[pltpu.VMEM((SSM_STATE, gw), F32), pltpu.VMEM((CHUNK, LANES), F32), pltpu.VMEM((CHUNK, gw), F32)],
        compiler_params=_cp("arbitrary", "arbitrary"))(act, act, act, dtp, acum, dexp)


def _ssd_core_bwd(name, dy, act, dtp, acum, dexp, sprev, inner, groups):
    T = act.shape[0]
    nc = T // CHUNK
    gw = SSM_HPG * SSM_HEAD_DIM
    npair = gw // LANES
    bcol, ccol = inner // SSM_STATE, inner // SSM_STATE + groups

    def rc(g, c):
        return nc - 1 - c

    def body(dy_ref, x_ref, b_ref, c_ref, dtp_ref, ac_ref, d_ref, sp_ref,
             dx_ref, db_ref, dc_ref, ddtp_ref, dac_ref, dd_ref,
             dst_scr, act_scr, ae_scr, dxdt_scr, dd_scr, dact_scr):
        g = pl.program_id(0)
        step = pl.program_id(1)

        @pl.when(step == 0)
        def _():
            dst_scr[...] = jnp.zeros_like(dst_scr)
            dd_scr[...] = jnp.zeros_like(dd_scr)

        dst = dst_scr[...]
        sp = sp_ref[...]
        e = _head_expand(g, gw)
        acum = ac_ref[...]
        ae = _dot(acum, e, _NN, exact=True)
        dte = _dot(dtp_ref[...], e, _NN, exact=True)
        ae_scr[...] = ae
        act_scr[...] = acum.T
        alast = ae_scr[pl.ds(CHUNK - 1, 1), :]
        xv = x_ref[...]
        xdt = xv * dte
        bm, cm = b_ref[...], c_ref[...]
        dyv = dy_ref[...]
        cb = _dot(cm, bm, _NT)
        causal = _causal_mask()
        lane = lax.broadcasted_iota(jnp.int32, (CHUNK, LANES), 1)
        ea = jnp.exp(ae)
        cde = jnp.exp(alast)
        w = jnp.exp(alast - ae)
        z = xdt * w

        dd_scr[...] += jnp.sum(dyv * xv, axis=0, keepdims=True)
        qm = _dot(cm, sp, _NN)
        dq = dyv * ea
        dae = dq * qm
        dc = _dot(dq, sp, _NT)
        dsp = _dot(cm, dq, _TN) + dst * cde
        dal = jnp.sum(dst * sp, axis=0, keepdims=True) * cde
        db = _dot(z, dst, _NT)
        dz = _dot(bm, dst, _NN)
        gw_ = dz * z
        dae = dae - gw_
        dal = dal + jnp.sum(gw_, axis=0, keepdims=True)
        dxdt_scr[...] = dz * w
        dcb = jnp.zeros((CHUNK, CHUNK), F32)
        dacol = jnp.zeros((CHUNK, LANES), F32)
        dact_scr[...] = jnp.zeros_like(dact_scr)
        sub = lax.broadcasted_iota(jnp.int32, (CHUNK, LANES), 0)
        for q in range(npair):
            sl = slice(q * LANES, (q + 1) * LANES)
            dec = _pair_decay(g, q, acum, act_scr, causal)
            x2, dy2 = xdt[:, sl], dyv[:, sl]
            xs_ = (jnp.where(lane < SSM_HEAD_DIM, x2, 0.0),)
            xs_ = xs_ + (x2 - xs_[0],)
            dys = (jnp.where(lane < SSM_HEAD_DIM, dy2, 0.0),)
            dys = dys + (dy2 - dys[0],)
            dx2 = jnp.zeros((CHUNK, LANES), F32)
            for hh in range(2):
                h = g * SSM_HPG + 2 * q + hh
                m = dec[hh] * cb
                dm = _dot(dys[hh], xs_[hh], _NT)
                dx2 = dx2 + _dot(m, dys[hh], _TN)
                dcb = dcb + dm * dec[hh]
                r = dm * m
                dacol = dacol + jnp.where(lane == h, jnp.sum(r, axis=1, keepdims=True), 0.0)
                dact_scr[...] -= jnp.where(sub == h, jnp.sum(r, axis=0, keepdims=True), 0.0)
            dxdt_scr[:, sl] += dx2
        dc = dc + _dot(dcb, bm, _NN)
        db = db + _dot(dcb, cm, _TN)
        dxdt = dxdt_scr[...]
        dx_ref[...] = dyv * d_ref[...] + dxdt * dte
        db_ref[...] = db
        dc_ref[...] = dc
        ddtp_ref[...] = _dot(dxdt * xv, e, _NT, exact=True)
        dal_h = _dot(jnp.broadcast_to(dal, (8, gw)), e, _NT, exact=True)
        dal_row = jnp.max(dal_h, axis=0, keepdims=True)
        dac = _dot(dae, e, _NT, exact=True) + dacol + dact_scr[...].T
        dac_ref[...] = dac + jnp.where(sub == CHUNK - 1, dal_row, 0.0)
        dst_scr[...] = dsp

        @pl.when(step == nc - 1)
        def _():
            dd_ref[...] = _dot(jnp.broadcast_to(dd_scr[...], (8, gw)), e, _NT, exact=True)

    return pl.pallas_call(
        body, name=name, grid=(groups, nc),
        in_specs=[pl.BlockSpec((CHUNK, gw), lambda g, c: (rc(g, c), g)),
                  pl.BlockSpec((CHUNK, gw), lambda g, c: (rc(g, c), g)),
                  pl.BlockSpec((CHUNK, SSM_STATE), lambda g, c: (rc(g, c), bcol + g)),
                  pl.BlockSpec((CHUNK, SSM_STATE), lambda g, c: (rc(g, c), ccol + g)),
                  pl.BlockSpec((CHUNK, LANES), lambda g, c: (rc(g, c), 0)),
                  pl.BlockSpec((CHUNK, LANES), lambda g, c: (rc(g, c), 0)),
                  pl.BlockSpec((1, gw), lambda g, c: (0, g)),
                  pl.BlockSpec((None, SSM_STATE, gw), lambda g, c: (rc(g, c), 0, g))],
        out_specs=[pl.BlockSpec((CHUNK, gw), lambda g, c: (rc(g, c), g)),
                   pl.BlockSpec((CHUNK, SSM_STATE), lambda g, c: (rc(g, c), g)),
                   pl.BlockSpec((CHUNK, SSM_STATE), lambda g, c: (rc(g, c), g)),
                   pl.BlockSpec((None, CHUNK, LANES), lambda g, c: (g, rc(g, c), 0)),
                   pl.BlockSpec((None, CHUNK, LANES), lambda g, c: (g, rc(g, c), 0)),
                   pl.BlockSpec((None, 8, LANES), lambda g, c: (g, 0, 0))],
        out_shape=[_sds((T, inner), F32), _sds((T, groups * SSM_STATE), F32), _sds((T, groups * SSM_STATE), F32),
                   _sds((groups, T, LANES), F32), _sds((groups, T, LANES), F32), _sds((groups, 8, LANES), F32)],
        scratch_shapes=[pltpu.VMEM((SSM_STATE, gw), F32), pltpu.VMEM((CHUNK, LANES), F32), pltpu.VMEM((CHUNK, gw), F32),
                        pltpu.VMEM((CHUNK, gw), F32), pltpu.VMEM((1, gw), F32), pltpu.VMEM((CHUNK, LANES), F32)],
        compiler_params=_cp("arbitrary", "arbitrary"))(dy, act, act, act, dtp, acum, dexp, sprev)


def _ssd_post_fwd(name, y, proj, ng, inner, groups):
    T = y.shape[0]
    tr = _tile(T, 128, 8)
    gs = inner // groups

    def body(y_ref, z_ref, g_ref, o_ref):
        zv = z_ref[...]
        gy = y_ref[...] * (zv * _sigmoid(zv))
        for k in range(groups):
            sl = slice(k * gs, (k + 1) * gs)
            seg = gy[:, sl]
            r = lax.rsqrt(jnp.mean(seg * seg, axis=-1, keepdims=True) + EPS)
            o_ref[:, sl] = (seg * r * g_ref[:, sl]).astype(BF16)

    return pl.pallas_call(
        body, name=name, grid=(T // tr,),
        in_specs=[pl.BlockSpec((tr, inner), _rows), pl.BlockSpec((tr, inner), _rows), pl.BlockSpec((1, inner), _row0)],
        out_specs=pl.BlockSpec((tr, inner), _rows), out_shape=_sds((T, inner), BF16),
        compiler_params=_cp("parallel"))(y, proj, ng)


def _ssd_post_bwd(name, dyn, y, proj, ng, inner, groups):
    T = y.shape[0]
    tr = _tile(T, 128, 8)
    gs = inner // groups

    def body(dyn_ref, y_ref, z_ref, g_ref, dy_ref, dz_ref, dg_ref):
        @pl.when(pl.program_id(0) == 0)
        def _():
            dg_ref[...] = jnp.zeros_like(dg_ref)

        zv = z_ref[...]
        sg = _sigmoid(zv)
        sz = zv * sg
        yv = y_ref[...]
        gy = yv * sz
        dv = dyn_ref[...]
        for k in range(groups):
            sl = slice(k * gs, (k + 1) * gs)
            seg = gy[:, sl]
            r = lax.rsqrt(jnp.mean(seg * seg, axis=-1, keepdims=True) + EPS)
            xh = seg * r
            d = dv[:, sl]
            dg_ref[:, sl] += jnp.sum(d * xh, axis=0, keepdims=True)
            dxh = d * g_ref[:, sl]
            dgy = r * (dxh - xh * jnp.mean(dxh * xh, axis=-1, keepdims=True))
            dy_ref[:, sl] = dgy * sz[:, sl]
            dz_ref[:, sl] = (dgy * yv[:, sl] * (sg[:, sl] * (1.0 + zv[:, sl] * (1.0 - sg[:, sl])))).astype(BF16)

    return pl.pallas_call(
        body, name=name, grid=(T // tr,),
        in_specs=[pl.BlockSpec((tr, inner), _rows), pl.BlockSpec((tr, inner), _rows), pl.BlockSpec((tr, inner), _rows),
                  pl.BlockSpec((1, inner), _row0)],
        out_specs=[pl.BlockSpec((tr, inner), _rows), pl.BlockSpec((tr, inner), _rows), pl.BlockSpec((1, inner), _row0)],
        out_shape=[_sds((T, inner), F32), _sds((T, inner), BF16), _sds((1, inner), F32)],
        compiler_params=_cp("arbitrary"))(dyn, y, proj, ng)


def _place():
    return lax.axis_index("x"), lax.axis_index("y"), lax.axis_index("c")


def _other_chips(x, y):
    return [(1 - x, y), (x, 1 - y), (1 - x, 1 - y)]


def _remote(src, dst, ssem, rsem, dev):
    return pltpu.make_async_remote_copy(src_ref=src, dst_ref=dst, send_sem=ssem, recv_sem=rsem, device_id=dev,
                                        device_id_type=MESH)


def _all_gather_chips(name, shard):
    R, C = shard.shape
    hr = R // 2

    def body(x_ref, o_ref, ssem, rsem, lsem):
        x, y, c = _place()
        k = 2 * x + y
        sib = (x, y, 1 - c)
        chips = _other_chips(x, y)

        def half(blk, cc):
            return o_ref.at[blk, pl.ds(cc * hr, hr), :]

        local = pltpu.make_async_copy(x_ref, o_ref.at[k], lsem)
        local.start()
        first = [_remote(x_ref.at[pl.ds(c * hr, hr), :], half(k, c), ssem.at[r], rsem.at[r], (px, py, c))
                 for r, (px, py) in enumerate(chips)]
        for cp in first:
            cp.start()
        passed = []
        for r, (px, py) in enumerate(chips):
            kj = 2 * px + py
            _remote(half(kj, c), half(kj, c), ssem.at[r], rsem.at[r], (px, py, c)).wait_recv()
            fw = _remote(half(kj, c), half(kj, c), ssem.at[3 + r], rsem.at[3 + r], sib)
            fw.start()
            passed.append(fw)
        for r, (px, py) in enumerate(chips):
            kj = 2 * px + py
            _remote(half(kj, 1 - c), half(kj, 1 - c), ssem.at[3 + r], rsem.at[3 + r], sib).wait_recv()
        for cp in first + passed:
            cp.wait_send()
        local.wait()

    return pl.pallas_call(
        body, name=name, in_specs=[_ANY], out_specs=_ANY, out_shape=_sds((4, R, C), shard.dtype),
        scratch_shapes=[pltpu.SemaphoreType.DMA((6,)), pltpu.SemaphoreType.DMA((6,)), pltpu.SemaphoreType.DMA],
    )(shard)


_HBM = pl.BlockSpec(memory_space=pltpu.HBM)
_SEM = pl.BlockSpec(memory_space=pltpu.SEMAPHORE)
_EFFECT = pltpu.SideEffectType.DATAFLOW_SIDE_EFFECTING


def _hbm(a):
    return pltpu.with_memory_space_constraint(a, pltpu.HBM)


NORTH = 1


def _cast_into_block(name, w, layer, after):
    _, R, C = w.shape
    tr = _tile(R, 256, 16)

    def body(w_ref, a_ref, o_ref):
        o_ref[...] = w_ref[...].astype(BF16)

    return pl.pallas_call(
        body, name=name, grid=(R // tr,),
        in_specs=[pl.BlockSpec((None, tr, C), lambda i: (layer, i, 0)), _ANY],
        out_specs=pl.BlockSpec((None, tr, C), lambda i: (2 * lax.axis_index("x") + lax.axis_index("y"), i, 0)),
        out_shape=_sds((4, R, C), BF16), compiler_params=_cp("parallel"))(w, after)


def _gather_copies(refs, ssem, rsem):
    x, y, _ = _place()
    k = 2 * x + y
    sends, arrivals = [], []
    for a, lr in enumerate(refs):
        for r, (px, py) in enumerate(_other_chips(x, y)):
            i = 3 * a + r
            sends.append(_remote(lr.at[k], lr.at[k], ssem.at[i], rsem.at[i], (px, py, NORTH)))
            arrivals.append(_remote(lr.at[k], lr.at[2 * px + py], ssem.at[i], rsem.at[i], (px, py, NORTH)))
    return sends, arrivals


def _scatter_copies(refs, ssem, rsem):
    x, y, c = _place()
    k = 2 * x + y
    n = len(refs) // 2
    sends, arrivals = [], []
    for a, (xr, lr) in enumerate(zip(refs[:n], refs[n:])):
        for r, (px, py) in enumerate(_other_chips(x, y)):
            i = 3 * a + r
            kj = 2 * px + py
            sends.append(_remote(xr.at[kj], lr.at[k], ssem.at[i], rsem.at[i], (px, py, c)))
            arrivals.append(_remote(xr.at[kj], lr.at[kj], ssem.at[i], rsem.at[i], (px, py, c)))
    return sends, arrivals


def _on_cores(north_only, fn):
    if north_only:
        pl.when(lax.axis_index("c") == NORTH)(fn)
    else:
        fn()


def _split_start(name, bufs, ncopies, copies_fn, north_only=False):
    n = len(bufs)

    def body(*refs):
        def go():
            sends, _ = copies_fn(refs[:n], refs[n], refs[n + 1])
            for cp in sends:
                cp.start()

        _on_cores(north_only, go)
        token = refs[-1]
        token[...] = jnp.zeros_like(token)

    outs = pl.pallas_call(
        body, name=name, in_specs=[_HBM] * n,
        out_specs=(_SEM, _SEM) + (_HBM,) * n + (pl.BlockSpec(memory_space=pltpu.VMEM),),
        out_shape=(pltpu.SemaphoreType.DMA((ncopies,)), pltpu.SemaphoreType.DMA((ncopies,)))
        + tuple(pltpu.HBM(b.shape, b.dtype) for b in bufs) + (_sds((8, LANES), F32),),
        input_output_aliases={i: 2 + i for i in range(n)},
        compiler_params=pltpu.CompilerParams(has_side_effects=_EFFECT),
    )(*[_hbm(b) for b in bufs])
    return outs[0], outs[1], list(outs[2:2 + n]), outs[-1]


def _split_wait(name, started, copies_fn, after, north_only=False):
    ssem, rsem, bufs, _ = started
    n = len(bufs)

    def body(*refs):
        def go():
            sends, arrivals = copies_fn(refs[:n], refs[n], refs[n + 1])
            for cp in sends:
                cp.wait_send()
            for cp in arrivals:
                cp.wait_recv()

        _on_cores(north_only, go)

    outs = pl.pallas_call(
        body, name=name, in_specs=[_HBM] * n + [_SEM, _SEM, _ANY], out_specs=(_HBM,) * n,
        out_shape=tuple(pltpu.HBM(t.shape, t.dtype) for t in bufs),
        input_output_aliases={i: i for i in range(n)},
        compiler_params=pltpu.CompilerParams(has_side_effects=_EFFECT),
    )(*bufs, ssem, rsem, after)
    return list(outs)


def _share_sibling(name, lands):
    n = len(lands)

    def body(*refs):
        l_in, l_out = refs[:n], refs[n:2 * n]
        ssem, rsem = refs[2 * n:]
        x, y, c = _place()

        def hand_on(a, r, px, py):
            blk = 2 * px + py
            return _remote(l_in[a].at[blk], l_out[a].at[blk], ssem.at[3 * a + r], rsem.at[3 * a + r], (x, y, 1 - NORTH))

        @pl.when(c == NORTH)
        def _():
            copies = [hand_on(a, r, px, py) for a in range(n) for r, (px, py) in enumerate(_other_chips(x, y))]
            for cp in copies:
                cp.start()
            for cp in copies:
                cp.wait_send()

        @pl.when(c != NORTH)
        def _():
            for a in range(n):
                for r, (px, py) in enumerate(_other_chips(x, y)):
                    hand_on(a, r, px, py).wait_recv()

    outs = pl.pallas_call(
        body, name=name, in_specs=[_ANY] * n, out_specs=[_ANY] * n,
        out_shape=[_sds(l.shape, l.dtype) for l in lands], input_output_aliases={i: i for i in range(n)},
        scratch_shapes=[pltpu.SemaphoreType.DMA((3 * n,)), pltpu.SemaphoreType.DMA((3 * n,))],
    )(*lands)
    return list(outs)


def _swap_halves(name, fulls, after=None):
    n = len(fulls)
    extra = [] if after is None else [after]

    def body(*refs):
        g_refs, o_refs = refs[:n], refs[n + len(extra):2 * n + len(extra)]
        ssem, rsem = refs[2 * n + len(extra):]
        x, y, c = _place()
        copies = []
        for a in range(n):
            hr = g_refs[a].shape[1] // 2
            cp = _remote(g_refs[a].at[:, pl.ds((1 - c) * hr, hr), :], o_refs[a], ssem.at[a], rsem.at[a], (x, y, 1 - c))
            cp.start()
            copies.append(cp)
        for cp in copies:
            cp.wait()

    outs = pl.pallas_call(
        body, name=name, in_specs=[_ANY] * (n + len(extra)), out_specs=[_ANY] * n,
        out_shape=[_sds((f.shape[0], f.shape[1] // 2, f.shape[2]), f.dtype) for f in fulls],
        scratch_shapes=[pltpu.SemaphoreType.DMA((n,)), pltpu.SemaphoreType.DMA((n,))],
    )(*fulls, *extra)
    return list(outs)


def _add_halves(name, full, other):
    nb, R, C = full.shape
    hr = R // 2
    tr = _tile(hr, 256, 16)
    nh = hr // tr

    def body(f_ref, o_ref, s_ref):
        s_ref[...] = (f_ref[...].astype(F32) + o_ref[...].astype(F32)).astype(s_ref.dtype)

    return pl.pallas_call(
        body, name=name, grid=(nb, nh),
        in_specs=[pl.BlockSpec((None, tr, C), lambda b, i: (b, lax.axis_index("c") * nh + i, 0)),
                  pl.BlockSpec((None, tr, C), lambda b, i: (b, i, 0))],
        out_specs=pl.BlockSpec((None, tr, C), lambda b, i: (b, i, 0)), out_shape=_sds((nb, hr, C), full.dtype),
        compiler_params=_cp("parallel", "parallel"))(full, other)


def _sum_owner(name, land, mine):
    nb, hr, C = land.shape
    tr = _tile(hr, 128, 16)

    def body(l_ref, m_ref, o_ref):
        k = 2 * lax.axis_index("x") + lax.axis_index("y")
        own = m_ref[...].astype(F32)
        acc = jnp.where(k == 0, own, l_ref[0].astype(F32))
        for j in range(1, nb):
            acc = acc + jnp.where(k == j, own, l_ref[j].astype(F32))
        o_ref[...] = acc

    return pl.pallas_call(
        body, name=name, grid=(hr // tr,),
        in_specs=[pl.BlockSpec((nb, tr, C), lambda i: (0, i, 0)),
                  pl.BlockSpec((None, tr, C), lambda i: (2 * lax.axis_index("x") + lax.axis_index("y"), i, 0))],
        out_specs=pl.BlockSpec((None, tr, C), lambda i: (lax.axis_index("c"), i, 0)),
        out_shape=_sds((2, hr, C), F32), compiler_params=_cp("parallel"))(land, mine)


def _join_halves(name, bufs):
    n = len(bufs)

    def body(*refs):
        i_refs, o_refs = refs[:n], refs[n:2 * n]
        ssem, rsem = refs[2 * n:]
        x, y, c = _place()
        copies = []
        for a in range(n):
            cp = _remote(i_refs[a].at[c], o_refs[a].at[c], ssem.at[a], rsem.at[a], (x, y, 1 - c))
            cp.start()
            copies.append(cp)
        for a, cp in enumerate(copies):
            _remote(i_refs[a].at[c], o_refs[a].at[1 - c], ssem.at[a], rsem.at[a], (x, y, 1 - c)).wait_recv()
            cp.wait_send()

    outs = pl.pallas_call(
        body, name=name, in_specs=[_ANY] * n, out_specs=[_ANY] * n,
        out_shape=[_sds(b.shape, b.dtype) for b in bufs], input_output_aliases={i: i for i in range(n)},
        scratch_shapes=[pltpu.SemaphoreType.DMA((n,)), pltpu.SemaphoreType.DMA((n,))],
    )(*bufs)
    return list(outs)


def _all_gather_devices(name, buf):
    def body(b_ref, o_ref, ssem, rsem, lsem):
        x, y, c = _place()
        me = 4 * x + 2 * y + c
        flips = [(fx, fy, fc) for fx in (0, 1) for fy in (0, 1) for fc in (0, 1) if fx or fy or fc]
        peers = [((1 - x) if fx else x, (1 - y) if fy else y, (1 - c) if fc else c) for fx, fy, fc in flips]
        local = pltpu.make_async_copy(b_ref, o_ref.at[me], lsem)
        local.start()
        sends = [_remote(b_ref, o_ref.at[me], ssem.at[r], rsem.at[r], p) for r, p in enumerate(peers)]
        for cp in sends:
            cp.start()
        for r, (px, py, pc) in enumerate(peers):
            pid = 4 * px + 2 * py + pc
            _remote(b_ref, o_ref.at[pid], ssem.at[r], rsem.at[r], (px, py, pc)).wait_recv()
        for cp in sends:
            cp.wait_send()
        local.wait()

    return pl.pallas_call(
        body, name=name, in_specs=[_ANY], out_specs=_ANY, out_shape=_sds((8,) + buf.shape, buf.dtype),
        scratch_shapes=[pltpu.SemaphoreType.DMA((7,)), pltpu.SemaphoreType.DMA((7,)), pltpu.SemaphoreType.DMA],
    )(buf)


def _sum_blocks(name, parts):
    n, R, C = parts.shape
    tr = _tile(R, 256, 8)

    def body(p_ref, o_ref):
        acc = p_ref[0].astype(F32)
        for j in range(1, n):
            acc = acc + p_ref[j].astype(F32)
        o_ref[...] = acc

    return pl.pallas_call(
        body, name=name, grid=(R // tr,), in_specs=[pl.BlockSpec((n, tr, C), lambda i: (0, i, 0))],
        out_specs=pl.BlockSpec((tr, C), _rows), out_shape=_sds((R, C), F32), compiler_params=_cp("parallel"))(parts)


def _adamw(name, w, m, v, parts, tok=None):
    L, R, C = w.shape
    np_ = len(parts[0])
    tr = _tile(R, max(8, (VMEM_LIMIT // 2) // (2 * 4 * C * (7 + L * np_))), 8)
    flat = [p for lp in parts for p in lp]
    if tok is not None:
        flat = flat + [tok]
    c1 = 1.0 / (1.0 - ADAM_B1 ** ADAM_STEP)
    c2 = 1.0 / (1.0 - ADAM_B2 ** ADAM_STEP)

    def body(*refs):
        w_ref, m_ref, v_ref = refs[:3]
        p_refs = refs[3:3 + L * np_]
        g_ref, d_ref, nm_ref, nv_ref = refs[-4:]
        layer = pl.program_id(0)
        g = jnp.zeros((tr, C), F32)
        for l in range(L):
            gl = p_refs[l * np_][...]
            for j in range(1, np_):
                gl = gl + p_refs[l * np_ + j][...]
            g = jnp.where(layer == l, gl, g) if L > 1 else gl
        if tok is not None:
            g = g + refs[3 + L * np_][...]
        nm = ADAM_B1 * m_ref[...] + (1.0 - ADAM_B1) * g
        nv = ADAM_B2 * v_ref[...] + (1.0 - ADAM_B2) * (g * g)
        g_ref[...] = g
        nm_ref[...] = nm
        nv_ref[...] = nv
        d_ref[...] = -ADAM_LR * ((nm * c1) / (jnp.sqrt(nv * c2) + ADAM_EPS) + ADAM_WD * w_ref[...])

    stacked = pl.BlockSpec((None, tr, C), lambda l, i: (l, i, 0))
    part_specs = [pl.BlockSpec((tr, C), (lambda l, i, ll=ll: (jnp.where(l == ll, i, 0), 0)))
                  for ll in range(L) for _ in range(np_)]
    if tok is not None:
        part_specs.append(pl.BlockSpec((1, 1), lambda l, i: (0, 0)))
    return pl.pallas_call(
        body, name=name, grid=(L, R // tr), in_specs=[stacked] * 3 + part_specs, out_specs=[stacked] * 4,
        out_shape=[_sds(w.shape, F32)] * 4, compiler_params=_cp("arbitrary", "arbitrary"))(w, m, v, *flat)


_PACK_ROWS = 16


def _pack(arrs):
    pieces = []
    for a in arrs:
        f = a.reshape(-1).astype(F32)
        unit = _PACK_ROWS * LANES
        pad = (-f.shape[0]) % unit
        pieces.append(jnp.pad(f, (0, pad)))
    return jnp.concatenate(pieces).reshape(-1, LANES)


def _unpack(buf, shapes):
    flat = buf.reshape(-1)
    out, off = [], 0
    unit = _PACK_ROWS * LANES
    for s in shapes:
        n = 1
        for d in s:
            n *= d
        out.append(flat[off:off + n].reshape(s))
        off += n + ((-n) % unit)
    return out


def _reduce_start(tag, fulls, after=None):
    others = _swap_halves("rs_swap", fulls, after)
    sums = [_add_halves("rs_add", f, o) for f, o in zip(fulls, others)]
    lands = [lax.empty(t.shape, t.dtype) for t in sums]
    return _split_start("rs_start_" + tag, sums + lands, 3 * len(sums), _scatter_copies)


def _reduce_finish(tag, started, after):
    bufs = _split_wait("rs_wait_" + tag, started, _scatter_copies, after)
    n = len(bufs) // 2
    halves = [_sum_owner("rs_sum", l, s) for s, l in zip(bufs[:n], bufs[n:])]
    return [j.reshape(2 * j.shape[1], j.shape[2]) for j in _join_halves("rs_join", halves)]


def kernel(x, ln_ffn_pre, ffn_pre_w_in, ffn_pre_w_out, ln_mix, ln_ffn_post, ffn_post_w_in, ffn_post_w_out, gm_w_in, gm_v_norm, gm_w_s, gm_b_s, gm_w_out, ssm_w_in, ssm_conv_w, ssm_conv_b, ssm_dt_bias, ssm_a_log, ssm_d, ssm_norm, ssm_w_out, ln_final, loss_target, m_ln_ffn_pre, m_ffn_pre_w_in, m_ffn_pre_w_out, m_ln_mix, m_ln_ffn_post, m_ffn_post_w_in, m_ffn_post_w_out, m_gm_w_in, m_gm_v_norm, m_gm_w_s, m_gm_b_s, m_gm_w_out, m_ssm_w_in, m_ssm_conv_w, m_ssm_conv_b, m_ssm_dt_bias, m_ssm_a_log, m_ssm_d, m_ssm_norm, m_ssm_w_out, m_ln_final, v_ln_ffn_pre, v_ffn_pre_w_in, v_ffn_pre_w_out, v_ln_mix, v_ln_ffn_post, v_ffn_post_w_in, v_ffn_post_w_out, v_gm_w_in, v_gm_v_norm, v_gm_w_s, v_gm_b_s, v_gm_w_out, v_ssm_w_in, v_ssm_conv_w, v_ssm_conv_b, v_ssm_dt_bias, v_ssm_a_log, v_ssm_d, v_ssm_norm, v_ssm_w_out, v_ln_final):
    T, D = x.shape[1], x.shape[2]
    depth = ln_ffn_pre.shape[0]
    n_gm, n_ssm = gm_w_in.shape[0], ssm_w_in.shape[0]
    F = ffn_pre_w_out.shape[1] * 4
    GI = gm_w_out.shape[1] * 4
    GG = gm_w_s.shape[1]
    inner = ssm_w_out.shape[1] * 4
    heads = ssm_dt_bias.shape[1]
    cd = ssm_conv_w.shape[1] * 4
    groups = (cd - inner) // (2 * SSM_STATE)
    pshard = ssm_w_in.shape[2]
    pw = inner + cd + LANES
    dtcol = (inner + cd) // LANES
    kchip = 2 * lax.axis_index("x") + lax.axis_index("y")

    small_shard = _pack([jnp.swapaxes(ssm_conv_w, 1, 2), ssm_conv_b, ssm_norm])
    small_all = _all_gather_chips("ag_small", small_shard)
    cq, iq = cd // 4, inner // 4
    parts = [_unpack(small_all[k], [(n_ssm, SSM_CONV, cq), (n_ssm, cq), (n_ssm, iq)]) for k in range(4)]
    conv_wt = jnp.concatenate([p[0] for p in parts], axis=2)
    conv_b = jnp.concatenate([p[1] for p in parts], axis=1)
    norm_g = jnp.concatenate([p[2] for p in parts], axis=1)

    def pad_lanes(v):
        return jnp.pad(v, (0, LANES - v.shape[0]))[None, :]

    xc = x[0]
    saved = []

    def shards_of(i, kind):
        if kind == "pre":
            return [(ffn_pre_w_in, i), (ffn_pre_w_out, i)]
        if kind == "post":
            return [(ffn_post_w_in, i), (ffn_post_w_out, i)]
        if i % 2 == 0:
            return [(gm_w_in, i // 2), (gm_w_out, i // 2)]
        return [(ssm_w_in, i // 2), (ssm_w_out, i // 2)]

    subs = [(i, kind) for i in range(depth) for kind in ("pre", "mix", "post")]
    nsub = len(subs)
    gathers = [None] * nsub
    ahead = 2

    def gather_start(s, after):
        lands = [_cast_into_block("ag_cast", w, l, after) for w, l in shards_of(*subs[s])]
        gathers[s] = _split_start("ag_start_%d" % s, lands, 3 * len(lands), _gather_copies, north_only=True)

    def gathered(s, after):
        lands = _split_wait("ag_wait_%d" % s, gathers[s], _gather_copies, after, north_only=True)
        full = _share_sibling("ag_share", lands)
        if s + ahead < nsub:
            gather_start(s + ahead, after)
            return full, gathers[s + ahead][3][0:1, 0:1]
        return full, jnp.zeros((1, 1), F32)

    for s in range(min(ahead, nsub)):
        gather_start(s, xc)

    def ffn_fwd(s, xin, g, l):
        (wi, wo), tok = gathered(s, xin)
        wo = wo.reshape(F, D)
        h = _rms_fwd("rms_fwd", xin, g[l][None, :] + tok)
        a = _matmul("ffn_in", h, wi, "nn", BF16, out_blocks=2)
        s_ = _swiglu_fwd("swiglu_fwd", a)
        xo = _matmul("ffn_out", s_, wo, "nn", F32, res=xin, scale=0.5)
        return xo, (xin, h, a, s_, wi, wo)

    for i in range(depth):
        xc, sv_pre = ffn_fwd(3 * i, xc, ln_ffn_pre, i)
        j = i // 2
        (wi, wo), tok = gathered(3 * i + 1, xc)
        h = _rms_fwd("rms_fwd", xc, ln_mix[i][None, :] + tok)
        if i % 2 == 0:
            wo = wo.reshape(GI, D)
            zpre = _matmul("gm_in", h, wi, "nn", BF16, out_blocks=2)
            u, vn = _gm_act_fwd("gm_act_fwd", zpre, gm_v_norm[j][None, :])
            bt = jnp.pad(gm_b_s[j].T, ((0, 0), (0, LANES - GG)))
            gated = _gm_spatial_fwd("gm_spatial_fwd", u, vn, gm_w_s[j], bt)
            xn = _matmul("mix_out", gated, wo, "nn", F32, res=xc, scale=1.0)
            sv_mix = (xc, h, zpre, u, vn, bt, gated, wi, wo)
        else:
            wg, wo = wi, wo.reshape(inner, D)
            wp = jnp.pad(jnp.swapaxes(wg, 0, 1).reshape(D, 4 * pshard), ((0, 0), (0, pw - 4 * pshard)))
            proj = _matmul("ssm_in", h, wp, "nn", F32)
            wt, cb_ = conv_wt[j], conv_b[j][None, :]
            act = _ssd_conv_fwd("ssd_conv_fwd", proj, wt, cb_, inner, cd)
            bias, alog = pad_lanes(ssm_dt_bias[j]), pad_lanes(ssm_a_log[j])
            dtp, acum = _ssd_dt_fwd("ssd_dt_fwd", proj, bias, alog, dtcol, heads)
            dexp = jnp.repeat(ssm_d[j], SSM_HEAD_DIM)[None, :]
            ycore, sprev = _ssd_core_fwd("ssd_core_fwd", act, dtp, acum, dexp, inner, groups)
            ng = norm_g[j][None, :]
            yn = _ssd_post_fwd("ssd_post_fwd", ycore, proj, ng, inner, groups)
            xn = _matmul("mix_out", yn, wo, "nn", F32, res=xc, scale=1.0)
            sv_mix = (xc, h, proj, act, dtp, acum, dexp, ycore, sprev, yn, wt, cb_, bias, alog, ng, wp, wo)
        xc = xn
        xc, sv_post = ffn_fwd(3 * i + 2, xc, ln_ffn_post, i)
        saved.append((sv_pre, sv_mix, sv_post))

    loss_tile, dx, dxb, dg_final = _loss_head("loss_head", xc, ln_final[None, :], loss_target[0])
    loss = lax.psum(loss_tile[0, 0], ("x", "y", "c"))

    wgrad = [None] * nsub
    in_flight = []
    deferred = []

    def reduce_finish(after):
        while in_flight:
            s, st = in_flight.pop(0)
            wgrad[s] = _reduce_finish(str(s), st, after)

    def reduce_start(s, fulls):
        reduce_finish(fulls[0])
        if s == 0:
            deferred.append(fulls)
            return jnp.zeros((1, 1), F32)
        st = _reduce_start(str(s), fulls)
        in_flight.append((s, st))
        return st[3][0:1, 0:1]

    def ffn_bwd(s, dx, dxb, sv, g, l):
        xin, h, a, s_, wi, wo = sv
        ds = _matmul("ffn_ds", dxb, wo, "nt", BF16, scale=0.5)
        dwo = _matmul("ffn_dwo", s_, dxb, "tn", BF16, scale=0.5)
        da = _swiglu_bwd("swiglu_bwd", a, ds)
        dh = _matmul("ffn_dh", da, wi, "nt", F32)
        dwi = _matmul("ffn_dwi", h, da, "tn", BF16, out_blocks=4)
        tok = reduce_start(s, [dwi, dwo.reshape(4, F // 4, D)])
        return _rms_bwd("rms_bwd", xin, g[l][None, :] + tok, dh, dx)

    gl = {n: [None] * depth for n in ("pre", "mix", "post")}
    gm_g = {n: [None] * n_gm for n in ("vnorm", "ws", "bs")}
    ssm_g = {n: [None] * n_ssm for n in ("convw", "convb", "dtb", "alog", "d", "norm")}

    for i in reversed(range(depth)):
        sv_pre, sv_mix, sv_post = saved[i]
        j = i // 2
        dx, dxb, gl["post"][i] = ffn_bwd(3 * i + 2, dx, dxb, sv_post, ln_ffn_post, i)
        if i % 2 == 0:
            xin, h, zpre, u, vn, bt, gated, wi, wo = sv_mix
            dgated = _matmul("mix_dy", dxb, wo, "nt", BF16)
            dwo = _matmul("mix_dwo", gated, dxb, "tn", BF16)
            du, dvn, dws, dbt = _gm_spatial_bwd("gm_spatial_bwd", dgated, u, vn, gm_w_s[j], bt)
            dzpre, dvnorm = _gm_act_bwd("gm_act_bwd", zpre, du, dvn, gm_v_norm[j][None, :])
            dh = _matmul("gm_dh", dzpre, wi, "nt", F32)
            dwi = _matmul("gm_dwi", h, dzpre, "tn", BF16, out_blocks=4)
            tok = reduce_start(3 * i + 1, [dwi, dwo.reshape(4, GI // 4, D)])
            gm_g["vnorm"][j], gm_g["ws"][j], gm_g["bs"][j] = dvnorm[0], dws, dbt.T[:GG]
        else:
            xin, h, proj, act, dtp, acum, dexp, ycore, sprev, yn, wt, cb_, bias, alog, ng, wp, wo = sv_mix
            dyn = _matmul("ssm_dy", dxb, wo, "nt", F32)
            dwo = _matmul("mix_dwo", yn, dxb, "tn", BF16)
            dyc, dz, dnorm = _ssd_post_bwd("ssd_post_bwd", dyn, ycore, proj, ng, inner, groups)
            dxs, db_, dc_, ddtp_g, dac_g, dd_g = _ssd_core_bwd("ssd_core_bwd", dyc, act, dtp, acum, dexp, sprev,
                                                               inner, groups)
            ddt, dbias, dalog, dds = _ssd_dt_bwd("ssd_dt_bwd", ddtp_g, dac_g, dd_g, proj, bias, alog, dtp, dtcol, heads)
            dact = jnp.concatenate([dxs, db_, dc_], axis=1)
            dxbc, dwt, dcb = _ssd_conv_bwd("ssd_conv_bwd", dact, proj, wt, cb_, inner, cd)
            dproj = jnp.concatenate([dz, dxbc, ddt], axis=1)
            dh = _matmul("ssm_dh", dproj, wp, "nt", F32)
            dwp = _matmul("ssm_dwi", h, dproj, "tn", BF16)
            dwi = jnp.swapaxes(dwp[:, :4 * pshard].reshape(D, 4, pshard), 0, 1)
            tok = reduce_start(3 * i + 1, [dwi, dwo.reshape(4, inner // 4, D)])
            ssm_g["convw"][j], ssm_g["convb"][j] = dwt.T, dcb[0]
            ssm_g["dtb"][j], ssm_g["alog"][j], ssm_g["d"][j] = dbias[0, :heads], dalog[0, :heads], dds[0, :heads]
            ssm_g["norm"][j] = dnorm[0]
        dx, dxb, gl["mix"][i] = _rms_bwd("rms_bwd", xin, ln_mix[i][None, :] + tok, dh, dx)
        dx, dxb, gl["pre"][i] = ffn_bwd(3 * i, dx, dxb, sv_pre, ln_ffn_pre, i)

    small_full = [
        jnp.concatenate(gl["pre"], 0), jnp.concatenate(gl["mix"], 0), jnp.concatenate(gl["post"], 0),
        jnp.stack(gm_g["vnorm"]), jnp.stack(gm_g["ws"]), jnp.stack(gm_g["bs"]),
        jnp.stack(ssm_g["convw"]), jnp.stack(ssm_g["convb"]), jnp.stack(ssm_g["dtb"]), jnp.stack(ssm_g["alog"]),
        jnp.stack(ssm_g["d"]), jnp.stack(ssm_g["norm"]), dg_final[0],
    ]
    full_shapes = [tuple(a.shape) for a in small_full]
    gathered = _all_gather_devices("ag8_small", _pack(small_full))
    summed = _sum_blocks("sum8_small", gathered)
    (g_pre, g_mix, g_post, g_vn, g_ws, g_bs, g_cw, g_cb, g_dtb, g_al, g_d, g_nm, g_fin) = _unpack(summed, full_shapes)
    g_cw = lax.dynamic_slice_in_dim(g_cw, kchip * cq, cq, axis=1)
    g_cb = lax.dynamic_slice_in_dim(g_cb, kchip * cq, cq, axis=1)
    g_nm = lax.dynamic_slice_in_dim(g_nm, kchip * iq, iq, axis=1)
    sm_g = [g_pre, g_mix, g_post, g_vn, g_ws, g_bs, g_cw, g_cb, g_dtb, g_al, g_d, g_nm, g_fin]
    sm_w = [ln_ffn_pre, ln_mix, ln_ffn_post, gm_v_norm, gm_w_s, gm_b_s, ssm_conv_w, ssm_conv_b, ssm_dt_bias,
            ssm_a_log, ssm_d, ssm_norm, ln_final]
    sm_m = [m_ln_ffn_pre, m_ln_mix, m_ln_ffn_post, m_gm_v_norm, m_gm_w_s, m_gm_b_s, m_ssm_conv_w, m_ssm_conv_b,
            m_ssm_dt_bias, m_ssm_a_log, m_ssm_d, m_ssm_norm, m_ln_final]
    sm_v = [v_ln_ffn_pre, v_ln_mix, v_ln_ffn_post, v_gm_v_norm, v_gm_w_s, v_gm_b_s, v_ssm_conv_w, v_ssm_conv_b,
            v_ssm_dt_bias, v_ssm_a_log, v_ssm_d, v_ssm_norm, v_ln_final]
    sm_shapes = [tuple(a.shape) for a in sm_w]
    pk = [_pack(lst)[None] for lst in (sm_w, sm_m, sm_v)]
    sg_, sd_, snm_, snv_ = _adamw("adamw_small", pk[0], pk[1], pk[2], [[_pack(sm_g)]])
    small_out = [_unpack(t[0], sm_shapes) for t in (sg_, sd_, snm_, snv_)]
    small_names = ["ln_ffn_pre", "ln_mix", "ln_ffn_post", "gm_v_norm", "gm_w_s", "gm_b_s", "ssm_conv_w", "ssm_conv_b",
                   "ssm_dt_bias", "ssm_a_log", "ssm_d", "ssm_norm", "ln_final"]

    def big_update(tag, w, m, v, sub_ids, which, tok=None):
        L = w.shape[0]
        shp = w.shape
        w2, m2, v2 = (t.reshape(L, -1, shp[-1]) for t in (w, m, v))
        outs = _adamw("adamw_" + tag, w2, m2, v2, [[wgrad[s][which]] for s in sub_ids], tok)
        return [o.reshape(shp) for o in outs]

    last = _reduce_start("0", deferred[0], after=sd_)
    tok = last[3][0:1, 0:1]
    pre_ids = [3 * i for i in range(depth)]
    post_ids = [3 * i + 2 for i in range(depth)]
    gm_ids = [3 * i + 1 for i in range(depth) if i % 2 == 0]
    ssm_ids = [3 * i + 1 for i in range(depth) if i % 2 == 1]
    big_out = {
        "ffn_post_w_in": big_update("ffn_in", ffn_post_w_in, m_ffn_post_w_in, v_ffn_post_w_in, post_ids, 0, tok),
        "ffn_post_w_out": big_update("ffn_out", ffn_post_w_out, m_ffn_post_w_out, v_ffn_post_w_out, post_ids, 1, tok),
        "gm_w_in": big_update("gm_in", gm_w_in, m_gm_w_in, v_gm_w_in, gm_ids, 0, tok),
        "gm_w_out": big_update("mix_out", gm_w_out, m_gm_w_out, v_gm_w_out, gm_ids, 1, tok),
        "ssm_w_in": big_update("ssm_in", ssm_w_in, m_ssm_w_in, v_ssm_w_in, ssm_ids, 0, tok),
        "ssm_w_out": big_update("mix_out", ssm_w_out, m_ssm_w_out, v_ssm_w_out, ssm_ids, 1, tok),
    }
    after = big_out["ffn_post_w_in"][1][0, 0:1, 0:1]
    for n in ("ffn_post_w_out", "gm_w_in", "gm_w_out", "ssm_w_in", "ssm_w_out"):
        after = after + big_out[n][1][0, 0:1, 0:1]
    wgrad[0] = _reduce_finish("0", last, after)
    big_out["ffn_pre_w_in"] = big_update("ffn_in", ffn_pre_w_in, m_ffn_pre_w_in, v_ffn_pre_w_in, pre_ids, 0)
    big_out["ffn_pre_w_out"] = big_update("ffn_out", ffn_pre_w_out, m_ffn_pre_w_out, v_ffn_pre_w_out, pre_ids, 1)

    order = ["ln_ffn_pre", "ffn_pre_w_in", "ffn_pre_w_out", "ln_mix", "ln_ffn_post", "ffn_post_w_in", "ffn_post_w_out",
             "gm_w_in", "gm_v_norm", "gm_w_s", "gm_b_s", "gm_w_out", "ssm_w_in", "ssm_conv_w", "ssm_conv_b",
             "ssm_dt_bias", "ssm_a_log", "ssm_d", "ssm_norm", "ssm_w_out", "ln_final"]

    def pick(kind, n):
        if n in big_out:
            return big_out[n][kind]
        return small_out[kind][small_names.index(n)]

    outs = [loss, dx[None]]
    for kind in range(4):
        outs.extend(pick(kind, n) for n in order)
    return tuple(outs)
```

```python
import jax
import jax.numpy as jnp
from jax import lax
from jax.experimental import pallas as pl
from jax.experimental.pallas import tpu as pltpu

F32 = jnp.float32
BF16 = jnp.bfloat16
HIGHEST = lax.Precision.HIGHEST
MESH = pl.DeviceIdType.MESH

EPS = 1e-6
ADAM_LR, ADAM_B1, ADAM_B2, ADAM_EPS, ADAM_WD, ADAM_STEP = 0.001, 0.9, 0.999, 1e-08, 0.01, 10

LANES = 128
CHUNK = 128
SSM_STATE = 128
SSM_HEAD_DIM = 64
SSM_HPG = 8
SSM_CONV = 4
VMEM_LIMIT = 56 * 1024 * 1024
MM_TILE = 1408
MM_VMEM_BUDGET = 46 * 1024 * 1024

_ANY = pl.BlockSpec(memory_space=pl.ANY)


def _cp(*sem):
    return pltpu.CompilerParams(dimension_semantics=sem if sem else None, vmem_limit_bytes=VMEM_LIMIT)


def _tile(n, target, mult=LANES):
    best = None
    t = mult
    while t <= min(n, target):
        if n % t == 0:
            best = t
        t += mult
    return n if best is None else best


def _gcd(*v):
    import math
    g = 0
    for a in v:
        g = math.gcd(g, a)
    return g


def _sds(shape, dtype):
    return jax.ShapeDtypeStruct(tuple(shape), dtype)


def _ldims(shape):
    return tuple(shape) if len(shape) == 2 else (shape[1], shape[0] * shape[2])


def _colblock(shape):
    return None if len(shape) == 2 else shape[2]


def _mspec(shape, tr, tc, rc):
    if len(shape) == 2:
        return pl.BlockSpec((tr, tc), rc)
    per = shape[2] // tc

    def im(i, j, k):
        r, c = rc(i, j, k)
        return (c // per, r, c % per)

    return pl.BlockSpec((None, tr, tc), im)


def _matmul(name, a, b, mode, out_dtype, out_blocks=None, res=None, scale=1.0):
    la, lb = _ldims(a.shape), _ldims(b.shape)
    if mode == "nn":
        (M, K), (K2, N) = la, lb
    elif mode == "nt":
        (M, K), (N, K2) = la, lb
    else:
        (K, M), (K2, N) = la, lb
    assert K == K2, (name, a.shape, b.shape, mode)
    ca, cb = _colblock(a.shape), _colblock(b.shape)
    out_shape = (M, N) if out_blocks is None else (out_blocks, M, N // out_blocks)
    co = _colblock(out_shape)
    m_c, n_c, k_c = [M], [N], [K]
    if ca is not None:
        (m_c if mode == "tn" else k_c).append(ca)
    if cb is not None:
        (k_c if mode == "nt" else n_c).append(cb)
    if co is not None:
        n_c.append(co)
    tm, tn = _tile(_gcd(*m_c), MM_TILE), _tile(_gcd(*n_c), MM_TILE)
    out_bytes = jnp.dtype(out_dtype).itemsize + (4 if res is not None else 0)
    kg = _gcd(*k_c)
    tk = LANES if kg % LANES == 0 else kg
    for cand in range(LANES, kg + 1, LANES):
        if kg % cand == 0 and 2 * (2 * cand * (tm + tn) + tm * tn * out_bytes) + 2 * 4 * tm * tn <= MM_VMEM_BUDGET:
            tk = cand
    nk = K // tk
    if mode == "tn":
        a_spec = _mspec(a.shape, tk, tm, lambda i, j, k: (k, i))
        dims = (((0,), (0,)), ((), ()))
    else:
        a_spec = _mspec(a.shape, tm, tk, lambda i, j, k: (i, k))
        dims = (((1,), (1,)), ((), ())) if mode == "nt" else (((1,), (0,)), ((), ()))
    if mode == "nt":
        b_spec = _mspec(b.shape, tn, tk, lambda i, j, k: (j, k))
    else:
        b_spec = _mspec(b.shape, tk, tn, lambda i, j, k: (k, j))
    o_spec = _mspec(out_shape, tm, tn, lambda i, j, k: (i, j))
    in_specs, args = [a_spec, b_spec], [a, b]
    if res is not None:
        in_specs.append(pl.BlockSpec((tm, tn), lambda i, j, k: (i, j)))
        args.append(res)

    def body(*refs):
        a_ref, b_ref = refs[0], refs[1]
        res_ref = refs[2] if res is not None else None
        o_ref = refs[3] if res is not None else refs[2]

        def product():
            return lax.dot_general(a_ref[...].astype(BF16), b_ref[...].astype(BF16), dims, preferred_element_type=F32)

        def finish(r):
            if scale != 1.0:
                r = r * scale
            if res_ref is not None:
                r = res_ref[...] + r
            o_ref[...] = r.astype(o_ref.dtype)

        if nk == 1:
            finish(product())
            return
        acc_ref = refs[-1]
        kk = pl.program_id(2)

        @pl.when(kk == 0)
        def _():
            acc_ref[...] = product()

        @pl.when((kk > 0) & (kk < nk - 1))
        def _():
            acc_ref[...] += product()

        @pl.when(kk == nk - 1)
        def _():
            finish(acc_ref[...] + product())

    return pl.pallas_call(
        body, name=name, grid=(M // tm, N // tn, nk), in_specs=in_specs, out_specs=o_spec,
        out_shape=_sds(out_shape, out_dtype), scratch_shapes=[pltpu.VMEM((tm, tn), F32)] if nk > 1 else [],
        compiler_params=_cp("parallel", "parallel", "arbitrary"))(*args)


def _rows(i):
    return (i, 0)


def _row0(i):
    return (0, 0)


def _rms_fwd(name, x, g):
    T, D = x.shape
    tr = _tile(T, 256, 8)

    def body(x_ref, g_ref, o_ref):
        xv = x_ref[...]
        r = lax.rsqrt(jnp.mean(xv * xv, axis=-1, keepdims=True) + EPS)
        o_ref[...] = (xv * r * g_ref[...]).astype(o_ref.dtype)

    return pl.pallas_call(
        body, name=name, grid=(T // tr,),
        in_specs=[pl.BlockSpec((tr, D), _rows), pl.BlockSpec((1, D), _row0)],
        out_specs=pl.BlockSpec((tr, D), _rows), out_shape=_sds((T, D), BF16),
        compiler_params=_cp("parallel"))(x, g)


def _rms_bwd(name, x, g, dh, dres):
    T, D = x.shape
    tr = _tile(T, 256, 8)

    def body(x_ref, g_ref, dh_ref, dres_ref, dx_ref, dxb_ref, dg_ref):
        xv = x_ref[...]
        r = lax.rsqrt(jnp.mean(xv * xv, axis=-1, keepdims=True) + EPS)
        xh = xv * r
        dhv = dh_ref[...]
        dxh = dhv * g_ref[...]
        dx = dres_ref[...] + r * (dxh - xh * jnp.mean(dxh * xh, axis=-1, keepdims=True))
        dx_ref[...] = dx
        dxb_ref[...] = dx.astype(BF16)

        @pl.when(pl.program_id(0) == 0)
        def _():
            dg_ref[...] = jnp.zeros_like(dg_ref)

        dg_ref[...] += jnp.sum(dhv * xh, axis=0, keepdims=True)

    return pl.pallas_call(
        body, name=name, grid=(T // tr,),
        in_specs=[pl.BlockSpec((tr, D), _rows), pl.BlockSpec((1, D), _row0),
                  pl.BlockSpec((tr, D), _rows), pl.BlockSpec((tr, D), _rows)],
        out_specs=[pl.BlockSpec((tr, D), _rows), pl.BlockSpec((tr, D), _rows), pl.BlockSpec((1, D), _row0)],
        out_shape=[_sds((T, D), F32), _sds((T, D), BF16), _sds((1, D), F32)],
        compiler_params=_cp("arbitrary"))(x, g, dh, dres)


def _sigmoid(v):
    return 1.0 / (1.0 + jnp.exp(-v))


def _swiglu_fwd(name, a):
    _, T, F = a.shape
    tr, tc = _tile(T, 512, 8), _tile(F, MM_TILE)

    def body(a_ref, o_ref):
        gate = a_ref[0].astype(F32)
        up = a_ref[1].astype(F32)
        o_ref[...] = (gate * _sigmoid(gate) * up).astype(o_ref.dtype)

    return pl.pallas_call(
        body, name=name, grid=(T // tr, F // tc),
        in_specs=[pl.BlockSpec((2, tr, tc), lambda i, j: (0, i, j))],
        out_specs=pl.BlockSpec((tr, tc), lambda i, j: (i, j)), out_shape=_sds((T, F), BF16),
        compiler_params=_cp("parallel", "parallel"))(a)


def _swiglu_bwd(name, a, ds):
    _, T, F = a.shape
    tr, tc = _tile(T, 512, 8), _tile(F, MM_TILE)

    def body(a_ref, ds_ref, o_ref):
        gate = a_ref[0].astype(F32)
        up = a_ref[1].astype(F32)
        dsv = ds_ref[...].astype(F32)
        sg = _sigmoid(gate)
        o_ref[0] = (dsv * up * sg * (1.0 + gate * (1.0 - sg))).astype(o_ref.dtype)
        o_ref[1] = (dsv * gate * sg).astype(o_ref.dtype)

    return pl.pallas_call(
        body, name=name, grid=(T // tr, F // tc),
        in_specs=[pl.BlockSpec((2, tr, tc), lambda i, j: (0, i, j)), pl.BlockSpec((tr, tc), lambda i, j: (i, j))],
        out_specs=pl.BlockSpec((2, tr, tc), lambda i, j: (0, i, j)), out_shape=_sds((2, T, F), BF16),
        compiler_params=_cp("parallel", "parallel"))(a, ds)


def _loss_head(name, x, g, tgt):
    T, D = x.shape
    tr = _tile(T, 256, 8)

    def body(x_ref, g_ref, t_ref, loss_ref, dx_ref, dxb_ref, dg_ref):
        xv = x_ref[...]
        gv = g_ref[...]
        r = lax.rsqrt(jnp.mean(xv * xv, axis=-1, keepdims=True) + EPS)
        xh = xv * r
        err = xh * gv - t_ref[...]
        dy = err * (1.0 / D)
        dxh = dy * gv
        dx = r * (dxh - xh * jnp.mean(dxh * xh, axis=-1, keepdims=True))
        dx_ref[...] = dx
        dxb_ref[...] = dx.astype(BF16)

        @pl.when(pl.program_id(0) == 0)
        def _():
            dg_ref[...] = jnp.zeros_like(dg_ref)
            loss_ref[...] = jnp.zeros_like(loss_ref)

        dg_ref[...] += jnp.sum(dy * xh, axis=0, keepdims=True)
        part = jnp.sum(jnp.sum(err * err, axis=-1, keepdims=True), axis=0, keepdims=True) * (0.5 / D)
        loss_ref[...] += part

    return pl.pallas_call(
        body, name=name, grid=(T // tr,),
        in_specs=[pl.BlockSpec((tr, D), _rows), pl.BlockSpec((1, D), _row0), pl.BlockSpec((tr, D), _rows)],
        out_specs=[pl.BlockSpec((8, LANES), _row0), pl.BlockSpec((tr, D), _rows), pl.BlockSpec((tr, D), _rows),
                   pl.BlockSpec((1, D), _row0)],
        out_shape=[_sds((8, LANES), F32), _sds((T, D), F32), _sds((T, D), BF16), _sds((1, D), F32)],
        compiler_params=_cp("arbitrary"))(x, g, tgt)


_SQRT_HALF = 0.7071067811865476
_INV_SQRT_2PI = 0.3989422804014327


def _gelu(v):
    return 0.5 * v * (1.0 + lax.erf(v * _SQRT_HALF))


def _gelu_grad(v):
    return 0.5 * (1.0 + lax.erf(v * _SQRT_HALF)) + v * _INV_SQRT_2PI * jnp.exp(-0.5 * v * v)


def _group_expand(rows, width, gd):
    gi = lax.broadcasted_iota(jnp.int32, (rows, width), 0)
    fi = lax.broadcasted_iota(jnp.int32, (rows, width), 1)
    return ((fi >= gi * gd) & (fi < (gi + 1) * gd)).astype(F32)


def _gm_act_fwd(name, zpre, vnorm):
    _, T, GI = zpre.shape
    tr = _tile(T, 128, 8)

    def body(z_ref, g_ref, u_ref, v_ref):
        u_ref[...] = _gelu(z_ref[0].astype(F32)).astype(BF16)
        zv = _gelu(z_ref[1].astype(F32))
        r = lax.rsqrt(jnp.mean(zv * zv, axis=-1, keepdims=True) + EPS)
        v_ref[...] = (zv * r * g_ref[...]).astype(BF16)

    return pl.pallas_call(
        body, name=name, grid=(T // tr,),
        in_specs=[pl.BlockSpec((2, tr, GI), lambda i: (0, i, 0)), pl.BlockSpec((1, GI), _row0)],
        out_specs=[pl.BlockSpec((tr, GI), _rows), pl.BlockSpec((tr, GI), _rows)],
        out_shape=[_sds((T, GI), BF16), _sds((T, GI), BF16)], compiler_params=_cp("parallel"))(zpre, vnorm)


def _gm_act_bwd(name, zpre, du, dvn, vnorm):
    _, T, GI = zpre.shape
    tr = _tile(T, 128, 8)

    def body(z_ref, du_ref, dvn_ref, g_ref, dz_ref, dg_ref):
        xu = z_ref[0].astype(F32)
        xv = z_ref[1].astype(F32)
        zv = _gelu(xv)
        r = lax.rsqrt(jnp.mean(zv * zv, axis=-1, keepdims=True) + EPS)
        xh = zv * r
        dv = dvn_ref[...]

        @pl.when(pl.program_id(0) == 0)
        def _():
            dg_ref[...] = jnp.zeros_like(dg_ref)

        dg_ref[...] += jnp.sum(dv * xh, axis=0, keepdims=True)
        dxh = dv * g_ref[...]
        dzv = r * (dxh - xh * jnp.mean(dxh * xh, axis=-1, keepdims=True))
        dz_ref[0] = (du_ref[...] * _gelu_grad(xu)).astype(BF16)
        dz_ref[1] = (dzv * _gelu_grad(xv)).astype(BF16)

    return pl.pallas_call(
        body, name=name, grid=(T // tr,),
        in_specs=[pl.BlockSpec((2, tr, GI), lambda i: (0, i, 0)), pl.BlockSpec((tr, GI), _rows),
                  pl.BlockSpec((tr, GI), _rows), pl.BlockSpec((1, GI), _row0)],
        out_specs=[pl.BlockSpec((2, tr, GI), lambda i: (0, i, 0)), pl.BlockSpec((1, GI), _row0)],
        out_shape=[_sds((2, T, GI), BF16), _sds((1, GI), F32)], compiler_params=_cp("arbitrary"))(zpre, du, dvn, vnorm)


def _causal_mask():
    r = lax.broadcasted_iota(jnp.int32, (CHUNK, CHUNK), 0)
    c = lax.broadcasted_iota(jnp.int32, (CHUNK, CHUNK), 1)
    return r >= c


def _gm_spatial_fwd(name, u, vn, ws, bt):
    T, GI = u.shape
    G = ws.shape[0]
    gd = GI // G

    def body(u_ref, v_ref, ws_ref, bt_ref, o_ref, bias_scr):
        @pl.when(pl.program_id(0) == 0)
        def _():
            bias_scr[...] = jnp.dot(bt_ref[...], _group_expand(LANES, GI, gd), precision=HIGHEST,
                                    preferred_element_type=F32)

        causal = _causal_mask()
        for g in range(G):
            sl = slice(g * gd, (g + 1) * gd)
            wc = jnp.where(causal, ws_ref[g], 0.0).astype(BF16)
            mixed = jnp.dot(wc, v_ref[:, sl], preferred_element_type=F32) + bias_scr[:, sl]
            o_ref[:, sl] = (u_ref[:, sl].astype(F32) * mixed).astype(o_ref.dtype)

    return pl.pallas_call(
        body, name=name, grid=(T // CHUNK,),
        in_specs=[pl.BlockSpec((CHUNK, GI), _rows), pl.BlockSpec((CHUNK, GI), _rows),
                  pl.BlockSpec((G, CHUNK, CHUNK), lambda i: (0, 0, 0)), pl.BlockSpec((CHUNK, LANES), _row0)],
        out_specs=pl.BlockSpec((CHUNK, GI), _rows), out_shape=_sds((T, GI), BF16),
        scratch_shapes=[pltpu.VMEM((CHUNK, GI), F32)], compiler_params=_cp("arbitrary"))(u, vn, ws, bt)


def _gm_spatial_bwd(name, dgated, u, vn, ws, bt):
    T, GI = u.shape
    G = ws.shape[0]
    gd = GI // G
    nc = T // CHUNK

    def body(dg_ref, u_ref, v_ref, ws_ref, bt_ref, du_ref, dv_ref, dws_ref, dbt_ref, bias_scr, dm_scr):
        step = pl.program_id(0)

        @pl.when(step == 0)
        def _():
            bias_scr[...] = jnp.dot(bt_ref[...], _group_expand(LANES, GI, gd), precision=HIGHEST,
                                    preferred_element_type=F32)
            dm_scr[...] = jnp.zeros_like(dm_scr)
            dws_ref[...] = jnp.zeros_like(dws_ref)

        causal = _causal_mask()
        for g in range(G):
            sl = slice(g * gd, (g + 1) * gd)
            wc = jnp.where(causal, ws_ref[g], 0.0).astype(BF16)
            vv = v_ref[:, sl]
            dgv = dg_ref[:, sl].astype(F32)
            mixed = jnp.dot(wc, vv, preferred_element_type=F32) + bias_scr[:, sl]
            du_ref[:, sl] = dgv * mixed
            dm = dgv * u_ref[:, sl].astype(F32)
            dmb = dm.astype(BF16)
            dv_ref[:, sl] = lax.dot_general(wc, dmb, (((0,), (0,)), ((), ())), preferred_element_type=F32)
            dw = lax.dot_general(dmb, vv, (((1,), (1,)), ((), ())), preferred_element_type=F32)
            dws_ref[g] += jnp.where(causal, dw, 0.0)
            dm_scr[:, sl] += dm

        @pl.when(step == nc - 1)
        def _():
            dbt_ref[...] = lax.dot_general(dm_scr[...], _group_expand(LANES, GI, gd), (((1,), (1,)), ((), ())),
                                           precision=HIGHEST, preferred_element_type=F32)

    return pl.pallas_call(
        body, name=name, grid=(nc,),
        in_specs=[pl.BlockSpec((CHUNK, GI), _rows), pl.BlockSpec((CHUNK, GI), _rows), pl.BlockSpec((CHUNK, GI), _rows),
                  pl.BlockSpec((G, CHUNK, CHUNK), lambda i: (0, 0, 0)), pl.BlockSpec((CHUNK, LANES), _row0)],
        out_specs=[pl.BlockSpec((CHUNK, GI), _rows), pl.BlockSpec((CHUNK, GI), _rows),
                   pl.BlockSpec((G, CHUNK, CHUNK), lambda i: (0, 0, 0)), pl.BlockSpec((CHUNK, LANES), _row0)],
        out_shape=[_sds((T, GI), F32), _sds((T, GI), F32), _sds((G, CHUNK, CHUNK), F32), _sds((CHUNK, LANES), F32)],
        scratch_shapes=[pltpu.VMEM((CHUNK, GI), F32), pltpu.VMEM((CHUNK, GI), F32)],
        compiler_params=_cp("arbitrary"))(dgated, u, vn, ws, bt)


def _conv_taps(xv, w_ref, b_ref):
    rows = lax.broadcasted_iota(jnp.int32, xv.shape, 0)
    acc = xv * w_ref[pl.ds(SSM_CONV - 1, 1), :] + b_ref[...]
    for k in range(1, SSM_CONV):
        sh = jnp.where(rows >= k, pltpu.roll(xv, k, 0), 0.0)
        acc = acc + sh * w_ref[pl.ds(SSM_CONV - 1 - k, 1), :]
    return acc


def _ssd_conv_fwd(name, proj, wt, b, inner, cd):
    T = proj.shape[0]
    tc = _tile(_gcd(inner, cd), 512)
    off = inner // tc

    def body(x_ref, w_ref, b_ref, o_ref):
        pre = _conv_taps(x_ref[...], w_ref, b_ref)
        o_ref[...] = pre * _sigmoid(pre)

    return pl.pallas_call(
        body, name=name, grid=(cd // tc,),
        in_specs=[pl.BlockSpec((T, tc), lambda j: (0, off + j)), pl.BlockSpec((SSM_CONV, tc), lambda j: (0, j)),
                  pl.BlockSpec((1, tc), lambda j: (0, j))],
        out_specs=pl.BlockSpec((T, tc), lambda j: (0, j)), out_shape=_sds((T, cd), F32),
        compiler_params=_cp("parallel"))(proj, wt, b)


def _ssd_conv_bwd(name, dact, proj, wt, b, inner, cd):
    T = proj.shape[0]
    tc = _tile(_gcd(inner, cd), 512)
    off = inner // tc

    def body(da_ref, x_ref, w_ref, b_ref, dx_ref, dw_ref, db_ref):
        xv = x_ref[...]
        pre = _conv_taps(xv, w_ref, b_ref)
        sg = _sigmoid(pre)
        dpre = da_ref[...] * sg * (1.0 + pre * (1.0 - sg))
        rows = lax.broadcasted_iota(jnp.int32, xv.shape, 0)
        db_ref[...] = jnp.sum(dpre, axis=0, keepdims=True)
        dx = dpre * w_ref[pl.ds(SSM_CONV - 1, 1), :]
        dw_ref[pl.ds(SSM_CONV - 1, 1), :] = jnp.sum(dpre * xv, axis=0, keepdims=True)
        for k in range(1, SSM_CONV):
            sh = jnp.where(rows >= k, pltpu.roll(xv, k, 0), 0.0)
            dw_ref[pl.ds(SSM_CONV - 1 - k, 1), :] = jnp.sum(dpre * sh, axis=0, keepdims=True)
            fw = jnp.where(rows < T - k, pltpu.roll(dpre, T - k, 0), 0.0)
            dx = dx + fw * w_ref[pl.ds(SSM_CONV - 1 - k, 1), :]
        dx_ref[...] = dx.astype(BF16)

    return pl.pallas_call(
        body, name=name, grid=(cd // tc,),
        in_specs=[pl.BlockSpec((T, tc), lambda j: (0, j)), pl.BlockSpec((T, tc), lambda j: (0, off + j)),
                  pl.BlockSpec((SSM_CONV, tc), lambda j: (0, j)), pl.BlockSpec((1, tc), lambda j: (0, j))],
        out_specs=[pl.BlockSpec((T, tc), lambda j: (0, j)), pl.BlockSpec((SSM_CONV, tc), lambda j: (0, j)),
                   pl.BlockSpec((1, tc), lambda j: (0, j))],
        out_shape=[_sds((T, cd), BF16), _sds((SSM_CONV, cd), F32), _sds((1, cd), F32)],
        compiler_params=_cp("parallel"))(dact, proj, wt, b)


def _softplus(v):
    return jnp.maximum(v, 0.0) + jnp.log(1.0 + jnp.exp(-jnp.abs(v)))


def _tri(lower):
    r = lax.broadcasted_iota(jnp.int32, (CHUNK, CHUNK), 0)
    c = lax.broadcasted_iota(jnp.int32, (CHUNK, CHUNK), 1)
    return ((c <= r) if lower else (c >= r)).astype(F32)


def _ssd_dt_fwd(name, proj, bias, alog, dtcol, heads):
    T = proj.shape[0]

    def body(dt_ref, b_ref, al_ref, dtp_ref, ac_ref):
        live = lax.broadcasted_iota(jnp.int32, (CHUNK, LANES), 1) < heads
        dtp = jnp.where(live, _softplus(dt_ref[...] + b_ref[...]), 0.0)
        da = dtp * (-jnp.exp(al_ref[...]))
        dtp_ref[...] = dtp
        ac_ref[...] = jnp.dot(_tri(True), da, precision=HIGHEST, preferred_element_type=F32)

    return pl.pallas_call(
        body, name=name, grid=(T // CHUNK,),
        in_specs=[pl.BlockSpec((CHUNK, LANES), lambda i: (i, dtcol)), pl.BlockSpec((1, LANES), _row0),
                  pl.BlockSpec((1, LANES), _row0)],
        out_specs=[pl.BlockSpec((CHUNK, LANES), _rows), pl.BlockSpec((CHUNK, LANES), _rows)],
        out_shape=[_sds((T, LANES), F32), _sds((T, LANES), F32)], compiler_params=_cp("parallel"))(proj, bias, alog)


def _ssd_dt_bwd(name, ddtp_g, dacum_g, dd_g, proj, bias, alog, dtp, dtcol, heads):
    T = proj.shape[0]
    G = ddtp_g.shape[0]

    def body(ddtp_ref, dac_ref, dd_ref, dt_ref, b_ref, al_ref, dtp_ref, ddt_ref, db_ref, dal_ref, dds_ref, da_scr):
        step = pl.program_id(0)

        @pl.when(step == 0)
        def _():
            db_ref[...] = jnp.zeros_like(db_ref)
            da_scr[...] = jnp.zeros_like(da_scr)
            dds_ref[...] = jnp.sum(dd_ref[...], axis=0)

        live = lax.broadcasted_iota(jnp.int32, (CHUNK, LANES), 1) < heads
        a = -jnp.exp(al_ref[...])
        dac = jnp.sum(dac_ref[...], axis=0)
        dda = jnp.dot(_tri(False), dac, precision=HIGHEST, preferred_element_type=F32)
        dtp_v = dtp_ref[...]
        ddtp = jnp.sum(ddtp_ref[...], axis=0) + dda * a
        da_scr[...] += jnp.sum(dda * dtp_v, axis=0, keepdims=True)
        ddt = jnp.where(live, ddtp * _sigmoid(dt_ref[...] + b_ref[...]), 0.0)
        ddt_ref[...] = ddt.astype(BF16)
        db_ref[...] += jnp.sum(ddt, axis=0, keepdims=True)
        dal_ref[...] = da_scr[...] * a

    return pl.pallas_call(
        body, name=name, grid=(T // CHUNK,),
        in_specs=[pl.BlockSpec((G, CHUNK, LANES), lambda i: (0, i, 0)), pl.BlockSpec((G, CHUNK, LANES), lambda i: (0, i, 0)),
                  pl.BlockSpec((G, 8, LANES), lambda i: (0, 0, 0)),
                  pl.BlockSpec((CHUNK, LANES), lambda i: (i, dtcol)), pl.BlockSpec((1, LANES), _row0),
                  pl.BlockSpec((1, LANES), _row0), pl.BlockSpec((CHUNK, LANES), _rows)],
        out_specs=[pl.BlockSpec((CHUNK, LANES), _rows), pl.BlockSpec((8, LANES), _row0), pl.BlockSpec((8, LANES), _row0),
                   pl.BlockSpec((8, LANES), _row0)],
        out_shape=[_sds((T, LANES), BF16), _sds((8, LANES), F32), _sds((8, LANES), F32), _sds((8, LANES), F32)],
        scratch_shapes=[pltpu.VMEM((8, LANES), F32)],
        compiler_params=_cp("arbitrary"))(ddtp_g, dacum_g, dd_g, proj, bias, alog, dtp)


def _head_expand(g, gw):
    hi = lax.broadcasted_iota(jnp.int32, (LANES, gw), 0) - g * SSM_HPG
    fi = lax.broadcasted_iota(jnp.int32, (LANES, gw), 1)
    return ((fi >= hi * SSM_HEAD_DIM) & (fi < (hi + 1) * SSM_HEAD_DIM)).astype(F32)


def _dot(a, b, dims, exact=False):
    if exact:
        return lax.dot_general(a, b, (dims, ((), ())), precision=HIGHEST, preferred_element_type=F32)
    return lax.dot_general(a.astype(BF16), b.astype(BF16), (dims, ((), ())), preferred_element_type=F32)


_NN = ((1,), (0,))
_NT = ((1,), (1,))
_TN = ((0,), (0,))


def _pair_decay(g, q, acum, acum_t_ref, causal):
    lane = lax.broadcasted_iota(jnp.int32, (CHUNK, LANES), 1)
    out = []
    for e in range(2):
        h = g * SSM_HPG + 2 * q + e
        acol = jnp.sum(jnp.where(lane == h, acum, 0.0), axis=1, keepdims=True)
        arow = acum_t_ref[pl.ds(h, 1), :]
        out.append(jnp.exp(jnp.where(causal, acol - arow, -1e30)))
    return out


def _ssd_core_fwd(name, act, dtp, acum, dexp, inner, groups):
    T = act.shape[0]
    nc = T // CHUNK
    gw = SSM_HPG * SSM_HEAD_DIM
    npair = gw // LANES
    bcol, ccol = inner // SSM_STATE, inner // SSM_STATE + groups

    def body(x_ref, b_ref, c_ref, dtp_ref, ac_ref, d_ref, y_ref, sp_ref, st_scr, act_scr, ae_scr):
        g = pl.program_id(0)

        @pl.when(pl.program_id(1) == 0)
        def _():
            st_scr[...] = jnp.zeros_like(st_scr)

        st = st_scr[...]
        sp_ref[...] = st
        e = _head_expand(g, gw)
        acum = ac_ref[...]
        ae = _dot(acum, e, _NN, exact=True)
        dte = _dot(dtp_ref[...], e, _NN, exact=True)
        ae_scr[...] = ae
        act_scr[...] = acum.T
        xv = x_ref[...]
        xdt = xv * dte
        bm, cm = b_ref[...], c_ref[...]
        cb = _dot(cm, bm, _NT)
        causal = _causal_mask()
        lane = lax.broadcasted_iota(jnp.int32, (CHUNK, LANES), 1)
        yoff = _dot(cm, st, _NN) * jnp.exp(ae)
        skip = xv * d_ref[...]
        for q in range(npair):
            sl = slice(q * LANES, (q + 1) * LANES)
            dec = _pair_decay(g, q, acum, act_scr, causal)
            x2 = xdt[:, sl]
            xa = jnp.where(lane < SSM_HEAD_DIM, x2, 0.0)
            yd = _dot(dec[0] * cb, xa, _NN) + _dot(dec[1] * cb, x2 - xa, _NN)
            y_ref[:, sl] = yd + yoff[:, sl] + skip[:, sl]
        alast = ae_scr[pl.ds(CHUNK - 1, 1), :]
        z = xdt * jnp.exp(alast - ae)
        st_scr[...] = st * jnp.exp(alast) + _dot(bm, z, _TN)

    return pl.pallas_call(
        body, name=name, grid=(groups, nc),
        in_specs=[pl.BlockSpec((CHUNK, gw), lambda g, c: (c, g)),
                  pl.BlockSpec((CHUNK, SSM_STATE), lambda g, c: (c, bcol + g)),
                  pl.BlockSpec((CHUNK, SSM_STATE), lambda g, c: (c, ccol + g)),
                  pl.BlockSpec((CHUNK, LANES), lambda g, c: (c, 0)), pl.BlockSpec((CHUNK, LANES), lambda g, c: (c, 0)),
                  pl.BlockSpec((1, gw), lambda g, c: (0, g))],
        out_specs=[pl.BlockSpec((CHUNK, gw), lambda g, c: (c, g)),
                   pl.BlockSpec((None, SSM_STATE, gw), lambda g, c: (c, 0, g))],
        out_shape=[_sds((T, inner), F32), _sds((nc, SSM_STATE, inner), F32)],
        scratch_shapes=[pltpu.VMEM((SSM_STATE, gw), F32), pltpu.VMEM((CHUNK, LANES), F32), pltpu.VMEM((CHUNK, gw), F32)],
        compiler_params=_cp("arbitrary", "arbitrary"))(act, act, act, dtp, acum, dexp)


def _ssd_core_bwd(name, dy, act, dtp, acum, dexp, sprev, inner, groups):
    T = act.shape[0]
    nc = T // CHUNK
    gw = SSM_HPG * SSM_HEAD_DIM
    npair = gw // LANES
    bcol, ccol = inner // SSM_STATE, inner // SSM_STATE + groups

    def rc(g, c):
        return nc - 1 - c

    def body(dy_ref, x_ref, b_ref, c_ref, dtp_ref, ac_ref, d_ref, sp_ref,
             dx_ref, db_ref, dc_ref, ddtp_ref, dac_ref, dd_ref,
             dst_scr, act_scr, ae_scr, dxdt_scr, dd_scr, dact_scr):
        g = pl.program_id(0)
        step = pl.program_id(1)

        @pl.when(step == 0)
        def _():
            dst_scr[...] = jnp.zeros_like(dst_scr)
            dd_scr[...] = jnp.zeros_like(dd_scr)

        dst = dst_scr[...]
        sp = sp_ref[...]
        e = _head_expand(g, gw)
        acum = ac_ref[...]
        ae = _dot(acum, e, _NN, exact=True)
        dte = _dot(dtp_ref[...], e, _NN, exact=True)
        ae_scr[...] = ae
        act_scr[...] = acum.T
        alast = ae_scr[pl.ds(CHUNK - 1, 1), :]
        xv = x_ref[...]
        xdt = xv * dte
        bm, cm = b_ref[...], c_ref[...]
        dyv = dy_ref[...]
        cb = _dot(cm, bm, _NT)
        causal = _causal_mask()
        lane = lax.broadcasted_iota(jnp.int32, (CHUNK, LANES), 1)
        ea = jnp.exp(ae)
        cde = jnp.exp(alast)
        w = jnp.exp(alast - ae)
        z = xdt * w

        dd_scr[...] += jnp.sum(dyv * xv, axis=0, keepdims=True)
        qm = _dot(cm, sp, _NN)
        dq = dyv * ea
        dae = dq * qm
        dc = _dot(dq, sp, _NT)
        dsp = _dot(cm, dq, _TN) + dst * cde
        dal = jnp.sum(dst * sp, axis=0, keepdims=True) * cde
        db = _dot(z, dst, _NT)
        dz = _dot(bm, dst, _NN)
        gw_ = dz * z
        dae = dae - gw_
        dal = dal + jnp.sum(gw_, axis=0, keepdims=True)
        dxdt_scr[...] = dz * w
        dcb = jnp.zeros((CHUNK, CHUNK), F32)
        dacol = jnp.zeros((CHUNK, LANES), F32)
        dact_scr[...] = jnp.zeros_like(dact_scr)
        sub = lax.broadcasted_iota(jnp.int32, (CHUNK, LANES), 0)
        for q in range(npair):
            sl = slice(q * LANES, (q + 1) * LANES)
            dec = _pair_decay(g, q, acum, act_scr, causal)
            x2, dy2 = xdt[:, sl], dyv[:, sl]
            xs_ = (jnp.where(lane < SSM_HEAD_DIM, x2, 0.0),)
            xs_ = xs_ + (x2 - xs_[0],)
            dys = (jnp.where(lane < SSM_HEAD_DIM, dy2, 0.0),)
            dys = dys + (dy2 - dys[0],)
            dx2 = jnp.zeros((CHUNK, LANES), F32)
            for hh in range(2):
                h = g * SSM_HPG + 2 * q + hh
                m = dec[hh] * cb
                dm = _dot(dys[hh], xs_[hh], _NT)
                dx2 = dx2 + _dot(m, dys[hh], _TN)
                dcb = dcb + dm * dec[hh]
                r = dm * m
                dacol = dacol + jnp.where(lane == h, jnp.sum(r, axis=1, keepdims=True), 0.0)
                dact_scr[...] -= jnp.where(sub == h, jnp.sum(r, axis=0, keepdims=True), 0.0)
            dxdt_scr[:, sl] += dx2
        dc = dc + _dot(dcb, bm, _NN)
        db = db + _dot(dcb, cm, _TN)
        dxdt = dxdt_scr[...]
        dx_ref[...] = dyv * d_ref[...] + dxdt * dte
        db_ref[...] = db
        dc_ref[...] = dc
        ddtp_ref[...] = _dot(dxdt * xv, e, _NT, exact=True)
        dal_h = _dot(jnp.broadcast_to(dal, (8, gw)), e, _NT, exact=True)
        dal_row = jnp.max(dal_h, axis=0, keepdims=True)
        dac = _dot(dae, e, _NT, exact=True) + dacol + dact_scr[...].T
        dac_ref[...] = dac + jnp.where(sub == CHUNK - 1, dal_row, 0.0)
        dst_scr[...] = dsp

        @pl.when(step == nc - 1)
        def _():
            dd_ref[...] = _dot(jnp.broadcast_to(dd_scr[...], (8, gw)), e, _NT, exact=True)

    return pl.pallas_call(
        body, name=name, grid=(groups, nc),
        in_specs=[pl.BlockSpec((CHUNK, gw), lambda g, c: (rc(g, c), g)),
                  pl.BlockSpec((CHUNK, gw), lambda g, c: (rc(g, c), g)),
                  pl.BlockSpec((CHUNK, SSM_STATE), lambda g, c: (rc(g, c), bcol + g)),
                  pl.BlockSpec((CHUNK, SSM_STATE), lambda g, c: (rc(g, c), ccol + g)),
                  pl.BlockSpec((CHUNK, LANES), lambda g, c: (rc(g, c), 0)),
                  pl.BlockSpec((CHUNK, LANES), lambda g, c: (rc(g, c), 0)),
                  pl.BlockSpec((1, gw), lambda g, c: (0, g)),
                  pl.BlockSpec((None, SSM_STATE, gw), lambda g, c: (rc(g, c), 0, g))],
        out_specs=[pl.BlockSpec((CHUNK, gw), lambda g, c: (rc(g, c), g)),
                   pl.BlockSpec((CHUNK, SSM_STATE), lambda g, c: (rc(g, c), g)),
                   pl.BlockSpec((CHUNK, SSM_STATE), lambda g, c: (rc(g, c), g)),
                   pl.BlockSpec((None, CHUNK, LANES), lambda g, c: (g, rc(g, c), 0)),
                   pl.BlockSpec((None, CHUNK, LANES), lambda g, c: (g, rc(g, c), 0)),
                   pl.BlockSpec((None, 8, LANES), lambda g, c: (g, 0, 0))],
        out_shape=[_sds((T, inner), F32), _sds((T, groups * SSM_STATE), F32), _sds((T, groups * SSM_STATE), F32),
                   _sds((groups, T, LANES), F32), _sds((groups, T, LANES), F32), _sds((groups, 8, LANES), F32)],
        scratch_shapes=[pltpu.VMEM((SSM_STATE, gw), F32), pltpu.VMEM((CHUNK, LANES), F32), pltpu.VMEM((CHUNK, gw), F32),
                        pltpu.VMEM((CHUNK, gw), F32), pltpu.VMEM((1, gw), F32), pltpu.VMEM((CHUNK, LANES), F32)],
        compiler_params=_cp("arbitrary", "arbitrary"))(dy, act, act, act, dtp, acum, dexp, sprev)


def _ssd_post_fwd(name, y, proj, ng, inner, groups):
    T = y.shape[0]
    tr = _tile(T, 128, 8)
    gs = inner // groups

    def body(y_ref, z_ref, g_ref, o_ref):
        zv = z_ref[...]
        gy = y_ref[...] * (zv * _sigmoid(zv))
        for k in range(groups):
            sl = slice(k * gs, (k + 1) * gs)
            seg = gy[:, sl]
            r = lax.rsqrt(jnp.mean(seg * seg, axis=-1, keepdims=True) + EPS)
            o_ref[:, sl] = (seg * r * g_ref[:, sl]).astype(BF16)

    return pl.pallas_call(
        body, name=name, grid=(T // tr,),
        in_specs=[pl.BlockSpec((tr, inner), _rows), pl.BlockSpec((tr, inner), _rows), pl.BlockSpec((1, inner), _row0)],
        out_specs=pl.BlockSpec((tr, inner), _rows), out_shape=_sds((T, inner), BF16),
        compiler_params=_cp("parallel"))(y, proj, ng)


def _ssd_post_bwd(name, dyn, y, proj, ng, inner, groups):
    T = y.shape[0]
    tr = _tile(T, 128, 8)
    gs = inner // groups

    def body(dyn_ref, y_ref, z_ref, g_ref, dy_ref, dz_ref, dg_ref):
        @pl.when(pl.program_id(0) == 0)
        def _():
            dg_ref[...] = jnp.zeros_like(dg_ref)

        zv = z_ref[...]
        sg = _sigmoid(zv)
        sz = zv * sg
        yv = y_ref[...]
        gy = yv * sz
        dv = dyn_ref[...]
        for k in range(groups):
            sl = slice(k * gs, (k + 1) * gs)
            seg = gy[:, sl]
            r = lax.rsqrt(jnp.mean(seg * seg, axis=-1, keepdims=True) + EPS)
            xh = seg * r
            d = dv[:, sl]
            dg_ref[:, sl] += jnp.sum(d * xh, axis=0, keepdims=True)
            dxh = d * g_ref[:, sl]
            dgy = r * (dxh - xh * jnp.mean(dxh * xh, axis=-1, keepdims=True))
            dy_ref[:, sl] = dgy * sz[:, sl]
            dz_ref[:, sl] = (dgy * yv[:, sl] * (sg[:, sl] * (1.0 + zv[:, sl] * (1.0 - sg[:, sl])))).astype(BF16)

    return pl.pallas_call(
        body, name=name, grid=(T // tr,),
        in_specs=[pl.BlockSpec((tr, inner), _rows), pl.BlockSpec((tr, inner), _rows), pl.BlockSpec((tr, inner), _rows),
                  pl.BlockSpec((1, inner), _row0)],
        out_specs=[pl.BlockSpec((tr, inner), _rows), pl.BlockSpec((tr, inner), _rows), pl.BlockSpec((1, inner), _row0)],
        out_shape=[_sds((T, inner), F32), _sds((T, inner), BF16), _sds((1, inner), F32)],
        compiler_params=_cp("arbitrary"))(dyn, y, proj, ng)


def _place():
    return lax.axis_index("x"), lax.axis_index("y"), lax.axis_index("c")


def _other_chips(x, y):
    return [(1 - x, y), (x, 1 - y), (1 - x, 1 - y)]


def _remote(src, dst, ssem, rsem, dev):
    return pltpu.make_async_remote_copy(src_ref=src, dst_ref=dst, send_sem=ssem, recv_sem=rsem, device_id=dev,
                                        device_id_type=MESH)


def _all_gather_chips(name, shard):
    R, C = shard.shape
    hr = R // 2

    def body(x_ref, o_ref, ssem, rsem, lsem):
        x, y, c = _place()
        k = 2 * x + y
        sib = (x, y, 1 - c)
        chips = _other_chips(x, y)

        def half(blk, cc):
            return o_ref.at[blk, pl.ds(cc * hr, hr), :]

        local = pltpu.make_async_copy(x_ref, o_ref.at[k], lsem)
        local.start()
        first = [_remote(x_ref.at[pl.ds(c * hr, hr), :], half(k, c), ssem.at[r], rsem.at[r], (px, py, c))
                 for r, (px, py) in enumerate(chips)]
        for cp in first:
            cp.start()
        passed = []
        for r, (px, py) in enumerate(chips):
            kj = 2 * px + py
            _remote(half(kj, c), half(kj, c), ssem.at[r], rsem.at[r], (px, py, c)).wait_recv()
            fw = _remote(half(kj, c), half(kj, c), ssem.at[3 + r], rsem.at[3 + r], sib)
            fw.start()
            passed.append(fw)
        for r, (px, py) in enumerate(chips):
            kj = 2 * px + py
            _remote(half(kj, 1 - c), half(kj, 1 - c), ssem.at[3 + r], rsem.at[3 + r], sib).wait_recv()
        for cp in first + passed:
            cp.wait_send()
        local.wait()

    return pl.pallas_call(
        body, name=name, in_specs=[_ANY], out_specs=_ANY, out_shape=_sds((4, R, C), shard.dtype),
        scratch_shapes=[pltpu.SemaphoreType.DMA((6,)), pltpu.SemaphoreType.DMA((6,)), pltpu.SemaphoreType.DMA],
    )(shard)


_HBM = pl.BlockSpec(memory_space=pltpu.HBM)
_SEM = pl.BlockSpec(memory_space=pltpu.SEMAPHORE)
_EFFECT = pltpu.SideEffectType.DATAFLOW_SIDE_EFFECTING


def _hbm(a):
    return pltpu.with_memory_space_constraint(a, pltpu.HBM)


NORTH = 1


def _cast_into_block(name, w, layer, after):
    _, R, C = w.shape
    tr = _tile(R, 256, 16)

    def body(w_ref, a_ref, o_ref):
        o_ref[...] = w_ref[...].astype(BF16)

    return pl.pallas_call(
        body, name=name, grid=(R // tr,),
        in_specs=[pl.BlockSpec((None, tr, C), lambda i: (layer, i, 0)), _ANY],
        out_specs=pl.BlockSpec((None, tr, C), lambda i: (2 * lax.axis_index("x") + lax.axis_index("y"), i, 0)),
        out_shape=_sds((4, R, C), BF16), compiler_params=_cp("parallel"))(w, after)


def _gather_copies(refs, ssem, rsem):
    x, y, _ = _place()
    k = 2 * x + y
    sends, arrivals = [], []
    for a, lr in enumerate(refs):
        for r, (px, py) in enumerate(_other_chips(x, y)):
            i = 3 * a + r
            sends.append(_remote(lr.at[k], lr.at[k], ssem.at[i], rsem.at[i], (px, py, NORTH)))
            arrivals.append(_remote(lr.at[k], lr.at[2 * px + py], ssem.at[i], rsem.at[i], (px, py, NORTH)))
    return sends, arrivals


def _scatter_copies(refs, ssem, rsem):
    x, y, c = _place()
    k = 2 * x + y
    n = len(refs) // 2
    sends, arrivals = [], []
    for a, (xr, lr) in enumerate(zip(refs[:n], refs[n:])):
        for r, (px, py) in enumerate(_other_chips(x, y)):
            i = 3 * a + r
            kj = 2 * px + py
            sends.append(_remote(xr.at[kj], lr.at[k], ssem.at[i], rsem.at[i], (px, py, c)))
            arrivals.append(_remote(xr.at[kj], lr.at[kj], ssem.at[i], rsem.at[i], (px, py, c)))
    return sends, arrivals


def _on_cores(north_only, fn):
    if north_only:
        pl.when(lax.axis_index("c") == NORTH)(fn)
    else:
        fn()


def _split_start(name, bufs, ncopies, copies_fn, north_only=False):
    n = len(bufs)

    def body(*refs):
        def go():
            sends, _ = copies_fn(refs[:n], refs[n], refs[n + 1])
            for cp in sends:
                cp.start()

        _on_cores(north_only, go)
        token = refs[-1]
        token[...] = jnp.zeros_like(token)

    outs = pl.pallas_call(
        body, name=name, in_specs=[_HBM] * n,
        out_specs=(_SEM, _SEM) + (_HBM,) * n + (pl.BlockSpec(memory_space=pltpu.VMEM),),
        out_shape=(pltpu.SemaphoreType.DMA((ncopies,)), pltpu.SemaphoreType.DMA((ncopies,)))
        + tuple(pltpu.HBM(b.shape, b.dtype) for b in bufs) + (_sds((8, LANES), F32),),
        input_output_aliases={i: 2 + i for i in range(n)},
        compiler_params=pltpu.CompilerParams(has_side_effects=_EFFECT),
    )(*[_hbm(b) for b in bufs])
    return outs[0], outs[1], list(outs[2:2 + n]), outs[-1]


def _split_wait(name, started, copies_fn, after, north_only=False):
    ssem, rsem, bufs, _ = started
    n = len(bufs)

    def body(*refs):
        def go():
            sends, arrivals = copies_fn(refs[:n], refs[n], refs[n + 1])
            for cp in sends:
                cp.wait_send()
            for cp in arrivals:
                cp.wait_recv()

        _on_cores(north_only, go)

    outs = pl.pallas_call(
        body, name=name, in_specs=[_HBM] * n + [_SEM, _SEM, _ANY], out_specs=(_HBM,) * n,
        out_shape=tuple(pltpu.HBM(t.shape, t.dtype) for t in bufs),
        input_output_aliases={i: i for i in range(n)},
        compiler_params=pltpu.CompilerParams(has_side_effects=_EFFECT),
    )(*bufs, ssem, rsem, after)
    return list(outs)


def _share_sibling(name, lands):
    n = len(lands)

    def body(*refs):
        l_in, l_out = refs[:n], refs[n:2 * n]
        ssem, rsem = refs[2 * n:]
        x, y, c = _place()

        def hand_on(a, r, px, py):
            blk = 2 * px + py
            return _remote(l_in[a].at[blk], l_out[a].at[blk], ssem.at[3 * a + r], rsem.at[3 * a + r], (x, y, 1 - NORTH))

        @pl.when(c == NORTH)
        def _():
            copies = [hand_on(a, r, px, py) for a in range(n) for r, (px, py) in enumerate(_other_chips(x, y))]
            for cp in copies:
                cp.start()
            for cp in copies:
                cp.wait_send()

        @pl.when(c != NORTH)
        def _():
            for a in range(n):
                for r, (px, py) in enumerate(_other_chips(x, y)):
                    hand_on(a, r, px, py).wait_recv()

    outs = pl.pallas_call(
        body, name=name, in_specs=[_ANY] * n, out_specs=[_ANY] * n,
        out_shape=[_sds(l.shape, l.dtype) for l in lands], input_output_aliases={i: i for i in range(n)},
        scratch_shapes=[pltpu.SemaphoreType.DMA((3 * n,)), pltpu.SemaphoreType.DMA((3 * n,))],
    )(*lands)
    return list(outs)


def _swap_halves(name, fulls, after=None):
    n = len(fulls)
    extra = [] if after is None else [after]

    def body(*refs):
        g_refs, o_refs = refs[:n], refs[n + len(extra):2 * n + len(extra)]
        ssem, rsem = refs[2 * n + len(extra):]
        x, y, c = _place()
        copies = []
        for a in range(n):
            hr = g_refs[a].shape[1] // 2
            cp = _remote(g_refs[a].at[:, pl.ds((1 - c) * hr, hr), :], o_refs[a], ssem.at[a], rsem.at[a], (x, y, 1 - c))
            cp.start()
            copies.append(cp)
        for cp in copies:
            cp.wait()

    outs = pl.pallas_call(
        body, name=name, in_specs=[_ANY] * (n + len(extra)), out_specs=[_ANY] * n,
        out_shape=[_sds((f.shape[0], f.shape[1] // 2, f.shape[2]), f.dtype) for f in fulls],
        scratch_shapes=[pltpu.SemaphoreType.DMA((n,)), pltpu.SemaphoreType.DMA((n,))],
    )(*fulls, *extra)
    return list(outs)


def _add_halves(name, full, other):
    nb, R, C = full.shape
    hr = R // 2
    tr = _tile(hr, 256, 16)
    nh = hr // tr

    def body(f_ref, o_ref, s_ref):
        s_ref[...] = (f_ref[...].astype(F32) + o_ref[...].astype(F32)).astype(s_ref.dtype)

    return pl.pallas_call(
        body, name=name, grid=(nb, nh),
        in_specs=[pl.BlockSpec((None, tr, C), lambda b, i: (b, lax.axis_index("c") * nh + i, 0)),
                  pl.BlockSpec((None, tr, C), lambda b, i: (b, i, 0))],
        out_specs=pl.BlockSpec((None, tr, C), lambda b, i: (b, i, 0)), out_shape=_sds((nb, hr, C), full.dtype),
        compiler_params=_cp("parallel", "parallel"))(full, other)


def _sum_owner(name, land, mine):
    nb, hr, C = land.shape
    tr = _tile(hr, 128, 16)

    def body(l_ref, m_ref, o_ref):
        k = 2 * lax.axis_index("x") + lax.axis_index("y")
        own = m_ref[...].astype(F32)
        acc = jnp.where(k == 0, own, l_ref[0].astype(F32))
        for j in range(1, nb):
            acc = acc + jnp.where(k == j, own, l_ref[j].astype(F32))
        o_ref[...] = acc

    return pl.pallas_call(
        body, name=name, grid=(hr // tr,),
        in_specs=[pl.BlockSpec((nb, tr, C), lambda i: (0, i, 0)),
                  pl.BlockSpec((None, tr, C), lambda i: (2 * lax.axis_index("x") + lax.axis_index("y"), i, 0))],
        out_specs=pl.BlockSpec((None, tr, C), lambda i: (lax.axis_index("c"), i, 0)),
        out_shape=_sds((2, hr, C), F32), compiler_params=_cp("parallel"))(land, mine)


def _join_halves(name, bufs):
    n = len(bufs)

    def body(*refs):
        i_refs, o_refs = refs[:n], refs[n:2 * n]
        ssem, rsem = refs[2 * n:]
        x, y, c = _place()
        copies = []
        for a in range(n):
            cp = _remote(i_refs[a].at[c], o_refs[a].at[c], ssem.at[a], rsem.at[a], (x, y, 1 - c))
            cp.start()
            copies.append(cp)
        for a, cp in enumerate(copies):
            _remote(i_refs[a].at[c], o_refs[a].at[1 - c], ssem.at[a], rsem.at[a], (x, y, 1 - c)).wait_recv()
            cp.wait_send()

    outs = pl.pallas_call(
        body, name=name, in_specs=[_ANY] * n, out_specs=[_ANY] * n,
        out_shape=[_sds(b.shape, b.dtype) for b in bufs], input_output_aliases={i: i for i in range(n)},
        scratch_shapes=[pltpu.SemaphoreType.DMA((n,)), pltpu.SemaphoreType.DMA((n,))],
    )(*bufs)
    return list(outs)


def _all_gather_devices(name, buf):
    def body(b_ref, o_ref, ssem, rsem, lsem):
        x, y, c = _place()
        me = 4 * x + 2 * y + c
        flips = [(fx, fy, fc) for fx in (0, 1) for fy in (0, 1) for fc in (0, 1) if fx or fy or fc]
        peers = [((1 - x) if fx else x, (1 - y) if fy else y, (1 - c) if fc else c) for fx, fy, fc in flips]
        local = pltpu.make_async_copy(b_ref, o_ref.at[me], lsem)
        local.start()
        sends = [_remote(b_ref, o_ref.at[me], ssem.at[r], rsem.at[r], p) for r, p in enumerate(peers)]
        for cp in sends:
            cp.start()
        for r, (px, py, pc) in enumerate(peers):
            pid = 4 * px + 2 * py + pc
            _remote(b_ref, o_ref.at[pid], ssem.at[r], rsem.at[r], (px, py, pc)).wait_recv()
        for cp in sends:
            cp.wait_send()
        local.wait()

    return pl.pallas_call(
        body, name=name, in_specs=[_ANY], out_specs=_ANY, out_shape=_sds((8,) + buf.shape, buf.dtype),
        scratch_shapes=[pltpu.SemaphoreType.DMA((7,)), pltpu.SemaphoreType.DMA((7,)), pltpu.SemaphoreType.DMA],
    )(buf)


def _sum_blocks(name, parts):
    n, R, C = parts.shape
    tr = _tile(R, 256, 8)

    def body(p_ref, o_ref):
        acc = p_ref[0].astype(F32)
        for j in range(1, n):
            acc = acc + p_ref[j].astype(F32)
        o_ref[...] = acc

    return pl.pallas_call(
        body, name=name, grid=(R // tr,), in_specs=[pl.BlockSpec((n, tr, C), lambda i: (0, i, 0))],
        out_specs=pl.BlockSpec((tr, C), _rows), out_shape=_sds((R, C), F32), compiler_params=_cp("parallel"))(parts)


def _adamw(name, w, m, v, parts, tok=None):
    L, R, C = w.shape
    np_ = len(parts[0])
    tr = _tile(R, max(8, (VMEM_LIMIT // 2) // (2 * 4 * C * (7 + L * np_))), 8)
    flat = [p for lp in parts for p in lp]
    if tok is not None:
        flat = flat + [tok]
    c1 = 1.0 / (1.0 - ADAM_B1 ** ADAM_STEP)
    c2 = 1.0 / (1.0 - ADAM_B2 ** ADAM_STEP)

    def body(*refs):
        w_ref, m_ref, v_ref = refs[:3]
        p_refs = refs[3:3 + L * np_]
        g_ref, d_ref, nm_ref, nv_ref = refs[-4:]
        layer = pl.program_id(0)
        g = jnp.zeros((tr, C), F32)
        for l in range(L):
            gl = p_refs[l * np_][...]
            for j in range(1, np_):
                gl = gl + p_refs[l * np_ + j][...]
            g = jnp.where(layer == l, gl, g) if L > 1 else gl
        if tok is not None:
            g = g + refs[3 + L * np_][...]
        nm = ADAM_B1 * m_ref[...] + (1.0 - ADAM_B1) * g
        nv = ADAM_B2 * v_ref[...] + (1.0 - ADAM_B2) * (g * g)
        g_ref[...] = g
        nm_ref[...] = nm
        nv_ref[...] = nv
        d_ref[...] = -ADAM_LR * ((nm * c1) / (jnp.sqrt(nv * c2) + ADAM_EPS) + ADAM_WD * w_ref[...])

    stacked = pl.BlockSpec((None, tr, C), lambda l, i: (l, i, 0))
    part_specs = [pl.BlockSpec((tr, C), (lambda l, i, ll=ll: (jnp.where(l == ll, i, 0), 0)))
                  for ll in range(L) for _ in range(np_)]
    if tok is not None:
        part_specs.append(pl.BlockSpec((1, 1), lambda l, i: (0, 0)))
    return pl.pallas_call(
        body, name=name, grid=(L, R // tr), in_specs=[stacked] * 3 + part_specs, out_specs=[stacked] * 4,
        out_shape=[_sds(w.shape, F32)] * 4, compiler_params=_cp("arbitrary", "arbitrary"))(w, m, v, *flat)


_PACK_ROWS = 16


def _pack(arrs):
    pieces = []
    for a in arrs:
        f = a.reshape(-1).astype(F32)
        unit = _PACK_ROWS * LANES
        pad = (-f.shape[0]) % unit
        pieces.append(jnp.pad(f, (0, pad)))
    return jnp.concatenate(pieces).reshape(-1, LANES)


def _unpack(buf, shapes):
    flat = buf.reshape(-1)
    out, off = [], 0
    unit = _PACK_ROWS * LANES
    for s in shapes:
        n = 1
        for d in s:
            n *= d
        out.append(flat[off:off + n].reshape(s))
        off += n + ((-n) % unit)
    return out


def _reduce_start(tag, fulls, after=None):
    others = _swap_halves("rs_swap", fulls, after)
    sums = [_add_halves("rs_add", f, o) for f, o in zip(fulls, others)]
    lands = [lax.empty(t.shape, t.dtype) for t in sums]
    return _split_start("rs_start_" + tag, sums + lands, 3 * len(sums), _scatter_copies)


def _reduce_finish(tag, started, after):
    bufs = _split_wait("rs_wait_" + tag, started, _scatter_copies, after)
    n = len(bufs) // 2
    halves = [_sum_owner("rs_sum", l, s) for s, l in zip(bufs[:n], bufs[n:])]
    return [j.reshape(2 * j.shape[1], j.shape[2]) for j in _join_halves("rs_join", halves)]


def kernel(x, ln_ffn_pre, ffn_pre_w_in, ffn_pre_w_out, ln_mix, ln_ffn_post, ffn_post_w_in, ffn_post_w_out, gm_w_in, gm_v_norm, gm_w_s, gm_b_s, gm_w_out, ssm_w_in, ssm_conv_w, ssm_conv_b, ssm_dt_bias, ssm_a_log, ssm_d, ssm_norm, ssm_w_out, ln_final, loss_target, m_ln_ffn_pre, m_ffn_pre_w_in, m_ffn_pre_w_out, m_ln_mix, m_ln_ffn_post, m_ffn_post_w_in, m_ffn_post_w_out, m_gm_w_in, m_gm_v_norm, m_gm_w_s, m_gm_b_s, m_gm_w_out, m_ssm_w_in, m_ssm_conv_w, m_ssm_conv_b, m_ssm_dt_bias, m_ssm_a_log, m_ssm_d, m_ssm_norm, m_ssm_w_out, m_ln_final, v_ln_ffn_pre, v_ffn_pre_w_in, v_ffn_pre_w_out, v_ln_mix, v_ln_ffn_post, v_ffn_post_w_in, v_ffn_post_w_out, v_gm_w_in, v_gm_v_norm, v_gm_w_s, v_gm_b_s, v_gm_w_out, v_ssm_w_in, v_ssm_conv_w, v_ssm_conv_b, v_ssm_dt_bias, v_ssm_a_log, v_ssm_d, v_ssm_norm, v_ssm_w_out, v_ln_final):
    T, D = x.shape[1], x.shape[2]
    depth = ln_ffn_pre.shape[0]
    n_gm, n_ssm = gm_w_in.shape[0], ssm_w_in.shape[0]
    F = ffn_pre_w_out.shape[1] * 4
    GI = gm_w_out.shape[1] * 4
    GG = gm_w_s.shape[1]
    inner = ssm_w_out.shape[1] * 4
    heads = ssm_dt_bias.shape[1]
    cd = ssm_conv_w.shape[1] * 4
    groups = (cd - inner) // (2 * SSM_STATE)
    pshard = ssm_w_in.shape[2]
    pw = inner + cd + LANES
    dtcol = (inner + cd) // LANES
    kchip = 2 * lax.axis_index("x") + lax.axis_index("y")

    small_shard = _pack([jnp.swapaxes(ssm_conv_w, 1, 2), ssm_conv_b, ssm_norm])
    small_all = _all_gather_chips("ag_small", small_shard)
    cq, iq = cd // 4, inner // 4
    parts = [_unpack(small_all[k], [(n_ssm, SSM_CONV, cq), (n_ssm, cq), (n_ssm, iq)]) for k in range(4)]
    conv_wt = jnp.concatenate([p[0] for p in parts], axis=2)
    conv_b = jnp.concatenate([p[1] for p in parts], axis=1)
    norm_g = jnp.concatenate([p[2] for p in parts], axis=1)

    def pad_lanes(v):
        return jnp.pad(v, (0, LANES - v.shape[0]))[None, :]

    xc = x[0]
    saved = []

    def shards_of(i, kind):
        if kind == "pre":
            return [(ffn_pre_w_in, i), (ffn_pre_w_out, i)]
        if kind == "post":
            return [(ffn_post_w_in, i), (ffn_post_w_out, i)]
        if i % 2 == 0:
            return [(gm_w_in, i // 2), (gm_w_out, i // 2)]
        return [(ssm_w_in, i // 2), (ssm_w_out, i // 2)]

    subs = [(i, kind) for i in range(depth) for kind in ("pre", "mix", "post")]
    nsub = len(subs)
    gathers = [None] * nsub
    ahead = 2

    def gather_start(s, after):
        lands = [_cast_into_block("ag_cast", w, l, after) for w, l in shards_of(*subs[s])]
        gathers[s] = _split_start("ag_start_%d" % s, lands, 3 * len(lands), _gather_copies, north_only=True)

    def gathered(s, after):
        lands = _split_wait("ag_wait_%d" % s, gathers[s], _gather_copies, after, north_only=True)
        full = _share_sibling("ag_share", lands)
        tok = jnp.zeros((1, 1), F32)
        for nxt in range(s + 1, min(s + ahead, nsub - 1) + 1):
            if gathers[nxt] is None:
                gather_start(nxt, full[0])
                tok = tok + gathers[nxt][3][0:1, 0:1]
        return full, tok

    gather_start(0, xc)

    def ffn_fwd(s, xin, g, l):
        (wi, wo), tok = gathered(s, xin)
        wo = wo.reshape(F, D)
        h = _rms_fwd("rms_fwd", xin, g[l][None, :] + tok)
        a = _matmul("ffn_in", h, wi, "nn", BF16, out_blocks=2)
        s_ = _swiglu_fwd("swiglu_fwd", a)
        xo = _matmul("ffn_out", s_, wo, "nn", F32, res=xin, scale=0.5)
        return xo, (xin, h, a, s_, wi, wo)

    for i in range(depth):
        xc, sv_pre = ffn_fwd(3 * i, xc, ln_ffn_pre, i)
        j = i // 2
        (wi, wo), tok = gathered(3 * i + 1, xc)
        h = _rms_fwd("rms_fwd", xc, ln_mix[i][None, :] + tok)
        if i % 2 == 0:
            wo = wo.reshape(GI, D)
            zpre = _matmul("gm_in", h, wi, "nn", BF16, out_blocks=2)
            u, vn = _gm_act_fwd("gm_act_fwd", zpre, gm_v_norm[j][None, :])
            bt = jnp.pad(gm_b_s[j].T, ((0, 0), (0, LANES - GG)))
            gated = _gm_spatial_fwd("gm_spatial_fwd", u, vn, gm_w_s[j], bt)
            xn = _matmul("mix_out", gated, wo, "nn", F32, res=xc, scale=1.0)
            sv_mix = (xc, h, zpre, u, vn, bt, gated, wi, wo)
        else:
            wg, wo = wi, wo.reshape(inner, D)
            wp = jnp.pad(jnp.swapaxes(wg, 0, 1).reshape(D, 4 * pshard), ((0, 0), (0, pw - 4 * pshard)))
            proj = _matmul("ssm_in", h, wp, "nn", F32)
            wt, cb_ = conv_wt[j], conv_b[j][None, :]
            act = _ssd_conv_fwd("ssd_conv_fwd", proj, wt, cb_, inner, cd)
            bias, alog = pad_lanes(ssm_dt_bias[j]), pad_lanes(ssm_a_log[j])
            dtp, acum = _ssd_dt_fwd("ssd_dt_fwd", proj, bias, alog, dtcol, heads)
            dexp = jnp.repeat(ssm_d[j], SSM_HEAD_DIM)[None, :]
            ycore, sprev = _ssd_core_fwd("ssd_core_fwd", act, dtp, acum, dexp, inner, groups)
            ng = norm_g[j][None, :]
            yn = _ssd_post_fwd("ssd_post_fwd", ycore, proj, ng, inner, groups)
            xn = _matmul("mix_out", yn, wo, "nn", F32, res=xc, scale=1.0)
            sv_mix = (xc, h, proj, act, dtp, acum, dexp, ycore, sprev, yn, wt, cb_, bias, alog, ng, wp, wo)
        xc = xn
        xc, sv_post = ffn_fwd(3 * i + 2, xc, ln_ffn_post, i)
        saved.append((sv_pre, sv_mix, sv_post))

    loss_tile, dx, dxb, dg_final = _loss_head("loss_head", xc, ln_final[None, :], loss_target[0])
    loss = lax.psum(loss_tile[0, 0], ("x", "y", "c"))

    wgrad = [None] * nsub
    in_flight = []
    deferred = []

    def reduce_finish(after):
        while in_flight:
            s, st = in_flight.pop(0)
            wgrad[s] = _reduce_finish(str(s), st, after)

    def reduce_start(s, fulls):
        reduce_finish(fulls[0])
        if s == 0:
            deferred.append(fulls)
            return jnp.zeros((1, 1), F32)
        st = _reduce_start(str(s), fulls)
        in_flight.append((s, st))
        return st[3][0:1, 0:1]

    def ffn_bwd(s, dx, dxb, sv, g, l):
        xin, h, a, s_, wi, wo = sv
        ds = _matmul("ffn_ds", dxb, wo, "nt", BF16, scale=0.5)
        dwo = _matmul("ffn_dwo", s_, dxb, "tn", BF16, scale=0.5)
        da = _swiglu_bwd("swiglu_bwd", a, ds)
        dh = _matmul("ffn_dh", da, wi, "nt", F32)
        dwi = _matmul("ffn_dwi", h, da, "tn", BF16, out_blocks=4)
        tok = reduce_start(s, [dwi, dwo.reshape(4, F // 4, D)])
        return _rms_bwd("rms_bwd", xin, g[l][None, :] + tok, dh, dx)

    gl = {n: [None] * depth for n in ("pre", "mix", "post")}
    gm_g = {n: [None] * n_gm for n in ("vnorm", "ws", "bs")}
    ssm_g = {n: [None] * n_ssm for n in ("convw", "convb", "dtb", "alog", "d", "norm")}

    for i in reversed(range(depth)):
        sv_pre, sv_mix, sv_post = saved[i]
        j = i // 2
        dx, dxb, gl["post"][i] = ffn_bwd(3 * i + 2, dx, dxb, sv_post, ln_ffn_post, i)
        if i % 2 == 0:
            xin, h, zpre, u, vn, bt, gated, wi, wo = sv_mix
            dgated = _matmul("mix_dy", dxb, wo, "nt", BF16)
            dwo = _matmul("mix_dwo", gated, dxb, "tn", BF16)
            du, dvn, dws, dbt = _gm_spatial_bwd("gm_spatial_bwd", dgated, u, vn, gm_w_s[j], bt)
            dzpre, dvnorm = _gm_act_bwd("gm_act_bwd", zpre, du, dvn, gm_v_norm[j][None, :])
            dh = _matmul("gm_dh", dzpre, wi, "nt", F32)
            dwi = _matmul("gm_dwi", h, dzpre, "tn", BF16, out_blocks=4)
            tok = reduce_start(3 * i + 1, [dwi, dwo.reshape(4, GI // 4, D)])
            gm_g["vnorm"][j], gm_g["ws"][j], gm_g["bs"][j] = dvnorm[0], dws, dbt.T[:GG]
        else:
            xin, h, proj, act, dtp, acum, dexp, ycore, sprev, yn, wt, cb_, bias, alog, ng, wp, wo = sv_mix
            dyn = _matmul("ssm_dy", dxb, wo, "nt", F32)
            dwo = _matmul("mix_dwo", yn, dxb, "tn", BF16)
            dyc, dz, dnorm = _ssd_post_bwd("ssd_post_bwd", dyn, ycore, proj, ng, inner, groups)
            dxs, db_, dc_, ddtp_g, dac_g, dd_g = _ssd_core_bwd("ssd_core_bwd", dyc, act, dtp, acum, dexp, sprev,
                                                               inner, groups)
            ddt, dbias, dalog, dds = _ssd_dt_bwd("ssd_dt_bwd", ddtp_g, dac_g, dd_g, proj, bias, alog, dtp, dtcol, heads)
            dact = jnp.concatenate([dxs, db_, dc_], axis=1)
            dxbc, dwt, dcb = _ssd_conv_bwd("ssd_conv_bwd", dact, proj, wt, cb_, inner, cd)
            dproj = jnp.concatenate([dz, dxbc, ddt], axis=1)
            dh = _matmul("ssm_dh", dproj, wp, "nt", F32)
            dwp = _matmul("ssm_dwi", h, dproj, "tn", BF16)
            dwi = jnp.swapaxes(dwp[:, :4 * pshard].reshape(D, 4, pshard), 0, 1)
            tok = reduce_start(3 * i + 1, [dwi, dwo.reshape(4, inner // 4, D)])
            ssm_g["convw"][j], ssm_g["convb"][j] = dwt.T, dcb[0]
            ssm_g["dtb"][j], ssm_g["alog"][j], ssm_g["d"][j] = dbias[0, :heads], dalog[0, :heads], dds[0, :heads]
            ssm_g["norm"][j] = dnorm[0]
        dx, dxb, gl["mix"][i] = _rms_bwd("rms_bwd", xin, ln_mix[i][None, :] + tok, dh, dx)
        dx, dxb, gl["pre"][i] = ffn_bwd(3 * i, dx, dxb, sv_pre, ln_ffn_pre, i)

    small_full = [
        jnp.concatenate(gl["pre"], 0), jnp.concatenate(gl["mix"], 0), jnp.concatenate(gl["post"], 0),
        jnp.stack(gm_g["vnorm"]), jnp.stack(gm_g["ws"]), jnp.stack(gm_g["bs"]),
        jnp.stack(ssm_g["convw"]), jnp.stack(ssm_g["convb"]), jnp.stack(ssm_g["dtb"]), jnp.stack(ssm_g["alog"]),
        jnp.stack(ssm_g["d"]), jnp.stack(ssm_g["norm"]), dg_final[0],
    ]
    full_shapes = [tuple(a.shape) for a in small_full]
    gathered = _all_gather_devices("ag8_small", _pack(small_full))
    summed = _sum_blocks("sum8_small", gathered)
    (g_pre, g_mix, g_post, g_vn, g_ws, g_bs, g_cw, g_cb, g_dtb, g_al, g_d, g_nm, g_fin) = _unpack(summed, full_shapes)
    g_cw = lax.dynamic_slice_in_dim(g_cw, kchip * cq, cq, axis=1)
    g_cb = lax.dynamic_slice_in_dim(g_cb, kchip * cq, cq, axis=1)
    g_nm = lax.dynamic_slice_in_dim(g_nm, kchip * iq, iq, axis=1)
    sm_g = [g_pre, g_mix, g_post, g_vn, g_ws, g_bs, g_cw, g_cb, g_dtb, g_al, g_d, g_nm, g_fin]
    sm_w = [ln_ffn_pre, ln_mix, ln_ffn_post, gm_v_norm, gm_w_s, gm_b_s, ssm_conv_w, ssm_conv_b, ssm_dt_bias,
            ssm_a_log, ssm_d, ssm_norm, ln_final]
    sm_m = [m_ln_ffn_pre, m_ln_mix, m_ln_ffn_post, m_gm_v_norm, m_gm_w_s, m_gm_b_s, m_ssm_conv_w, m_ssm_conv_b,
            m_ssm_dt_bias, m_ssm_a_log, m_ssm_d, m_ssm_norm, m_ln_final]
    sm_v = [v_ln_ffn_pre, v_ln_mix, v_ln_ffn_post, v_gm_v_norm, v_gm_w_s, v_gm_b_s, v_ssm_conv_w, v_ssm_conv_b,
            v_ssm_dt_bias, v_ssm_a_log, v_ssm_d, v_ssm_norm, v_ln_final]
    sm_shapes = [tuple(a.shape) for a in sm_w]
    pk = [_pack(lst)[None] for lst in (sm_w, sm_m, sm_v)]
    sg_, sd_, snm_, snv_ = _adamw("adamw_small", pk[0], pk[1], pk[2], [[_pack(sm_g)]])
    small_out = [_unpack(t[0], sm_shapes) for t in (sg_, sd_, snm_, snv_)]
    small_names = ["ln_ffn_pre", "ln_mix", "ln_ffn_post", "gm_v_norm", "gm_w_s", "gm_b_s", "ssm_conv_w", "ssm_conv_b",
                   "ssm_dt_bias", "ssm_a_log", "ssm_d", "ssm_norm", "ln_final"]

    def big_update(tag, w, m, v, sub_ids, which, tok=None):
        L = w.shape[0]
        shp = w.shape
        w2, m2, v2 = (t.reshape(L, -1, shp[-1]) for t in (w, m, v))
        outs = _adamw("adamw_" + tag, w2, m2, v2, [[wgrad[s][which]] for s in sub_ids], tok)
        return [o.reshape(shp) for o in outs]

    last = _reduce_start("0", deferred[0], after=sd_)
    tok = last[3][0:1, 0:1]
    pre_ids = [3 * i for i in range(depth)]
    post_ids = [3 * i + 2 for i in range(depth)]
    gm_ids = [3 * i + 1 for i in range(depth) if i % 2 == 0]
    ssm_ids = [3 * i + 1 for i in range(depth) if i % 2 == 1]
    big_out = {
        "ffn_post_w_in": big_update("ffn_in", ffn_post_w_in, m_ffn_post_w_in, v_ffn_post_w_in, post_ids, 0, tok),
        "ffn_post_w_out": big_update("ffn_out", ffn_post_w_out, m_ffn_post_w_out, v_ffn_post_w_out, post_ids, 1, tok),
        "gm_w_in": big_update("gm_in", gm_w_in, m_gm_w_in, v_gm_w_in, gm_ids, 0, tok),
        "gm_w_out": big_update("mix_out", gm_w_out, m_gm_w_out, v_gm_w_out, gm_ids, 1, tok),
        "ssm_w_in": big_update("ssm_in", ssm_w_in, m_ssm_w_in, v_ssm_w_in, ssm_ids, 0, tok),
        "ssm_w_out": big_update("mix_out", ssm_w_out, m_ssm_w_out, v_ssm_w_out, ssm_ids, 1, tok),
    }
    after = big_out["ffn_post_w_in"][1][0, 0:1, 0:1]
    for n in ("ffn_post_w_out", "gm_w_in", "gm_w_out", "ssm_w_in", "ssm_w_out"):
        after = after + big_out[n][1][0, 0:1, 0:1]
    wgrad[0] = _reduce_finish("0", last, after)
    big_out["ffn_pre_w_in"] = big_update("ffn_in", ffn_pre_w_in, m_ffn_pre_w_in, v_ffn_pre_w_in, pre_ids, 0)
    big_out["ffn_pre_w_out"] = big_update("ffn_out", ffn_pre_w_out, m_ffn_pre_w_out, v_ffn_pre_w_out, pre_ids, 1)

    order = ["ln_ffn_pre", "ffn_pre_w_in", "ffn_pre_w_out", "ln_mix", "ln_ffn_post", "ffn_post_w_in", "ffn_post_w_out",
             "gm_w_in", "gm_v_norm", "gm_w_s", "gm_b_s", "gm_w_out", "ssm_w_in", "ssm_conv_w", "ssm_conv_b",
             "ssm_dt_bias", "ssm_a_log", "ssm_d", "ssm_norm", "ssm_w_out", "ln_final"]

    def pick(kind, n):
        if n in big_out:
            return big_out[n][kind]
        return small_out[kind][small_names.index(n)]

    outs = [loss, dx[None]]
    for kind in range(4):
        outs.extend(pick(kind, n) for n in order)
    return tuple(outs)
```

```python
import jax
import jax.numpy as jnp
from jax import lax
from jax.experimental import pallas as pl
from jax.experimental.pallas import tpu as pltpu

F32 = jnp.float32
BF16 = jnp.bfloat16
HIGHEST = lax.Precision.HIGHEST
MESH = pl.DeviceIdType.MESH

EPS = 1e-6
ADAM_LR, ADAM_B1, ADAM_B2, ADAM_EPS, ADAM_WD, ADAM_STEP = 0.001, 0.9, 0.999, 1e-08, 0.01, 10

LANES = 128
CHUNK = 128
SSM_STATE = 128
SSM_HEAD_DIM = 64
SSM_HPG = 8
SSM_CONV = 4
VMEM_LIMIT = 56 * 1024 * 1024
MM_TILE = 1408
MM_VMEM_BUDGET = 46 * 1024 * 1024

_ANY = pl.BlockSpec(memory_space=pl.ANY)


def _cp(*sem):
    return pltpu.CompilerParams(dimension_semantics=sem if sem else None, vmem_limit_bytes=VMEM_LIMIT)


def _tile(n, target, mult=LANES):
    best = None
    t = mult
    while t <= min(n, target):
        if n % t == 0:
            best = t
        t += mult
    return n if best is None else best


def _gcd(*v):
    import math
    g = 0
    for a in v:
        g = math.gcd(g, a)
    return g


def _sds(shape, dtype):
    return jax.ShapeDtypeStruct(tuple(shape), dtype)


def _ldims(shape):
    return tuple(shape) if len(shape) == 2 else (shape[1], shape[0] * shape[2])


def _colblock(shape):
    return None if len(shape) == 2 else shape[2]


def _mspec(shape, tr, tc, rc):
    if len(shape) == 2:
        return pl.BlockSpec((tr, tc), rc)
    per = shape[2] // tc

    def im(i, j, k):
        r, c = rc(i, j, k)
        return (c // per, r, c % per)

    return pl.BlockSpec((None, tr, tc), im)


def _matmul(name, a, b, mode, out_dtype, out_blocks=None, res=None, scale=1.0, tok=None):
    la, lb = _ldims(a.shape), _ldims(b.shape)
    if mode == "nn":
        (M, K), (K2, N) = la, lb
    elif mode == "nt":
        (M, K), (N, K2) = la, lb
    else:
        (K, M), (K2, N) = la, lb
    assert K == K2, (name, a.shape, b.shape, mode)
    ca, cb = _colblock(a.shape), _colblock(b.shape)
    out_shape = (M, N) if out_blocks is None else (out_blocks, M, N // out_blocks)
    co = _colblock(out_shape)
    m_c, n_c, k_c = [M], [N], [K]
    if ca is not None:
        (m_c if mode == "tn" else k_c).append(ca)
    if cb is not None:
        (k_c if mode == "nt" else n_c).append(cb)
    if co is not None:
        n_c.append(co)
    tm, tn = _tile(_gcd(*m_c), MM_TILE), _tile(_gcd(*n_c), MM_TILE)
    out_bytes = jnp.dtype(out_dtype).itemsize + (4 if res is not None else 0)
    kg = _gcd(*k_c)
    tk = LANES if kg % LANES == 0 else kg
    for cand in range(LANES, kg + 1, LANES):
        if kg % cand == 0 and 2 * (2 * cand * (tm + tn) + tm * tn * out_bytes) + 2 * 4 * tm * tn <= MM_VMEM_BUDGET:
            tk = cand
    nk = K // tk
    if mode == "tn":
        a_spec = _mspec(a.shape, tk, tm, lambda i, j, k: (k, i))
        dims = (((0,), (0,)), ((), ()))
    else:
        a_spec = _mspec(a.shape, tm, tk, lambda i, j, k: (i, k))
        dims = (((1,), (1,)), ((), ())) if mode == "nt" else (((1,), (0,)), ((), ()))
    if mode == "nt":
        b_spec = _mspec(b.shape, tn, tk, lambda i, j, k: (j, k))
    else:
        b_spec = _mspec(b.shape, tk, tn, lambda i, j, k: (k, j))
    o_spec = _mspec(out_shape, tm, tn, lambda i, j, k: (i, j))
    in_specs, args = [a_spec, b_spec], [a, b]
    if res is not None:
        in_specs.append(pl.BlockSpec((tm, tn), lambda i, j, k: (i, j)))
        args.append(res)
    if tok is not None:
        in_specs.append(pl.BlockSpec((1, 1), lambda i, j, k: (0, 0)))
        args.append(tok)
    n_in = len(args)

    def body(*refs):
        a_ref, b_ref = refs[0], refs[1]
        res_ref = refs[2] if res is not None else None
        tok_ref = refs[n_in - 1] if tok is not None else None
        o_ref = refs[n_in]

        def product():
            return lax.dot_general(a_ref[...].astype(BF16), b_ref[...].astype(BF16), dims, preferred_element_type=F32)

        def finish(r):
            if scale != 1.0:
                r = r * scale
            if res_ref is not None:
                r = res_ref[...] + r
            if tok_ref is not None:
                r = r + tok_ref[...]
            o_ref[...] = r.astype(o_ref.dtype)

        if nk == 1:
            finish(product())
            return
        acc_ref = refs[-1]
        kk = pl.program_id(2)

        @pl.when(kk == 0)
        def _():
            acc_ref[...] = product()

        @pl.when((kk > 0) & (kk < nk - 1))
        def _():
            acc_ref[...] += product()

        @pl.when(kk == nk - 1)
        def _():
            finish(acc_ref[...] + product())

    return pl.pallas_call(
        body, name=name, grid=(M // tm, N // tn, nk), in_specs=in_specs, out_specs=o_spec,
        out_shape=_sds(out_shape, out_dtype), scratch_shapes=[pltpu.VMEM((tm, tn), F32)] if nk > 1 else [],
        compiler_params=_cp("parallel", "parallel", "arbitrary"))(*args)


def _rows(i):
    return (i, 0)


def _row0(i):
    return (0, 0)


def _rms_fwd(name, x, g):
    T, D = x.shape
    tr = _tile(T, 256, 8)

    def body(x_ref, g_ref, o_ref):
        xv = x_ref[...]
        r = lax.rsqrt(jnp.mean(xv * xv, axis=-1, keepdims=True) + EPS)
        o_ref[...] = (xv * r * g_ref[...]).astype(o_ref.dtype)

    return pl.pallas_call(
        body, name=name, grid=(T // tr,),
        in_specs=[pl.BlockSpec((tr, D), _rows), pl.BlockSpec((1, D), _row0)],
        out_specs=pl.BlockSpec((tr, D), _rows), out_shape=_sds((T, D), BF16),
        compiler_params=_cp("parallel"))(x, g)


def _rms_bwd(name, x, g, dh, dres):
    T, D = x.shape
    tr = _tile(T, 256, 8)

    def body(x_ref, g_ref, dh_ref, dres_ref, dx_ref, dxb_ref, dg_ref):
        xv = x_ref[...]
        r = lax.rsqrt(jnp.mean(xv * xv, axis=-1, keepdims=True) + EPS)
        xh = xv * r
        dhv = dh_ref[...]
        dxh = dhv * g_ref[...]
        dx = dres_ref[...] + r * (dxh - xh * jnp.mean(dxh * xh, axis=-1, keepdims=True))
        dx_ref[...] = dx
        dxb_ref[...] = dx.astype(BF16)

        @pl.when(pl.program_id(0) == 0)
        def _():
            dg_ref[...] = jnp.zeros_like(dg_ref)

        dg_ref[...] += jnp.sum(dhv * xh, axis=0, keepdims=True)

    return pl.pallas_call(
        body, name=name, grid=(T // tr,),
        in_specs=[pl.BlockSpec((tr, D), _rows), pl.BlockSpec((1, D), _row0),
                  pl.BlockSpec((tr, D), _rows), pl.BlockSpec((tr, D), _rows)],
        out_specs=[pl.BlockSpec((tr, D), _rows), pl.BlockSpec((tr, D), _rows), pl.BlockSpec((1, D), _row0)],
        out_shape=[_sds((T, D), F32), _sds((T, D), BF16), _sds((1, D), F32)],
        compiler_params=_cp("arbitrary"))(x, g, dh, dres)


def _sigmoid(v):
    return 1.0 / (1.0 + jnp.exp(-v))


def _swiglu_fwd(name, a):
    _, T, F = a.shape
    tr, tc = _tile(T, 512, 8), _tile(F, MM_TILE)

    def body(a_ref, o_ref):
        gate = a_ref[0].astype(F32)
        up = a_ref[1].astype(F32)
        o_ref[...] = (gate * _sigmoid(gate) * up).astype(o_ref.dtype)

    return pl.pallas_call(
        body, name=name, grid=(T // tr, F // tc),
        in_specs=[pl.BlockSpec((2, tr, tc), lambda i, j: (0, i, j))],
        out_specs=pl.BlockSpec((tr, tc), lambda i, j: (i, j)), out_shape=_sds((T, F), BF16),
        compiler_params=_cp("parallel", "parallel"))(a)


def _swiglu_bwd(name, a, ds):
    _, T, F = a.shape
    tr, tc = _tile(T, 512, 8), _tile(F, MM_TILE)

    def body(a_ref, ds_ref, o_ref):
        gate = a_ref[0].astype(F32)
        up = a_ref[1].astype(F32)
        dsv = ds_ref[...].astype(F32)
        sg = _sigmoid(gate)
        o_ref[0] = (dsv * up * sg * (1.0 + gate * (1.0 - sg))).astype(o_ref.dtype)
        o_ref[1] = (dsv * gate * sg).astype(o_ref.dtype)

    return pl.pallas_call(
        body, name=name, grid=(T // tr, F // tc),
        in_specs=[pl.BlockSpec((2, tr, tc), lambda i, j: (0, i, j)), pl.BlockSpec((tr, tc), lambda i, j: (i, j))],
        out_specs=pl.BlockSpec((2, tr, tc), lambda i, j: (0, i, j)), out_shape=_sds((2, T, F), BF16),
        compiler_params=_cp("parallel", "parallel"))(a, ds)


def _loss_head(name, x, g, tgt):
    T, D = x.shape
    tr = _tile(T, 256, 8)

    def body(x_ref, g_ref, t_ref, loss_ref, dx_ref, dxb_ref, dg_ref):
        xv = x_ref[...]
        gv = g_ref[...]
        r = lax.rsqrt(jnp.mean(xv * xv, axis=-1, keepdims=True) + EPS)
        xh = xv * r
        err = xh * gv - t_ref[...]
        dy = err * (1.0 / D)
        dxh = dy * gv
        dx = r * (dxh - xh * jnp.mean(dxh * xh, axis=-1, keepdims=True))
        dx_ref[...] = dx
        dxb_ref[...] = dx.astype(BF16)

        @pl.when(pl.program_id(0) == 0)
        def _():
            dg_ref[...] = jnp.zeros_like(dg_ref)
            loss_ref[...] = jnp.zeros_like(loss_ref)

        dg_ref[...] += jnp.sum(dy * xh, axis=0, keepdims=True)
        part = jnp.sum(jnp.sum(err * err, axis=-1, keepdims=True), axis=0, keepdims=True) * (0.5 / D)
        loss_ref[...] += part

    return pl.pallas_call(
        body, name=name, grid=(T // tr,),
        in_specs=[pl.BlockSpec((tr, D), _rows), pl.BlockSpec((1, D), _row0), pl.BlockSpec((tr, D), _rows)],
        out_specs=[pl.BlockSpec((8, LANES), _row0), pl.BlockSpec((tr, D), _rows), pl.BlockSpec((tr, D), _rows),
                   pl.BlockSpec((1, D), _row0)],
        out_shape=[_sds((8, LANES), F32), _sds((T, D), F32), _sds((T, D), BF16), _sds((1, D), F32)],
        compiler_params=_cp("arbitrary"))(x, g, tgt)


_SQRT_HALF = 0.7071067811865476
_INV_SQRT_2PI = 0.3989422804014327


def _gelu(v):
    return 0.5 * v * (1.0 + lax.erf(v * _SQRT_HALF))


def _gelu_grad(v):
    return 0.5 * (1.0 + lax.erf(v * _SQRT_HALF)) + v * _INV_SQRT_2PI * jnp.exp(-0.5 * v * v)


def _group_expand(rows, width, gd):
    gi = lax.broadcasted_iota(jnp.int32, (rows, width), 0)
    fi = lax.broadcasted_iota(jnp.int32, (rows, width), 1)
    return ((fi >= gi * gd) & (fi < (gi + 1) * gd)).astype(F32)


def _gm_act_fwd(name, zpre, vnorm):
    _, T, GI = zpre.shape
    tr = _tile(T, 128, 8)

    def body(z_ref, g_ref, u_ref, v_ref):
        u_ref[...] = _gelu(z_ref[0].astype(F32)).astype(BF16)
        zv = _gelu(z_ref[1].astype(F32))
        r = lax.rsqrt(jnp.mean(zv * zv, axis=-1, keepdims=True) + EPS)
        v_ref[...] = (zv * r * g_ref[...]).astype(BF16)

    return pl.pallas_call(
        body, name=name, grid=(T // tr,),
        in_specs=[pl.BlockSpec((2, tr, GI), lambda i: (0, i, 0)), pl.BlockSpec((1, GI), _row0)],
        out_specs=[pl.BlockSpec((tr, GI), _rows), pl.BlockSpec((tr, GI), _rows)],
        out_shape=[_sds((T, GI), BF16), _sds((T, GI), BF16)], compiler_params=_cp("parallel"))(zpre, vnorm)


def _gm_act_bwd(name, zpre, du, dvn, vnorm):
    _, T, GI = zpre.shape
    tr = _tile(T, 128, 8)

    def body(z_ref, du_ref, dvn_ref, g_ref, dz_ref, dg_ref):
        xu = z_ref[0].astype(F32)
        xv = z_ref[1].astype(F32)
        zv = _gelu(xv)
        r = lax.rsqrt(jnp.mean(zv * zv, axis=-1, keepdims=True) + EPS)
        xh = zv * r
        dv = dvn_ref[...]

        @pl.when(pl.program_id(0) == 0)
        def _():
            dg_ref[...] = jnp.zeros_like(dg_ref)

        dg_ref[...] += jnp.sum(dv * xh, axis=0, keepdims=True)
        dxh = dv * g_ref[...]
        dzv = r * (dxh - xh * jnp.mean(dxh * xh, axis=-1, keepdims=True))
        dz_ref[0] = (du_ref[...] * _gelu_grad(xu)).astype(BF16)
        dz_ref[1] = (dzv * _gelu_grad(xv)).astype(BF16)

    return pl.pallas_call(
        body, name=name, grid=(T // tr,),
        in_specs=[pl.BlockSpec((2, tr, GI), lambda i: (0, i, 0)), pl.BlockSpec((tr, GI), _rows),
                  pl.BlockSpec((tr, GI), _rows), pl.BlockSpec((1, GI), _row0)],
        out_specs=[pl.BlockSpec((2, tr, GI), lambda i: (0, i, 0)), pl.BlockSpec((1, GI), _row0)],
        out_shape=[_sds((2, T, GI), BF16), _sds((1, GI), F32)], compiler_params=_cp("arbitrary"))(zpre, du, dvn, vnorm)


def _causal_mask():
    r = lax.broadcasted_iota(jnp.int32, (CHUNK, CHUNK), 0)
    c = lax.broadcasted_iota(jnp.int32, (CHUNK, CHUNK), 1)
    return r >= c


def _gm_spatial_fwd(name, u, vn, ws, bt):
    T, GI = u.shape
    G = ws.shape[0]
    gd = GI // G

    def body(u_ref, v_ref, ws_ref, bt_ref, o_ref, bias_scr):
        @pl.when(pl.program_id(0) == 0)
        def _():
            bias_scr[...] = jnp.dot(bt_ref[...], _group_expand(LANES, GI, gd), precision=HIGHEST,
                                    preferred_element_type=F32)

        causal = _causal_mask()
        for g in range(G):
            sl = slice(g * gd, (g + 1) * gd)
            wc = jnp.where(causal, ws_ref[g], 0.0).astype(BF16)
            mixed = jnp.dot(wc, v_ref[:, sl], preferred_element_type=F32) + bias_scr[:, sl]
            o_ref[:, sl] = (u_ref[:, sl].astype(F32) * mixed).astype(o_ref.dtype)

    return pl.pallas_call(
        body, name=name, grid=(T // CHUNK,),
        in_specs=[pl.BlockSpec((CHUNK, GI), _rows), pl.BlockSpec((CHUNK, GI), _rows),
                  pl.BlockSpec((G, CHUNK, CHUNK), lambda i: (0, 0, 0)), pl.BlockSpec((CHUNK, LANES), _row0)],
        out_specs=pl.BlockSpec((CHUNK, GI), _rows), out_shape=_sds((T, GI), BF16),
        scratch_shapes=[pltpu.VMEM((CHUNK, GI), F32)], compiler_params=_cp("arbitrary"))(u, vn, ws, bt)


def _gm_spatial_bwd(name, dgated, u, vn, ws, bt):
    T, GI = u.shape
    G = ws.shape[0]
    gd = GI // G
    nc = T // CHUNK

    def body(dg_ref, u_ref, v_ref, ws_ref, bt_ref, du_ref, dv_ref, dws_ref, dbt_ref, bias_scr, dm_scr):
        step = pl.program_id(0)

        @pl.when(step == 0)
        def _():
            bias_scr[...] = jnp.dot(bt_ref[...], _group_expand(LANES, GI, gd), precision=HIGHEST,
                                    preferred_element_type=F32)
            dm_scr[...] = jnp.zeros_like(dm_scr)
            dws_ref[...] = jnp.zeros_like(dws_ref)

        causal = _causal_mask()
        for g in range(G):
            sl = slice(g * gd, (g + 1) * gd)
            wc = jnp.where(causal, ws_ref[g], 0.0).astype(BF16)
            vv = v_ref[:, sl]
            dgv = dg_ref[:, sl].astype(F32)
            mixed = jnp.dot(wc, vv, preferred_element_type=F32) + bias_scr[:, sl]
            du_ref[:, sl] = dgv * mixed
            dm = dgv * u_ref[:, sl].astype(F32)
            dmb = dm.astype(BF16)
            dv_ref[:, sl] = lax.dot_general(wc, dmb, (((0,), (0,)), ((), ())), preferred_element_type=F32)
            dw = lax.dot_general(dmb, vv, (((1,), (1,)), ((), ())), preferred_element_type=F32)
            dws_ref[g] += jnp.where(causal, dw, 0.0)
            dm_scr[:, sl] += dm

        @pl.when(step == nc - 1)
        def _():
            dbt_ref[...] = lax.dot_general(dm_scr[...], _group_expand(LANES, GI, gd), (((1,), (1,)), ((), ())),
                                           precision=HIGHEST, preferred_element_type=F32)

    return pl.pallas_call(
        body, name=name, grid=(nc,),
        in_specs=[pl.BlockSpec((CHUNK, GI), _rows), pl.BlockSpec((CHUNK, GI), _rows), pl.BlockSpec((CHUNK, GI), _rows),
                  pl.BlockSpec((G, CHUNK, CHUNK), lambda i: (0, 0, 0)), pl.BlockSpec((CHUNK, LANES), _row0)],
        out_specs=[pl.BlockSpec((CHUNK, GI), _rows), pl.BlockSpec((CHUNK, GI), _rows),
                   pl.BlockSpec((G, CHUNK, CHUNK), lambda i: (0, 0, 0)), pl.BlockSpec((CHUNK, LANES), _row0)],
        out_shape=[_sds((T, GI), F32), _sds((T, GI), F32), _sds((G, CHUNK, CHUNK), F32), _sds((CHUNK, LANES), F32)],
        scratch_shapes=[pltpu.VMEM((CHUNK, GI), F32), pltpu.VMEM((CHUNK, GI), F32)],
        compiler_params=_cp("arbitrary"))(dgated, u, vn, ws, bt)


def _conv_taps(xv, w_ref, b_ref):
    rows = lax.broadcasted_iota(jnp.int32, xv.shape, 0)
    acc = xv * w_ref[pl.ds(SSM_CONV - 1, 1), :] + b_ref[...]
    for k in range(1, SSM_CONV):
        sh = jnp.where(rows >= k, pltpu.roll(xv, k, 0), 0.0)
        acc = acc + sh * w_ref[pl.ds(SSM_CONV - 1 - k, 1), :]
    return acc


def _ssd_conv_fwd(name, proj, wt, b, inner, cd):
    T = proj.shape[0]
    tc = _tile(_gcd(inner, cd), 512)
    off = inner // tc

    def body(x_ref, w_ref, b_ref, o_ref):
        pre = _conv_taps(x_ref[...], w_ref, b_ref)
        o_ref[...] = pre * _sigmoid(pre)

    return pl.pallas_call(
        body, name=name, grid=(cd // tc,),
        in_specs=[pl.BlockSpec((T, tc), lambda j: (0, off + j)), pl.BlockSpec((SSM_CONV, tc), lambda j: (0, j)),
                  pl.BlockSpec((1, tc), lambda j: (0, j))],
        out_specs=pl.BlockSpec((T, tc), lambda j: (0, j)), out_shape=_sds((T, cd), F32),
        compiler_params=_cp("parallel"))(proj, wt, b)


def _ssd_conv_bwd(name, dact, proj, wt, b, inner, cd):
    T = proj.shape[0]
    tc = _tile(_gcd(inner, cd), 512)
    off = inner // tc

    def body(da_ref, x_ref, w_ref, b_ref, dx_ref, dw_ref, db_ref):
        xv = x_ref[...]
        pre = _conv_taps(xv, w_ref, b_ref)
        sg = _sigmoid(pre)
        dpre = da_ref[...] * sg * (1.0 + pre * (1.0 - sg))
        rows = lax.broadcasted_iota(jnp.int32, xv.shape, 0)
        db_ref[...] = jnp.sum(dpre, axis=0, keepdims=True)
        dx = dpre * w_ref[pl.ds(SSM_CONV - 1, 1), :]
        dw_ref[pl.ds(SSM_CONV - 1, 1), :] = jnp.sum(dpre * xv, axis=0, keepdims=True)
        for k in range(1, SSM_CONV):
            sh = jnp.where(rows >= k, pltpu.roll(xv, k, 0), 0.0)
            dw_ref[pl.ds(SSM_CONV - 1 - k, 1), :] = jnp.sum(dpre * sh, axis=0, keepdims=True)
            fw = jnp.where(rows < T - k, pltpu.roll(dpre, T - k, 0), 0.0)
            dx = dx + fw * w_ref[pl.ds(SSM_CONV - 1 - k, 1), :]
        dx_ref[...] = dx.astype(BF16)

    return pl.pallas_call(
        body, name=name, grid=(cd // tc,),
        in_specs=[pl.BlockSpec((T, tc), lambda j: (0, j)), pl.BlockSpec((T, tc), lambda j: (0, off + j)),
                  pl.BlockSpec((SSM_CONV, tc), lambda j: (0, j)), pl.BlockSpec((1, tc), lambda j: (0, j))],
        out_specs=[pl.BlockSpec((T, tc), lambda j: (0, j)), pl.BlockSpec((SSM_CONV, tc), lambda j: (0, j)),
                   pl.BlockSpec((1, tc), lambda j: (0, j))],
        out_shape=[_sds((T, cd), BF16), _sds((SSM_CONV, cd), F32), _sds((1, cd), F32)],
        compiler_params=_cp("parallel"))(dact, proj, wt, b)


def _softplus(v):
    return jnp.maximum(v, 0.0) + jnp.log(1.0 + jnp.exp(-jnp.abs(v)))


def _tri(lower):
    r = lax.broadcasted_iota(jnp.int32, (CHUNK, CHUNK), 0)
    c = lax.broadcasted_iota(jnp.int32, (CHUNK, CHUNK), 1)
    return ((c <= r) if lower else (c >= r)).astype(F32)


def _ssd_dt_fwd(name, proj, bias, alog, dtcol, heads):
    T = proj.shape[0]

    def body(dt_ref, b_ref, al_ref, dtp_ref, ac_ref):
        live = lax.broadcasted_iota(jnp.int32, (CHUNK, LANES), 1) < heads
        dtp = jnp.where(live, _softplus(dt_ref[...] + b_ref[...]), 0.0)
        da = dtp * (-jnp.exp(al_ref[...]))
        dtp_ref[...] = dtp
        ac_ref[...] = jnp.dot(_tri(True), da, precision=HIGHEST, preferred_element_type=F32)

    return pl.pallas_call(
        body, name=name, grid=(T // CHUNK,),
        in_specs=[pl.BlockSpec((CHUNK, LANES), lambda i: (i, dtcol)), pl.BlockSpec((1, LANES), _row0),
                  pl.BlockSpec((1, LANES), _row0)],
        out_specs=[pl.BlockSpec((CHUNK, LANES), _rows), pl.BlockSpec((CHUNK, LANES), _rows)],
        out_shape=[_sds((T, LANES), F32), _sds((T, LANES), F32)], compiler_params=_cp("parallel"))(proj, bias, alog)


def _ssd_dt_bwd(name, ddtp_g, dacum_g, dd_g, proj, bias, alog, dtp, dtcol, heads):
    T = proj.shape[0]
    G = ddtp_g.shape[0]

    def body(ddtp_ref, dac_ref, dd_ref, dt_ref, b_ref, al_ref, dtp_ref, ddt_ref, db_ref, dal_ref, dds_ref, da_scr):
        step = pl.program_id(0)

        @pl.when(step == 0)
        def _():
            db_ref[...] = jnp.zeros_like(db_ref)
            da_scr[...] = jnp.zeros_like(da_scr)
            dds_ref[...] = jnp.sum(dd_ref[...], axis=0)

        live = lax.broadcasted_iota(jnp.int32, (CHUNK, LANES), 1) < heads
        a = -jnp.exp(al_ref[...])
        dac = jnp.sum(dac_ref[...], axis=0)
        dda = jnp.dot(_tri(False), dac, precision=HIGHEST, preferred_element_type=F32)
        dtp_v = dtp_ref[...]
        ddtp = jnp.sum(ddtp_ref[...], axis=0) + dda * a
        da_scr[...] += jnp.sum(dda * dtp_v, axis=0, keepdims=True)
        ddt = jnp.where(live, ddtp * _sigmoid(dt_ref[...] + b_ref[...]), 0.0)
        ddt_ref[...] = ddt.astype(BF16)
        db_ref[...] += jnp.sum(ddt, axis=0, keepdims=True)
        dal_ref[...] = da_scr[...] * a

    return pl.pallas_call(
        body, name=name, grid=(T // CHUNK,),
        in_specs=[pl.BlockSpec((G, CHUNK, LANES), lambda i: (0, i, 0)), pl.BlockSpec((G, CHUNK, LANES), lambda i: (0, i, 0)),
                  pl.BlockSpec((G, 8, LANES), lambda i: (0, 0, 0)),
                  pl.BlockSpec((CHUNK, LANES), lambda i: (i, dtcol)), pl.BlockSpec((1, LANES), _row0),
                  pl.BlockSpec((1, LANES), _row0), pl.BlockSpec((CHUNK, LANES), _rows)],
        out_specs=[pl.BlockSpec((CHUNK, LANES), _rows), pl.BlockSpec((8, LANES), _row0), pl.BlockSpec((8, LANES), _row0),
                   pl.BlockSpec((8, LANES), _row0)],
        out_shape=[_sds((T, LANES), BF16), _sds((8, LANES), F32), _sds((8, LANES), F32), _sds((8, LANES), F32)],
        scratch_shapes=[pltpu.VMEM((8, LANES), F32)],
        compiler_params=_cp("arbitrary"))(ddtp_g, dacum_g, dd_g, proj, bias, alog, dtp)


def _head_expand(g, gw):
    hi = lax.broadcasted_iota(jnp.int32, (LANES, gw), 0) - g * SSM_HPG
    fi = lax.broadcasted_iota(jnp.int32, (LANES, gw), 1)
    return ((fi >= hi * SSM_HEAD_DIM) & (fi < (hi + 1) * SSM_HEAD_DIM)).astype(F32)


def _dot(a, b, dims, exact=False):
    if exact:
        return lax.dot_general(a, b, (dims, ((), ())), precision=HIGHEST, preferred_element_type=F32)
    return lax.dot_general(a.astype(BF16), b.astype(BF16), (dims, ((), ())), preferred_element_type=F32)


_NN = ((1,), (0,))
_NT = ((1,), (1,))
_TN = ((0,), (0,))


def _pair_decay(g, q, acum, acum_t_ref, causal):
    lane = lax.broadcasted_iota(jnp.int32, (CHUNK, LANES), 1)
    out = []
    for e in range(2):
        h = g * SSM_HPG + 2 * q + e
        acol = jnp.sum(jnp.where(lane == h, acum, 0.0), axis=1, keepdims=True)
        arow = acum_t_ref[pl.ds(h, 1), :]
        out.append(jnp.exp(jnp.where(causal, acol - arow, -1e30)))
    return out


def _ssd_core_fwd(name, act, dtp, acum, dexp, inner, groups):
    T = act.shape[0]
    nc = T // CHUNK
    gw = SSM_HPG * SSM_HEAD_DIM
    npair = gw // LANES
    bcol, ccol = inner // SSM_STATE, inner // SSM_STATE + groups

    def body(x_ref, b_ref, c_ref, dtp_ref, ac_ref, d_ref, y_ref, sp_ref, st_scr, act_scr, ae_scr):
        g = pl.program_id(0)

        @pl.when(pl.program_id(1) == 0)
        def _():
            st_scr[...] = jnp.zeros_like(st_scr)

        st = st_scr[...]
        sp_ref[...] = st
        e = _head_expand(g, gw)
        acum = ac_ref[...]
        ae = _dot(acum, e, _NN, exact=True)
        dte = _dot(dtp_ref[...], e, _NN, exact=True)
        ae_scr[...] = ae
        act_scr[...] = acum.T
        xv = x_ref[...]
        xdt = xv * dte
        bm, cm = b_ref[...], c_ref[...]
        cb = _dot(cm, bm, _NT)
        causal = _causal_mask()
        lane = lax.broadcasted_iota(jnp.int32, (CHUNK, LANES), 1)
        yoff = _dot(cm, st, _NN) * jnp.exp(ae)
        skip = xv * d_ref[...]
        for q in range(npair):
            sl = slice(q * LANES, (q + 1) * LANES)
            dec = _pair_decay(g, q, acum, act_scr, causal)
            x2 = xdt[:, sl]
            xa = jnp.where(lane < SSM_HEAD_DIM, x2, 0.0)
            yd = _dot(dec[0] * cb, xa, _NN) + _dot(dec[1] * cb, x2 - xa, _NN)
            y_ref[:, sl] = yd + yoff[:, sl] + skip[:, sl]
        alast = ae_scr[pl.ds(CHUNK - 1, 1), :]
        z = xdt * jnp.exp(alast - ae)
        st_scr[...] = st * jnp.exp(alast) + _dot(bm, z, _TN)

    return pl.pallas_call(
        body, name=name, grid=(groups, nc),
        in_specs=[pl.BlockSpec((CHUNK, gw), lambda g, c: (c, g)),
                  pl.BlockSpec((CHUNK, SSM_STATE), lambda g, c: (c, bcol + g)),
                  pl.BlockSpec((CHUNK, SSM_STATE), lambda g, c: (c, ccol + g)),
                  pl.BlockSpec((CHUNK, LANES), lambda g, c: (c, 0)), pl.BlockSpec((CHUNK, LANES), lambda g, c: (c, 0)),
                  pl.BlockSpec((1, gw), lambda g, c: (0, g))],
        out_specs=[pl.BlockSpec((CHUNK, gw), lambda g, c: (c, g)),
                   pl.BlockSpec((None, SSM_STATE, gw), lambda g, c: (c, 0, g))],
        out_shape=[_sds((T, inner), F32), _sds((nc, SSM_STATE, inner), F32)],
        scratch_shapes=[pltpu.VMEM((SSM_STATE, gw), F32), pltpu.VMEM((CHUNK, LANES), F32), pltpu.VMEM((CHUNK, gw), F32)],
        compiler_params=_cp("arbitrary", "arbitrary"))(act, act, act, dtp, acum, dexp)


def _ssd_core_bwd(name, dy, act, dtp, acum, dexp, sprev, inner, groups):
    T = act.shape[0]
    nc = T // CHUNK
    gw = SSM_HPG * SSM_HEAD_DIM
    npair = gw // LANES
    bcol, ccol = inner // SSM_STATE, inner // SSM_STATE + groups

    def rc(g, c):
        return nc - 1 - c

    def body(dy_ref, x_ref, b_ref, c_ref, dtp_ref, ac_ref, d_ref, sp_ref,
             dx_ref, db_ref, dc_ref, ddtp_ref, dac_ref, dd_ref,
             dst_scr, act_scr, ae_scr, dxdt_scr, dd_scr, dact_scr):
        g = pl.program_id(0)
        step = pl.program_id(1)

        @pl.when(step == 0)
        def _():
            dst_scr[...] = jnp.zeros_like(dst_scr)
            dd_scr[...] = jnp.zeros_like(dd_scr)

        dst = dst_scr[...]
        sp = sp_ref[...]
        e = _head_expand(g, gw)
        acum = ac_ref[...]
        ae = _dot(acum, e, _NN, exact=True)
        dte = _dot(dtp_ref[...], e, _NN, exact=True)
        ae_scr[...] = ae
        act_scr[...] = acum.T
        alast = ae_scr[pl.ds(CHUNK - 1, 1), :]
        xv = x_ref[...]
        xdt = xv * dte
        bm, cm = b_ref[...], c_ref[...]
        dyv = dy_ref[...]
        cb = _dot(cm, bm, _NT)
        causal = _causal_mask()
        lane = lax.broadcasted_iota(jnp.int32, (CHUNK, LANES), 1)
        ea = jnp.exp(ae)
        cde = jnp.exp(alast)
        w = jnp.exp(alast - ae)
        z = xdt * w

        dd_scr[...] += jnp.sum(dyv * xv, axis=0, keepdims=True)
        qm = _dot(cm, sp, _NN)
        dq = dyv * ea
        dae = dq * qm
        dc = _dot(dq, sp, _NT)
        dsp = _dot(cm, dq, _TN) + dst * cde
        dal = jnp.sum(dst * sp, axis=0, keepdims=True) * cde
        db = _dot(z, dst, _NT)
        dz = _dot(bm, dst, _NN)
        gw_ = dz * z
        dae = dae - gw_
        dal = dal + jnp.sum(gw_, axis=0, keepdims=True)
        dxdt_scr[...] = dz * w
        dcb = jnp.zeros((CHUNK, CHUNK), F32)
        dacol = jnp.zeros((CHUNK, LANES), F32)
        dact_scr[...] = jnp.zeros_like(dact_scr)
        sub = lax.broadcasted_iota(jnp.int32, (CHUNK, LANES), 0)
        for q in range(npair):
            sl = slice(q * LANES, (q + 1) * LANES)
            dec = _pair_decay(g, q, acum, act_scr, causal)
            x2, dy2 = xdt[:, sl], dyv[:, sl]
            xs_ = (jnp.where(lane < SSM_HEAD_DIM, x2, 0.0),)
            xs_ = xs_ + (x2 - xs_[0],)
            dys = (jnp.where(lane < SSM_HEAD_DIM, dy2, 0.0),)
            dys = dys + (dy2 - dys[0],)
            dx2 = jnp.zeros((CHUNK, LANES), F32)
            for hh in range(2):
                h = g * SSM_HPG + 2 * q + hh
                m = dec[hh] * cb
                dm = _dot(dys[hh], xs_[hh], _NT)
                dx2 = dx2 + _dot(m, dys[hh], _TN)
                dcb = dcb + dm * dec[hh]
                r = dm * m
                dacol = dacol + jnp.where(lane == h, jnp.sum(r, axis=1, keepdims=True), 0.0)
                dact_scr[...] -= jnp.where(sub == h, jnp.sum(r, axis=0, keepdims=True), 0.0)
            dxdt_scr[:, sl] += dx2
        dc = dc + _dot(dcb, bm, _NN)
        db = db + _dot(dcb, cm, _TN)
        dxdt = dxdt_scr[...]
        dx_ref[...] = dyv * d_ref[...] + dxdt * dte
        db_ref[...] = db
        dc_ref[...] = dc
        ddtp_ref[...] = _dot(dxdt * xv, e, _NT, exact=True)
        dal_h = _dot(jnp.broadcast_to(dal, (8, gw)), e, _NT, exact=True)
        dal_row = jnp.max(dal_h, axis=0, keepdims=True)
        dac = _dot(dae, e, _NT, exact=True) + dacol + dact_scr[...].T
        dac_ref[...] = dac + jnp.where(sub == CHUNK - 1, dal_row, 0.0)
        dst_scr[...] = dsp

        @pl.when(step == nc - 1)
        def _():
            dd_ref[...] = _dot(jnp.broadcast_to(dd_scr[...], (8, gw)), e, _NT, exact=True)

    return pl.pallas_call(
        body, name=name, grid=(groups, nc),
        in_specs=[pl.BlockSpec((CHUNK, gw), lambda g, c: (rc(g, c), g)),
                  pl.BlockSpec((CHUNK, gw), lambda g, c: (rc(g, c), g)),
                  pl.BlockSpec((CHUNK, SSM_STATE), lambda g, c: (rc(g, c), bcol + g)),
                  pl.BlockSpec((CHUNK, SSM_STATE), lambda g, c: (rc(g, c), ccol + g)),
                  pl.BlockSpec((CHUNK, LANES), lambda g, c: (rc(g, c), 0)),
                  pl.BlockSpec((CHUNK, LANES), lambda g, c: (rc(g, c), 0)),
                  pl.BlockSpec((1, gw), lambda g, c: (0, g)),
                  pl.BlockSpec((None, SSM_STATE, gw), lambda g, c: (rc(g, c), 0, g))],
        out_specs=[pl.BlockSpec((CHUNK, gw), lambda g, c: (rc(g, c), g)),
                   pl.BlockSpec((CHUNK, SSM_STATE), lambda g, c: (rc(g, c), g)),
                   pl.BlockSpec((CHUNK, SSM_STATE), lambda g, c: (rc(g, c), g)),
                   pl.BlockSpec((None, CHUNK, LANES), lambda g, c: (g, rc(g, c), 0)),
                   pl.BlockSpec((None, CHUNK, LANES), lambda g, c: (g, rc(g, c), 0)),
                   pl.BlockSpec((None, 8, LANES), lambda g, c: (g, 0, 0))],
        out_shape=[_sds((T, inner), F32), _sds((T, groups * SSM_STATE), F32), _sds((T, groups * SSM_STATE), F32),
                   _sds((groups, T, LANES), F32), _sds((groups, T, LANES), F32), _sds((groups, 8, LANES), F32)],
        scratch_shapes=[pltpu.VMEM((SSM_STATE, gw), F32), pltpu.VMEM((CHUNK, LANES), F32), pltpu.VMEM((CHUNK, gw), F32),
                        pltpu.VMEM((CHUNK, gw), F32), pltpu.VMEM((1, gw), F32), pltpu.VMEM((CHUNK, LANES), F32)],
        compiler_params=_cp("arbitrary", "arbitrary"))(dy, act, act, act, dtp, acum, dexp, sprev)


def _ssd_post_fwd(name, y, proj, ng, inner, groups):
    T = y.shape[0]
    tr = _tile(T, 128, 8)
    gs = inner // groups

    def body(y_ref, z_ref, g_ref, o_ref):
        zv = z_ref[...]
        gy = y_ref[...] * (zv * _sigmoid(zv))
        for k in range(groups):
            sl = slice(k * gs, (k + 1) * gs)
            seg = gy[:, sl]
            r = lax.rsqrt(jnp.mean(seg * seg, axis=-1, keepdims=True) + EPS)
            o_ref[:, sl] = (seg * r * g_ref[:, sl]).astype(BF16)

    return pl.pallas_call(
        body, name=name, grid=(T // tr,),
        in_specs=[pl.BlockSpec((tr, inner), _rows), pl.BlockSpec((tr, inner), _rows), pl.BlockSpec((1, inner), _row0)],
        out_specs=pl.BlockSpec((tr, inner), _rows), out_shape=_sds((T, inner), BF16),
        compiler_params=_cp("parallel"))(y, proj, ng)


def _ssd_post_bwd(name, dyn, y, proj, ng, inner, groups):
    T = y.shape[0]
    tr = _tile(T, 128, 8)
    gs = inner // groups

    def body(dyn_ref, y_ref, z_ref, g_ref, dy_ref, dz_ref, dg_ref):
        @pl.when(pl.program_id(0) == 0)
        def _():
            dg_ref[...] = jnp.zeros_like(dg_ref)

        zv = z_ref[...]
        sg = _sigmoid(zv)
        sz = zv * sg
        yv = y_ref[...]
        gy = yv * sz
        dv = dyn_ref[...]
        for k in range(groups):
            sl = slice(k * gs, (k + 1) * gs)
            seg = gy[:, sl]
            r = lax.rsqrt(jnp.mean(seg * seg, axis=-1, keepdims=True) + EPS)
            xh = seg * r
            d = dv[:, sl]
            dg_ref[:, sl] += jnp.sum(d * xh, axis=0, keepdims=True)
            dxh = d * g_ref[:, sl]
            dgy = r * (dxh - xh * jnp.mean(dxh * xh, axis=-1, keepdims=True))
            dy_ref[:, sl] = dgy * sz[:, sl]
            dz_ref[:, sl] = (dgy * yv[:, sl] * (sg[:, sl] * (1.0 + zv[:, sl] * (1.0 - sg[:, sl])))).astype(BF16)

    return pl.pallas_call(
        body, name=name, grid=(T // tr,),
        in_specs=[pl.BlockSpec((tr, inner), _rows), pl.BlockSpec((tr, inner), _rows), pl.BlockSpec((tr, inner), _rows),
                  pl.BlockSpec((1, inner), _row0)],
        out_specs=[pl.BlockSpec((tr, inner), _rows), pl.BlockSpec((tr, inner), _rows), pl.BlockSpec((1, inner), _row0)],
        out_shape=[_sds((T, inner), F32), _sds((T, inner), BF16), _sds((1, inner), F32)],
        compiler_params=_cp("arbitrary"))(dyn, y, proj, ng)


def _place():
    return lax.axis_index("x"), lax.axis_index("y"), lax.axis_index("c")


def _other_chips(x, y):
    return [(1 - x, y), (x, 1 - y), (1 - x, 1 - y)]


def _remote(src, dst, ssem, rsem, dev):
    return pltpu.make_async_remote_copy(src_ref=src, dst_ref=dst, send_sem=ssem, recv_sem=rsem, device_id=dev,
                                        device_id_type=MESH)


def _all_gather_chips(name, shard):
    R, C = shard.shape
    hr = R // 2

    def body(x_ref, o_ref, ssem, rsem, lsem):
        x, y, c = _place()
        k = 2 * x + y
        sib = (x, y, 1 - c)
        chips = _other_chips(x, y)

        def half(blk, cc):
            return o_ref.at[blk, pl.ds(cc * hr, hr), :]

        local = pltpu.make_async_copy(x_ref, o_ref.at[k], lsem)
        local.start()
        first = [_remote(x_ref.at[pl.ds(c * hr, hr), :], half(k, c), ssem.at[r], rsem.at[r], (px, py, c))
                 for r, (px, py) in enumerate(chips)]
        for cp in first:
            cp.start()
        passed = []
        for r, (px, py) in enumerate(chips):
            kj = 2 * px + py
            _remote(half(kj, c), half(kj, c), ssem.at[r], rsem.at[r], (px, py, c)).wait_recv()
            fw = _remote(half(kj, c), half(kj, c), ssem.at[3 + r], rsem.at[3 + r], sib)
            fw.start()
            passed.append(fw)
        for r, (px, py) in enumerate(chips):
            kj = 2 * px + py
            _remote(half(kj, 1 - c), half(kj, 1 - c), ssem.at[3 + r], rsem.at[3 + r], sib).wait_recv()
        for cp in first + passed:
            cp.wait_send()
        local.wait()

    return pl.pallas_call(
        body, name=name, in_specs=[_ANY], out_specs=_ANY, out_shape=_sds((4, R, C), shard.dtype),
        scratch_shapes=[pltpu.SemaphoreType.DMA((6,)), pltpu.SemaphoreType.DMA((6,)), pltpu.SemaphoreType.DMA],
    )(shard)


_HBM = pl.BlockSpec(memory_space=pltpu.HBM)
_SEM = pl.BlockSpec(memory_space=pltpu.SEMAPHORE)
_EFFECT = pltpu.SideEffectType.DATAFLOW_SIDE_EFFECTING


def _hbm(a):
    return pltpu.with_memory_space_constraint(a, pltpu.HBM)


def _cast_into_block(name, w, layer, after):
    _, R, C = w.shape
    tr = _tile(R, 256, 16)

    def body(w_ref, a_ref, o_ref):
        o_ref[...] = w_ref[...].astype(BF16)

    return pl.pallas_call(
        body, name=name, grid=(R // tr,),
        in_specs=[pl.BlockSpec((None, tr, C), lambda i: (layer, i, 0)), _ANY],
        out_specs=pl.BlockSpec((None, tr, C), lambda i: (2 * lax.axis_index("x") + lax.axis_index("y"), i, 0)),
        out_shape=_sds((4, R, C), BF16), compiler_params=_cp("parallel"))(w, after)


def _gather_copies(refs, ssem, rsem):
    x, y, c = _place()
    k = 2 * x + y
    sends, arrivals = [], []
    for a, lr in enumerate(refs):
        hr = lr.shape[1] // 2
        mine = lr.at[k, pl.ds(c * hr, hr), :]
        for r, (px, py) in enumerate(_other_chips(x, y)):
            i = 3 * a + r
            sends.append(_remote(mine, mine, ssem.at[i], rsem.at[i], (px, py, c)))
            arrivals.append(_remote(mine, lr.at[2 * px + py, pl.ds(c * hr, hr), :], ssem.at[i], rsem.at[i], (px, py, c)))
    return sends, arrivals


def _swap_copies(refs, ssem, rsem):
    x, y, c = _place()
    n = len(refs) // 2
    sends = []
    for a, (g, o) in enumerate(zip(refs[:n], refs[n:])):
        hr = g.shape[1] // 2
        sends.append(_remote(g.at[:, pl.ds((1 - c) * hr, hr), :], o, ssem.at[a], rsem.at[a], (x, y, 1 - c)))
    return sends, sends


def _scatter_copies(refs, ssem, rsem):
    x, y, c = _place()
    k = 2 * x + y
    n = len(refs) // 2
    sends, arrivals = [], []
    for a, (xr, lr) in enumerate(zip(refs[:n], refs[n:])):
        for r, (px, py) in enumerate(_other_chips(x, y)):
            i = 3 * a + r
            kj = 2 * px + py
            sends.append(_remote(xr.at[kj], lr.at[k], ssem.at[i], rsem.at[i], (px, py, c)))
            arrivals.append(_remote(xr.at[kj], lr.at[kj], ssem.at[i], rsem.at[i], (px, py, c)))
    return sends, arrivals


def _split_start(name, bufs, ncopies, copies_fn):
    n = len(bufs)

    def body(*refs):
        sends, _ = copies_fn(refs[:n], refs[n], refs[n + 1])
        for cp in sends:
            cp.start()
        token = refs[-1]
        token[...] = jnp.zeros_like(token)

    outs = pl.pallas_call(
        body, name=name, in_specs=[_HBM] * n,
        out_specs=(_SEM, _SEM) + (_HBM,) * n + (pl.BlockSpec(memory_space=pltpu.VMEM),),
        out_shape=(pltpu.SemaphoreType.DMA((ncopies,)), pltpu.SemaphoreType.DMA((ncopies,)))
        + tuple(pltpu.HBM(b.shape, b.dtype) for b in bufs) + (_sds((8, LANES), F32),),
        input_output_aliases={i: 2 + i for i in range(n)},
        compiler_params=pltpu.CompilerParams(has_side_effects=_EFFECT),
    )(*[_hbm(b) for b in bufs])
    return outs[0], outs[1], list(outs[2:2 + n]), outs[-1]


def _split_wait(name, started, copies_fn, after):
    ssem, rsem, bufs, _ = started
    n = len(bufs)

    def body(*refs):
        sends, arrivals = copies_fn(refs[:n], refs[n], refs[n + 1])
        for cp in sends:
            cp.wait_send()
        for cp in arrivals:
            cp.wait_recv()

    outs = pl.pallas_call(
        body, name=name, in_specs=[_HBM] * n + [_SEM, _SEM, _ANY], out_specs=(_HBM,) * n,
        out_shape=tuple(pltpu.HBM(t.shape, t.dtype) for t in bufs),
        input_output_aliases={i: i for i in range(n)},
        compiler_params=pltpu.CompilerParams(has_side_effects=_EFFECT),
    )(*bufs, ssem, rsem, after)
    return list(outs)


def _share_sibling(name, lands):
    n = len(lands)

    def body(*refs):
        l_in, l_out = refs[:n], refs[n:2 * n]
        ssem, rsem = refs[2 * n:]
        x, y, c = _place()
        sib = (x, y, 1 - c)

        def rows(ref, a, px, py, cc):
            hr = ref[a].shape[1] // 2
            return ref[a].at[2 * px + py, pl.ds(cc * hr, hr), :]

        copies = []
        for a in range(n):
            for r, (px, py) in enumerate(_other_chips(x, y)):
                cp = _remote(rows(l_in, a, px, py, c), rows(l_out, a, px, py, c), ssem.at[3 * a + r], rsem.at[3 * a + r], sib)
                cp.start()
                copies.append(cp)
        for a in range(n):
            for r, (px, py) in enumerate(_other_chips(x, y)):
                theirs = rows(l_out, a, px, py, 1 - c)
                _remote(theirs, theirs, ssem.at[3 * a + r], rsem.at[3 * a + r], sib).wait_recv()
        for cp in copies:
            cp.wait_send()

    outs = pl.pallas_call(
        body, name=name, in_specs=[_ANY] * n, out_specs=[_ANY] * n,
        out_shape=[_sds(l.shape, l.dtype) for l in lands], input_output_aliases={i: i for i in range(n)},
        scratch_shapes=[pltpu.SemaphoreType.DMA((3 * n,)), pltpu.SemaphoreType.DMA((3 * n,))],
    )(*lands)
    return list(outs)


def _add_halves(name, full, other):
    nb, R, C = full.shape
    hr = R // 2
    tr = _tile(hr, 256, 16)
    nh = hr // tr

    def body(f_ref, o_ref, s_ref):
        s_ref[...] = (f_ref[...].astype(F32) + o_ref[...].astype(F32)).astype(s_ref.dtype)

    return pl.pallas_call(
        body, name=name, grid=(nb, nh),
        in_specs=[pl.BlockSpec((None, tr, C), lambda b, i: (b, lax.axis_index("c") * nh + i, 0)),
                  pl.BlockSpec((None, tr, C), lambda b, i: (b, i, 0))],
        out_specs=pl.BlockSpec((None, tr, C), lambda b, i: (b, i, 0)), out_shape=_sds((nb, hr, C), full.dtype),
        compiler_params=_cp("parallel", "parallel"))(full, other)


def _sum_owner(name, land, mine):
    nb, hr, C = land.shape
    tr = _tile(hr, 128, 16)

    def body(l_ref, m_ref, o_ref):
        k = 2 * lax.axis_index("x") + lax.axis_index("y")
        own = m_ref[...].astype(F32)
        acc = jnp.where(k == 0, own, l_ref[0].astype(F32))
        for j in range(1, nb):
            acc = acc + jnp.where(k == j, own, l_ref[j].astype(F32))
        o_ref[...] = acc

    return pl.pallas_call(
        body, name=name, grid=(hr // tr,),
        in_specs=[pl.BlockSpec((nb, tr, C), lambda i: (0, i, 0)),
                  pl.BlockSpec((None, tr, C), lambda i: (2 * lax.axis_index("x") + lax.axis_index("y"), i, 0))],
        out_specs=pl.BlockSpec((None, tr, C), lambda i: (lax.axis_index("c"), i, 0)),
        out_shape=_sds((2, hr, C), F32), compiler_params=_cp("parallel"))(land, mine)


def _join_halves(name, bufs):
    n = len(bufs)

    def body(*refs):
        i_refs, o_refs = refs[:n], refs[n:2 * n]
        ssem, rsem = refs[2 * n:]
        x, y, c = _place()
        copies = []
        for a in range(n):
            cp = _remote(i_refs[a].at[c], o_refs[a].at[c], ssem.at[a], rsem.at[a], (x, y, 1 - c))
            cp.start()
            copies.append(cp)
        for a, cp in enumerate(copies):
            _remote(i_refs[a].at[c], o_refs[a].at[1 - c], ssem.at[a], rsem.at[a], (x, y, 1 - c)).wait_recv()
            cp.wait_send()

    outs = pl.pallas_call(
        body, name=name, in_specs=[_ANY] * n, out_specs=[_ANY] * n,
        out_shape=[_sds(b.shape, b.dtype) for b in bufs], input_output_aliases={i: i for i in range(n)},
        scratch_shapes=[pltpu.SemaphoreType.DMA((n,)), pltpu.SemaphoreType.DMA((n,))],
    )(*bufs)
    return list(outs)


def _all_gather_devices(name, buf):
    def body(b_ref, o_ref, ssem, rsem, lsem):
        x, y, c = _place()
        me = 4 * x + 2 * y + c
        flips = [(fx, fy, fc) for fx in (0, 1) for fy in (0, 1) for fc in (0, 1) if fx or fy or fc]
        peers = [((1 - x) if fx else x, (1 - y) if fy else y, (1 - c) if fc else c) for fx, fy, fc in flips]
        local = pltpu.make_async_copy(b_ref, o_ref.at[me], lsem)
        local.start()
        sends = [_remote(b_ref, o_ref.at[me], ssem.at[r], rsem.at[r], p) for r, p in enumerate(peers)]
        for cp in sends:
            cp.start()
        for r, (px, py, pc) in enumerate(peers):
            pid = 4 * px + 2 * py + pc
            _remote(b_ref, o_ref.at[pid], ssem.at[r], rsem.at[r], (px, py, pc)).wait_recv()
        for cp in sends:
            cp.wait_send()
        local.wait()

    return pl.pallas_call(
        body, name=name, in_specs=[_ANY], out_specs=_ANY, out_shape=_sds((8,) + buf.shape, buf.dtype),
        scratch_shapes=[pltpu.SemaphoreType.DMA((7,)), pltpu.SemaphoreType.DMA((7,)), pltpu.SemaphoreType.DMA],
    )(buf)


def _sum_blocks(name, parts):
    n, R, C = parts.shape
    tr = _tile(R, 256, 8)

    def body(p_ref, o_ref):
        acc = p_ref[0].astype(F32)
        for j in range(1, n):
            acc = acc + p_ref[j].astype(F32)
        o_ref[...] = acc

    return pl.pallas_call(
        body, name=name, grid=(R // tr,), in_specs=[pl.BlockSpec((n, tr, C), lambda i: (0, i, 0))],
        out_specs=pl.BlockSpec((tr, C), _rows), out_shape=_sds((R, C), F32), compiler_params=_cp("parallel"))(parts)


def _adamw(name, w, m, v, parts, tok=None):
    L, R, C = w.shape
    np_ = len(parts[0])
    tr = _tile(R, max(8, (VMEM_LIMIT // 2) // (2 * 4 * C * (7 + L * np_))), 8)
    flat = [p for lp in parts for p in lp]
    if tok is not None:
        flat = flat + [tok]
    c1 = 1.0 / (1.0 - ADAM_B1 ** ADAM_STEP)
    c2 = 1.0 / (1.0 - ADAM_B2 ** ADAM_STEP)

    def body(*refs):
        w_ref, m_ref, v_ref = refs[:3]
        p_refs = refs[3:3 + L * np_]
        g_ref, d_ref, nm_ref, nv_ref = refs[-4:]
        layer = pl.program_id(0)
        g = jnp.zeros((tr, C), F32)
        for l in range(L):
            gl = p_refs[l * np_][...]
            for j in range(1, np_):
                gl = gl + p_refs[l * np_ + j][...]
            g = jnp.where(layer == l, gl, g) if L > 1 else gl
        if tok is not None:
            g = g + refs[3 + L * np_][...]
        nm = ADAM_B1 * m_ref[...] + (1.0 - ADAM_B1) * g
        nv = ADAM_B2 * v_ref[...] + (1.0 - ADAM_B2) * (g * g)
        g_ref[...] = g
        nm_ref[...] = nm
        nv_ref[...] = nv
        d_ref[...] = -ADAM_LR * ((nm * c1) / (jnp.sqrt(nv * c2) + ADAM_EPS) + ADAM_WD * w_ref[...])

    stacked = pl.BlockSpec((None, tr, C), lambda l, i: (l, i, 0))
    part_specs = [pl.BlockSpec((tr, C), (lambda l, i, ll=ll: (jnp.where(l == ll, i, 0), 0)))
                  for ll in range(L) for _ in range(np_)]
    if tok is not None:
        part_specs.append(pl.BlockSpec((1, 1), lambda l, i: (0, 0)))
    return pl.pallas_call(
        body, name=name, grid=(L, R // tr), in_specs=[stacked] * 3 + part_specs, out_specs=[stacked] * 4,
        out_shape=[_sds(w.shape, F32)] * 4, compiler_params=_cp("arbitrary", "arbitrary"))(w, m, v, *flat)


_PACK_ROWS = 16


def _pack(arrs):
    pieces = []
    for a in arrs:
        f = a.reshape(-1).astype(F32)
        unit = _PACK_ROWS * LANES
        pad = (-f.shape[0]) % unit
        pieces.append(jnp.pad(f, (0, pad)))
    return jnp.concatenate(pieces).reshape(-1, LANES)


def _unpack(buf, shapes):
    flat = buf.reshape(-1)
    out, off = [], 0
    unit = _PACK_ROWS * LANES
    for s in shapes:
        n = 1
        for d in s:
            n *= d
        out.append(flat[off:off + n].reshape(s))
        off += n + ((-n) % unit)
    return out


def _swap_start(tag, full):
    nb, R, C = full.shape
    return _split_start("sw_start_" + tag, [full, lax.empty((nb, R // 2, C), full.dtype)], 1, _swap_copies)


def _reduce_start(tag, swaps, after):
    pairs = [_split_wait("sw_wait_%s_%d" % (tag, i), st, _swap_copies, after) for i, st in enumerate(swaps)]
    sums = [_add_halves("rs_add", f, o) for f, o in pairs]
    lands = [lax.empty(t.shape, t.dtype) for t in sums]
    return _split_start("rs_start_" + tag, sums + lands, 3 * len(sums), _scatter_copies)


def _reduce_finish(tag, started, after):
    bufs = _split_wait("rs_wait_" + tag, started, _scatter_copies, after)
    n = len(bufs) // 2
    halves = [_sum_owner("rs_sum", l, s) for s, l in zip(bufs[:n], bufs[n:])]
    return [j.reshape(2 * j.shape[1], j.shape[2]) for j in _join_halves("rs_join", halves)]


def kernel(x, ln_ffn_pre, ffn_pre_w_in, ffn_pre_w_out, ln_mix, ln_ffn_post, ffn_post_w_in, ffn_post_w_out, gm_w_in, gm_v_norm, gm_w_s, gm_b_s, gm_w_out, ssm_w_in, ssm_conv_w, ssm_conv_b, ssm_dt_bias, ssm_a_log, ssm_d, ssm_norm, ssm_w_out, ln_final, loss_target, m_ln_ffn_pre, m_ffn_pre_w_in, m_ffn_pre_w_out, m_ln_mix, m_ln_ffn_post, m_ffn_post_w_in, m_ffn_post_w_out, m_gm_w_in, m_gm_v_norm, m_gm_w_s, m_gm_b_s, m_gm_w_out, m_ssm_w_in, m_ssm_conv_w, m_ssm_conv_b, m_ssm_dt_bias, m_ssm_a_log, m_ssm_d, m_ssm_norm, m_ssm_w_out, m_ln_final, v_ln_ffn_pre, v_ffn_pre_w_in, v_ffn_pre_w_out, v_ln_mix, v_ln_ffn_post, v_ffn_post_w_in, v_ffn_post_w_out, v_gm_w_in, v_gm_v_norm, v_gm_w_s, v_gm_b_s, v_gm_w_out, v_ssm_w_in, v_ssm_conv_w, v_ssm_conv_b, v_ssm_dt_bias, v_ssm_a_log, v_ssm_d, v_ssm_norm, v_ssm_w_out, v_ln_final):
    T, D = x.shape[1], x.shape[2]
    depth = ln_ffn_pre.shape[0]
    n_gm, n_ssm = gm_w_in.shape[0], ssm_w_in.shape[0]
    F = ffn_pre_w_out.shape[1] * 4
    GI = gm_w_out.shape[1] * 4
    GG = gm_w_s.shape[1]
    inner = ssm_w_out.shape[1] * 4
    heads = ssm_dt_bias.shape[1]
    cd = ssm_conv_w.shape[1] * 4
    groups = (cd - inner) // (2 * SSM_STATE)
    pshard = ssm_w_in.shape[2]
    pw = inner + cd + LANES
    dtcol = (inner + cd) // LANES
    kchip = 2 * lax.axis_index("x") + lax.axis_index("y")

    small_shard = _pack([jnp.swapaxes(ssm_conv_w, 1, 2), ssm_conv_b, ssm_norm])
    small_all = _all_gather_chips("ag_small", small_shard)
    cq, iq = cd // 4, inner // 4
    parts = [_unpack(small_all[k], [(n_ssm, SSM_CONV, cq), (n_ssm, cq), (n_ssm, iq)]) for k in range(4)]
    conv_wt = jnp.concatenate([p[0] for p in parts], axis=2)
    conv_b = jnp.concatenate([p[1] for p in parts], axis=1)
    norm_g = jnp.concatenate([p[2] for p in parts], axis=1)

    def pad_lanes(v):
        return jnp.pad(v, (0, LANES - v.shape[0]))[None, :]

    xc = x[0]
    saved = []

    def shards_of(i, kind):
        if kind == "pre":
            return [(ffn_pre_w_in, i), (ffn_pre_w_out, i)]
        if kind == "post":
            return [(ffn_post_w_in, i), (ffn_post_w_out, i)]
        if i % 2 == 0:
            return [(gm_w_in, i // 2), (gm_w_out, i // 2)]
        return [(ssm_w_in, i // 2), (ssm_w_out, i // 2)]

    subs = [(i, kind) for i in range(depth) for kind in ("pre", "mix", "post")]
    nsub = len(subs)
    gathers = [None] * nsub
    ahead = 2

    def gather_start(s, after):
        lands = [_cast_into_block("ag_cast", w, l, after) for w, l in shards_of(*subs[s])]
        gathers[s] = _split_start("ag_start_%d" % s, lands, 3 * len(lands), _gather_copies)

    def gathered(s, after):
        lands = _split_wait("ag_wait_%d" % s, gathers[s], _gather_copies, after)
        full = _share_sibling("ag_share", lands)
        tok = jnp.zeros((1, 1), F32)
        depth_now = 1 if s == 0 else ahead
        for nxt in range(s + 1, min(s + depth_now, nsub - 1) + 1):
            if gathers[nxt] is None:
                gather_start(nxt, full[0])
                tok = tok + gathers[nxt][3][0:1, 0:1]
        return full, tok

    gather_start(0, xc)

    def ffn_fwd(s, xin, g, l):
        (wi, wo), tok = gathered(s, xin)
        wo = wo.reshape(F, D)
        h = _rms_fwd("rms_fwd", xin, g[l][None, :] + tok)
        a = _matmul("ffn_in", h, wi, "nn", BF16, out_blocks=2)
        s_ = _swiglu_fwd("swiglu_fwd", a)
        xo = _matmul("ffn_out", s_, wo, "nn", F32, res=xin, scale=0.5)
        return xo, (xin, h, a, s_, wi, wo)

    for i in range(depth):
        xc, sv_pre = ffn_fwd(3 * i, xc, ln_ffn_pre, i)
        j = i // 2
        (wi, wo), tok = gathered(3 * i + 1, xc)
        h = _rms_fwd("rms_fwd", xc, ln_mix[i][None, :] + tok)
        if i % 2 == 0:
            wo = wo.reshape(GI, D)
            zpre = _matmul("gm_in", h, wi, "nn", BF16, out_blocks=2)
            u, vn = _gm_act_fwd("gm_act_fwd", zpre, gm_v_norm[j][None, :])
            bt = jnp.pad(gm_b_s[j].T, ((0, 0), (0, LANES - GG)))
            gated = _gm_spatial_fwd("gm_spatial_fwd", u, vn, gm_w_s[j], bt)
            xn = _matmul("mix_out", gated, wo, "nn", F32, res=xc, scale=1.0)
            sv_mix = (xc, h, zpre, u, vn, bt, gated, wi, wo)
        else:
            wg, wo = wi, wo.reshape(inner, D)
            wp = jnp.pad(jnp.swapaxes(wg, 0, 1).reshape(D, 4 * pshard), ((0, 0), (0, pw - 4 * pshard)))
            proj = _matmul("ssm_in", h, wp, "nn", F32)
            wt, cb_ = conv_wt[j], conv_b[j][None, :]
            act = _ssd_conv_fwd("ssd_conv_fwd", proj, wt, cb_, inner, cd)
            bias, alog = pad_lanes(ssm_dt_bias[j]), pad_lanes(ssm_a_log[j])
            dtp, acum = _ssd_dt_fwd("ssd_dt_fwd", proj, bias, alog, dtcol, heads)
            dexp = jnp.repeat(ssm_d[j], SSM_HEAD_DIM)[None, :]
            ycore, sprev = _ssd_core_fwd("ssd_core_fwd", act, dtp, acum, dexp, inner, groups)
            ng = norm_g[j][None, :]
            yn = _ssd_post_fwd("ssd_post_fwd", ycore, proj, ng, inner, groups)
            xn = _matmul("mix_out", yn, wo, "nn", F32, res=xc, scale=1.0)
            sv_mix = (xc, h, proj, act, dtp, acum, dexp, ycore, sprev, yn, wt, cb_, bias, alog, ng, wp, wo)
        xc = xn
        xc, sv_post = ffn_fwd(3 * i + 2, xc, ln_ffn_post, i)
        saved.append((sv_pre, sv_mix, sv_post))

    loss_tile, dx, dxb, dg_final = _loss_head("loss_head", xc, ln_final[None, :], loss_target[0])
    loss = lax.psum(loss_tile[0, 0], ("x", "y", "c"))

    wgrad = [None] * nsub
    in_flight = []
    deferred = []

    def reduce_finish(after):
        while in_flight:
            s, st = in_flight.pop(0)
            wgrad[s] = _reduce_finish(str(s), st, after)

    def swap_start(s, which, full):
        st = _swap_start("%d_%s" % (s, which), full)
        return st, st[3][0:1, 0:1]

    def reduce_start(s, swaps, after):
        reduce_finish(after)
        if s == 0:
            deferred.append(swaps)
            return jnp.zeros((1, 1), F32)
        st = _reduce_start(str(s), swaps, after)
        in_flight.append((s, st))
        return st[3][0:1, 0:1]

    def ffn_bwd(s, dx, dxb, sv, g, l):
        xin, h, a, s_, wi, wo = sv
        ds = _matmul("ffn_ds", dxb, wo, "nt", BF16, scale=0.5)
        dwo = _matmul("ffn_dwo", s_, dxb, "tn", BF16, scale=0.5)
        sw_o, tok_o = swap_start(s, "o", dwo.reshape(4, F // 4, D))
        da = _swiglu_bwd("swiglu_bwd", a, ds)
        dwi = _matmul("ffn_dwi", h, da, "tn", BF16, out_blocks=4, tok=tok_o)
        sw_i, tok_i = swap_start(s, "i", dwi)
        dh = _matmul("ffn_dh", da, wi, "nt", F32, tok=tok_i)
        tok = reduce_start(s, [sw_i, sw_o], dh)
        return _rms_bwd("rms_bwd", xin, g[l][None, :] + tok, dh, dx)

    gl = {n: [None] * depth for n in ("pre", "mix", "post")}
    gm_g = {n: [None] * n_gm for n in ("vnorm", "ws", "bs")}
    ssm_g = {n: [None] * n_ssm for n in ("convw", "convb", "dtb", "alog", "d", "norm")}

    for i in reversed(range(depth)):
        sv_pre, sv_mix, sv_post = saved[i]
        j = i // 2
        dx, dxb, gl["post"][i] = ffn_bwd(3 * i + 2, dx, dxb, sv_post, ln_ffn_post, i)
        if i % 2 == 0:
            xin, h, zpre, u, vn, bt, gated, wi, wo = sv_mix
            dgated = _matmul("mix_dy", dxb, wo, "nt", BF16)
            dwo = _matmul("mix_dwo", gated, dxb, "tn", BF16)
            sw_o, tok_o = swap_start(3 * i + 1, "o", dwo.reshape(4, GI // 4, D))
            du, dvn, dws, dbt = _gm_spatial_bwd("gm_spatial_bwd", dgated, u, vn, gm_w_s[j], bt)
            dzpre, dvnorm = _gm_act_bwd("gm_act_bwd", zpre, du, dvn, gm_v_norm[j][None, :])
            dwi = _matmul("gm_dwi", h, dzpre, "tn", BF16, out_blocks=4, tok=tok_o)
            sw_i, tok_i = swap_start(3 * i + 1, "i", dwi)
            dh = _matmul("gm_dh", dzpre, wi, "nt", F32, tok=tok_i)
            tok = reduce_start(3 * i + 1, [sw_i, sw_o], dh)
            gm_g["vnorm"][j], gm_g["ws"][j], gm_g["bs"][j] = dvnorm[0], dws, dbt.T[:GG]
        else:
            xin, h, proj, act, dtp, acum, dexp, ycore, sprev, yn, wt, cb_, bias, alog, ng, wp, wo = sv_mix
            dyn = _matmul("ssm_dy", dxb, wo, "nt", F32)
            dwo = _matmul("mix_dwo", yn, dxb, "tn", BF16)
            sw_o, tok_o = swap_start(3 * i + 1, "o", dwo.reshape(4, inner // 4, D))
            dyc, dz, dnorm = _ssd_post_bwd("ssd_post_bwd", dyn, ycore, proj, ng, inner, groups)
            dxs, db_, dc_, ddtp_g, dac_g, dd_g = _ssd_core_bwd("ssd_core_bwd", dyc, act, dtp, acum, dexp, sprev,
                                                               inner, groups)
            ddt, dbias, dalog, dds = _ssd_dt_bwd("ssd_dt_bwd", ddtp_g, dac_g, dd_g, proj, bias, alog, dtp, dtcol, heads)
            dact = jnp.concatenate([dxs, db_, dc_], axis=1)
            dxbc, dwt, dcb = _ssd_conv_bwd("ssd_conv_bwd", dact, proj, wt, cb_, inner, cd)
            dproj = jnp.concatenate([dz, dxbc, ddt], axis=1)
            dwp = _matmul("ssm_dwi", h, dproj, "tn", BF16, tok=tok_o)
            dwi = jnp.swapaxes(dwp[:, :4 * pshard].reshape(D, 4, pshard), 0, 1)
            sw_i, tok_i = swap_start(3 * i + 1, "i", dwi)
            dh = _matmul("ssm_dh", dproj, wp, "nt", F32, tok=tok_i)
            tok = reduce_start(3 * i + 1, [sw_i, sw_o], dh)
            ssm_g["convw"][j], ssm_g["convb"][j] = dwt.T, dcb[0]
            ssm_g["dtb"][j], ssm_g["alog"][j], ssm_g["d"][j] = dbias[0, :heads], dalog[0, :heads], dds[0, :heads]
            ssm_g["norm"][j] = dnorm[0]
        dx, dxb, gl["mix"][i] = _rms_bwd("rms_bwd", xin, ln_mix[i][None, :] + tok, dh, dx)
        dx, dxb, gl["pre"][i] = ffn_bwd(3 * i, dx, dxb, sv_pre, ln_ffn_pre, i)

    small_full = [
        jnp.concatenate(gl["pre"], 0), jnp.concatenate(gl["mix"], 0), jnp.concatenate(gl["post"], 0),
        jnp.stack(gm_g["vnorm"]), jnp.stack(gm_g["ws"]), jnp.stack(gm_g["bs"]),
        jnp.stack(ssm_g["convw"]), jnp.stack(ssm_g["convb"]), jnp.stack(ssm_g["dtb"]), jnp.stack(ssm_g["alog"]),
        jnp.stack(ssm_g["d"]), jnp.stack(ssm_g["norm"]), dg_final[0],
    ]
    full_shapes = [tuple(a.shape) for a in small_full]
    gathered = _all_gather_devices("ag8_small", _pack(small_full))
    summed = _sum_blocks("sum8_small", gathered)
    (g_pre, g_mix, g_post, g_vn, g_ws, g_bs, g_cw, g_cb, g_dtb, g_al, g_d, g_nm, g_fin) = _unpack(summed, full_shapes)
    g_cw = lax.dynamic_slice_in_dim(g_cw, kchip * cq, cq, axis=1)
    g_cb = lax.dynamic_slice_in_dim(g_cb, kchip * cq, cq, axis=1)
    g_nm = lax.dynamic_slice_in_dim(g_nm, kchip * iq, iq, axis=1)
    sm_g = [g_pre, g_mix, g_post, g_vn, g_ws, g_bs, g_cw, g_cb, g_dtb, g_al, g_d, g_nm, g_fin]
    sm_w = [ln_ffn_pre, ln_mix, ln_ffn_post, gm_v_norm, gm_w_s, gm_b_s, ssm_conv_w, ssm_conv_b, ssm_dt_bias,
            ssm_a_log, ssm_d, ssm_norm, ln_final]
    sm_m = [m_ln_ffn_pre, m_ln_mix, m_ln_ffn_post, m_gm_v_norm, m_gm_w_s, m_gm_b_s, m_ssm_conv_w, m_ssm_conv_b,
            m_ssm_dt_bias, m_ssm_a_log, m_ssm_d, m_ssm_norm, m_ln_final]
    sm_v = [v_ln_ffn_pre, v_ln_mix, v_ln_ffn_post, v_gm_v_norm, v_gm_w_s, v_gm_b_s, v_ssm_conv_w, v_ssm_conv_b,
            v_ssm_dt_bias, v_ssm_a_log, v_ssm_d, v_ssm_norm, v_ln_final]
    sm_shapes = [tuple(a.shape) for a in sm_w]
    pk = [_pack(lst)[None] for lst in (sm_w, sm_m, sm_v)]
    sg_, sd_, snm_, snv_ = _adamw("adamw_small", pk[0], pk[1], pk[2], [[_pack(sm_g)]])
    small_out = [_unpack(t[0], sm_shapes) for t in (sg_, sd_, snm_, snv_)]
    small_names = ["ln_ffn_pre", "ln_mix", "ln_ffn_post", "gm_v_norm", "gm_w_s", "gm_b_s", "ssm_conv_w", "ssm_conv_b",
                   "ssm_dt_bias", "ssm_a_log", "ssm_d", "ssm_norm", "ln_final"]

    def big_update(tag, w, m, v, sub_ids, which, tok=None):
        L = w.shape[0]
        shp = w.shape
        w2, m2, v2 = (t.reshape(L, -1, shp[-1]) for t in (w, m, v))
        outs = _adamw("adamw_" + tag, w2, m2, v2, [[wgrad[s][which]] for s in sub_ids], tok)
        return [o.reshape(shp) for o in outs]

    last = _reduce_start("0", deferred[0], sd_)
    tok = last[3][0:1, 0:1]
    pre_ids = [3 * i for i in range(depth)]
    post_ids = [3 * i + 2 for i in range(depth)]
    gm_ids = [3 * i + 1 for i in range(depth) if i % 2 == 0]
    ssm_ids = [3 * i + 1 for i in range(depth) if i % 2 == 1]
    big_out = {
        "ffn_post_w_in": big_update("ffn_in", ffn_post_w_in, m_ffn_post_w_in, v_ffn_post_w_in, post_ids, 0, tok),
        "ffn_post_w_out": big_update("ffn_out", ffn_post_w_out, m_ffn_post_w_out, v_ffn_post_w_out, post_ids, 1, tok),
        "gm_w_in": big_update("gm_in", gm_w_in, m_gm_w_in, v_gm_w_in, gm_ids, 0, tok),
        "gm_w_out": big_update("mix_out", gm_w_out, m_gm_w_out, v_gm_w_out, gm_ids, 1, tok),
        "ssm_w_in": big_update("ssm_in", ssm_w_in, m_ssm_w_in, v_ssm_w_in, ssm_ids, 0, tok),
        "ssm_w_out": big_update("mix_out", ssm_w_out, m_ssm_w_out, v_ssm_w_out, ssm_ids, 1, tok),
    }
    after = big_out["ffn_post_w_in"][1][0, 0:1, 0:1]
    for n in ("ffn_post_w_out", "gm_w_in", "gm_w_out", "ssm_w_in", "ssm_w_out"):
        after = after + big_out[n][1][0, 0:1, 0:1]
    wgrad[0] = _reduce_finish("0", last, after)
    big_out["ffn_pre_w_in"] = big_update("ffn_in", ffn_pre_w_in, m_ffn_pre_w_in, v_ffn_pre_w_in, pre_ids, 0)
    big_out["ffn_pre_w_out"] = big_update("ffn_out", ffn_pre_w_out, m_ffn_pre_w_out, v_ffn_pre_w_out, pre_ids, 1)

    order = ["ln_ffn_pre", "ffn_pre_w_in", "ffn_pre_w_out", "ln_mix", "ln_ffn_post", "ffn_post_w_in", "ffn_post_w_out",
             "gm_w_in", "gm_v_norm", "gm_w_s", "gm_b_s", "gm_w_out", "ssm_w_in", "ssm_conv_w", "ssm_conv_b",
             "ssm_dt_bias", "ssm_a_log", "ssm_d", "ssm_norm", "ssm_w_out", "ln_final"]

    def pick(kind, n):
        if n in big_out:
            return big_out[n][kind]
        return small_out[kind][small_names.index(n)]

    outs = [loss, dx[None]]
    for kind in range(4):
        outs.extend(pick(kind, n) for n in order)
    return tuple(outs)
```

```python
import jax
import jax.numpy as jnp
from jax import lax
from jax.experimental import pallas as pl
from jax.experimental.pallas import tpu as pltpu

F32 = jnp.float32
BF16 = jnp.bfloat16
HIGHEST = lax.Precision.HIGHEST
MESH = pl.DeviceIdType.MESH

EPS = 1e-6
ADAM_LR, ADAM_B1, ADAM_B2, ADAM_EPS, ADAM_WD, ADAM_STEP = 0.001, 0.9, 0.999, 1e-08, 0.01, 10

LANES = 128
CHUNK = 128
SSM_STATE = 128
SSM_HEAD_DIM = 64
SSM_HPG = 8
SSM_CONV = 4
VMEM_LIMIT = 56 * 1024 * 1024
MM_TILE = 1408
MM_VMEM_BUDGET = 46 * 1024 * 1024

_ANY = pl.BlockSpec(memory_space=pl.ANY)


def _cp(*sem):
    return pltpu.CompilerParams(dimension_semantics=sem if sem else None, vmem_limit_bytes=VMEM_LIMIT)


def _tile(n, target, mult=LANES):
    best = None
    t = mult
    while t <= min(n, target):
        if n % t == 0:
            best = t
        t += mult
    return n if best is None else best


def _gcd(*v):
    import math
    g = 0
    for a in v:
        g = math.gcd(g, a)
    return g


def _sds(shape, dtype):
    return jax.ShapeDtypeStruct(tuple(shape), dtype)


def _ldims(shape):
    return tuple(shape) if len(shape) == 2 else (shape[1], shape[0] * shape[2])


def _colblock(shape):
    return None if len(shape) == 2 else shape[2]


def _mspec(shape, tr, tc, rc):
    if len(shape) == 2:
        return pl.BlockSpec((tr, tc), rc)
    per = shape[2] // tc

    def im(i, j, k):
        r, c = rc(i, j, k)
        return (c // per, r, c % per)

    return pl.BlockSpec((None, tr, tc), im)


def _matmul(name, a, b, mode, out_dtype, out_blocks=None, res=None, scale=1.0, tok=None):
    la, lb = _ldims(a.shape), _ldims(b.shape)
    if mode == "nn":
        (M, K), (K2, N) = la, lb
    elif mode == "nt":
        (M, K), (N, K2) = la, lb
    else:
        (K, M), (K2, N) = la, lb
    assert K == K2, (name, a.shape, b.shape, mode)
    ca, cb = _colblock(a.shape), _colblock(b.shape)
    out_shape = (M, N) if out_blocks is None else (out_blocks, M, N // out_blocks)
    co = _colblock(out_shape)
    m_c, n_c, k_c = [M], [N], [K]
    if ca is not None:
        (m_c if mode == "tn" else k_c).append(ca)
    if cb is not None:
        (k_c if mode == "nt" else n_c).append(cb)
    if co is not None:
        n_c.append(co)
    tm, tn = _tile(_gcd(*m_c), MM_TILE), _tile(_gcd(*n_c), MM_TILE)
    out_bytes = jnp.dtype(out_dtype).itemsize + (4 if res is not None else 0)
    kg = _gcd(*k_c)
    tk = LANES if kg % LANES == 0 else kg
    for cand in range(LANES, kg + 1, LANES):
        if kg % cand == 0 and 2 * (2 * cand * (tm + tn) + tm * tn * out_bytes) + 2 * 4 * tm * tn <= MM_VMEM_BUDGET:
            tk = cand
    nk = K // tk
    if mode == "tn":
        a_spec = _mspec(a.shape, tk, tm, lambda i, j, k: (k, i))
        dims = (((0,), (0,)), ((), ()))
    else:
        a_spec = _mspec(a.shape, tm, tk, lambda i, j, k: (i, k))
        dims = (((1,), (1,)), ((), ())) if mode == "nt" else (((1,), (0,)), ((), ()))
    if mode == "nt":
        b_spec = _mspec(b.shape, tn, tk, lambda i, j, k: (j, k))
    else:
        b_spec = _mspec(b.shape, tk, tn, lambda i, j, k: (k, j))
    o_spec = _mspec(out_shape, tm, tn, lambda i, j, k: (i, j))
    in_specs, args = [a_spec, b_spec], [a, b]
    if res is not None:
        in_specs.append(pl.BlockSpec((tm, tn), lambda i, j, k: (i, j)))
        args.append(res)
    if tok is not None:
        in_specs.append(pl.BlockSpec((1, 1), lambda i, j, k: (0, 0)))
        args.append(tok)
    n_in = len(args)

    def body(*refs):
        a_ref, b_ref = refs[0], refs[1]
        res_ref = refs[2] if res is not None else None
        tok_ref = refs[n_in - 1] if tok is not None else None
        o_ref = refs[n_in]

        def product():
            return lax.dot_general(a_ref[...].astype(BF16), b_ref[...].astype(BF16), dims, preferred_element_type=F32)

        def finish(r):
            if scale != 1.0:
                r = r * scale
            if res_ref is not None:
                r = res_ref[...] + r
            if tok_ref is not None:
                r = r + tok_ref[...]
            o_ref[...] = r.astype(o_ref.dtype)

        if nk == 1:
            finish(product())
            return
        acc_ref = refs[-1]
        kk = pl.program_id(2)

        @pl.when(kk == 0)
        def _():
            acc_ref[...] = product()

        @pl.when((kk > 0) & (kk < nk - 1))
        def _():
            acc_ref[...] += product()

        @pl.when(kk == nk - 1)
        def _():
            finish(acc_ref[...] + product())

    return pl.pallas_call(
        body, name=name, grid=(M // tm, N // tn, nk), in_specs=in_specs, out_specs=o_spec,
        out_shape=_sds(out_shape, out_dtype), scratch_shapes=[pltpu.VMEM((tm, tn), F32)] if nk > 1 else [],
        compiler_params=_cp("parallel", "parallel", "arbitrary"))(*args)


def _rows(i):
    return (i, 0)


def _row0(i):
    return (0, 0)


def _rms_fwd(name, x, g):
    T, D = x.shape
    tr = _tile(T, 256, 8)

    def body(x_ref, g_ref, o_ref):
        xv = x_ref[...]
        r = lax.rsqrt(jnp.mean(xv * xv, axis=-1, keepdims=True) + EPS)
        o_ref[...] = (xv * r * g_ref[...]).astype(o_ref.dtype)

    return pl.pallas_call(
        body, name=name, grid=(T // tr,),
        in_specs=[pl.BlockSpec((tr, D), _rows), pl.BlockSpec((1, D), _row0)],
        out_specs=pl.BlockSpec((tr, D), _rows), out_shape=_sds((T, D), BF16),
        compiler_params=_cp("parallel"))(x, g)


def _rms_bwd(name, x, g, dh, dres):
    T, D = x.shape
    tr = _tile(T, 256, 8)

    def body(x_ref, g_ref, dh_ref, dres_ref, dx_ref, dxb_ref, dg_ref):
        xv = x_ref[...]
        r = lax.rsqrt(jnp.mean(xv * xv, axis=-1, keepdims=True) + EPS)
        xh = xv * r
        dhv = dh_ref[...]
        dxh = dhv * g_ref[...]
        dx = dres_ref[...] + r * (dxh - xh * jnp.mean(dxh * xh, axis=-1, keepdims=True))
        dx_ref[...] = dx
        dxb_ref[...] = dx.astype(BF16)

        @pl.when(pl.program_id(0) == 0)
        def _():
            dg_ref[...] = jnp.zeros_like(dg_ref)

        dg_ref[...] += jnp.sum(dhv * xh, axis=0, keepdims=True)

    return pl.pallas_call(
        body, name=name, grid=(T // tr,),
        in_specs=[pl.BlockSpec((tr, D), _rows), pl.BlockSpec((1, D), _row0),
                  pl.BlockSpec((tr, D), _rows), pl.BlockSpec((tr, D), _rows)],
        out_specs=[pl.BlockSpec((tr, D), _rows), pl.BlockSpec((tr, D), _rows), pl.BlockSpec((1, D), _row0)],
        out_shape=[_sds((T, D), F32), _sds((T, D), BF16), _sds((1, D), F32)],
        compiler_params=_cp("arbitrary"))(x, g, dh, dres)


def _sigmoid(v):
    return 1.0 / (1.0 + jnp.exp(-v))


def _swiglu_fwd(name, a):
    _, T, F = a.shape
    tr, tc = _tile(T, 512, 8), _tile(F, MM_TILE)

    def body(a_ref, o_ref):
        gate = a_ref[0].astype(F32)
        up = a_ref[1].astype(F32)
        o_ref[...] = (gate * _sigmoid(gate) * up).astype(o_ref.dtype)

    return pl.pallas_call(
        body, name=name, grid=(T // tr, F // tc),
        in_specs=[pl.BlockSpec((2, tr, tc), lambda i, j: (0, i, j))],
        out_specs=pl.BlockSpec((tr, tc), lambda i, j: (i, j)), out_shape=_sds((T, F), BF16),
        compiler_params=_cp("parallel", "parallel"))(a)


def _swiglu_bwd(name, a, ds):
    _, T, F = a.shape
    tr, tc = _tile(T, 512, 8), _tile(F, MM_TILE)

    def body(a_ref, ds_ref, o_ref):
        gate = a_ref[0].astype(F32)
        up = a_ref[1].astype(F32)
        dsv = ds_ref[...].astype(F32)
        sg = _sigmoid(gate)
        o_ref[0] = (dsv * up * sg * (1.0 + gate * (1.0 - sg))).astype(o_ref.dtype)
        o_ref[1] = (dsv * gate * sg).astype(o_ref.dtype)

    return pl.pallas_call(
        body, name=name, grid=(T // tr, F // tc),
        in_specs=[pl.BlockSpec((2, tr, tc), lambda i, j: (0, i, j)), pl.BlockSpec((tr, tc), lambda i, j: (i, j))],
        out_specs=pl.BlockSpec((2, tr, tc), lambda i, j: (0, i, j)), out_shape=_sds((2, T, F), BF16),
        compiler_params=_cp("parallel", "parallel"))(a, ds)


def _loss_head(name, x, g, tgt):
    T, D = x.shape
    tr = _tile(T, 256, 8)

    def body(x_ref, g_ref, t_ref, loss_ref, dx_ref, dxb_ref, dg_ref):
        xv = x_ref[...]
        gv = g_ref[...]
        r = lax.rsqrt(jnp.mean(xv * xv, axis=-1, keepdims=True) + EPS)
        xh = xv * r
        err = xh * gv - t_ref[...]
        dy = err * (1.0 / D)
        dxh = dy * gv
        dx = r * (dxh - xh * jnp.mean(dxh * xh, axis=-1, keepdims=True))
        dx_ref[...] = dx
        dxb_ref[...] = dx.astype(BF16)

        @pl.when(pl.program_id(0) == 0)
        def _():
            dg_ref[...] = jnp.zeros_like(dg_ref)
            loss_ref[...] = jnp.zeros_like(loss_ref)

        dg_ref[...] += jnp.sum(dy * xh, axis=0, keepdims=True)
        part = jnp.sum(jnp.sum(err * err, axis=-1, keepdims=True), axis=0, keepdims=True) * (0.5 / D)
        loss_ref[...] += part

    return pl.pallas_call(
        body, name=name, grid=(T // tr,),
        in_specs=[pl.BlockSpec((tr, D), _rows), pl.BlockSpec((1, D), _row0), pl.BlockSpec((tr, D), _rows)],
        out_specs=[pl.BlockSpec((8, LANES), _row0), pl.BlockSpec((tr, D), _rows), pl.BlockSpec((tr, D), _rows),
                   pl.BlockSpec((1, D), _row0)],
        out_shape=[_sds((8, LANES), F32), _sds((T, D), F32), _sds((T, D), BF16), _sds((1, D), F32)],
        compiler_params=_cp("arbitrary"))(x, g, tgt)


_SQRT_HALF = 0.7071067811865476
_INV_SQRT_2PI = 0.3989422804014327


def _gelu(v):
    return 0.5 * v * (1.0 + lax.erf(v * _SQRT_HALF))


def _gelu_grad(v):
    return 0.5 * (1.0 + lax.erf(v * _SQRT_HALF)) + v * _INV_SQRT_2PI * jnp.exp(-0.5 * v * v)


def _group_expand(rows, width, gd):
    gi = lax.broadcasted_iota(jnp.int32, (rows, width), 0)
    fi = lax.broadcasted_iota(jnp.int32, (rows, width), 1)
    return ((fi >= gi * gd) & (fi < (gi + 1) * gd)).astype(F32)


def _gm_act_fwd(name, zpre, vnorm):
    _, T, GI = zpre.shape
    tr = _tile(T, 128, 8)

    def body(z_ref, g_ref, u_ref, v_ref):
        u_ref[...] = _gelu(z_ref[0].astype(F32)).astype(BF16)
        zv = _gelu(z_ref[1].astype(F32))
        r = lax.rsqrt(jnp.mean(zv * zv, axis=-1, keepdims=True) + EPS)
        v_ref[...] = (zv * r * g_ref[...]).astype(BF16)

    return pl.pallas_call(
        body, name=name, grid=(T // tr,),
        in_specs=[pl.BlockSpec((2, tr, GI), lambda i: (0, i, 0)), pl.BlockSpec((1, GI), _row0)],
        out_specs=[pl.BlockSpec((tr, GI), _rows), pl.BlockSpec((tr, GI), _rows)],
        out_shape=[_sds((T, GI), BF16), _sds((T, GI), BF16)], compiler_params=_cp("parallel"))(zpre, vnorm)


def _gm_act_bwd(name, zpre, du, dvn, vnorm):
    _, T, GI = zpre.shape
    tr = _tile(T, 128, 8)

    def body(z_ref, du_ref, dvn_ref, g_ref, dz_ref, dg_ref):
        xu = z_ref[0].astype(F32)
        xv = z_ref[1].astype(F32)
        zv = _gelu(xv)
        r = lax.rsqrt(jnp.mean(zv * zv, axis=-1, keepdims=True) + EPS)
        xh = zv * r
        dv = dvn_ref[...]

        @pl.when(pl.program_id(0) == 0)
        def _():
            dg_ref[...] = jnp.zeros_like(dg_ref)

        dg_ref[...] += jnp.sum(dv * xh, axis=0, keepdims=True)
        dxh = dv * g_ref[...]
        dzv = r * (dxh - xh * jnp.mean(dxh * xh, axis=-1, keepdims=True))
        dz_ref[0] = (du_ref[...] * _gelu_grad(xu)).astype(BF16)
        dz_ref[1] = (dzv * _gelu_grad(xv)).astype(BF16)

    return pl.pallas_call(
        body, name=name, grid=(T // tr,),
        in_specs=[pl.BlockSpec((2, tr, GI), lambda i: (0, i, 0)), pl.BlockSpec((tr, GI), _rows),
                  pl.BlockSpec((tr, GI), _rows), pl.BlockSpec((1, GI), _row0)],
        out_specs=[pl.BlockSpec((2, tr, GI), lambda i: (0, i, 0)), pl.BlockSpec((1, GI), _row0)],
        out_shape=[_sds((2, T, GI), BF16), _sds((1, GI), F32)], compiler_params=_cp("arbitrary"))(zpre, du, dvn, vnorm)


def _causal_mask():
    r = lax.broadcasted_iota(jnp.int32, (CHUNK, CHUNK), 0)
    c = lax.broadcasted_iota(jnp.int32, (CHUNK, CHUNK), 1)
    return r >= c


def _gm_spatial_fwd(name, u, vn, ws, bt):
    T, GI = u.shape
    G = ws.shape[0]
    gd = GI // G

    def body(u_ref, v_ref, ws_ref, bt_ref, o_ref, bias_scr):
        @pl.when(pl.program_id(0) == 0)
        def _():
            bias_scr[...] = jnp.dot(bt_ref[...], _group_expand(LANES, GI, gd), precision=HIGHEST,
                                    preferred_element_type=F32)

        causal = _causal_mask()
        for g in range(G):
            sl = slice(g * gd, (g + 1) * gd)
            wc = jnp.where(causal, ws_ref[g], 0.0).astype(BF16)
            mixed = jnp.dot(wc, v_ref[:, sl], preferred_element_type=F32) + bias_scr[:, sl]
            o_ref[:, sl] = (u_ref[:, sl].astype(F32) * mixed).astype(o_ref.dtype)

    return pl.pallas_call(
        body, name=name, grid=(T // CHUNK,),
        in_specs=[pl.BlockSpec((CHUNK, GI), _rows), pl.BlockSpec((CHUNK, GI), _rows),
                  pl.BlockSpec((G, CHUNK, CHUNK), lambda i: (0, 0, 0)), pl.BlockSpec((CHUNK, LANES), _row0)],
        out_specs=pl.BlockSpec((CHUNK, GI), _rows), out_shape=_sds((T, GI), BF16),
        scratch_shapes=[pltpu.VMEM((CHUNK, GI), F32)], compiler_params=_cp("arbitrary"))(u, vn, ws, bt)


def _gm_spatial_bwd(name, dgated, u, vn, ws, bt):
    T, GI = u.shape
    G = ws.shape[0]
    gd = GI // G
    nc = T // CHUNK

    def body(dg_ref, u_ref, v_ref, ws_ref, bt_ref, du_ref, dv_ref, dws_ref, dbt_ref, bias_scr, dm_scr):
        step = pl.program_id(0)

        @pl.when(step == 0)
        def _():
            bias_scr[...] = jnp.dot(bt_ref[...], _group_expand(LANES, GI, gd), precision=HIGHEST,
                                    preferred_element_type=F32)
            dm_scr[...] = jnp.zeros_like(dm_scr)
            dws_ref[...] = jnp.zeros_like(dws_ref)

        causal = _causal_mask()
        for g in range(G):
            sl = slice(g * gd, (g + 1) * gd)
            wc = jnp.where(causal, ws_ref[g], 0.0).astype(BF16)
            vv = v_ref[:, sl]
            dgv = dg_ref[:, sl].astype(F32)
            mixed = jnp.dot(wc, vv, preferred_element_type=F32) + bias_scr[:, sl]
            du_ref[:, sl] = dgv * mixed
            dm = dgv * u_ref[:, sl].astype(F32)
            dmb = dm.astype(BF16)
            dv_ref[:, sl] = lax.dot_general(wc, dmb, (((0,), (0,)), ((), ())), preferred_element_type=F32)
            dw = lax.dot_general(dmb, vv, (((1,), (1,)), ((), ())), preferred_element_type=F32)
            dws_ref[g] += jnp.where(causal, dw, 0.0)
            dm_scr[:, sl] += dm

        @pl.when(step == nc - 1)
        def _():
            dbt_ref[...] = lax.dot_general(dm_scr[...], _group_expand(LANES, GI, gd), (((1,), (1,)), ((), ())),
                                           precision=HIGHEST, preferred_element_type=F32)

    return pl.pallas_call(
        body, name=name, grid=(nc,),
        in_specs=[pl.BlockSpec((CHUNK, GI), _rows), pl.BlockSpec((CHUNK, GI), _rows), pl.BlockSpec((CHUNK, GI), _rows),
                  pl.BlockSpec((G, CHUNK, CHUNK), lambda i: (0, 0, 0)), pl.BlockSpec((CHUNK, LANES), _row0)],
        out_specs=[pl.BlockSpec((CHUNK, GI), _rows), pl.BlockSpec((CHUNK, GI), _rows),
                   pl.BlockSpec((G, CHUNK, CHUNK), lambda i: (0, 0, 0)), pl.BlockSpec((CHUNK, LANES), _row0)],
        out_shape=[_sds((T, GI), F32), _sds((T, GI), F32), _sds((G, CHUNK, CHUNK), F32), _sds((CHUNK, LANES), F32)],
        scratch_shapes=[pltpu.VMEM((CHUNK, GI), F32), pltpu.VMEM((CHUNK, GI), F32)],
        compiler_params=_cp("arbitrary"))(dgated, u, vn, ws, bt)


def _conv_taps(xv, w_ref, b_ref):
    rows = lax.broadcasted_iota(jnp.int32, xv.shape, 0)
    acc = xv * w_ref[pl.ds(SSM_CONV - 1, 1), :] + b_ref[...]
    for k in range(1, SSM_CONV):
        sh = jnp.where(rows >= k, pltpu.roll(xv, k, 0), 0.0)
        acc = acc + sh * w_ref[pl.ds(SSM_CONV - 1 - k, 1), :]
    return acc


def _ssd_conv_fwd(name, proj, wt, b, inner, cd):
    T = proj.shape[0]
    tc = _tile(_gcd(inner, cd), 512)
    off = inner // tc

    def body(x_ref, w_ref, b_ref, o_ref):
        pre = _conv_taps(x_ref[...], w_ref, b_ref)
        o_ref[...] = pre * _sigmoid(pre)

    return pl.pallas_call(
        body, name=name, grid=(cd // tc,),
        in_specs=[pl.BlockSpec((T, tc), lambda j: (0, off + j)), pl.BlockSpec((SSM_CONV, tc), lambda j: (0, j)),
                  pl.BlockSpec((1, tc), lambda j: (0, j))],
        out_specs=pl.BlockSpec((T, tc), lambda j: (0, j)), out_shape=_sds((T, cd), F32),
        compiler_params=_cp("parallel"))(proj, wt, b)


def _ssd_conv_bwd(name, dact, proj, wt, b, inner, cd):
    T = proj.shape[0]
    tc = _tile(_gcd(inner, cd), 512)
    off = inner // tc

    def body(da_ref, x_ref, w_ref, b_ref, dx_ref, dw_ref, db_ref):
        xv = x_ref[...]
        pre = _conv_taps(xv, w_ref, b_ref)
        sg = _sigmoid(pre)
        dpre = da_ref[...] * sg * (1.0 + pre * (1.0 - sg))
        rows = lax.broadcasted_iota(jnp.int32, xv.shape, 0)
        db_ref[...] = jnp.sum(dpre, axis=0, keepdims=True)
        dx = dpre * w_ref[pl.ds(SSM_CONV - 1, 1), :]
        dw_ref[pl.ds(SSM_CONV - 1, 1), :] = jnp.sum(dpre * xv, axis=0, keepdims=True)
        for k in range(1, SSM_CONV):
            sh = jnp.where(rows >= k, pltpu.roll(xv, k, 0), 0.0)
            dw_ref[pl.ds(SSM_CONV - 1 - k, 1), :] = jnp.sum(dpre * sh, axis=0, keepdims=True)
            fw = jnp.where(rows < T - k, pltpu.roll(dpre, T - k, 0), 0.0)
            dx = dx + fw * w_ref[pl.ds(SSM_CONV - 1 - k, 1), :]
        dx_ref[...] = dx.astype(BF16)

    return pl.pallas_call(
        body, name=name, grid=(cd // tc,),
        in_specs=[pl.BlockSpec((T, tc), lambda j: (0, j)), pl.BlockSpec((T, tc), lambda j: (0, off + j)),
                  pl.BlockSpec((SSM_CONV, tc), lambda j: (0, j)), pl.BlockSpec((1, tc), lambda j: (0, j))],
        out_specs=[pl.BlockSpec((T, tc), lambda j: (0, j)), pl.BlockSpec((SSM_CONV, tc), lambda j: (0, j)),
                   pl.BlockSpec((1, tc), lambda j: (0, j))],
        out_shape=[_sds((T, cd), BF16), _sds((SSM_CONV, cd), F32), _sds((1, cd), F32)],
        compiler_params=_cp("parallel"))(dact, proj, wt, b)


def _softplus(v):
    return jnp.maximum(v, 0.0) + jnp.log(1.0 + jnp.exp(-jnp.abs(v)))


def _tri(lower):
    r = lax.broadcasted_iota(jnp.int32, (CHUNK, CHUNK), 0)
    c = lax.broadcasted_iota(jnp.int32, (CHUNK, CHUNK), 1)
    return ((c <= r) if lower else (c >= r)).astype(F32)


def _ssd_dt_fwd(name, proj, bias, alog, dtcol, heads):
    T = proj.shape[0]

    def body(dt_ref, b_ref, al_ref, dtp_ref, ac_ref):
        live = lax.broadcasted_iota(jnp.int32, (CHUNK, LANES), 1) < heads
        dtp = jnp.where(live, _softplus(dt_ref[...] + b_ref[...]), 0.0)
        da = dtp * (-jnp.exp(al_ref[...]))
        dtp_ref[...] = dtp
        ac_ref[...] = jnp.dot(_tri(True), da, precision=HIGHEST, preferred_element_type=F32)

    return pl.pallas_call(
        body, name=name, grid=(T // CHUNK,),
        in_specs=[pl.BlockSpec((CHUNK, LANES), lambda i: (i, dtcol)), pl.BlockSpec((1, LANES), _row0),
                  pl.BlockSpec((1, LANES), _row0)],
        out_specs=[pl.BlockSpec((CHUNK, LANES), _rows), pl.BlockSpec((CHUNK, LANES), _rows)],
        out_shape=[_sds((T, LANES), F32), _sds((T, LANES), F32)], compiler_params=_cp("parallel"))(proj, bias, alog)


def _ssd_dt_bwd(name, ddtp_g, dacum_g, dd_g, proj, bias, alog, dtp, dtcol, heads):
    T = proj.shape[0]
    G = ddtp_g.shape[0]

    def body(ddtp_ref, dac_ref, dd_ref, dt_ref, b_ref, al_ref, dtp_ref, ddt_ref, db_ref, dal_ref, dds_ref, da_scr):
        step = pl.program_id(0)

        @pl.when(step == 0)
        def _():
            db_ref[...] = jnp.zeros_like(db_ref)
            da_scr[...] = jnp.zeros_like(da_scr)
            dds_ref[...] = jnp.sum(dd_ref[...], axis=0)

        live = lax.broadcasted_iota(jnp.int32, (CHUNK, LANES), 1) < heads
        a = -jnp.exp(al_ref[...])
        dac = jnp.sum(dac_ref[...], axis=0)
        dda = jnp.dot(_tri(False), dac, precision=HIGHEST, preferred_element_type=F32)
        dtp_v = dtp_ref[...]
        ddtp = jnp.sum(ddtp_ref[...], axis=0) + dda * a
        da_scr[...] += jnp.sum(dda * dtp_v, axis=0, keepdims=True)
        ddt = jnp.where(live, ddtp * _sigmoid(dt_ref[...] + b_ref[...]), 0.0)
        ddt_ref[...] = ddt.astype(BF16)
        db_ref[...] += jnp.sum(ddt, axis=0, keepdims=True)
        dal_ref[...] = da_scr[...] * a

    return pl.pallas_call(
        body, name=name, grid=(T // CHUNK,),
        in_specs=[pl.BlockSpec((G, CHUNK, LANES), lambda i: (0, i, 0)), pl.BlockSpec((G, CHUNK, LANES), lambda i: (0, i, 0)),
                  pl.BlockSpec((G, 8, LANES), lambda i: (0, 0, 0)),
                  pl.BlockSpec((CHUNK, LANES), lambda i: (i, dtcol)), pl.BlockSpec((1, LANES), _row0),
                  pl.BlockSpec((1, LANES), _row0), pl.BlockSpec((CHUNK, LANES), _rows)],
        out_specs=[pl.BlockSpec((CHUNK, LANES), _rows), pl.BlockSpec((8, LANES), _row0), pl.BlockSpec((8, LANES), _row0),
                   pl.BlockSpec((8, LANES), _row0)],
        out_shape=[_sds((T, LANES), BF16), _sds((8, LANES), F32), _sds((8, LANES), F32), _sds((8, LANES), F32)],
        scratch_shapes=[pltpu.VMEM((8, LANES), F32)],
        compiler_params=_cp("arbitrary"))(ddtp_g, dacum_g, dd_g, proj, bias, alog, dtp)


def _head_expand(g, gw):
    hi = lax.broadcasted_iota(jnp.int32, (LANES, gw), 0) - g * SSM_HPG
    fi = lax.broadcasted_iota(jnp.int32, (LANES, gw), 1)
    return ((fi >= hi * SSM_HEAD_DIM) & (fi < (hi + 1) * SSM_HEAD_DIM)).astype(F32)


def _dot(a, b, dims, exact=False):
    if exact:
        return lax.dot_general(a, b, (dims, ((), ())), precision=HIGHEST, preferred_element_type=F32)
    return lax.dot_general(a.astype(BF16), b.astype(BF16), (dims, ((), ())), preferred_element_type=F32)


_NN = ((1,), (0,))
_NT = ((1,), (1,))
_TN = ((0,), (0,))


def _pair_decay(g, q, acum, acum_t_ref, causal):
    lane = lax.broadcasted_iota(jnp.int32, (CHUNK, LANES), 1)
    out = []
    for e in range(2):
        h = g * SSM_HPG + 2 * q + e
        acol = jnp.sum(jnp.where(lane == h, acum, 0.0), axis=1, keepdims=True)
        arow = acum_t_ref[pl.ds(h, 1), :]
        out.append(jnp.exp(jnp.where(causal, acol - arow, -1e30)))
    return out


def _ssd_core_fwd(name, act, dtp, acum, dexp, inner, groups):
    T = act.shape[0]
    nc = T // CHUNK
    gw = SSM_HPG * SSM_HEAD_DIM
    npair = gw // LANES
    bcol, ccol = inner // SSM_STATE, inner // SSM_STATE + groups

    def body(x_ref, b_ref, c_ref, dtp_ref, ac_ref, d_ref, y_ref, sp_ref, st_scr, act_scr, ae_scr):
        g = pl.program_id(0)

        @pl.when(pl.program_id(1) == 0)
        def _():
            st_scr[...] = jnp.zeros_like(st_scr)

        st = st_scr[...]
        sp_ref[...] = st
        e = _head_expand(g, gw)
        acum = ac_ref[...]
        ae = _dot(acum, e, _NN, exact=True)
        dte = _dot(dtp_ref[...], e, _NN, exact=True)
        ae_scr[...] = ae
        act_scr[...] = acum.T
        xv = x_ref[...]
        xdt = xv * dte
        bm, cm = b_ref[...], c_ref[...]
        cb = _dot(cm, bm, _NT)
        causal = _causal_mask()
        lane = lax.broadcasted_iota(jnp.int32, (CHUNK, LANES), 1)
        yoff = _dot(cm, st, _NN) * jnp.exp(ae)
        skip = xv * d_ref[...]
        for q in range(npair):
            sl = slice(q * LANES, (q + 1) * LANES)
            dec = _pair_decay(g, q, acum, act_scr, causal)
            x2 = xdt[:, sl]
            xa = jnp.where(lane < SSM_HEAD_DIM, x2, 0.0)
            yd = _dot(dec[0] * cb, xa, _NN) + _dot(dec[1] * cb, x2 - xa, _NN)
            y_ref[:, sl] = yd + yoff[:, sl] + skip[:, sl]
        alast = ae_scr[pl.ds(CHUNK - 1, 1), :]
        z = xdt * jnp.exp(alast - ae)
        st_scr[...] = st * jnp.exp(alast) + _dot(bm, z, _TN)

    return pl.pallas_call(
        body, name=name, grid=(groups, nc),
        in_specs=[pl.BlockSpec((CHUNK, gw), lambda g, c: (c, g)),
                  pl.BlockSpec((CHUNK, SSM_STATE), lambda g, c: (c, bcol + g)),
                  pl.BlockSpec((CHUNK, SSM_STATE), lambda g, c: (c, ccol + g)),
                  pl.BlockSpec((CHUNK, LANES), lambda g, c: (c, 0)), pl.BlockSpec((CHUNK, LANES), lambda g, c: (c, 0)),
                  pl.BlockSpec((1, gw), lambda g, c: (0, g))],
        out_specs=[pl.BlockSpec((CHUNK, gw), lambda g, c: (c, g)),
                   pl.BlockSpec((None, SSM_STATE, gw), lambda g, c: (c, 0, g))],
        out_shape=[_sds((T, inner), F32), _sds((nc, SSM_STATE, inner), F32)],
        scratch_shapes=[pltpu.VMEM((SSM_STATE, gw), F32), pltpu.VMEM((CHUNK, LANES), F32), pltpu.VMEM((CHUNK, gw), F32)],
        compiler_params=_cp("arbitrary", "arbitrary"))(act, act, act, dtp, acum, dexp)


def _ssd_core_bwd(name, dy, act, dtp, acum, dexp, sprev, inner, groups):
    T = act.shape[0]
    nc = T // CHUNK
    gw = SSM_HPG * SSM_HEAD_DIM
    npair = gw // LANES
    bcol, ccol = inner // SSM_STATE, inner // SSM_STATE + groups

    def rc(g, c):
        return nc - 1 - c

    def body(dy_ref, x_ref, b_ref, c_ref, dtp_ref, ac_ref, d_ref, sp_ref,
             dx_ref, db_ref, dc_ref, ddtp_ref, dac_ref, dd_ref,
             dst_scr, act_scr, ae_scr, dxdt_scr, dd_scr, dact_scr):
        g = pl.program_id(0)
        step = pl.program_id(1)

        @pl.when(step == 0)
        def _():
            dst_scr[...] = jnp.zeros_like(dst_scr)
            dd_scr[...] = jnp.zeros_like(dd_scr)

        dst = dst_scr[...]
        sp = sp_ref[...]
        e = _head_expand(g, gw)
        acum = ac_ref[...]
        ae = _dot(acum, e, _NN, exact=True)
        dte = _dot(dtp_ref[...], e, _NN, exact=True)
        ae_scr[...] = ae
        act_scr[...] = acum.T
        alast = ae_scr[pl.ds(CHUNK - 1, 1), :]
        xv = x_ref[...]
        xdt = xv * dte
        bm, cm = b_ref[...], c_ref[...]
        dyv = dy_ref[...]
        cb = _dot(cm, bm, _NT)
        causal = _causal_mask()
        lane = lax.broadcasted_iota(jnp.int32, (CHUNK, LANES), 1)
        ea = jnp.exp(ae)
        cde = jnp.exp(alast)
        w = jnp.exp(alast - ae)
        z = xdt * w

        dd_scr[...] += jnp.sum(dyv * xv, axis=0, keepdims=True)
        qm = _dot(cm, sp, _NN)
        dq = dyv * ea
        dae = dq * qm
        dc = _dot(dq, sp, _NT)
        dsp = _dot(cm, dq, _TN) + dst * cde
        dal = jnp.sum(dst * sp, axis=0, keepdims=True) * cde
        db = _dot(z, dst, _NT)
        dz = _dot(bm, dst, _NN)
        gw_ = dz * z
        dae = dae - gw_
        dal = dal + jnp.sum(gw_, axis=0, keepdims=True)
        dxdt_scr[...] = dz * w
        dcb = jnp.zeros((CHUNK, CHUNK), F32)
        dacol = jnp.zeros((CHUNK, LANES), F32)
        dact_scr[...] = jnp.zeros_like(dact_scr)
        sub = lax.broadcasted_iota(jnp.int32, (CHUNK, LANES), 0)
        for q in range(npair):
            sl = slice(q * LANES, (q + 1) * LANES)
            dec = _pair_decay(g, q, acum, act_scr, causal)
            x2, dy2 = xdt[:, sl], dyv[:, sl]
            xs_ = (jnp.where(lane < SSM_HEAD_DIM, x2, 0.0),)
            xs_ = xs_ + (x2 - xs_[0],)
            dys = (jnp.where(lane < SSM_HEAD_DIM, dy2, 0.0),)
            dys = dys + (dy2 - dys[0],)
            dx2 = jnp.zeros((CHUNK, LANES), F32)
            for hh in range(2):
                h = g * SSM_HPG + 2 * q + hh
                m = dec[hh] * cb
                dm = _dot(dys[hh], xs_[hh], _NT)
                dx2 = dx2 + _dot(m, dys[hh], _TN)
                dcb = dcb + dm * dec[hh]
                r = dm * m
                dacol = dacol + jnp.where(lane == h, jnp.sum(r, axis=1, keepdims=True), 0.0)
                dact_scr[...] -= jnp.where(sub == h, jnp.sum(r, axis=0, keepdims=True), 0.0)
            dxdt_scr[:, sl] += dx2
        dc = dc + _dot(dcb, bm, _NN)
        db = db + _dot(dcb, cm, _TN)
        dxdt = dxdt_scr[...]
        dx_ref[...] = dyv * d_ref[...] + dxdt * dte
        db_ref[...] = db
        dc_ref[...] = dc
        ddtp_ref[...] = _dot(dxdt * xv, e, _NT, exact=True)
        dal_h = _dot(jnp.broadcast_to(dal, (8, gw)), e, _NT, exact=True)
        dal_row = jnp.max(dal_h, axis=0, keepdims=True)
        dac = _dot(dae, e, _NT, exact=True) + dacol + dact_scr[...].T
        dac_ref[...] = dac + jnp.where(sub == CHUNK - 1, dal_row, 0.0)
        dst_scr[...] = dsp

        @pl.when(step == nc - 1)
        def _():
            dd_ref[...] = _dot(jnp.broadcast_to(dd_scr[...], (8, gw)), e, _NT, exact=True)

    return pl.pallas_call(
        body, name=name, grid=(groups, nc),
        in_specs=[pl.BlockSpec((CHUNK, gw), lambda g, c: (rc(g, c), g)),
                  pl.BlockSpec((CHUNK, gw), lambda g, c: (rc(g, c), g)),
                  pl.BlockSpec((CHUNK, SSM_STATE), lambda g, c: (rc(g, c), bcol + g)),
                  pl.BlockSpec((CHUNK, SSM_STATE), lambda g, c: (rc(g, c), ccol + g)),
                  pl.BlockSpec((CHUNK, LANES), lambda g, c: (rc(g, c), 0)),
                  pl.BlockSpec((CHUNK, LANES), lambda g, c: (rc(g, c), 0)),
                  pl.BlockSpec((1, gw), lambda g, c: (0, g)),
                  pl.BlockSpec((None, SSM_STATE, gw), lambda g, c: (rc(g, c), 0, g))],
        out_specs=[pl.BlockSpec((CHUNK, gw), lambda g, c: (rc(g, c), g)),
                   pl.BlockSpec((CHUNK, SSM_STATE), lambda g, c: (rc(g, c), g)),
                   pl.BlockSpec((CHUNK, SSM_STATE), lambda g, c: (rc(g, c), g)),
                   pl.BlockSpec((None, CHUNK, LANES), lambda g, c: (g, rc(g, c), 0)),
                   pl.BlockSpec((None, CHUNK, LANES), lambda g, c: (g, rc(g, c), 0)),
                   pl.BlockSpec((None, 8, LANES), lambda g, c: (g, 0, 0))],
        out_shape=[_sds((T, inner), F32), _sds((T, groups * SSM_STATE), F32), _sds((T, groups * SSM_STATE), F32),
                   _sds((groups, T, LANES), F32), _sds((groups, T, LANES), F32), _sds((groups, 8, LANES), F32)],
        scratch_shapes=[pltpu.VMEM((SSM_STATE, gw), F32), pltpu.VMEM((CHUNK, LANES), F32), pltpu.VMEM((CHUNK, gw), F32),
                        pltpu.VMEM((CHUNK, gw), F32), pltpu.VMEM((1, gw), F32), pltpu.VMEM((CHUNK, LANES), F32)],
        compiler_params=_cp("arbitrary", "arbitrary"))(dy, act, act, act, dtp, acum, dexp, sprev)


def _ssd_post_fwd(name, y, proj, ng, inner, groups):
    T = y.shape[0]
    tr = _tile(T, 128, 8)
    gs = inner // groups

    def body(y_ref, z_ref, g_ref, o_ref):
        zv = z_ref[...]
        gy = y_ref[...] * (zv * _sigmoid(zv))
        for k in range(groups):
            sl = slice(k * gs, (k + 1) * gs)
            seg = gy[:, sl]
            r = lax.rsqrt(jnp.mean(seg * seg, axis=-1, keepdims=True) + EPS)
            o_ref[:, sl] = (seg * r * g_ref[:, sl]).astype(BF16)

    return pl.pallas_call(
        body, name=name, grid=(T // tr,),
        in_specs=[pl.BlockSpec((tr, inner), _rows), pl.BlockSpec((tr, inner), _rows), pl.BlockSpec((1, inner), _row0)],
        out_specs=pl.BlockSpec((tr, inner), _rows), out_shape=_sds((T, inner), BF16),
        compiler_params=_cp("parallel"))(y, proj, ng)


def _ssd_post_bwd(name, dyn, y, proj, ng, inner, groups):
    T = y.shape[0]
    tr = _tile(T, 128, 8)
    gs = inner // groups

    def body(dyn_ref, y_ref, z_ref, g_ref, dy_ref, dz_ref, dg_ref):
        @pl.when(pl.program_id(0) == 0)
        def _():
            dg_ref[...] = jnp.zeros_like(dg_ref)

        zv = z_ref[...]
        sg = _sigmoid(zv)
        sz = zv * sg
        yv = y_ref[...]
        gy = yv * sz
        dv = dyn_ref[...]
        for k in range(groups):
            sl = slice(k * gs, (k + 1) * gs)
            seg = gy[:, sl]
            r = lax.rsqrt(jnp.mean(seg * seg, axis=-1, keepdims=True) + EPS)
            xh = seg * r
            d = dv[:, sl]
            dg_ref[:, sl] += jnp.sum(d * xh, axis=0, keepdims=True)
            dxh = d * g_ref[:, sl]
            dgy = r * (dxh - xh * jnp.mean(dxh * xh, axis=-1, keepdims=True))
            dy_ref[:, sl] = dgy * sz[:, sl]
            dz_ref[:, sl] = (dgy * yv[:, sl] * (sg[:, sl] * (1.0 + zv[:, sl] * (1.0 - sg[:, sl])))).astype(BF16)

    return pl.pallas_call(
        body, name=name, grid=(T // tr,),
        in_specs=[pl.BlockSpec((tr, inner), _rows), pl.BlockSpec((tr, inner), _rows), pl.BlockSpec((tr, inner), _rows),
                  pl.BlockSpec((1, inner), _row0)],
        out_specs=[pl.BlockSpec((tr, inner), _rows), pl.BlockSpec((tr, inner), _rows), pl.BlockSpec((1, inner), _row0)],
        out_shape=[_sds((T, inner), F32), _sds((T, inner), BF16), _sds((1, inner), F32)],
        compiler_params=_cp("arbitrary"))(dyn, y, proj, ng)


def _place():
    return lax.axis_index("x"), lax.axis_index("y"), lax.axis_index("c")


def _other_chips(x, y):
    return [(1 - x, y), (x, 1 - y), (1 - x, 1 - y)]


def _remote(src, dst, ssem, rsem, dev):
    return pltpu.make_async_remote_copy(src_ref=src, dst_ref=dst, send_sem=ssem, recv_sem=rsem, device_id=dev,
                                        device_id_type=MESH)


def _all_gather_chips(name, shard):
    R, C = shard.shape
    hr = R // 2

    def body(x_ref, o_ref, ssem, rsem, lsem):
        x, y, c = _place()
        k = 2 * x + y
        sib = (x, y, 1 - c)
        chips = _other_chips(x, y)

        def half(blk, cc):
            return o_ref.at[blk, pl.ds(cc * hr, hr), :]

        local = pltpu.make_async_copy(x_ref, o_ref.at[k], lsem)
        local.start()
        first = [_remote(x_ref.at[pl.ds(c * hr, hr), :], half(k, c), ssem.at[r], rsem.at[r], (px, py, c))
                 for r, (px, py) in enumerate(chips)]
        for cp in first:
            cp.start()
        passed = []
        for r, (px, py) in enumerate(chips):
            kj = 2 * px + py
            _remote(half(kj, c), half(kj, c), ssem.at[r], rsem.at[r], (px, py, c)).wait_recv()
            fw = _remote(half(kj, c), half(kj, c), ssem.at[3 + r], rsem.at[3 + r], sib)
            fw.start()
            passed.append(fw)
        for r, (px, py) in enumerate(chips):
            kj = 2 * px + py
            _remote(half(kj, 1 - c), half(kj, 1 - c), ssem.at[3 + r], rsem.at[3 + r], sib).wait_recv()
        for cp in first + passed:
            cp.wait_send()
        local.wait()

    return pl.pallas_call(
        body, name=name, in_specs=[_ANY], out_specs=_ANY, out_shape=_sds((4, R, C), shard.dtype),
        scratch_shapes=[pltpu.SemaphoreType.DMA((6,)), pltpu.SemaphoreType.DMA((6,)), pltpu.SemaphoreType.DMA],
    )(shard)


_HBM = pl.BlockSpec(memory_space=pltpu.HBM)
_SEM = pl.BlockSpec(memory_space=pltpu.SEMAPHORE)
_EFFECT = pltpu.SideEffectType.DATAFLOW_SIDE_EFFECTING


def _hbm(a):
    return pltpu.with_memory_space_constraint(a, pltpu.HBM)


def _cast_into_block(name, w, layer, after):
    _, R, C = w.shape
    tr = _tile(R, 256, 16)

    def body(w_ref, a_ref, o_ref):
        o_ref[...] = w_ref[...].astype(BF16)

    return pl.pallas_call(
        body, name=name, grid=(R // tr,),
        in_specs=[pl.BlockSpec((None, tr, C), lambda i: (layer, i, 0)), _ANY],
        out_specs=pl.BlockSpec((None, tr, C), lambda i: (2 * lax.axis_index("x") + lax.axis_index("y"), i, 0)),
        out_shape=_sds((4, R, C), BF16), compiler_params=_cp("parallel"))(w, after)


def _gather_copies(refs, ssem, rsem):
    x, y, c = _place()
    k = 2 * x + y
    sends, arrivals = [], []
    for a, lr in enumerate(refs):
        hr = lr.shape[1] // 2
        mine = lr.at[k, pl.ds(c * hr, hr), :]
        for r, (px, py) in enumerate(_other_chips(x, y)):
            i = 3 * a + r
            sends.append(_remote(mine, mine, ssem.at[i], rsem.at[i], (px, py, c)))
            arrivals.append(_remote(mine, lr.at[2 * px + py, pl.ds(c * hr, hr), :], ssem.at[i], rsem.at[i], (px, py, c)))
    return sends, arrivals


def _swap_copies(refs, ssem, rsem):
    x, y, c = _place()
    n = len(refs) // 2
    sends = []
    for a, (g, o) in enumerate(zip(refs[:n], refs[n:])):
        hr = g.shape[1] // 2
        sends.append(_remote(g.at[:, pl.ds((1 - c) * hr, hr), :], o, ssem.at[a], rsem.at[a], (x, y, 1 - c)))
    return sends, sends


def _scatter_copies(refs, ssem, rsem):
    x, y, c = _place()
    k = 2 * x + y
    n = len(refs) // 2
    sends, arrivals = [], []
    for a, (xr, lr) in enumerate(zip(refs[:n], refs[n:])):
        for r, (px, py) in enumerate(_other_chips(x, y)):
            i = 3 * a + r
            kj = 2 * px + py
            sends.append(_remote(xr.at[kj], lr.at[k], ssem.at[i], rsem.at[i], (px, py, c)))
            arrivals.append(_remote(xr.at[kj], lr.at[kj], ssem.at[i], rsem.at[i], (px, py, c)))
    return sends, arrivals


def _split_start(name, bufs, ncopies, copies_fn):
    n = len(bufs)

    def body(*refs):
        sends, _ = copies_fn(refs[:n], refs[n], refs[n + 1])
        for cp in sends:
            cp.start()
        token = refs[-1]
        token[...] = jnp.zeros_like(token)

    outs = pl.pallas_call(
        body, name=name, in_specs=[_HBM] * n,
        out_specs=(_SEM, _SEM) + (_HBM,) * n + (pl.BlockSpec(memory_space=pltpu.VMEM),),
        out_shape=(pltpu.SemaphoreType.DMA((ncopies,)), pltpu.SemaphoreType.DMA((ncopies,)))
        + tuple(pltpu.HBM(b.shape, b.dtype) for b in bufs) + (_sds((8, LANES), F32),),
        input_output_aliases={i: 2 + i for i in range(n)},
        compiler_params=pltpu.CompilerParams(has_side_effects=_EFFECT),
    )(*[_hbm(b) for b in bufs])
    return outs[0], outs[1], list(outs[2:2 + n]), outs[-1]


def _split_wait(name, started, copies_fn, after):
    ssem, rsem, bufs, _ = started
    n = len(bufs)

    def body(*refs):
        sends, arrivals = copies_fn(refs[:n], refs[n], refs[n + 1])
        for cp in sends:
            cp.wait_send()
        for cp in arrivals:
            cp.wait_recv()

    outs = pl.pallas_call(
        body, name=name, in_specs=[_HBM] * n + [_SEM, _SEM, _ANY], out_specs=(_HBM,) * n,
        out_shape=tuple(pltpu.HBM(t.shape, t.dtype) for t in bufs),
        input_output_aliases={i: i for i in range(n)},
        compiler_params=pltpu.CompilerParams(has_side_effects=_EFFECT),
    )(*bufs, ssem, rsem, after)
    return list(outs)


def _share_sibling(name, lands):
    n = len(lands)

    def body(*refs):
        l_in, l_out = refs[:n], refs[n:2 * n]
        ssem, rsem = refs[2 * n:]
        x, y, c = _place()
        sib = (x, y, 1 - c)

        def rows(ref, a, px, py, cc):
            hr = ref[a].shape[1] // 2
            return ref[a].at[2 * px + py, pl.ds(cc * hr, hr), :]

        copies = []
        for a in range(n):
            for r, (px, py) in enumerate(_other_chips(x, y)):
                cp = _remote(rows(l_in, a, px, py, c), rows(l_out, a, px, py, c), ssem.at[3 * a + r], rsem.at[3 * a + r], sib)
                cp.start()
                copies.append(cp)
        for a in range(n):
            for r, (px, py) in enumerate(_other_chips(x, y)):
                theirs = rows(l_out, a, px, py, 1 - c)
                _remote(theirs, theirs, ssem.at[3 * a + r], rsem.at[3 * a + r], sib).wait_recv()
        for cp in copies:
            cp.wait_send()

    outs = pl.pallas_call(
        body, name=name, in_specs=[_ANY] * n, out_specs=[_ANY] * n,
        out_shape=[_sds(l.shape, l.dtype) for l in lands], input_output_aliases={i: i for i in range(n)},
        scratch_shapes=[pltpu.SemaphoreType.DMA((3 * n,)), pltpu.SemaphoreType.DMA((3 * n,))],
    )(*lands)
    return list(outs)


def _add_halves(name, full, other):
    nb, R, C = full.shape
    hr = R // 2
    tr = _tile(hr, 256, 16)
    nh = hr // tr

    def body(f_ref, o_ref, s_ref):
        s_ref[...] = (f_ref[...].astype(F32) + o_ref[...].astype(F32)).astype(s_ref.dtype)

    return pl.pallas_call(
        body, name=name, grid=(nb, nh),
        in_specs=[pl.BlockSpec((None, tr, C), lambda b, i: (b, lax.axis_index("c") * nh + i, 0)),
                  pl.BlockSpec((None, tr, C), lambda b, i: (b, i, 0))],
        out_specs=pl.BlockSpec((None, tr, C), lambda b, i: (b, i, 0)), out_shape=_sds((nb, hr, C), full.dtype),
        compiler_params=_cp("parallel", "parallel"))(full, other)


def _sum_owner(name, land, mine):
    nb, hr, C = land.shape
    tr = _tile(hr, 128, 16)

    def body(l_ref, m_ref, o_ref):
        k = 2 * lax.axis_index("x") + lax.axis_index("y")
        own = m_ref[...].astype(F32)
        acc = jnp.where(k == 0, own, l_ref[0].astype(F32))
        for j in range(1, nb):
            acc = acc + jnp.where(k == j, own, l_ref[j].astype(F32))
        o_ref[...] = acc

    return pl.pallas_call(
        body, name=name, grid=(hr // tr,),
        in_specs=[pl.BlockSpec((nb, tr, C), lambda i: (0, i, 0)),
                  pl.BlockSpec((None, tr, C), lambda i: (2 * lax.axis_index("x") + lax.axis_index("y"), i, 0))],
        out_specs=pl.BlockSpec((None, tr, C), lambda i: (lax.axis_index("c"), i, 0)),
        out_shape=_sds((2, hr, C), F32), compiler_params=_cp("parallel"))(land, mine)


def _join_copies(refs, ssem, rsem):
    x, y, c = _place()
    sends, arrivals = [], []
    for a, b in enumerate(refs):
        sends.append(_remote(b.at[c], b.at[c], ssem.at[a], rsem.at[a], (x, y, 1 - c)))
        arrivals.append(_remote(b.at[c], b.at[1 - c], ssem.at[a], rsem.at[a], (x, y, 1 - c)))
    return sends, arrivals


def _all_gather_devices(name, buf):
    def body(b_ref, o_ref, ssem, rsem):
        x, y, c = _place()
        me = 4 * x + 2 * y + c
        flips = [(fx, fy, fc) for fx in (0, 1) for fy in (0, 1) for fc in (0, 1) if fx or fy or fc]
        peers = [((1 - x) if fx else x, (1 - y) if fy else y, (1 - c) if fc else c) for fx, fy, fc in flips]
        sends = [_remote(b_ref, o_ref.at[me], ssem.at[r], rsem.at[r], p) for r, p in enumerate(peers)]
        for cp in sends:
            cp.start()
        for r, (px, py, pc) in enumerate(peers):
            pid = 4 * px + 2 * py + pc
            _remote(b_ref, o_ref.at[pid], ssem.at[r], rsem.at[r], (px, py, pc)).wait_recv()
        for cp in sends:
            cp.wait_send()

    return pl.pallas_call(
        body, name=name, in_specs=[_ANY], out_specs=_ANY, out_shape=_sds((8,) + buf.shape, buf.dtype),
        scratch_shapes=[pltpu.SemaphoreType.DMA((7,)), pltpu.SemaphoreType.DMA((7,))],
    )(buf)


def _sum_devices(name, parts, own):
    n, R, C = parts.shape
    tr = _tile(R, 256, 8)

    def body(p_ref, own_ref, o_ref):
        x, y, c = _place()
        me = 4 * x + 2 * y + c
        mine = own_ref[...]
        acc = jnp.where(me == 0, mine, p_ref[0])
        for j in range(1, n):
            acc = acc + jnp.where(me == j, mine, p_ref[j])
        o_ref[...] = acc

    return pl.pallas_call(
        body, name=name, grid=(R // tr,),
        in_specs=[pl.BlockSpec((n, tr, C), lambda i: (0, i, 0)), pl.BlockSpec((tr, C), _rows)],
        out_specs=pl.BlockSpec((tr, C), _rows), out_shape=_sds((R, C), F32), compiler_params=_cp("parallel"))(parts, own)


def _adamw(name, w, m, v, parts, tok=None):
    L, R, C = w.shape
    np_ = len(parts[0])
    tr = _tile(R, max(8, (3 * VMEM_LIMIT // 4) // (2 * 4 * C * (7 + L * np_))), 8)
    flat = [p for lp in parts for p in lp]
    if tok is not None:
        flat = flat + [tok]
    c1 = 1.0 / (1.0 - ADAM_B1 ** ADAM_STEP)
    c2 = 1.0 / (1.0 - ADAM_B2 ** ADAM_STEP)

    def body(*refs):
        w_ref, m_ref, v_ref = refs[:3]
        p_refs = refs[3:3 + L * np_]
        g_ref, d_ref, nm_ref, nv_ref = refs[-4:]
        layer = pl.program_id(0)
        g = jnp.zeros((tr, C), F32)
        for l in range(L):
            gl = p_refs[l * np_][...]
            for j in range(1, np_):
                gl = gl + p_refs[l * np_ + j][...]
            g = jnp.where(layer == l, gl, g) if L > 1 else gl
        if tok is not None:
            g = g + refs[3 + L * np_][...]
        nm = ADAM_B1 * m_ref[...] + (1.0 - ADAM_B1) * g
        nv = ADAM_B2 * v_ref[...] + (1.0 - ADAM_B2) * (g * g)
        g_ref[...] = g
        nm_ref[...] = nm
        nv_ref[...] = nv
        d_ref[...] = -ADAM_LR * ((nm * c1) / (jnp.sqrt(nv * c2) + ADAM_EPS) + ADAM_WD * w_ref[...])

    stacked = pl.BlockSpec((None, tr, C), lambda l, i: (l, i, 0))
    part_specs = [pl.BlockSpec((tr, C), (lambda l, i, ll=ll: (jnp.where(l == ll, i, 0), 0)))
                  for ll in range(L) for _ in range(np_)]
    if tok is not None:
        part_specs.append(pl.BlockSpec((1, 1), lambda l, i: (0, 0)))
    return pl.pallas_call(
        body, name=name, grid=(L, R // tr), in_specs=[stacked] * 3 + part_specs, out_specs=[stacked] * 4,
        out_shape=[_sds(w.shape, F32)] * 4, compiler_params=_cp("arbitrary", "arbitrary"))(w, m, v, *flat)


_PACK_ROWS = 16


def _pack(arrs):
    pieces = []
    for a in arrs:
        f = a.reshape(-1).astype(F32)
        unit = _PACK_ROWS * LANES
        pad = (-f.shape[0]) % unit
        pieces.append(jnp.pad(f, (0, pad)))
    return jnp.concatenate(pieces).reshape(-1, LANES)


def _unpack(buf, shapes):
    flat = buf.reshape(-1)
    out, off = [], 0
    unit = _PACK_ROWS * LANES
    for s in shapes:
        n = 1
        for d in s:
            n *= d
        out.append(flat[off:off + n].reshape(s))
        off += n + ((-n) % unit)
    return out


def _swap_start(tag, full):
    nb, R, C = full.shape
    return _split_start("sw_start_" + tag, [full, lax.empty((nb, R // 2, C), full.dtype)], 1, _swap_copies)


def _reduce_start(tag, swaps, after):
    pairs = [_split_wait("sw_wait_%s_%d" % (tag, i), st, _swap_copies, after) for i, st in enumerate(swaps)]
    sums = [_add_halves("rs_add", f, o) for f, o in pairs]
    lands = [lax.empty(t.shape, t.dtype) for t in sums]
    return _split_start("rs_start_" + tag, sums + lands, 3 * len(sums), _scatter_copies)


def _reduce_finish(tag, started, after):
    bufs = _split_wait("rs_wait_" + tag, started, _scatter_copies, after)
    n = len(bufs) // 2
    halves = [_sum_owner("rs_sum", l, s) for s, l in zip(bufs[:n], bufs[n:])]
    return _split_start("jn_start_" + tag, halves, len(halves), _join_copies)


def _join_finish(tag, started, after):
    joined = _split_wait("jn_wait_" + tag, started, _join_copies, after)
    return [j.reshape(2 * j.shape[1], j.shape[2]) for j in joined]


def kernel(x, ln_ffn_pre, ffn_pre_w_in, ffn_pre_w_out, ln_mix, ln_ffn_post, ffn_post_w_in, ffn_post_w_out, gm_w_in, gm_v_norm, gm_w_s, gm_b_s, gm_w_out, ssm_w_in, ssm_conv_w, ssm_conv_b, ssm_dt_bias, ssm_a_log, ssm_d, ssm_norm, ssm_w_out, ln_final, loss_target, m_ln_ffn_pre, m_ffn_pre_w_in, m_ffn_pre_w_out, m_ln_mix, m_ln_ffn_post, m_ffn_post_w_in, m_ffn_post_w_out, m_gm_w_in, m_gm_v_norm, m_gm_w_s, m_gm_b_s, m_gm_w_out, m_ssm_w_in, m_ssm_conv_w, m_ssm_conv_b, m_ssm_dt_bias, m_ssm_a_log, m_ssm_d, m_ssm_norm, m_ssm_w_out, m_ln_final, v_ln_ffn_pre, v_ffn_pre_w_in, v_ffn_pre_w_out, v_ln_mix, v_ln_ffn_post, v_ffn_post_w_in, v_ffn_post_w_out, v_gm_w_in, v_gm_v_norm, v_gm_w_s, v_gm_b_s, v_gm_w_out, v_ssm_w_in, v_ssm_conv_w, v_ssm_conv_b, v_ssm_dt_bias, v_ssm_a_log, v_ssm_d, v_ssm_norm, v_ssm_w_out, v_ln_final):
    T, D = x.shape[1], x.shape[2]
    depth = ln_ffn_pre.shape[0]
    n_gm, n_ssm = gm_w_in.shape[0], ssm_w_in.shape[0]
    F = ffn_pre_w_out.shape[1] * 4
    GI = gm_w_out.shape[1] * 4
    GG = gm_w_s.shape[1]
    inner = ssm_w_out.shape[1] * 4
    heads = ssm_dt_bias.shape[1]
    cd = ssm_conv_w.shape[1] * 4
    groups = (cd - inner) // (2 * SSM_STATE)
    pshard = ssm_w_in.shape[2]
    pw = inner + cd + LANES
    dtcol = (inner + cd) // LANES
    kchip = 2 * lax.axis_index("x") + lax.axis_index("y")

    small_shard = _pack([jnp.swapaxes(ssm_conv_w, 1, 2), ssm_conv_b, ssm_norm])
    small_all = _all_gather_chips("ag_small", small_shard)
    cq, iq = cd // 4, inner // 4
    parts = [_unpack(small_all[k], [(n_ssm, SSM_CONV, cq), (n_ssm, cq), (n_ssm, iq)]) for k in range(4)]
    conv_wt = jnp.concatenate([p[0] for p in parts], axis=2)
    conv_b = jnp.concatenate([p[1] for p in parts], axis=1)
    norm_g = jnp.concatenate([p[2] for p in parts], axis=1)

    def pad_lanes(v):
        return jnp.pad(v, (0, LANES - v.shape[0]))[None, :]

    xc = x[0]
    saved = []

    def shards_of(i, kind):
        if kind == "pre":
            return [(ffn_pre_w_in, i), (ffn_pre_w_out, i)]
        if kind == "post":
            return [(ffn_post_w_in, i), (ffn_post_w_out, i)]
        if i % 2 == 0:
            return [(gm_w_in, i // 2), (gm_w_out, i // 2)]
        return [(ssm_w_in, i // 2), (ssm_w_out, i // 2)]

    subs = [(i, kind) for i in range(depth) for kind in ("pre", "mix", "post")]
    nsub = len(subs)
    gathers = [None] * nsub
    ahead = 2

    def gather_start(s, after):
        lands = [_cast_into_block("ag_cast", w, l, after) for w, l in shards_of(*subs[s])]
        gathers[s] = _split_start("ag_start_%d" % s, lands, 3 * len(lands), _gather_copies)

    def gathered(s, after):
        lands = _split_wait("ag_wait_%d" % s, gathers[s], _gather_copies, after)
        full = _share_sibling("ag_share", lands)
        tok = jnp.zeros((1, 1), F32)
        depth_now = 1 if s == 0 else ahead
        for nxt in range(s + 1, min(s + depth_now, nsub - 1) + 1):
            if gathers[nxt] is None:
                gather_start(nxt, full[0])
                tok = tok + gathers[nxt][3][0:1, 0:1]
        return full, tok

    gather_start(0, small_all)

    def ffn_fwd(s, xin, g, l):
        (wi, wo), tok = gathered(s, xin)
        wo = wo.reshape(F, D)
        h = _rms_fwd("rms_fwd", xin, g[l][None, :] + tok)
        a = _matmul("ffn_in", h, wi, "nn", BF16, out_blocks=2)
        s_ = _swiglu_fwd("swiglu_fwd", a)
        xo = _matmul("ffn_out", s_, wo, "nn", F32, res=xin, scale=0.5)
        return xo, (xin, h, a, s_, wi, wo)

    for i in range(depth):
        xc, sv_pre = ffn_fwd(3 * i, xc, ln_ffn_pre, i)
        j = i // 2
        (wi, wo), tok = gathered(3 * i + 1, xc)
        h = _rms_fwd("rms_fwd", xc, ln_mix[i][None, :] + tok)
        if i % 2 == 0:
            wo = wo.reshape(GI, D)
            zpre = _matmul("gm_in", h, wi, "nn", BF16, out_blocks=2)
            u, vn = _gm_act_fwd("gm_act_fwd", zpre, gm_v_norm[j][None, :])
            bt = jnp.pad(gm_b_s[j].T, ((0, 0), (0, LANES - GG)))
            gated = _gm_spatial_fwd("gm_spatial_fwd", u, vn, gm_w_s[j], bt)
            xn = _matmul("mix_out", gated, wo, "nn", F32, res=xc, scale=1.0)
            sv_mix = (xc, h, zpre, u, vn, bt, gated, wi, wo)
        else:
            wg, wo = wi, wo.reshape(inner, D)
            wp = jnp.pad(jnp.swapaxes(wg, 0, 1).reshape(D, 4 * pshard), ((0, 0), (0, pw - 4 * pshard)))
            proj = _matmul("ssm_in", h, wp, "nn", F32)
            wt, cb_ = conv_wt[j], conv_b[j][None, :]
            act = _ssd_conv_fwd("ssd_conv_fwd", proj, wt, cb_, inner, cd)
            bias, alog = pad_lanes(ssm_dt_bias[j]), pad_lanes(ssm_a_log[j])
            dtp, acum = _ssd_dt_fwd("ssd_dt_fwd", proj, bias, alog, dtcol, heads)
            dexp = jnp.repeat(ssm_d[j], SSM_HEAD_DIM)[None, :]
            ycore, sprev = _ssd_core_fwd("ssd_core_fwd", act, dtp, acum, dexp, inner, groups)
            ng = norm_g[j][None, :]
            yn = _ssd_post_fwd("ssd_post_fwd", ycore, proj, ng, inner, groups)
            xn = _matmul("mix_out", yn, wo, "nn", F32, res=xc, scale=1.0)
            sv_mix = (xc, h, proj, act, dtp, acum, dexp, ycore, sprev, yn, wt, cb_, bias, alog, ng, wp, wo)
        xc = xn
        xc, sv_post = ffn_fwd(3 * i + 2, xc, ln_ffn_post, i)
        saved.append((sv_pre, sv_mix, sv_post))

    loss_tile, dx, dxb, dg_final = _loss_head("loss_head", xc, ln_final[None, :], loss_target[0])
    loss = lax.psum(loss_tile[0, 0], ("x", "y", "c"))

    wgrad = [None] * nsub
    in_flight = []
    deferred = []

    def reduce_finish(after):
        tok = jnp.zeros((1, 1), F32)
        while in_flight:
            s, st = in_flight.pop(0)
            wgrad[s] = _reduce_finish(str(s), st, after)
            tok = tok + wgrad[s][3][0:1, 0:1]
        return tok

    def swap_start(s, which, full):
        st = _swap_start("%d_%s" % (s, which), full)
        return st, st[3][0:1, 0:1]

    def reduce_start(s, swaps, after):
        tok = reduce_finish(after)
        if s == 0:
            deferred.append(swaps)
            return tok
        st = _reduce_start(str(s), swaps, after)
        in_flight.append((s, st))
        return tok + st[3][0:1, 0:1]

    def ffn_bwd(s, dx, dxb, sv, g, l):
        xin, h, a, s_, wi, wo = sv
        ds = _matmul("ffn_ds", dxb, wo, "nt", BF16, scale=0.5)
        dwo = _matmul("ffn_dwo", s_, dxb, "tn", BF16, scale=0.5)
        sw_o, tok_o = swap_start(s, "o", dwo.reshape(4, F // 4, D))
        da = _swiglu_bwd("swiglu_bwd", a, ds)
        dwi = _matmul("ffn_dwi", h, da, "tn", BF16, out_blocks=4, tok=tok_o)
        sw_i, tok_i = swap_start(s, "i", dwi)
        dh = _matmul("ffn_dh", da, wi, "nt", F32, tok=tok_i)
        tok = reduce_start(s, [sw_i, sw_o], dh)
        return _rms_bwd("rms_bwd", xin, g[l][None, :] + tok, dh, dx)

    gl = {n: [None] * depth for n in ("pre", "mix", "post")}
    gm_g = {n: [None] * n_gm for n in ("vnorm", "ws", "bs")}
    ssm_g = {n: [None] * n_ssm for n in ("convw", "convb", "dtb", "alog", "d", "norm")}

    for i in reversed(range(depth)):
        sv_pre, sv_mix, sv_post = saved[i]
        j = i // 2
        dx, dxb, gl["post"][i] = ffn_bwd(3 * i + 2, dx, dxb, sv_post, ln_ffn_post, i)
        if i % 2 == 0:
            xin, h, zpre, u, vn, bt, gated, wi, wo = sv_mix
            dgated = _matmul("mix_dy", dxb, wo, "nt", BF16)
            dwo = _matmul("mix_dwo", gated, dxb, "tn", BF16)
            sw_o, tok_o = swap_start(3 * i + 1, "o", dwo.reshape(4, GI // 4, D))
            du, dvn, dws, dbt = _gm_spatial_bwd("gm_spatial_bwd", dgated, u, vn, gm_w_s[j], bt)
            dzpre, dvnorm = _gm_act_bwd("gm_act_bwd", zpre, du, dvn, gm_v_norm[j][None, :])
            dwi = _matmul("gm_dwi", h, dzpre, "tn", BF16, out_blocks=4, tok=tok_o)
            sw_i, tok_i = swap_start(3 * i + 1, "i", dwi)
            dh = _matmul("gm_dh", dzpre, wi, "nt", F32, tok=tok_i)
            tok = reduce_start(3 * i + 1, [sw_i, sw_o], dh)
            gm_g["vnorm"][j], gm_g["ws"][j], gm_g["bs"][j] = dvnorm[0], dws, dbt.T[:GG]
        else:
            xin, h, proj, act, dtp, acum, dexp, ycore, sprev, yn, wt, cb_, bias, alog, ng, wp, wo = sv_mix
            dyn = _matmul("ssm_dy", dxb, wo, "nt", F32)
            dwo = _matmul("mix_dwo", yn, dxb, "tn", BF16)
            sw_o, tok_o = swap_start(3 * i + 1, "o", dwo.reshape(4, inner // 4, D))
            dyc, dz, dnorm = _ssd_post_bwd("ssd_post_bwd", dyn, ycore, proj, ng, inner, groups)
            dxs, db_, dc_, ddtp_g, dac_g, dd_g = _ssd_core_bwd("ssd_core_bwd", dyc, act, dtp, acum, dexp, sprev,
                                                               inner, groups)
            ddt, dbias, dalog, dds = _ssd_dt_bwd("ssd_dt_bwd", ddtp_g, dac_g, dd_g, proj, bias, alog, dtp, dtcol, heads)
            dact = jnp.concatenate([dxs, db_, dc_], axis=1)
            dxbc, dwt, dcb = _ssd_conv_bwd("ssd_conv_bwd", dact, proj, wt, cb_, inner, cd)
            dproj = jnp.concatenate([dz, dxbc, ddt], axis=1)
            dwp = _matmul("ssm_dwi", h, dproj, "tn", BF16, tok=tok_o)
            dwi = jnp.swapaxes(dwp[:, :4 * pshard].reshape(D, 4, pshard), 0, 1)
            sw_i, tok_i = swap_start(3 * i + 1, "i", dwi)
            dh = _matmul("ssm_dh", dproj, wp, "nt", F32, tok=tok_i)
            tok = reduce_start(3 * i + 1, [sw_i, sw_o], dh)
            ssm_g["convw"][j], ssm_g["convb"][j] = dwt.T, dcb[0]
            ssm_g["dtb"][j], ssm_g["alog"][j], ssm_g["d"][j] = dbias[0, :heads], dalog[0, :heads], dds[0, :heads]
            ssm_g["norm"][j] = dnorm[0]
        dx, dxb, gl["mix"][i] = _rms_bwd("rms_bwd", xin, ln_mix[i][None, :] + tok, dh, dx)
        dx, dxb, gl["pre"][i] = ffn_bwd(3 * i, dx, dxb, sv_pre, ln_ffn_pre, i)

    small_full = [
        jnp.concatenate(gl["pre"], 0), jnp.concatenate(gl["mix"], 0), jnp.concatenate(gl["post"], 0),
        jnp.stack(gm_g["vnorm"]), jnp.stack(gm_g["ws"]), jnp.stack(gm_g["bs"]),
        jnp.stack(ssm_g["convw"]), jnp.stack(ssm_g["convb"]), jnp.stack(ssm_g["dtb"]), jnp.stack(ssm_g["alog"]),
        jnp.stack(ssm_g["d"]), jnp.stack(ssm_g["norm"]), dg_final[0],
    ]
    full_shapes = [tuple(a.shape) for a in small_full]
    packed = _pack(small_full)
    summed = _sum_devices("sum8_small", _all_gather_devices("ag8_small", packed), packed)
    (g_pre, g_mix, g_post, g_vn, g_ws, g_bs, g_cw, g_cb, g_dtb, g_al, g_d, g_nm, g_fin) = _unpack(summed, full_shapes)
    g_cw = lax.dynamic_slice_in_dim(g_cw, kchip * cq, cq, axis=1)
    g_cb = lax.dynamic_slice_in_dim(g_cb, kchip * cq, cq, axis=1)
    g_nm = lax.dynamic_slice_in_dim(g_nm, kchip * iq, iq, axis=1)
    sm_g = [g_pre, g_mix, g_post, g_vn, g_ws, g_bs, g_cw, g_cb, g_dtb, g_al, g_d, g_nm, g_fin]
    sm_w = [ln_ffn_pre, ln_mix, ln_ffn_post, gm_v_norm, gm_w_s, gm_b_s, ssm_conv_w, ssm_conv_b, ssm_dt_bias,
            ssm_a_log, ssm_d, ssm_norm, ln_final]
    sm_m = [m_ln_ffn_pre, m_ln_mix, m_ln_ffn_post, m_gm_v_norm, m_gm_w_s, m_gm_b_s, m_ssm_conv_w, m_ssm_conv_b,
            m_ssm_dt_bias, m_ssm_a_log, m_ssm_d, m_ssm_norm, m_ln_final]
    sm_v = [v_ln_ffn_pre, v_ln_mix, v_ln_ffn_post, v_gm_v_norm, v_gm_w_s, v_gm_b_s, v_ssm_conv_w, v_ssm_conv_b,
            v_ssm_dt_bias, v_ssm_a_log, v_ssm_d, v_ssm_norm, v_ln_final]
    sm_shapes = [tuple(a.shape) for a in sm_w]
    pk = [_pack(lst)[None] for lst in (sm_w, sm_m, sm_v)]
    sg_, sd_, snm_, snv_ = _adamw("adamw_small", pk[0], pk[1], pk[2], [[_pack(sm_g)]])
    small_out = [_unpack(t[0], sm_shapes) for t in (sg_, sd_, snm_, snv_)]
    small_names = ["ln_ffn_pre", "ln_mix", "ln_ffn_post", "gm_v_norm", "gm_w_s", "gm_b_s", "ssm_conv_w", "ssm_conv_b",
                   "ssm_dt_bias", "ssm_a_log", "ssm_d", "ssm_norm", "ln_final"]

    def big_update(tag, w, m, v, sub_ids, which, tok=None):
        L = w.shape[0]
        shp = w.shape
        w2, m2, v2 = (t.reshape(L, -1, shp[-1]) for t in (w, m, v))
        outs = _adamw("adamw_" + tag, w2, m2, v2, [[wfull[s][which]] for s in sub_ids], tok)
        return [o.reshape(shp) for o in outs]

    last = _reduce_start("0", deferred[0], sd_)
    tok = last[3][0:1, 0:1]
    wfull = [None] * nsub
    for s in range(1, nsub):
        wfull[s] = _join_finish(str(s), wgrad[s], sd_)
    pre_ids = [3 * i for i in range(depth)]
    post_ids = [3 * i + 2 for i in range(depth)]
    gm_ids = [3 * i + 1 for i in range(depth) if i % 2 == 0]
    ssm_ids = [3 * i + 1 for i in range(depth) if i % 2 == 1]
    big_out = {
        "ffn_post_w_in": big_update("ffn_in", ffn_post_w_in, m_ffn_post_w_in, v_ffn_post_w_in, post_ids, 0, tok),
        "ffn_post_w_out": big_update("ffn_out", ffn_post_w_out, m_ffn_post_w_out, v_ffn_post_w_out, post_ids, 1, tok),
        "gm_w_in": big_update("gm_in", gm_w_in, m_gm_w_in, v_gm_w_in, gm_ids, 0, tok),
        "gm_w_out": big_update("mix_out", gm_w_out, m_gm_w_out, v_gm_w_out, gm_ids, 1, tok),
        "ssm_w_in": big_update("ssm_in", ssm_w_in, m_ssm_w_in, v_ssm_w_in, ssm_ids, 0, tok),
        "ssm_w_out": big_update("mix_out", ssm_w_out, m_ssm_w_out, v_ssm_w_out, ssm_ids, 1, tok),
    }
    after = big_out["ffn_post_w_in"][1][0, 0:1, 0:1]
    for n in ("ffn_post_w_out", "gm_w_in", "gm_w_out", "ssm_w_in", "ssm_w_out"):
        after = after + big_out[n][1][0, 0:1, 0:1]
    wfull[0] = _join_finish("0", _reduce_finish("0", last, after), after)
    big_out["ffn_pre_w_in"] = big_update("ffn_in", ffn_pre_w_in, m_ffn_pre_w_in, v_ffn_pre_w_in, pre_ids, 0)
    big_out["ffn_pre_w_out"] = big_update("ffn_out", ffn_pre_w_out, m_ffn_pre_w_out, v_ffn_pre_w_out, pre_ids, 1)

    order = ["ln_ffn_pre", "ffn_pre_w_in", "ffn_pre_w_out", "ln_mix", "ln_ffn_post", "ffn_post_w_in", "ffn_post_w_out",
             "gm_w_in", "gm_v_norm", "gm_w_s", "gm_b_s", "gm_w_out", "ssm_w_in", "ssm_conv_w", "ssm_conv_b",
             "ssm_dt_bias", "ssm_a_log", "ssm_d", "ssm_norm", "ssm_w_out", "ln_final"]

    def pick(kind, n):
        if n in big_out:
            return big_out[n][kind]
        return small_out[kind][small_names.index(n)]

    outs = [loss, dx[None]]
    for kind in range(4):
        outs.extend(pick(kind, n) for n in order)
    return tuple(outs)
```

```python
import jax
import jax.numpy as jnp
from jax import lax
from jax.experimental import pallas as pl
from jax.experimental.pallas import tpu as pltpu

F32 = jnp.float32
BF16 = jnp.bfloat16
HIGHEST = lax.Precision.HIGHEST
MESH = pl.DeviceIdType.MESH

EPS = 1e-6
ADAM_LR, ADAM_B1, ADAM_B2, ADAM_EPS, ADAM_WD, ADAM_STEP = 0.001, 0.9, 0.999, 1e-08, 0.01, 10

LANES = 128
CHUNK = 128
SSM_STATE = 128
SSM_HEAD_DIM = 64
SSM_HPG = 8
SSM_CONV = 4
VMEM_LIMIT = 56 * 1024 * 1024
MM_TILE = 1408
MM_VMEM_BUDGET = 46 * 1024 * 1024

_ANY = pl.BlockSpec(memory_space=pl.ANY)


def _cp(*sem):
    return pltpu.CompilerParams(dimension_semantics=sem if sem else None, vmem_limit_bytes=VMEM_LIMIT)


def _tile(n, target, mult=LANES):
    best = None
    t = mult
    while t <= min(n, target):
        if n % t == 0:
            best = t
        t += mult
    return n if best is None else best


def _gcd(*v):
    import math
    g = 0
    for a in v:
        g = math.gcd(g, a)
    return g


def _sds(shape, dtype):
    return jax.ShapeDtypeStruct(tuple(shape), dtype)


def _ldims(shape):
    return tuple(shape) if len(shape) == 2 else (shape[1], shape[0] * shape[2])


def _colblock(shape):
    return None if len(shape) == 2 else shape[2]


def _mspec(shape, tr, tc, rc):
    if len(shape) == 2:
        return pl.BlockSpec((tr, tc), rc)
    per = shape[2] // tc

    def im(i, j, k):
        r, c = rc(i, j, k)
        return (c // per, r, c % per)

    return pl.BlockSpec((None, tr, tc), im)


def _matmul(name, a, b, mode, out_dtype, out_blocks=None, res=None, scale=1.0, tok=None):
    la, lb = _ldims(a.shape), _ldims(b.shape)
    if mode == "nn":
        (M, K), (K2, N) = la, lb
    elif mode == "nt":
        (M, K), (N, K2) = la, lb
    else:
        (K, M), (K2, N) = la, lb
    assert K == K2, (name, a.shape, b.shape, mode)
    ca, cb = _colblock(a.shape), _colblock(b.shape)
    out_shape = (M, N) if out_blocks is None else (out_blocks, M, N // out_blocks)
    co = _colblock(out_shape)
    m_c, n_c, k_c = [M], [N], [K]
    if ca is not None:
        (m_c if mode == "tn" else k_c).append(ca)
    if cb is not None:
        (k_c if mode == "nt" else n_c).append(cb)
    if co is not None:
        n_c.append(co)
    tm, tn = _tile(_gcd(*m_c), MM_TILE), _tile(_gcd(*n_c), MM_TILE)
    out_bytes = jnp.dtype(out_dtype).itemsize + (4 if res is not None else 0)
    kg = _gcd(*k_c)
    tk = LANES if kg % LANES == 0 else kg
    for cand in range(LANES, kg + 1, LANES):
        if kg % cand == 0 and 2 * (2 * cand * (tm + tn) + tm * tn * out_bytes) + 2 * 4 * tm * tn <= MM_VMEM_BUDGET:
            tk = cand
    nk = K // tk
    if mode == "tn":
        a_spec = _mspec(a.shape, tk, tm, lambda i, j, k: (k, i))
        dims = (((0,), (0,)), ((), ()))
    else:
        a_spec = _mspec(a.shape, tm, tk, lambda i, j, k: (i, k))
        dims = (((1,), (1,)), ((), ())) if mode == "nt" else (((1,), (0,)), ((), ()))
    if mode == "nt":
        b_spec = _mspec(b.shape, tn, tk, lambda i, j, k: (j, k))
    else:
        b_spec = _mspec(b.shape, tk, tn, lambda i, j, k: (k, j))
    o_spec = _mspec(out_shape, tm, tn, lambda i, j, k: (i, j))
    in_specs, args = [a_spec, b_spec], [a, b]
    if res is not None:
        in_specs.append(pl.BlockSpec((tm, tn), lambda i, j, k: (i, j)))
        args.append(res)
    if tok is not None:
        in_specs.append(pl.BlockSpec((1, 1), lambda i, j, k: (0, 0)))
        args.append(tok)
    n_in = len(args)

    def body(*refs):
        a_ref, b_ref = refs[0], refs[1]
        res_ref = refs[2] if res is not None else None
        tok_ref = refs[n_in - 1] if tok is not None else None
        o_ref = refs[n_in]

        def product():
            return lax.dot_general(a_ref[...].astype(BF16), b_ref[...].astype(BF16), dims, preferred_element_type=F32)

        def finish(r):
            if scale != 1.0:
                r = r * scale
            if res_ref is not None:
                r = res_ref[...] + r
            if tok_ref is not None:
                r = r + tok_ref[...]
            o_ref[...] = r.astype(o_ref.dtype)

        if nk == 1:
            finish(product())
            return
        acc_ref = refs[-1]
        kk = pl.program_id(2)

        @pl.when(kk == 0)
        def _():
            acc_ref[...] = product()

        @pl.when((kk > 0) & (kk < nk - 1))
        def _():
            acc_ref[...] += product()

        @pl.when(kk == nk - 1)
        def _():
            finish(acc_ref[...] + product())

    return pl.pallas_call(
        body, name=name, grid=(M // tm, N // tn, nk), in_specs=in_specs, out_specs=o_spec,
        out_shape=_sds(out_shape, out_dtype), scratch_shapes=[pltpu.VMEM((tm, tn), F32)] if nk > 1 else [],
        compiler_params=_cp("parallel", "parallel", "arbitrary"))(*args)


def _rows(i):
    return (i, 0)


def _row0(i):
    return (0, 0)


def _rms_fwd(name, x, g):
    T, D = x.shape
    tr = _tile(T, 256, 8)

    def body(x_ref, g_ref, o_ref):
        xv = x_ref[...]
        r = lax.rsqrt(jnp.mean(xv * xv, axis=-1, keepdims=True) + EPS)
        o_ref[...] = (xv * r * g_ref[...]).astype(o_ref.dtype)

    return pl.pallas_call(
        body, name=name, grid=(T // tr,),
        in_specs=[pl.BlockSpec((tr, D), _rows), pl.BlockSpec((1, D), _row0)],
        out_specs=pl.BlockSpec((tr, D), _rows), out_shape=_sds((T, D), BF16),
        compiler_params=_cp("parallel"))(x, g)


def _rms_bwd(name, x, g, dh, dres):
    T, D = x.shape
    tr = _tile(T, 256, 8)

    def body(x_ref, g_ref, dh_ref, dres_ref, dx_ref, dxb_ref, dg_ref):
        xv = x_ref[...]
        r = lax.rsqrt(jnp.mean(xv * xv, axis=-1, keepdims=True) + EPS)
        xh = xv * r
        dhv = dh_ref[...]
        dxh = dhv * g_ref[...]
        dx = dres_ref[...] + r * (dxh - xh * jnp.mean(dxh * xh, axis=-1, keepdims=True))
        dx_ref[...] = dx
        dxb_ref[...] = dx.astype(BF16)

        @pl.when(pl.program_id(0) == 0)
        def _():
            dg_ref[...] = jnp.zeros_like(dg_ref)

        dg_ref[...] += jnp.sum(dhv * xh, axis=0, keepdims=True)

    return pl.pallas_call(
        body, name=name, grid=(T // tr,),
        in_specs=[pl.BlockSpec((tr, D), _rows), pl.BlockSpec((1, D), _row0),
                  pl.BlockSpec((tr, D), _rows), pl.BlockSpec((tr, D), _rows)],
        out_specs=[pl.BlockSpec((tr, D), _rows), pl.BlockSpec((tr, D), _rows), pl.BlockSpec((1, D), _row0)],
        out_shape=[_sds((T, D), F32), _sds((T, D), BF16), _sds((1, D), F32)],
        compiler_params=_cp("arbitrary"))(x, g, dh, dres)


def _sigmoid(v):
    return 1.0 / (1.0 + jnp.exp(-v))


FFN_ROWS = 256


def _ffn_in_swiglu(name, h, wi):
    T, D = h.shape
    nb, _, cb = wi.shape
    F = nb * cb // 2
    tm, tn = _tile(T, FFN_ROWS, 16), _tile(cb, MM_TILE)
    per = cb // tn

    def body(h_ref, wg_ref, wu_ref, a_ref, s_ref):
        hv = h_ref[...]
        gate = jnp.dot(hv, wg_ref[...], preferred_element_type=F32)
        up = jnp.dot(hv, wu_ref[...], preferred_element_type=F32)
        a_ref[0] = gate.astype(BF16)
        a_ref[1] = up.astype(BF16)
        s_ref[...] = (gate * _sigmoid(gate) * up).astype(BF16)

    return pl.pallas_call(
        body, name=name, grid=(F // tn, T // tm),
        in_specs=[pl.BlockSpec((tm, D), lambda j, i: (i, 0)),
                  pl.BlockSpec((None, D, tn), lambda j, i: (j // per, 0, j % per)),
                  pl.BlockSpec((None, D, tn), lambda j, i: (nb // 2 + j // per, 0, j % per))],
        out_specs=[pl.BlockSpec((2, tm, tn), lambda j, i: (0, i, j)), pl.BlockSpec((tm, tn), lambda j, i: (i, j))],
        out_shape=[_sds((2, T, F), BF16), _sds((T, F), BF16)],
        compiler_params=_cp("parallel", "parallel"))(h, wi, wi)


def _ffn_ds_swiglu(name, dxb, wo, a, scale):
    T, D = dxb.shape
    F = wo.shape[0]
    tm, tn = _tile(T, FFN_ROWS, 16), _tile(F, MM_TILE)

    def body(dx_ref, w_ref, a_ref, o_ref):
        ds = lax.dot_general(dx_ref[...], w_ref[...], (((1,), (1,)), ((), ())), preferred_element_type=F32) * scale
        gate = a_ref[0].astype(F32)
        up = a_ref[1].astype(F32)
        sg = _sigmoid(gate)
        o_ref[0] = (ds * up * sg * (1.0 + gate * (1.0 - sg))).astype(BF16)
        o_ref[1] = (ds * gate * sg).astype(BF16)

    return pl.pallas_call(
        body, name=name, grid=(F // tn, T // tm),
        in_specs=[pl.BlockSpec((tm, D), lambda j, i: (i, 0)), pl.BlockSpec((tn, D), lambda j, i: (j, 0)),
                  pl.BlockSpec((2, tm, tn), lambda j, i: (0, i, j))],
        out_specs=pl.BlockSpec((2, tm, tn), lambda j, i: (0, i, j)), out_shape=_sds((2, T, F), BF16),
        compiler_params=_cp("parallel", "parallel"))(dxb, wo, a)


def _loss_head(name, x, g, tgt):
    T, D = x.shape
    tr = _tile(T, 256, 8)

    def body(x_ref, g_ref, t_ref, loss_ref, dx_ref, dxb_ref, dg_ref):
        xv = x_ref[...]
        gv = g_ref[...]
        r = lax.rsqrt(jnp.mean(xv * xv, axis=-1, keepdims=True) + EPS)
        xh = xv * r
        err = xh * gv - t_ref[...]
        dy = err * (1.0 / D)
        dxh = dy * gv
        dx = r * (dxh - xh * jnp.mean(dxh * xh, axis=-1, keepdims=True))
        dx_ref[...] = dx
        dxb_ref[...] = dx.astype(BF16)

        @pl.when(pl.program_id(0) == 0)
        def _():
            dg_ref[...] = jnp.zeros_like(dg_ref)
            loss_ref[...] = jnp.zeros_like(loss_ref)

        dg_ref[...] += jnp.sum(dy * xh, axis=0, keepdims=True)
        part = jnp.sum(jnp.sum(err * err, axis=-1, keepdims=True), axis=0, keepdims=True) * (0.5 / D)
        loss_ref[...] += part

    return pl.pallas_call(
        body, name=name, grid=(T // tr,),
        in_specs=[pl.BlockSpec((tr, D), _rows), pl.BlockSpec((1, D), _row0), pl.BlockSpec((tr, D), _rows)],
        out_specs=[pl.BlockSpec((8, LANES), _row0), pl.BlockSpec((tr, D), _rows), pl.BlockSpec((tr, D), _rows),
                   pl.BlockSpec((1, D), _row0)],
        out_shape=[_sds((8, LANES), F32), _sds((T, D), F32), _sds((T, D), BF16), _sds((1, D), F32)],
        compiler_params=_cp("arbitrary"))(x, g, tgt)


_SQRT_HALF = 0.7071067811865476
_INV_SQRT_2PI = 0.3989422804014327


def _gelu(v):
    return 0.5 * v * (1.0 + lax.erf(v * _SQRT_HALF))


def _gelu_grad(v):
    return 0.5 * (1.0 + lax.erf(v * _SQRT_HALF)) + v * _INV_SQRT_2PI * jnp.exp(-0.5 * v * v)


def _group_expand(rows, width, gd):
    gi = lax.broadcasted_iota(jnp.int32, (rows, width), 0)
    fi = lax.broadcasted_iota(jnp.int32, (rows, width), 1)
    return ((fi >= gi * gd) & (fi < (gi + 1) * gd)).astype(F32)


def _gm_act_fwd(name, zpre, vnorm):
    _, T, GI = zpre.shape
    tr = _tile(T, 128, 8)

    def body(z_ref, g_ref, u_ref, v_ref):
        u_ref[...] = _gelu(z_ref[0].astype(F32)).astype(BF16)
        zv = _gelu(z_ref[1].astype(F32))
        r = lax.rsqrt(jnp.mean(zv * zv, axis=-1, keepdims=True) + EPS)
        v_ref[...] = (zv * r * g_ref[...]).astype(BF16)

    return pl.pallas_call(
        body, name=name, grid=(T // tr,),
        in_specs=[pl.BlockSpec((2, tr, GI), lambda i: (0, i, 0)), pl.BlockSpec((1, GI), _row0)],
        out_specs=[pl.BlockSpec((tr, GI), _rows), pl.BlockSpec((tr, GI), _rows)],
        out_shape=[_sds((T, GI), BF16), _sds((T, GI), BF16)], compiler_params=_cp("parallel"))(zpre, vnorm)


def _gm_act_bwd(name, zpre, du, dvn, vnorm):
    _, T, GI = zpre.shape
    tr = _tile(T, 128, 8)

    def body(z_ref, du_ref, dvn_ref, g_ref, dz_ref, dg_ref):
        xu = z_ref[0].astype(F32)
        xv = z_ref[1].astype(F32)
        zv = _gelu(xv)
        r = lax.rsqrt(jnp.mean(zv * zv, axis=-1, keepdims=True) + EPS)
        xh = zv * r
        dv = dvn_ref[...]

        @pl.when(pl.program_id(0) == 0)
        def _():
            dg_ref[...] = jnp.zeros_like(dg_ref)

        dg_ref[...] += jnp.sum(dv * xh, axis=0, keepdims=True)
        dxh = dv * g_ref[...]
        dzv = r * (dxh - xh * jnp.mean(dxh * xh, axis=-1, keepdims=True))
        dz_ref[0] = (du_ref[...] * _gelu_grad(xu)).astype(BF16)
        dz_ref[1] = (dzv * _gelu_grad(xv)).astype(BF16)

    return pl.pallas_call(
        body, name=name, grid=(T // tr,),
        in_specs=[pl.BlockSpec((2, tr, GI), lambda i: (0, i, 0)), pl.BlockSpec((tr, GI), _rows),
                  pl.BlockSpec((tr, GI), _rows), pl.BlockSpec((1, GI), _row0)],
        out_specs=[pl.BlockSpec((2, tr, GI), lambda i: (0, i, 0)), pl.BlockSpec((1, GI), _row0)],
        out_shape=[_sds((2, T, GI), BF16), _sds((1, GI), F32)], compiler_params=_cp("arbitrary"))(zpre, du, dvn, vnorm)


def _causal_mask():
    r = lax.broadcasted_iota(jnp.int32, (CHUNK, CHUNK), 0)
    c = lax.broadcasted_iota(jnp.int32, (CHUNK, CHUNK), 1)
    return r >= c


def _gm_spatial_fwd(name, u, vn, ws, bt):
    T, GI = u.shape
    G = ws.shape[0]
    gd = GI // G

    def body(u_ref, v_ref, ws_ref, bt_ref, o_ref, bias_scr):
        @pl.when(pl.program_id(0) == 0)
        def _():
            bias_scr[...] = jnp.dot(bt_ref[...], _group_expand(LANES, GI, gd), precision=HIGHEST,
                                    preferred_element_type=F32)

        causal = _causal_mask()
        for g in range(G):
            sl = slice(g * gd, (g + 1) * gd)
            wc = jnp.where(causal, ws_ref[g], 0.0).astype(BF16)
            mixed = jnp.dot(wc, v_ref[:, sl], preferred_element_type=F32) + bias_scr[:, sl]
            o_ref[:, sl] = (u_ref[:, sl].astype(F32) * mixed).astype(o_ref.dtype)

    return pl.pallas_call(
        body, name=name, grid=(T // CHUNK,),
        in_specs=[pl.BlockSpec((CHUNK, GI), _rows), pl.BlockSpec((CHUNK, GI), _rows),
                  pl.BlockSpec((G, CHUNK, CHUNK), lambda i: (0, 0, 0)), pl.BlockSpec((CHUNK, LANES), _row0)],
        out_specs=pl.BlockSpec((CHUNK, GI), _rows), out_shape=_sds((T, GI), BF16),
        scratch_shapes=[pltpu.VMEM((CHUNK, GI), F32)], compiler_params=_cp("arbitrary"))(u, vn, ws, bt)


def _gm_spatial_bwd(name, dgated, u, vn, ws, bt):
    T, GI = u.shape
    G = ws.shape[0]
    gd = GI // G
    nc = T // CHUNK

    def body(dg_ref, u_ref, v_ref, ws_ref, bt_ref, du_ref, dv_ref, dws_ref, dbt_ref, bias_scr, dm_scr):
        step = pl.program_id(0)

        @pl.when(step == 0)
        def _():
            bias_scr[...] = jnp.dot(bt_ref[...], _group_expand(LANES, GI, gd), precision=HIGHEST,
                                    preferred_element_type=F32)
            dm_scr[...] = jnp.zeros_like(dm_scr)
            dws_ref[...] = jnp.zeros_like(dws_ref)

        causal = _causal_mask()
        for g in range(G):
            sl = slice(g * gd, (g + 1) * gd)
            wc = jnp.where(causal, ws_ref[g], 0.0).astype(BF16)
            vv = v_ref[:, sl]
            dgv = dg_ref[:, sl].astype(F32)
            mixed = jnp.dot(wc, vv, preferred_element_type=F32) + bias_scr[:, sl]
            du_ref[:, sl] = dgv * mixed
            dm = dgv * u_ref[:, sl].astype(F32)
            dmb = dm.astype(BF16)
            dv_ref[:, sl] = lax.dot_general(wc, dmb, (((0,), (0,)), ((), ())), preferred_element_type=F32)
            dw = lax.dot_general(dmb, vv, (((1,), (1,)), ((), ())), preferred_element_type=F32)
            dws_ref[g] += jnp.where(causal, dw, 0.0)
            dm_scr[:, sl] += dm

        @pl.when(step == nc - 1)
        def _():
            dbt_ref[...] = lax.dot_general(dm_scr[...], _group_expand(LANES, GI, gd), (((1,), (1,)), ((), ())),
                                           precision=HIGHEST, preferred_element_type=F32)

    return pl.pallas_call(
        body, name=name, grid=(nc,),
        in_specs=[pl.BlockSpec((CHUNK, GI), _rows), pl.BlockSpec((CHUNK, GI), _rows), pl.BlockSpec((CHUNK, GI), _rows),
                  pl.BlockSpec((G, CHUNK, CHUNK), lambda i: (0, 0, 0)), pl.BlockSpec((CHUNK, LANES), _row0)],
        out_specs=[pl.BlockSpec((CHUNK, GI), _rows), pl.BlockSpec((CHUNK, GI), _rows),
                   pl.BlockSpec((G, CHUNK, CHUNK), lambda i: (0, 0, 0)), pl.BlockSpec((CHUNK, LANES), _row0)],
        out_shape=[_sds((T, GI), F32), _sds((T, GI), F32), _sds((G, CHUNK, CHUNK), F32), _sds((CHUNK, LANES), F32)],
        scratch_shapes=[pltpu.VMEM((CHUNK, GI), F32), pltpu.VMEM((CHUNK, GI), F32)],
        compiler_params=_cp("arbitrary"))(dgated, u, vn, ws, bt)


def _conv_taps(xv, w_ref, b_ref):
    rows = lax.broadcasted_iota(jnp.int32, xv.shape, 0)
    acc = xv * w_ref[pl.ds(SSM_CONV - 1, 1), :] + b_ref[...]
    for k in range(1, SSM_CONV):
        sh = jnp.where(rows >= k, pltpu.roll(xv, k, 0), 0.0)
        acc = acc + sh * w_ref[pl.ds(SSM_CONV - 1 - k, 1), :]
    return acc


def _ssd_conv_fwd(name, proj, wt, b, inner, cd):
    T = proj.shape[0]
    tc = _tile(_gcd(inner, cd), 512)
    off = inner // tc

    def body(x_ref, w_ref, b_ref, o_ref):
        pre = _conv_taps(x_ref[...], w_ref, b_ref)
        o_ref[...] = pre * _sigmoid(pre)

    return pl.pallas_call(
        body, name=name, grid=(cd // tc,),
        in_specs=[pl.BlockSpec((T, tc), lambda j: (0, off + j)), pl.BlockSpec((SSM_CONV, tc), lambda j: (0, j)),
                  pl.BlockSpec((1, tc), lambda j: (0, j))],
        out_specs=pl.BlockSpec((T, tc), lambda j: (0, j)), out_shape=_sds((T, cd), F32),
        compiler_params=_cp("parallel"))(proj, wt, b)


def _ssd_conv_bwd(name, dact, proj, wt, b, inner, cd):
    T = proj.shape[0]
    tc = _tile(_gcd(inner, cd), 512)
    off = inner // tc

    def body(da_ref, x_ref, w_ref, b_ref, dx_ref, dw_ref, db_ref):
        xv = x_ref[...]
        pre = _conv_taps(xv, w_ref, b_ref)
        sg = _sigmoid(pre)
        dpre = da_ref[...] * sg * (1.0 + pre * (1.0 - sg))
        rows = lax.broadcasted_iota(jnp.int32, xv.shape, 0)
        db_ref[...] = jnp.sum(dpre, axis=0, keepdims=True)
        dx = dpre * w_ref[pl.ds(SSM_CONV - 1, 1), :]
        dw_ref[pl.ds(SSM_CONV - 1, 1), :] = jnp.sum(dpre * xv, axis=0, keepdims=True)
        for k in range(1, SSM_CONV):
            sh = jnp.where(rows >= k, pltpu.roll(xv, k, 0), 0.0)
            dw_ref[pl.ds(SSM_CONV - 1 - k, 1), :] = jnp.sum(dpre * sh, axis=0, keepdims=True)
            fw = jnp.where(rows < T - k, pltpu.roll(dpre, T - k, 0), 0.0)
            dx = dx + fw * w_ref[pl.ds(SSM_CONV - 1 - k, 1), :]
        dx_ref[...] = dx.astype(BF16)

    return pl.pallas_call(
        body, name=name, grid=(cd // tc,),
        in_specs=[pl.BlockSpec((T, tc), lambda j: (0, j)), pl.BlockSpec((T, tc), lambda j: (0, off + j)),
                  pl.BlockSpec((SSM_CONV, tc), lambda j: (0, j)), pl.BlockSpec((1, tc), lambda j: (0, j))],
        out_specs=[pl.BlockSpec((T, tc), lambda j: (0, j)), pl.BlockSpec((SSM_CONV, tc), lambda j: (0, j)),
                   pl.BlockSpec((1, tc), lambda j: (0, j))],
        out_shape=[_sds((T, cd), BF16), _sds((SSM_CONV, cd), F32), _sds((1, cd), F32)],
        compiler_params=_cp("parallel"))(dact, proj, wt, b)


def _softplus(v):
    return jnp.maximum(v, 0.0) + jnp.log(1.0 + jnp.exp(-jnp.abs(v)))


def _tri(lower):
    r = lax.broadcasted_iota(jnp.int32, (CHUNK, CHUNK), 0)
    c = lax.broadcasted_iota(jnp.int32, (CHUNK, CHUNK), 1)
    return ((c <= r) if lower else (c >= r)).astype(F32)


def _ssd_dt_fwd(name, proj, bias, alog, dtcol, heads):
    T = proj.shape[0]

    def body(dt_ref, b_ref, al_ref, dtp_ref, ac_ref):
        live = lax.broadcasted_iota(jnp.int32, (CHUNK, LANES), 1) < heads
        dtp = jnp.where(live, _softplus(dt_ref[...] + b_ref[...]), 0.0)
        da = dtp * (-jnp.exp(al_ref[...]))
        dtp_ref[...] = dtp
        ac_ref[...] = jnp.dot(_tri(True), da, precision=HIGHEST, preferred_element_type=F32)

    return pl.pallas_call(
        body, name=name, grid=(T // CHUNK,),
        in_specs=[pl.BlockSpec((CHUNK, LANES), lambda i: (i, dtcol)), pl.BlockSpec((1, LANES), _row0),
                  pl.BlockSpec((1, LANES), _row0)],
        out_specs=[pl.BlockSpec((CHUNK, LANES), _rows), pl.BlockSpec((CHUNK, LANES), _rows)],
        out_shape=[_sds((T, LANES), F32), _sds((T, LANES), F32)], compiler_params=_cp("parallel"))(proj, bias, alog)


def _ssd_dt_bwd(name, ddtp_g, dacum_g, dd_g, proj, bias, alog, dtp, dtcol, heads):
    T = proj.shape[0]
    G = ddtp_g.shape[0]

    def body(ddtp_ref, dac_ref, dd_ref, dt_ref, b_ref, al_ref, dtp_ref, ddt_ref, db_ref, dal_ref, dds_ref, da_scr):
        step = pl.program_id(0)

        @pl.when(step == 0)
        def _():
            db_ref[...] = jnp.zeros_like(db_ref)
            da_scr[...] = jnp.zeros_like(da_scr)
            dds_ref[...] = jnp.sum(dd_ref[...], axis=0)

        live = lax.broadcasted_iota(jnp.int32, (CHUNK, LANES), 1) < heads
        a = -jnp.exp(al_ref[...])
        dac = jnp.sum(dac_ref[...], axis=0)
        dda = jnp.dot(_tri(False), dac, precision=HIGHEST, preferred_element_type=F32)
        dtp_v = dtp_ref[...]
        ddtp = jnp.sum(ddtp_ref[...], axis=0) + dda * a
        da_scr[...] += jnp.sum(dda * dtp_v, axis=0, keepdims=True)
        ddt = jnp.where(live, ddtp * _sigmoid(dt_ref[...] + b_ref[...]), 0.0)
        ddt_ref[...] = ddt.astype(BF16)
        db_ref[...] += jnp.sum(ddt, axis=0, keepdims=True)
        dal_ref[...] = da_scr[...] * a

    return pl.pallas_call(
        body, name=name, grid=(T // CHUNK,),
        in_specs=[pl.BlockSpec((G, CHUNK, LANES), lambda i: (0, i, 0)), pl.BlockSpec((G, CHUNK, LANES), lambda i: (0, i, 0)),
                  pl.BlockSpec((G, 8, LANES), lambda i: (0, 0, 0)),
                  pl.BlockSpec((CHUNK, LANES), lambda i: (i, dtcol)), pl.BlockSpec((1, LANES), _row0),
                  pl.BlockSpec((1, LANES), _row0), pl.BlockSpec((CHUNK, LANES), _rows)],
        out_specs=[pl.BlockSpec((CHUNK, LANES), _rows), pl.BlockSpec((8, LANES), _row0), pl.BlockSpec((8, LANES), _row0),
                   pl.BlockSpec((8, LANES), _row0)],
        out_shape=[_sds((T, LANES), BF16), _sds((8, LANES), F32), _sds((8, LANES), F32), _sds((8, LANES), F32)],
        scratch_shapes=[pltpu.VMEM((8, LANES), F32)],
        compiler_params=_cp("arbitrary"))(ddtp_g, dacum_g, dd_g, proj, bias, alog, dtp)


def _head_expand(g, gw):
    hi = lax.broadcasted_iota(jnp.int32, (LANES, gw), 0) - g * SSM_HPG
    fi = lax.broadcasted_iota(jnp.int32, (LANES, gw), 1)
    return ((fi >= hi * SSM_HEAD_DIM) & (fi < (hi + 1) * SSM_HEAD_DIM)).astype(F32)


def _dot(a, b, dims, exact=False):
    if exact:
        return lax.dot_general(a, b, (dims, ((), ())), precision=HIGHEST, preferred_element_type=F32)
    return lax.dot_general(a.astype(BF16), b.astype(BF16), (dims, ((), ())), preferred_element_type=F32)


_NN = ((1,), (0,))
_NT = ((1,), (1,))
_TN = ((0,), (0,))


def _pair_decay(g, q, acum, acum_t_ref, causal):
    lane = lax.broadcasted_iota(jnp.int32, (CHUNK, LANES), 1)
    out = []
    for e in range(2):
        h = g * SSM_HPG + 2 * q + e
        acol = jnp.sum(jnp.where(lane == h, acum, 0.0), axis=1, keepdims=True)
        arow = acum_t_ref[pl.ds(h, 1), :]
        out.append(jnp.exp(jnp.where(causal, acol - arow, -1e30)))
    return out


def _ssd_core_fwd(name, act, dtp, acum, dexp, inner, groups):
    T = act.shape[0]
    nc = T // CHUNK
    gw = SSM_HPG * SSM_HEAD_DIM
    npair = gw // LANES
    bcol, ccol = inner // SSM_STATE, inner // SSM_STATE + groups

    def body(x_ref, b_ref, c_ref, dtp_ref, ac_ref, d_ref, y_ref, sp_ref, st_scr, act_scr, ae_scr):
        g = pl.program_id(0)

        @pl.when(pl.program_id(1) == 0)
        def _():
            st_scr[...] = jnp.zeros_like(st_scr)

        st = st_scr[...]
        sp_ref[...] = st
        e = _head_expand(g, gw)
        acum = ac_ref[...]
        ae = _dot(acum, e, _NN, exact=True)
        dte = _dot(dtp_ref[...], e, _NN, exact=True)
        ae_scr[...] = ae
        act_scr[...] = acum.T
        xv = x_ref[...]
        xdt = xv * dte
        bm, cm = b_ref[...], c_ref[...]
        cb = _dot(cm, bm, _NT)
        causal = _causal_mask()
        lane = lax.broadcasted_iota(jnp.int32, (CHUNK, LANES), 1)
        yoff = _dot(cm, st, _NN) * jnp.exp(ae)
        skip = xv * d_ref[...]
        for q in range(npair):
            sl = slice(q * LANES, (q + 1) * LANES)
            dec = _pair_decay(g, q, acum, act_scr, causal)
            x2 = xdt[:, sl]
            xa = jnp.where(lane < SSM_HEAD_DIM, x2, 0.0)
            yd = _dot(dec[0] * cb, xa, _NN) + _dot(dec[1] * cb, x2 - xa, _NN)
            y_ref[:, sl] = yd + yoff[:, sl] + skip[:, sl]
        alast = ae_scr[pl.ds(CHUNK - 1, 1), :]
        z = xdt * jnp.exp(alast - ae)
        st_scr[...] = st * jnp.exp(alast) + _dot(bm, z, _TN)

    return pl.pallas_call(
        body, name=name, grid=(groups, nc),
        in_specs=[pl.BlockSpec((CHUNK, gw), lambda g, c: (c, g)),
                  pl.BlockSpec((CHUNK, SSM_STATE), lambda g, c: (c, bcol + g)),
                  pl.BlockSpec((CHUNK, SSM_STATE), lambda g, c: (c, ccol + g)),
                  pl.BlockSpec((CHUNK, LANES), lambda g, c: (c, 0)), pl.BlockSpec((CHUNK, LANES), lambda g, c: (c, 0)),
                  pl.BlockSpec((1, gw), lambda g, c: (0, g))],
        out_specs=[pl.BlockSpec((CHUNK, gw), lambda g, c: (c, g)),
                   pl.BlockSpec((None, SSM_STATE, gw), lambda g, c: (c, 0, g))],
        out_shape=[_sds((T, inner), F32), _sds((nc, SSM_STATE, inner), F32)],
        scratch_shapes=[pltpu.VMEM((SSM_STATE, gw), F32), pltpu.VMEM((CHUNK, LANES), F32), pltpu.VMEM((CHUNK, gw), F32)],
        compiler_params=_cp("arbitrary", "arbitrary"))(act, act, act, dtp, acum, dexp)


def _ssd_core_bwd(name, dy, act, dtp, acum, dexp, sprev, inner, groups):
    T = act.shape[0]
    nc = T // CHUNK
    gw = SSM_HPG * SSM_HEAD_DIM
    npair = gw // LANES
    bcol, ccol = inner // SSM_STATE, inner // SSM_STATE + groups

    def rc(g, c):
        return nc - 1 - c

    def body(dy_ref, x_ref, b_ref, c_ref, dtp_ref, ac_ref, d_ref, sp_ref,
             dx_ref, db_ref, dc_ref, ddtp_ref, dac_ref, dd_ref,
             dst_scr, act_scr, ae_scr, dxdt_scr, dd_scr, dact_scr):
        g = pl.program_id(0)
        step = pl.program_id(1)

        @pl.when(step == 0)
        def _():
            dst_scr[...] = jnp.zeros_like(dst_scr)
            dd_scr[...] = jnp.zeros_like(dd_scr)

        dst = dst_scr[...]
        sp = sp_ref[...]
        e = _head_expand(g, gw)
        acum = ac_ref[...]
        ae = _dot(acum, e, _NN, exact=True)
        dte = _dot(dtp_ref[...], e, _NN, exact=True)
        ae_scr[...] = ae
        act_scr[...] = acum.T
        alast = ae_scr[pl.ds(CHUNK - 1, 1), :]
        xv = x_ref[...]
        xdt = xv * dte
        bm, cm = b_ref[...], c_ref[...]
        dyv = dy_ref[...]
        cb = _dot(cm, bm, _NT)
        causal = _causal_mask()
        lane = lax.broadcasted_iota(jnp.int32, (CHUNK, LANES), 1)
        ea = jnp.exp(ae)
        cde = jnp.exp(alast)
        w = jnp.exp(alast - ae)
        z = xdt * w

        dd_scr[...] += jnp.sum(dyv * xv, axis=0, keepdims=True)
        qm = _dot(cm, sp, _NN)
        dq = dyv * ea
        dae = dq * qm
        dc = _dot(dq, sp, _NT)
        dsp = _dot(cm, dq, _TN) + dst * cde
        dal = jnp.sum(dst * sp, axis=0, keepdims=True) * cde
        db = _dot(z, dst, _NT)
        dz = _dot(bm, dst, _NN)
        gw_ = dz * z
        dae = dae - gw_
        dal = dal + jnp.sum(gw_, axis=0, keepdims=True)
        dxdt_scr[...] = dz * w
        dcb = jnp.zeros((CHUNK, CHUNK), F32)
        dacol = jnp.zeros((CHUNK, LANES), F32)
        dact_scr[...] = jnp.zeros_like(dact_scr)
        sub = lax.broadcasted_iota(jnp.int32, (CHUNK, LANES), 0)
        for q in range(npair):
            sl = slice(q * LANES, (q + 1) * LANES)
            dec = _pair_decay(g, q, acum, act_scr, causal)
            x2, dy2 = xdt[:, sl], dyv[:, sl]
            xs_ = (jnp.where(lane < SSM_HEAD_DIM, x2, 0.0),)
            xs_ = xs_ + (x2 - xs_[0],)
            dys = (jnp.where(lane < SSM_HEAD_DIM, dy2, 0.0),)
            dys = dys + (dy2 - dys[0],)
            dx2 = jnp.zeros((CHUNK, LANES), F32)
            for hh in range(2):
                h = g * SSM_HPG + 2 * q + hh
                m = dec[hh] * cb
                dm = _dot(dys[hh], xs_[hh], _NT)
                dx2 = dx2 + _dot(m, dys[hh], _TN)
                dcb = dcb + dm * dec[hh]
                r = dm * m
                dacol = dacol + jnp.where(lane == h, jnp.sum(r, axis=1, keepdims=True), 0.0)
                dact_scr[...] -= jnp.where(sub == h, jnp.sum(r, axis=0, keepdims=True), 0.0)
            dxdt_scr[:, sl] += dx2
        dc = dc + _dot(dcb, bm, _NN)
        db = db + _dot(dcb, cm, _TN)
        dxdt = dxdt_scr[...]
        dx_ref[...] = dyv * d_ref[...] + dxdt * dte
        db_ref[...] = db
        dc_ref[...] = dc
        ddtp_ref[...] = _dot(dxdt * xv, e, _NT, exact=True)
        dal_h = _dot(jnp.broadcast_to(dal, (8, gw)), e, _NT, exact=True)
        dal_row = jnp.max(dal_h, axis=0, keepdims=True)
        dac = _dot(dae, e, _NT, exact=True) + dacol + dact_scr[...].T
        dac_ref[...] = dac + jnp.where(sub == CHUNK - 1, dal_row, 0.0)
        dst_scr[...] = dsp

        @pl.when(step == nc - 1)
        def _():
            dd_ref[...] = _dot(jnp.broadcast_to(dd_scr[...], (8, gw)), e, _NT, exact=True)

    return pl.pallas_call(
        body, name=name, grid=(groups, nc),
        in_specs=[pl.BlockSpec((CHUNK, gw), lambda g, c: (rc(g, c), g)),
                  pl.BlockSpec((CHUNK, gw), lambda g, c: (rc(g, c), g)),
                  pl.BlockSpec((CHUNK, SSM_STATE), lambda g, c: (rc(g, c), bcol + g)),
                  pl.BlockSpec((CHUNK, SSM_STATE), lambda g, c: (rc(g, c), ccol + g)),
                  pl.BlockSpec((CHUNK, LANES), lambda g, c: (rc(g, c), 0)),
                  pl.BlockSpec((CHUNK, LANES), lambda g, c: (rc(g, c), 0)),
                  pl.BlockSpec((1, gw), lambda g, c: (0, g)),
                  pl.BlockSpec((None, SSM_STATE, gw), lambda g, c: (rc(g, c), 0, g))],
        out_specs=[pl.BlockSpec((CHUNK, gw), lambda g, c: (rc(g, c), g)),
                   pl.BlockSpec((CHUNK, SSM_STATE), lambda g, c: (rc(g, c), g)),
                   pl.BlockSpec((CHUNK, SSM_STATE), lambda g, c: (rc(g, c), g)),
                   pl.BlockSpec((None, CHUNK, LANES), lambda g, c: (g, rc(g, c), 0)),
                   pl.BlockSpec((None, CHUNK, LANES), lambda g, c: (g, rc(g, c), 0)),
                   pl.BlockSpec((None, 8, LANES), lambda g, c: (g, 0, 0))],
        out_shape=[_sds((T, inner), F32), _sds((T, groups * SSM_STATE), F32), _sds((T, groups * SSM_STATE), F32),
                   _sds((groups, T, LANES), F32), _sds((groups, T, LANES), F32), _sds((groups, 8, LANES), F32)],
        scratch_shapes=[pltpu.VMEM((SSM_STATE, gw), F32), pltpu.VMEM((CHUNK, LANES), F32), pltpu.VMEM((CHUNK, gw), F32),
                        pltpu.VMEM((CHUNK, gw), F32), pltpu.VMEM((1, gw), F32), pltpu.VMEM((CHUNK, LANES), F32)],
        compiler_params=_cp("arbitrary", "arbitrary"))(dy, act, act, act, dtp, acum, dexp, sprev)


def _ssd_post_fwd(name, y, proj, ng, inner, groups):
    T = y.shape[0]
    tr = _tile(T, 128, 8)
    gs = inner // groups

    def body(y_ref, z_ref, g_ref, o_ref):
        zv = z_ref[...]
        gy = y_ref[...] * (zv * _sigmoid(zv))
        for k in range(groups):
            sl = slice(k * gs, (k + 1) * gs)
            seg = gy[:, sl]
            r = lax.rsqrt(jnp.mean(seg * seg, axis=-1, keepdims=True) + EPS)
            o_ref[:, sl] = (seg * r * g_ref[:, sl]).astype(BF16)

    return pl.pallas_call(
        body, name=name, grid=(T // tr,),
        in_specs=[pl.BlockSpec((tr, inner), _rows), pl.BlockSpec((tr, inner), _rows), pl.BlockSpec((1, inner), _row0)],
        out_specs=pl.BlockSpec((tr, inner), _rows), out_shape=_sds((T, inner), BF16),
        compiler_params=_cp("parallel"))(y, proj, ng)


def _ssd_post_bwd(name, dyn, y, proj, ng, inner, groups):
    T = y.shape[0]
    tr = _tile(T, 128, 8)
    gs = inner // groups

    def body(dyn_ref, y_ref, z_ref, g_ref, dy_ref, dz_ref, dg_ref):
        @pl.when(pl.program_id(0) == 0)
        def _():
            dg_ref[...] = jnp.zeros_like(dg_ref)

        zv = z_ref[...]
        sg = _sigmoid(zv)
        sz = zv * sg
        yv = y_ref[...]
        gy = yv * sz
        dv = dyn_ref[...]
        for k in range(groups):
            sl = slice(k * gs, (k + 1) * gs)
            seg = gy[:, sl]
            r = lax.rsqrt(jnp.mean(seg * seg, axis=-1, keepdims=True) + EPS)
            xh = seg * r
            d = dv[:, sl]
            dg_ref[:, sl] += jnp.sum(d * xh, axis=0, keepdims=True)
            dxh = d * g_ref[:, sl]
            dgy = r * (dxh - xh * jnp.mean(dxh * xh, axis=-1, keepdims=True))
            dy_ref[:, sl] = dgy * sz[:, sl]
            dz_ref[:, sl] = (dgy * yv[:, sl] * (sg[:, sl] * (1.0 + zv[:, sl] * (1.0 - sg[:, sl])))).astype(BF16)

    return pl.pallas_call(
        body, name=name, grid=(T // tr,),
        in_specs=[pl.BlockSpec((tr, inner), _rows), pl.BlockSpec((tr, inner), _rows), pl.BlockSpec((tr, inner), _rows),
                  pl.BlockSpec((1, inner), _row0)],
        out_specs=[pl.BlockSpec((tr, inner), _rows), pl.BlockSpec((tr, inner), _rows), pl.BlockSpec((1, inner), _row0)],
        out_shape=[_sds((T, inner), F32), _sds((T, inner), BF16), _sds((1, inner), F32)],
        compiler_params=_cp("arbitrary"))(dyn, y, proj, ng)


def _place():
    return lax.axis_index("x"), lax.axis_index("y"), lax.axis_index("c")


def _other_chips(x, y):
    return [(1 - x, y), (x, 1 - y), (1 - x, 1 - y)]


def _remote(src, dst, ssem, rsem, dev):
    return pltpu.make_async_remote_copy(src_ref=src, dst_ref=dst, send_sem=ssem, recv_sem=rsem, device_id=dev,
                                        device_id_type=MESH)


def _all_gather_chips(name, shard):
    R, C = shard.shape
    hr = R // 2

    def body(x_ref, o_ref, ssem, rsem, lsem):
        x, y, c = _place()
        k = 2 * x + y
        sib = (x, y, 1 - c)
        chips = _other_chips(x, y)

        def half(blk, cc):
            return o_ref.at[blk, pl.ds(cc * hr, hr), :]

        local = pltpu.make_async_copy(x_ref, o_ref.at[k], lsem)
        local.start()
        first = [_remote(x_ref.at[pl.ds(c * hr, hr), :], half(k, c), ssem.at[r], rsem.at[r], (px, py, c))
                 for r, (px, py) in enumerate(chips)]
        for cp in first:
            cp.start()
        passed = []
        for r, (px, py) in enumerate(chips):
            kj = 2 * px + py
            _remote(half(kj, c), half(kj, c), ssem.at[r], rsem.at[r], (px, py, c)).wait_recv()
            fw = _remote(half(kj, c), half(kj, c), ssem.at[3 + r], rsem.at[3 + r], sib)
            fw.start()
            passed.append(fw)
        for r, (px, py) in enumerate(chips):
            kj = 2 * px + py
            _remote(half(kj, 1 - c), half(kj, 1 - c), ssem.at[3 + r], rsem.at[3 + r], sib).wait_recv()
        for cp in first + passed:
            cp.wait_send()
        local.wait()

    return pl.pallas_call(
        body, name=name, in_specs=[_ANY], out_specs=_ANY, out_shape=_sds((4, R, C), shard.dtype),
        scratch_shapes=[pltpu.SemaphoreType.DMA((6,)), pltpu.SemaphoreType.DMA((6,)), pltpu.SemaphoreType.DMA],
    )(shard)


_HBM = pl.BlockSpec(memory_space=pltpu.HBM)
_SEM = pl.BlockSpec(memory_space=pltpu.SEMAPHORE)
_EFFECT = pltpu.SideEffectType.DATAFLOW_SIDE_EFFECTING


def _hbm(a):
    return pltpu.with_memory_space_constraint(a, pltpu.HBM)


def _cast_into_block(name, w, layer, after):
    _, R, C = w.shape
    tr = _tile(R, 256, 16)

    def body(w_ref, a_ref, o_ref):
        o_ref[...] = w_ref[...].astype(BF16)

    return pl.pallas_call(
        body, name=name, grid=(R // tr,),
        in_specs=[pl.BlockSpec((None, tr, C), lambda i: (layer, i, 0)), _ANY],
        out_specs=pl.BlockSpec((None, tr, C), lambda i: (2 * lax.axis_index("x") + lax.axis_index("y"), i, 0)),
        out_shape=_sds((4, R, C), BF16), compiler_params=_cp("parallel"))(w, after)


def _gather_copies(refs, ssem, rsem):
    x, y, c = _place()
    k = 2 * x + y
    sends, arrivals = [], []
    for a, lr in enumerate(refs):
        hr = lr.shape[1] // 2
        mine = lr.at[k, pl.ds(c * hr, hr), :]
        for r, (px, py) in enumerate(_other_chips(x, y)):
            i = 3 * a + r
            sends.append(_remote(mine, mine, ssem.at[i], rsem.at[i], (px, py, c)))
            arrivals.append(_remote(mine, lr.at[2 * px + py, pl.ds(c * hr, hr), :], ssem.at[i], rsem.at[i], (px, py, c)))
    return sends, arrivals


def _swap_copies(refs, ssem, rsem):
    x, y, c = _place()
    n = len(refs) // 2
    sends = []
    for a, (g, o) in enumerate(zip(refs[:n], refs[n:])):
        hr = g.shape[1] // 2
        sends.append(_remote(g.at[:, pl.ds((1 - c) * hr, hr), :], o, ssem.at[a], rsem.at[a], (x, y, 1 - c)))
    return sends, sends


def _scatter_copies(refs, ssem, rsem):
    x, y, c = _place()
    k = 2 * x + y
    n = len(refs) // 2
    sends, arrivals = [], []
    for a, (xr, lr) in enumerate(zip(refs[:n], refs[n:])):
        for r, (px, py) in enumerate(_other_chips(x, y)):
            i = 3 * a + r
            kj = 2 * px + py
            sends.append(_remote(xr.at[kj], lr.at[k], ssem.at[i], rsem.at[i], (px, py, c)))
            arrivals.append(_remote(xr.at[kj], lr.at[kj], ssem.at[i], rsem.at[i], (px, py, c)))
    return sends, arrivals


def _split_start(name, bufs, ncopies, copies_fn):
    n = len(bufs)

    def body(*refs):
        sends, _ = copies_fn(refs[:n], refs[n], refs[n + 1])
        for cp in sends:
            cp.start()
        token = refs[-1]
        token[...] = jnp.zeros_like(token)

    outs = pl.pallas_call(
        body, name=name, in_specs=[_HBM] * n,
        out_specs=(_SEM, _SEM) + (_HBM,) * n + (pl.BlockSpec(memory_space=pltpu.VMEM),),
        out_shape=(pltpu.SemaphoreType.DMA((ncopies,)), pltpu.SemaphoreType.DMA((ncopies,)))
        + tuple(pltpu.HBM(b.shape, b.dtype) for b in bufs) + (_sds((8, LANES), F32),),
        input_output_aliases={i: 2 + i for i in range(n)},
        compiler_params=pltpu.CompilerParams(has_side_effects=_EFFECT),
    )(*[_hbm(b) for b in bufs])
    return outs[0], outs[1], list(outs[2:2 + n]), outs[-1]


def _split_wait(name, started, copies_fn, after):
    ssem, rsem, bufs, _ = started
    n = len(bufs)

    def body(*refs):
        sends, arrivals = copies_fn(refs[:n], refs[n], refs[n + 1])
        for cp in sends:
            cp.wait_send()
        for cp in arrivals:
            cp.wait_recv()

    outs = pl.pallas_call(
        body, name=name, in_specs=[_HBM] * n + [_SEM, _SEM, _ANY], out_specs=(_HBM,) * n,
        out_shape=tuple(pltpu.HBM(t.shape, t.dtype) for t in bufs),
        input_output_aliases={i: i for i in range(n)},
        compiler_params=pltpu.CompilerParams(has_side_effects=_EFFECT),
    )(*bufs, ssem, rsem, after)
    return list(outs)


def _share_sibling(name, lands):
    n = len(lands)

    def body(*refs):
        l_in, l_out = refs[:n], refs[n:2 * n]
        ssem, rsem = refs[2 * n:]
        x, y, c = _place()
        sib = (x, y, 1 - c)

        def rows(ref, a, px, py, cc):
            hr = ref[a].shape[1] // 2
            return ref[a].at[2 * px + py, pl.ds(cc * hr, hr), :]

        copies = []
        for a in range(n):
            for r, (px, py) in enumerate(_other_chips(x, y)):
                cp = _remote(rows(l_in, a, px, py, c), rows(l_out, a, px, py, c), ssem.at[3 * a + r], rsem.at[3 * a + r], sib)
                cp.start()
                copies.append(cp)
        for a in range(n):
            for r, (px, py) in enumerate(_other_chips(x, y)):
                theirs = rows(l_out, a, px, py, 1 - c)
                _remote(theirs, theirs, ssem.at[3 * a + r], rsem.at[3 * a + r], sib).wait_recv()
        for cp in copies:
            cp.wait_send()

    outs = pl.pallas_call(
        body, name=name, in_specs=[_ANY] * n, out_specs=[_ANY] * n,
        out_shape=[_sds(l.shape, l.dtype) for l in lands], input_output_aliases={i: i for i in range(n)},
        scratch_shapes=[pltpu.SemaphoreType.DMA((3 * n,)), pltpu.SemaphoreType.DMA((3 * n,))],
    )(*lands)
    return list(outs)


def _add_halves(name, full, other):
    nb, R, C = full.shape
    hr = R // 2
    tr = _tile(hr, 256, 16)
    nh = hr // tr

    def body(f_ref, o_ref, s_ref):
        s_ref[...] = (f_ref[...].astype(F32) + o_ref[...].astype(F32)).astype(s_ref.dtype)

    return pl.pallas_call(
        body, name=name, grid=(nb, nh),
        in_specs=[pl.BlockSpec((None, tr, C), lambda b, i: (b, lax.axis_index("c") * nh + i, 0)),
                  pl.BlockSpec((None, tr, C), lambda b, i: (b, i, 0))],
        out_specs=pl.BlockSpec((None, tr, C), lambda b, i: (b, i, 0)), out_shape=_sds((nb, hr, C), full.dtype),
        compiler_params=_cp("parallel", "parallel"))(full, other)


def _sum_owner(name, land, mine):
    nb, hr, C = land.shape
    tr = _tile(hr, 128, 16)

    def body(l_ref, m_ref, o_ref):
        k = 2 * lax.axis_index("x") + lax.axis_index("y")
        own = m_ref[...].astype(F32)
        acc = jnp.where(k == 0, own, l_ref[0].astype(F32))
        for j in range(1, nb):
            acc = acc + jnp.where(k == j, own, l_ref[j].astype(F32))
        o_ref[...] = acc

    return pl.pallas_call(
        body, name=name, grid=(hr // tr,),
        in_specs=[pl.BlockSpec((nb, tr, C), lambda i: (0, i, 0)),
                  pl.BlockSpec((None, tr, C), lambda i: (2 * lax.axis_index("x") + lax.axis_index("y"), i, 0))],
        out_specs=pl.BlockSpec((None, tr, C), lambda i: (lax.axis_index("c"), i, 0)),
        out_shape=_sds((2, hr, C), F32), compiler_params=_cp("parallel"))(land, mine)


def _join_copies(refs, ssem, rsem):
    x, y, c = _place()
    sends, arrivals = [], []
    for a, b in enumerate(refs):
        sends.append(_remote(b.at[c], b.at[c], ssem.at[a], rsem.at[a], (x, y, 1 - c)))
        arrivals.append(_remote(b.at[c], b.at[1 - c], ssem.at[a], rsem.at[a], (x, y, 1 - c)))
    return sends, arrivals


def _all_gather_devices(name, buf):
    def body(b_ref, o_ref, ssem, rsem):
        x, y, c = _place()
        me = 4 * x + 2 * y + c
        flips = [(fx, fy, fc) for fx in (0, 1) for fy in (0, 1) for fc in (0, 1) if fx or fy or fc]
        peers = [((1 - x) if fx else x, (1 - y) if fy else y, (1 - c) if fc else c) for fx, fy, fc in flips]
        sends = [_remote(b_ref, o_ref.at[me], ssem.at[r], rsem.at[r], p) for r, p in enumerate(peers)]
        for cp in sends:
            cp.start()
        for r, (px, py, pc) in enumerate(peers):
            pid = 4 * px + 2 * py + pc
            _remote(b_ref, o_ref.at[pid], ssem.at[r], rsem.at[r], (px, py, pc)).wait_recv()
        for cp in sends:
            cp.wait_send()

    return pl.pallas_call(
        body, name=name, in_specs=[_ANY], out_specs=_ANY, out_shape=_sds((8,) + buf.shape, buf.dtype),
        scratch_shapes=[pltpu.SemaphoreType.DMA((7,)), pltpu.SemaphoreType.DMA((7,))],
    )(buf)


def _sum_devices(name, parts, own):
    n, R, C = parts.shape
    tr = _tile(R, 256, 8)

    def body(p_ref, own_ref, o_ref):
        x, y, c = _place()
        me = 4 * x + 2 * y + c
        mine = own_ref[...]
        acc = jnp.where(me == 0, mine, p_ref[0])
        for j in range(1, n):
            acc = acc + jnp.where(me == j, mine, p_ref[j])
        o_ref[...] = acc

    return pl.pallas_call(
        body, name=name, grid=(R // tr,),
        in_specs=[pl.BlockSpec((n, tr, C), lambda i: (0, i, 0)), pl.BlockSpec((tr, C), _rows)],
        out_specs=pl.BlockSpec((tr, C), _rows), out_shape=_sds((R, C), F32), compiler_params=_cp("parallel"))(parts, own)


def _adamw(name, w, m, v, parts, tok=None):
    L, R, C = w.shape
    np_ = len(parts[0])
    tr = _tile(R, max(8, (3 * VMEM_LIMIT // 4) // (2 * 4 * C * (7 + L * np_))), 8)
    flat = [p for lp in parts for p in lp]
    if tok is not None:
        flat = flat + [tok]
    c1 = 1.0 / (1.0 - ADAM_B1 ** ADAM_STEP)
    c2 = 1.0 / (1.0 - ADAM_B2 ** ADAM_STEP)

    def body(*refs):
        w_ref, m_ref, v_ref = refs[:3]
        p_refs = refs[3:3 + L * np_]
        g_ref, d_ref, nm_ref, nv_ref = refs[-4:]
        layer = pl.program_id(0)
        g = jnp.zeros((tr, C), F32)
        for l in range(L):
            gl = p_refs[l * np_][...]
            for j in range(1, np_):
                gl = gl + p_refs[l * np_ + j][...]
            g = jnp.where(layer == l, gl, g) if L > 1 else gl
        if tok is not None:
            g = g + refs[3 + L * np_][...]
        nm = ADAM_B1 * m_ref[...] + (1.0 - ADAM_B1) * g
        nv = ADAM_B2 * v_ref[...] + (1.0 - ADAM_B2) * (g * g)
        g_ref[...] = g
        nm_ref[...] = nm
        nv_ref[...] = nv
        d_ref[...] = -ADAM_LR * ((nm * c1) / (jnp.sqrt(nv * c2) + ADAM_EPS) + ADAM_WD * w_ref[...])

    stacked = pl.BlockSpec((None, tr, C), lambda l, i: (l, i, 0))
    part_specs = [pl.BlockSpec((tr, C), (lambda l, i, ll=ll: (jnp.where(l == ll, i, 0), 0)))
                  for ll in range(L) for _ in range(np_)]
    if tok is not None:
        part_specs.append(pl.BlockSpec((1, 1), lambda l, i: (0, 0)))
    return pl.pallas_call(
        body, name=name, grid=(L, R // tr), in_specs=[stacked] * 3 + part_specs, out_specs=[stacked] * 4,
        out_shape=[_sds(w.shape, F32)] * 4, compiler_params=_cp("arbitrary", "arbitrary"))(w, m, v, *flat)


_PACK_ROWS = 16


def _pack(arrs):
    pieces = []
    for a in arrs:
        f = a.reshape(-1).astype(F32)
        unit = _PACK_ROWS * LANES
        pad = (-f.shape[0]) % unit
        pieces.append(jnp.pad(f, (0, pad)))
    return jnp.concatenate(pieces).reshape(-1, LANES)


def _unpack(buf, shapes):
    flat = buf.reshape(-1)
    out, off = [], 0
    unit = _PACK_ROWS * LANES
    for s in shapes:
        n = 1
        for d in s:
            n *= d
        out.append(flat[off:off + n].reshape(s))
        off += n + ((-n) % unit)
    return out


def _swap_start(tag, full):
    nb, R, C = full.shape
    return _split_start("sw_start_" + tag, [full, lax.empty((nb, R // 2, C), full.dtype)], 1, _swap_copies)


def _reduce_start(tag, swaps, after):
    pairs = [_split_wait("sw_wait_%s_%d" % (tag, i), st, _swap_copies, after) for i, st in enumerate(swaps)]
    sums = [_add_halves("rs_add", f, o) for f, o in pairs]
    lands = [lax.empty(t.shape, t.dtype) for t in sums]
    return _split_start("rs_start_" + tag, sums + lands, 3 * len(sums), _scatter_copies)


def _reduce_finish(tag, started, after):
    bufs = _split_wait("rs_wait_" + tag, started, _scatter_copies, after)
    n = len(bufs) // 2
    halves = [_sum_owner("rs_sum", l, s) for s, l in zip(bufs[:n], bufs[n:])]
    return _split_start("jn_start_" + tag, halves, len(halves), _join_copies)


def _join_finish(tag, started, after):
    joined = _split_wait("jn_wait_" + tag, started, _join_copies, after)
    return [j.reshape(2 * j.shape[1], j.shape[2]) for j in joined]


def kernel(x, ln_ffn_pre, ffn_pre_w_in, ffn_pre_w_out, ln_mix, ln_ffn_post, ffn_post_w_in, ffn_post_w_out, gm_w_in, gm_v_norm, gm_w_s, gm_b_s, gm_w_out, ssm_w_in, ssm_conv_w, ssm_conv_b, ssm_dt_bias, ssm_a_log, ssm_d, ssm_norm, ssm_w_out, ln_final, loss_target, m_ln_ffn_pre, m_ffn_pre_w_in, m_ffn_pre_w_out, m_ln_mix, m_ln_ffn_post, m_ffn_post_w_in, m_ffn_post_w_out, m_gm_w_in, m_gm_v_norm, m_gm_w_s, m_gm_b_s, m_gm_w_out, m_ssm_w_in, m_ssm_conv_w, m_ssm_conv_b, m_ssm_dt_bias, m_ssm_a_log, m_ssm_d, m_ssm_norm, m_ssm_w_out, m_ln_final, v_ln_ffn_pre, v_ffn_pre_w_in, v_ffn_pre_w_out, v_ln_mix, v_ln_ffn_post, v_ffn_post_w_in, v_ffn_post_w_out, v_gm_w_in, v_gm_v_norm, v_gm_w_s, v_gm_b_s, v_gm_w_out, v_ssm_w_in, v_ssm_conv_w, v_ssm_conv_b, v_ssm_dt_bias, v_ssm_a_log, v_ssm_d, v_ssm_norm, v_ssm_w_out, v_ln_final):
    T, D = x.shape[1], x.shape[2]
    depth = ln_ffn_pre.shape[0]
    n_gm, n_ssm = gm_w_in.shape[0], ssm_w_in.shape[0]
    F = ffn_pre_w_out.shape[1] * 4
    GI = gm_w_out.shape[1] * 4
    GG = gm_w_s.shape[1]
    inner = ssm_w_out.shape[1] * 4
    heads = ssm_dt_bias.shape[1]
    cd = ssm_conv_w.shape[1] * 4
    groups = (cd - inner) // (2 * SSM_STATE)
    pshard = ssm_w_in.shape[2]
    pw = inner + cd + LANES
    dtcol = (inner + cd) // LANES
    kchip = 2 * lax.axis_index("x") + lax.axis_index("y")

    small_shard = _pack([jnp.swapaxes(ssm_conv_w, 1, 2), ssm_conv_b, ssm_norm])
    small_all = _all_gather_chips("ag_small", small_shard)
    cq, iq = cd // 4, inner // 4
    parts = [_unpack(small_all[k], [(n_ssm, SSM_CONV, cq), (n_ssm, cq), (n_ssm, iq)]) for k in range(4)]
    conv_wt = jnp.concatenate([p[0] for p in parts], axis=2)
    conv_b = jnp.concatenate([p[1] for p in parts], axis=1)
    norm_g = jnp.concatenate([p[2] for p in parts], axis=1)

    def pad_lanes(v):
        return jnp.pad(v, (0, LANES - v.shape[0]))[None, :]

    xc = x[0]
    saved = []

    def shards_of(i, kind):
        if kind == "pre":
            return [(ffn_pre_w_in, i), (ffn_pre_w_out, i)]
        if kind == "post":
            return [(ffn_post_w_in, i), (ffn_post_w_out, i)]
        if i % 2 == 0:
            return [(gm_w_in, i // 2), (gm_w_out, i // 2)]
        return [(ssm_w_in, i // 2), (ssm_w_out, i // 2)]

    subs = [(i, kind) for i in range(depth) for kind in ("pre", "mix", "post")]
    nsub = len(subs)
    gathers = [None] * nsub
    ahead = 2

    def gather_start(s, after):
        lands = [_cast_into_block("ag_cast", w, l, after) for w, l in shards_of(*subs[s])]
        gathers[s] = _split_start("ag_start_%d" % s, lands, 3 * len(lands), _gather_copies)

    def gathered(s, after):
        lands = _split_wait("ag_wait_%d" % s, gathers[s], _gather_copies, after)
        full = _share_sibling("ag_share", lands)
        tok = jnp.zeros((1, 1), F32)
        depth_now = 1 if s == 0 else ahead
        for nxt in range(s + 1, min(s + depth_now, nsub - 1) + 1):
            if gathers[nxt] is None:
                gather_start(nxt, full[0])
                tok = tok + gathers[nxt][3][0:1, 0:1]
        return full, tok

    gather_start(0, small_all)

    def ffn_fwd(s, xin, g, l):
        (wi, wo), tok = gathered(s, xin)
        wo = wo.reshape(F, D)
        h = _rms_fwd("rms_fwd", xin, g[l][None, :] + tok)
        a, s_ = _ffn_in_swiglu("ffn_in", h, wi)
        xo = _matmul("ffn_out", s_, wo, "nn", F32, res=xin, scale=0.5)
        return xo, (xin, h, a, s_, wi, wo)

    for i in range(depth):
        xc, sv_pre = ffn_fwd(3 * i, xc, ln_ffn_pre, i)
        j = i // 2
        (wi, wo), tok = gathered(3 * i + 1, xc)
        h = _rms_fwd("rms_fwd", xc, ln_mix[i][None, :] + tok)
        if i % 2 == 0:
            wo = wo.reshape(GI, D)
            zpre = _matmul("gm_in", h, wi, "nn", BF16, out_blocks=2)
            u, vn = _gm_act_fwd("gm_act_fwd", zpre, gm_v_norm[j][None, :])
            bt = jnp.pad(gm_b_s[j].T, ((0, 0), (0, LANES - GG)))
            gated = _gm_spatial_fwd("gm_spatial_fwd", u, vn, gm_w_s[j], bt)
            xn = _matmul("mix_out", gated, wo, "nn", F32, res=xc, scale=1.0)
            sv_mix = (xc, h, zpre, u, vn, bt, gated, wi, wo)
        else:
            wg, wo = wi, wo.reshape(inner, D)
            wp = jnp.pad(jnp.swapaxes(wg, 0, 1).reshape(D, 4 * pshard), ((0, 0), (0, pw - 4 * pshard)))
            proj = _matmul("ssm_in", h, wp, "nn", F32)
            wt, cb_ = conv_wt[j], conv_b[j][None, :]
            act = _ssd_conv_fwd("ssd_conv_fwd", proj, wt, cb_, inner, cd)
            bias, alog = pad_lanes(ssm_dt_bias[j]), pad_lanes(ssm_a_log[j])
            dtp, acum = _ssd_dt_fwd("ssd_dt_fwd", proj, bias, alog, dtcol, heads)
            dexp = jnp.repeat(ssm_d[j], SSM_HEAD_DIM)[None, :]
            ycore, sprev = _ssd_core_fwd("ssd_core_fwd", act, dtp, acum, dexp, inner, groups)
            ng = norm_g[j][None, :]
            yn = _ssd_post_fwd("ssd_post_fwd", ycore, proj, ng, inner, groups)
            xn = _matmul("mix_out", yn, wo, "nn", F32, res=xc, scale=1.0)
            sv_mix = (xc, h, proj, act, dtp, acum, dexp, ycore, sprev, yn, wt, cb_, bias, alog, ng, wp, wo)
        xc = xn
        xc, sv_post = ffn_fwd(3 * i + 2, xc, ln_ffn_post, i)
        saved.append((sv_pre, sv_mix, sv_post))

    loss_tile, dx, dxb, dg_final = _loss_head("loss_head", xc, ln_final[None, :], loss_target[0])
    loss = lax.psum(loss_tile[0, 0], ("x", "y", "c"))

    wgrad = [None] * nsub
    in_flight = []
    deferred = []

    def reduce_finish(after):
        tok = jnp.zeros((1, 1), F32)
        while in_flight:
            s, st = in_flight.pop(0)
            wgrad[s] = _reduce_finish(str(s), st, after)
            tok = tok + wgrad[s][3][0:1, 0:1]
        return tok

    def swap_start(s, which, full):
        st = _swap_start("%d_%s" % (s, which), full)
        return st, st[3][0:1, 0:1]

    def reduce_start(s, swaps, after):
        tok = reduce_finish(after)
        if s == 0:
            deferred.append(swaps)
            return tok
        st = _reduce_start(str(s), swaps, after)
        in_flight.append((s, st))
        return tok + st[3][0:1, 0:1]

    def ffn_bwd(s, dx, dxb, sv, g, l):
        xin, h, a, s_, wi, wo = sv
        dwo = _matmul("ffn_dwo", s_, dxb, "tn", BF16, scale=0.5)
        sw_o, tok_o = swap_start(s, "o", dwo.reshape(4, F // 4, D))
        da = _ffn_ds_swiglu("ffn_ds", dxb, wo, a, 0.5)
        dwi = _matmul("ffn_dwi", h, da, "tn", BF16, out_blocks=4, tok=tok_o)
        sw_i, tok_i = swap_start(s, "i", dwi)
        dh = _matmul("ffn_dh", da, wi, "nt", F32, tok=tok_i)
        tok = reduce_start(s, [sw_i, sw_o], dh)
        return _rms_bwd("rms_bwd", xin, g[l][None, :] + tok, dh, dx)

    gl = {n: [None] * depth for n in ("pre", "mix", "post")}
    gm_g = {n: [None] * n_gm for n in ("vnorm", "ws", "bs")}
    ssm_g = {n: [None] * n_ssm for n in ("convw", "convb", "dtb", "alog", "d", "norm")}

    for i in reversed(range(depth)):
        sv_pre, sv_mix, sv_post = saved[i]
        j = i // 2
        dx, dxb, gl["post"][i] = ffn_bwd(3 * i + 2, dx, dxb, sv_post, ln_ffn_post, i)
        if i % 2 == 0:
            xin, h, zpre, u, vn, bt, gated, wi, wo = sv_mix
            dgated = _matmul("mix_dy", dxb, wo, "nt", BF16)
            dwo = _matmul("mix_dwo", gated, dxb, "tn", BF16)
            sw_o, tok_o = swap_start(3 * i + 1, "o", dwo.reshape(4, GI // 4, D))
            du, dvn, dws, dbt = _gm_spatial_bwd("gm_spatial_bwd", dgated, u, vn, gm_w_s[j], bt)
            dzpre, dvnorm = _gm_act_bwd("gm_act_bwd", zpre, du, dvn, gm_v_norm[j][None, :])
            dwi = _matmul("gm_dwi", h, dzpre, "tn", BF16, out_blocks=4, tok=tok_o)
            sw_i, tok_i = swap_start(3 * i + 1, "i", dwi)
            dh = _matmul("gm_dh", dzpre, wi, "nt", F32, tok=tok_i)
            tok = reduce_start(3 * i + 1, [sw_i, sw_o], dh)
            gm_g["vnorm"][j], gm_g["ws"][j], gm_g["bs"][j] = dvnorm[0], dws, dbt.T[:GG]
        else:
            xin, h, proj, act, dtp, acum, dexp, ycore, sprev, yn, wt, cb_, bias, alog, ng, wp, wo = sv_mix
            dyn = _matmul("ssm_dy", dxb, wo, "nt", F32)
            dwo = _matmul("mix_dwo", yn, dxb, "tn", BF16)
            sw_o, tok_o = swap_start(3 * i + 1, "o", dwo.reshape(4, inner // 4, D))
            dyc, dz, dnorm = _ssd_post_bwd("ssd_post_bwd", dyn, ycore, proj, ng, inner, groups)
            dxs, db_, dc_, ddtp_g, dac_g, dd_g = _ssd_core_bwd("ssd_core_bwd", dyc, act, dtp, acum, dexp, sprev,
                                                               inner, groups)
            ddt, dbias, dalog, dds = _ssd_dt_bwd("ssd_dt_bwd", ddtp_g, dac_g, dd_g, proj, bias, alog, dtp, dtcol, heads)
            dact = jnp.concatenate([dxs, db_, dc_], axis=1)
            dxbc, dwt, dcb = _ssd_conv_bwd("ssd_conv_bwd", dact, proj, wt, cb_, inner, cd)
            dproj = jnp.concatenate([dz, dxbc, ddt], axis=1)
            dwp = _matmul("ssm_dwi", h, dproj, "tn", BF16, tok=tok_o)
            dwi = jnp.swapaxes(dwp[:, :4 * pshard].reshape(D, 4, pshard), 0, 1)
            sw_i, tok_i = swap_start(3 * i + 1, "i", dwi)
            dh = _matmul("ssm_dh", dproj, wp, "nt", F32, tok=tok_i)
            tok = reduce_start(3 * i + 1, [sw_i, sw_o], dh)
            ssm_g["convw"][j], ssm_g["convb"][j] = dwt.T, dcb[0]
            ssm_g["dtb"][j], ssm_g["alog"][j], ssm_g["d"][j] = dbias[0, :heads], dalog[0, :heads], dds[0, :heads]
            ssm_g["norm"][j] = dnorm[0]
        dx, dxb, gl["mix"][i] = _rms_bwd("rms_bwd", xin, ln_mix[i][None, :] + tok, dh, dx)
        dx, dxb, gl["pre"][i] = ffn_bwd(3 * i, dx, dxb, sv_pre, ln_ffn_pre, i)

    small_full = [
        jnp.concatenate(gl["pre"], 0), jnp.concatenate(gl["mix"], 0), jnp.concatenate(gl["post"], 0),
        jnp.stack(gm_g["vnorm"]), jnp.stack(gm_g["ws"]), jnp.stack(gm_g["bs"]),
        jnp.stack(ssm_g["convw"]), jnp.stack(ssm_g["convb"]), jnp.stack(ssm_g["dtb"]), jnp.stack(ssm_g["alog"]),
        jnp.stack(ssm_g["d"]), jnp.stack(ssm_g["norm"]), dg_final[0],
    ]
    full_shapes = [tuple(a.shape) for a in small_full]
    packed = _pack(small_full)
    summed = _sum_devices("sum8_small", _all_gather_devices("ag8_small", packed), packed)
    (g_pre, g_mix, g_post, g_vn, g_ws, g_bs, g_cw, g_cb, g_dtb, g_al, g_d, g_nm, g_fin) = _unpack(summed, full_shapes)
    g_cw = lax.dynamic_slice_in_dim(g_cw, kchip * cq, cq, axis=1)
    g_cb = lax.dynamic_slice_in_dim(g_cb, kchip * cq, cq, axis=1)
    g_nm = lax.dynamic_slice_in_dim(g_nm, kchip * iq, iq, axis=1)
    sm_g = [g_pre, g_mix, g_post, g_vn, g_ws, g_bs, g_cw, g_cb, g_dtb, g_al, g_d, g_nm, g_fin]
    sm_w = [ln_ffn_pre, ln_mix, ln_ffn_post, gm_v_norm, gm_w_s, gm_b_s, ssm_conv_w, ssm_conv_b, ssm_dt_bias,
            ssm_a_log, ssm_d, ssm_norm, ln_final]
    sm_m = [m_ln_ffn_pre, m_ln_mix, m_ln_ffn_post, m_gm_v_norm, m_gm_w_s, m_gm_b_s, m_ssm_conv_w, m_ssm_conv_b,
            m_ssm_dt_bias, m_ssm_a_log, m_ssm_d, m_ssm_norm, m_ln_final]
    sm_v = [v_ln_ffn_pre, v_ln_mix, v_ln_ffn_post, v_gm_v_norm, v_gm_w_s, v_gm_b_s, v_ssm_conv_w, v_ssm_conv_b,
            v_ssm_dt_bias, v_ssm_a_log, v_ssm_d, v_ssm_norm, v_ln_final]
    sm_shapes = [tuple(a.shape) for a in sm_w]
    pk = [_pack(lst)[None] for lst in (sm_w, sm_m, sm_v)]
    sg_, sd_, snm_, snv_ = _adamw("adamw_small", pk[0], pk[1], pk[2], [[_pack(sm_g)]])
    small_out = [_unpack(t[0], sm_shapes) for t in (sg_, sd_, snm_, snv_)]
    small_names = ["ln_ffn_pre", "ln_mix", "ln_ffn_post", "gm_v_norm", "gm_w_s", "gm_b_s", "ssm_conv_w", "ssm_conv_b",
                   "ssm_dt_bias", "ssm_a_log", "ssm_d", "ssm_norm", "ln_final"]

    def big_update(tag, w, m, v, sub_ids, which, tok=None):
        L = w.shape[0]
        shp = w.shape
        w2, m2, v2 = (t.reshape(L, -1, shp[-1]) for t in (w, m, v))
        outs = _adamw("adamw_" + tag, w2, m2, v2, [[wfull[s][which]] for s in sub_ids], tok)
        return [o.reshape(shp) for o in outs]

    last = _reduce_start("0", deferred[0], sd_)
    tok = last[3][0:1, 0:1]
    wfull = [None] * nsub
    for s in range(1, nsub):
        wfull[s] = _join_finish(str(s), wgrad[s], sd_)
    pre_ids = [3 * i for i in range(depth)]
    post_ids = [3 * i + 2 for i in range(depth)]
    gm_ids = [3 * i + 1 for i in range(depth) if i % 2 == 0]
    ssm_ids = [3 * i + 1 for i in range(depth) if i % 2 == 1]
    big_out = {
        "ffn_post_w_in": big_update("ffn_in", ffn_post_w_in, m_ffn_post_w_in, v_ffn_post_w_in, post_ids, 0, tok),
        "ffn_post_w_out": big_update("ffn_out", ffn_post_w_out, m_ffn_post_w_out, v_ffn_post_w_out, post_ids, 1, tok),
        "gm_w_in": big_update("gm_in", gm_w_in, m_gm_w_in, v_gm_w_in, gm_ids, 0, tok),
        "gm_w_out": big_update("mix_out", gm_w_out, m_gm_w_out, v_gm_w_out, gm_ids, 1, tok),
        "ssm_w_in": big_update("ssm_in", ssm_w_in, m_ssm_w_in, v_ssm_w_in, ssm_ids, 0, tok),
        "ssm_w_out": big_update("mix_out", ssm_w_out, m_ssm_w_out, v_ssm_w_out, ssm_ids, 1, tok),
    }
    after = big_out["ffn_post_w_in"][1][0, 0:1, 0:1]
    for n in ("ffn_post_w_out", "gm_w_in", "gm_w_out", "ssm_w_in", "ssm_w_out"):
        after = after + big_out[n][1][0, 0:1, 0:1]
    wfull[0] = _join_finish("0", _reduce_finish("0", last, after), after)
    big_out["ffn_pre_w_in"] = big_update("ffn_in", ffn_pre_w_in, m_ffn_pre_w_in, v_ffn_pre_w_in, pre_ids, 0)
    big_out["ffn_pre_w_out"] = big_update("ffn_out", ffn_pre_w_out, m_ffn_pre_w_out, v_ffn_pre_w_out, pre_ids, 1)

    order = ["ln_ffn_pre", "ffn_pre_w_in", "ffn_pre_w_out", "ln_mix", "ln_ffn_post", "ffn_post_w_in", "ffn_post_w_out",
             "gm_w_in", "gm_v_norm", "gm_w_s", "gm_b_s", "gm_w_out", "ssm_w_in", "ssm_conv_w", "ssm_conv_b",
             "ssm_dt_bias", "ssm_a_log", "ssm_d", "ssm_norm", "ssm_w_out", "ln_final"]

    def pick(kind, n):
        if n in big_out:
            return big_out[n][kind]
        return small_out[kind][small_names.index(n)]

    outs = [loss, dx[None]]
    for kind in range(4):
        outs.extend(pick(kind, n) for n in order)
    return tuple(outs)
```

```python
import jax
import jax.numpy as jnp
from jax import lax
from jax.experimental import pallas as pl
from jax.experimental.pallas import tpu as pltpu

F32 = jnp.float32
BF16 = jnp.bfloat16
HIGHEST = lax.Precision.HIGHEST
MESH = pl.DeviceIdType.MESH

EPS = 1e-6
ADAM_LR, ADAM_B1, ADAM_B2, ADAM_EPS, ADAM_WD, ADAM_STEP = 0.001, 0.9, 0.999, 1e-08, 0.01, 10

LANES = 128
CHUNK = 128
SSM_STATE = 128
SSM_HEAD_DIM = 64
SSM_HPG = 8
SSM_CONV = 4
VMEM_LIMIT = 56 * 1024 * 1024
MM_TILE = 1408
MM_VMEM_BUDGET = 46 * 1024 * 1024

_ANY = pl.BlockSpec(memory_space=pl.ANY)


def _cp(*sem):
    return pltpu.CompilerParams(dimension_semantics=sem if sem else None, vmem_limit_bytes=VMEM_LIMIT)


def _tile(n, target, mult=LANES):
    best = None
    t = mult
    while t <= min(n, target):
        if n % t == 0:
            best = t
        t += mult
    return n if best is None else best


def _gcd(*v):
    import math
    g = 0
    for a in v:
        g = math.gcd(g, a)
    return g


def _sds(shape, dtype):
    return jax.ShapeDtypeStruct(tuple(shape), dtype)


def _ldims(shape):
    return tuple(shape) if len(shape) == 2 else (shape[1], shape[0] * shape[2])


def _colblock(shape):
    return None if len(shape) == 2 else shape[2]


def _mspec(shape, tr, tc, rc):
    if len(shape) == 2:
        return pl.BlockSpec((tr, tc), rc)
    per = shape[2] // tc

    def im(i, j, k):
        r, c = rc(i, j, k)
        return (c // per, r, c % per)

    return pl.BlockSpec((None, tr, tc), im)


def _matmul(name, a, b, mode, out_dtype, out_blocks=None, res=None, scale=1.0, tok=None):
    la, lb = _ldims(a.shape), _ldims(b.shape)
    if mode == "nn":
        (M, K), (K2, N) = la, lb
    elif mode == "nt":
        (M, K), (N, K2) = la, lb
    else:
        (K, M), (K2, N) = la, lb
    assert K == K2, (name, a.shape, b.shape, mode)
    ca, cb = _colblock(a.shape), _colblock(b.shape)
    out_shape = (M, N) if out_blocks is None else (out_blocks, M, N // out_blocks)
    co = _colblock(out_shape)
    m_c, n_c, k_c = [M], [N], [K]
    if ca is not None:
        (m_c if mode == "tn" else k_c).append(ca)
    if cb is not None:
        (k_c if mode == "nt" else n_c).append(cb)
    if co is not None:
        n_c.append(co)
    tm, tn = _tile(_gcd(*m_c), MM_TILE), _tile(_gcd(*n_c), MM_TILE)
    out_bytes = jnp.dtype(out_dtype).itemsize + (4 if res is not None else 0)
    kg = _gcd(*k_c)
    tk = LANES if kg % LANES == 0 else kg
    for cand in range(LANES, kg + 1, LANES):
        if kg % cand == 0 and 2 * (2 * cand * (tm + tn) + tm * tn * out_bytes) + 2 * 4 * tm * tn <= MM_VMEM_BUDGET:
            tk = cand
    nk = K // tk
    if mode == "tn":
        a_spec = _mspec(a.shape, tk, tm, lambda i, j, k: (k, i))
        dims = (((0,), (0,)), ((), ()))
    else:
        a_spec = _mspec(a.shape, tm, tk, lambda i, j, k: (i, k))
        dims = (((1,), (1,)), ((), ())) if mode == "nt" else (((1,), (0,)), ((), ()))
    if mode == "nt":
        b_spec = _mspec(b.shape, tn, tk, lambda i, j, k: (j, k))
    else:
        b_spec = _mspec(b.shape, tk, tn, lambda i, j, k: (k, j))
    o_spec = _mspec(out_shape, tm, tn, lambda i, j, k: (i, j))
    in_specs, args = [a_spec, b_spec], [a, b]
    if res is not None:
        in_specs.append(pl.BlockSpec((tm, tn), lambda i, j, k: (i, j)))
        args.append(res)
    if tok is not None:
        in_specs.append(pl.BlockSpec((1, 1), lambda i, j, k: (0, 0)))
        args.append(tok)
    n_in = len(args)

    def body(*refs):
        a_ref, b_ref = refs[0], refs[1]
        res_ref = refs[2] if res is not None else None
        tok_ref = refs[n_in - 1] if tok is not None else None
        o_ref = refs[n_in]

        def product():
            return lax.dot_general(a_ref[...].astype(BF16), b_ref[...].astype(BF16), dims, preferred_element_type=F32)

        def finish(r):
            if scale != 1.0:
                r = r * scale
            if res_ref is not None:
                r = res_ref[...] + r
            if tok_ref is not None:
                r = r + tok_ref[...]
            o_ref[...] = r.astype(o_ref.dtype)

        if nk == 1:
            finish(product())
            return
        acc_ref = refs[-1]
        kk = pl.program_id(2)

        @pl.when(kk == 0)
        def _():
            acc_ref[...] = product()

        @pl.when((kk > 0) & (kk < nk - 1))
        def _():
            acc_ref[...] += product()

        @pl.when(kk == nk - 1)
        def _():
            finish(acc_ref[...] + product())

    return pl.pallas_call(
        body, name=name, grid=(M // tm, N // tn, nk), in_specs=in_specs, out_specs=o_spec,
        out_shape=_sds(out_shape, out_dtype), scratch_shapes=[pltpu.VMEM((tm, tn), F32)] if nk > 1 else [],
        compiler_params=_cp("parallel", "parallel", "arbitrary"))(*args)


def _rows(i):
    return (i, 0)


def _row0(i):
    return (0, 0)


def _rms_fwd(name, x, g):
    T, D = x.shape
    tr = _tile(T, 256, 8)

    def body(x_ref, g_ref, o_ref):
        xv = x_ref[...]
        r = lax.rsqrt(jnp.mean(xv * xv, axis=-1, keepdims=True) + EPS)
        o_ref[...] = (xv * r * g_ref[...]).astype(o_ref.dtype)

    return pl.pallas_call(
        body, name=name, grid=(T // tr,),
        in_specs=[pl.BlockSpec((tr, D), _rows), pl.BlockSpec((1, D), _row0)],
        out_specs=pl.BlockSpec((tr, D), _rows), out_shape=_sds((T, D), BF16),
        compiler_params=_cp("parallel"))(x, g)


def _rms_bwd(name, x, g, dh, dres):
    T, D = x.shape
    tr = _tile(T, 256, 8)

    def body(x_ref, g_ref, dh_ref, dres_ref, dx_ref, dxb_ref, dg_ref):
        xv = x_ref[...]
        r = lax.rsqrt(jnp.mean(xv * xv, axis=-1, keepdims=True) + EPS)
        xh = xv * r
        dhv = dh_ref[...]
        dxh = dhv * g_ref[...]
        dx = dres_ref[...] + r * (dxh - xh * jnp.mean(dxh * xh, axis=-1, keepdims=True))
        dx_ref[...] = dx
        dxb_ref[...] = dx.astype(BF16)

        @pl.when(pl.program_id(0) == 0)
        def _():
            dg_ref[...] = jnp.zeros_like(dg_ref)

        dg_ref[...] += jnp.sum(dhv * xh, axis=0, keepdims=True)

    return pl.pallas_call(
        body, name=name, grid=(T // tr,),
        in_specs=[pl.BlockSpec((tr, D), _rows), pl.BlockSpec((1, D), _row0),
                  pl.BlockSpec((tr, D), _rows), pl.BlockSpec((tr, D), _rows)],
        out_specs=[pl.BlockSpec((tr, D), _rows), pl.BlockSpec((tr, D), _rows), pl.BlockSpec((1, D), _row0)],
        out_shape=[_sds((T, D), F32), _sds((T, D), BF16), _sds((1, D), F32)],
        compiler_params=_cp("arbitrary"))(x, g, dh, dres)


def _sigmoid(v):
    return 1.0 / (1.0 + jnp.exp(-v))


FFN_ROWS = 256


def _ffn_in_swiglu(name, h, wi):
    T, D = h.shape
    nb, _, cb = wi.shape
    F = nb * cb // 2
    tm, tn = _tile(T, FFN_ROWS, 16), _tile(cb, MM_TILE)
    per = cb // tn

    def body(h_ref, wg_ref, wu_ref, a_ref, s_ref):
        hv = h_ref[...]
        gate = jnp.dot(hv, wg_ref[...], preferred_element_type=F32)
        up = jnp.dot(hv, wu_ref[...], preferred_element_type=F32)
        a_ref[0] = gate.astype(BF16)
        a_ref[1] = up.astype(BF16)
        s_ref[...] = (gate * _sigmoid(gate) * up).astype(BF16)

    return pl.pallas_call(
        body, name=name, grid=(F // tn, T // tm),
        in_specs=[pl.BlockSpec((tm, D), lambda j, i: (i, 0)),
                  pl.BlockSpec((None, D, tn), lambda j, i: (j // per, 0, j % per)),
                  pl.BlockSpec((None, D, tn), lambda j, i: (nb // 2 + j // per, 0, j % per))],
        out_specs=[pl.BlockSpec((2, tm, tn), lambda j, i: (0, i, j)), pl.BlockSpec((tm, tn), lambda j, i: (i, j))],
        out_shape=[_sds((2, T, F), BF16), _sds((T, F), BF16)],
        compiler_params=_cp("parallel", "parallel"))(h, wi, wi)


def _ffn_ds_swiglu(name, dxb, wo, a, scale):
    T, D = dxb.shape
    F = wo.shape[0]
    tm, tn = _tile(T, FFN_ROWS, 16), _tile(F, MM_TILE)

    def body(dx_ref, w_ref, a_ref, o_ref):
        ds = lax.dot_general(dx_ref[...], w_ref[...], (((1,), (1,)), ((), ())), preferred_element_type=F32) * scale
        gate = a_ref[0].astype(F32)
        up = a_ref[1].astype(F32)
        sg = _sigmoid(gate)
        o_ref[0] = (ds * up * sg * (1.0 + gate * (1.0 - sg))).astype(BF16)
        o_ref[1] = (ds * gate * sg).astype(BF16)

    return pl.pallas_call(
        body, name=name, grid=(F // tn, T // tm),
        in_specs=[pl.BlockSpec((tm, D), lambda j, i: (i, 0)), pl.BlockSpec((tn, D), lambda j, i: (j, 0)),
                  pl.BlockSpec((2, tm, tn), lambda j, i: (0, i, j))],
        out_specs=pl.BlockSpec((2, tm, tn), lambda j, i: (0, i, j)), out_shape=_sds((2, T, F), BF16),
        compiler_params=_cp("parallel", "parallel"))(dxb, wo, a)


def _loss_head(name, x, g, tgt):
    T, D = x.shape
    tr = _tile(T, 256, 8)

    def body(x_ref, g_ref, t_ref, loss_ref, dx_ref, dxb_ref, dg_ref):
        xv = x_ref[...]
        gv = g_ref[...]
        r = lax.rsqrt(jnp.mean(xv * xv, axis=-1, keepdims=True) + EPS)
        xh = xv * r
        err = xh * gv - t_ref[...]
        dy = err * (1.0 / D)
        dxh = dy * gv
        dx = r * (dxh - xh * jnp.mean(dxh * xh, axis=-1, keepdims=True))
        dx_ref[...] = dx
        dxb_ref[...] = dx.astype(BF16)

        @pl.when(pl.program_id(0) == 0)
        def _():
            dg_ref[...] = jnp.zeros_like(dg_ref)
            loss_ref[...] = jnp.zeros_like(loss_ref)

        dg_ref[...] += jnp.sum(dy * xh, axis=0, keepdims=True)
        part = jnp.sum(jnp.sum(err * err, axis=-1, keepdims=True), axis=0, keepdims=True) * (0.5 / D)
        loss_ref[...] += part

    return pl.pallas_call(
        body, name=name, grid=(T // tr,),
        in_specs=[pl.BlockSpec((tr, D), _rows), pl.BlockSpec((1, D), _row0), pl.BlockSpec((tr, D), _rows)],
        out_specs=[pl.BlockSpec((8, LANES), _row0), pl.BlockSpec((tr, D), _rows), pl.BlockSpec((tr, D), _rows),
                   pl.BlockSpec((1, D), _row0)],
        out_shape=[_sds((8, LANES), F32), _sds((T, D), F32), _sds((T, D), BF16), _sds((1, D), F32)],
        compiler_params=_cp("arbitrary"))(x, g, tgt)


_SQRT_HALF = 0.7071067811865476
_INV_SQRT_2PI = 0.3989422804014327


def _gelu(v):
    return 0.5 * v * (1.0 + lax.erf(v * _SQRT_HALF))


def _gelu_grad(v):
    return 0.5 * (1.0 + lax.erf(v * _SQRT_HALF)) + v * _INV_SQRT_2PI * jnp.exp(-0.5 * v * v)


def _group_expand(rows, width, gd):
    gi = lax.broadcasted_iota(jnp.int32, (rows, width), 0)
    fi = lax.broadcasted_iota(jnp.int32, (rows, width), 1)
    return ((fi >= gi * gd) & (fi < (gi + 1) * gd)).astype(F32)


def _gm_act_fwd(name, zpre, vnorm):
    _, T, GI = zpre.shape
    tr = _tile(T, 128, 8)

    def body(z_ref, g_ref, u_ref, v_ref):
        u_ref[...] = _gelu(z_ref[0].astype(F32)).astype(BF16)
        zv = _gelu(z_ref[1].astype(F32))
        r = lax.rsqrt(jnp.mean(zv * zv, axis=-1, keepdims=True) + EPS)
        v_ref[...] = (zv * r * g_ref[...]).astype(BF16)

    return pl.pallas_call(
        body, name=name, grid=(T // tr,),
        in_specs=[pl.BlockSpec((2, tr, GI), lambda i: (0, i, 0)), pl.BlockSpec((1, GI), _row0)],
        out_specs=[pl.BlockSpec((tr, GI), _rows), pl.BlockSpec((tr, GI), _rows)],
        out_shape=[_sds((T, GI), BF16), _sds((T, GI), BF16)], compiler_params=_cp("parallel"))(zpre, vnorm)


def _gm_act_bwd(name, zpre, du, dvn, vnorm):
    _, T, GI = zpre.shape
    tr = _tile(T, 128, 8)

    def body(z_ref, du_ref, dvn_ref, g_ref, dz_ref, dg_ref):
        xu = z_ref[0].astype(F32)
        xv = z_ref[1].astype(F32)
        zv = _gelu(xv)
        r = lax.rsqrt(jnp.mean(zv * zv, axis=-1, keepdims=True) + EPS)
        xh = zv * r
        dv = dvn_ref[...]

        @pl.when(pl.program_id(0) == 0)
        def _():
            dg_ref[...] = jnp.zeros_like(dg_ref)

        dg_ref[...] += jnp.sum(dv * xh, axis=0, keepdims=True)
        dxh = dv * g_ref[...]
        dzv = r * (dxh - xh * jnp.mean(dxh * xh, axis=-1, keepdims=True))
        dz_ref[0] = (du_ref[...] * _gelu_grad(xu)).astype(BF16)
        dz_ref[1] = (dzv * _gelu_grad(xv)).astype(BF16)

    return pl.pallas_call(
        body, name=name, grid=(T // tr,),
        in_specs=[pl.BlockSpec((2, tr, GI), lambda i: (0, i, 0)), pl.BlockSpec((tr, GI), _rows),
                  pl.BlockSpec((tr, GI), _rows), pl.BlockSpec((1, GI), _row0)],
        out_specs=[pl.BlockSpec((2, tr, GI), lambda i: (0, i, 0)), pl.BlockSpec((1, GI), _row0)],
        out_shape=[_sds((2, T, GI), BF16), _sds((1, GI), F32)], compiler_params=_cp("arbitrary"))(zpre, du, dvn, vnorm)


def _causal_mask():
    r = lax.broadcasted_iota(jnp.int32, (CHUNK, CHUNK), 0)
    c = lax.broadcasted_iota(jnp.int32, (CHUNK, CHUNK), 1)
    return r >= c


def _gm_spatial_fwd(name, u, vn, ws, bt):
    T, GI = u.shape
    G = ws.shape[0]
    gd = GI // G

    def body(u_ref, v_ref, ws_ref, bt_ref, o_ref, bias_scr):
        @pl.when(pl.program_id(0) == 0)
        def _():
            bias_scr[...] = jnp.dot(bt_ref[...], _group_expand(LANES, GI, gd), precision=HIGHEST,
                                    preferred_element_type=F32)

        causal = _causal_mask()
        for g in range(G):
            sl = slice(g * gd, (g + 1) * gd)
            wc = jnp.where(causal, ws_ref[g], 0.0).astype(BF16)
            mixed = jnp.dot(wc, v_ref[:, sl], preferred_element_type=F32) + bias_scr[:, sl]
            o_ref[:, sl] = (u_ref[:, sl].astype(F32) * mixed).astype(o_ref.dtype)

    return pl.pallas_call(
        body, name=name, grid=(T // CHUNK,),
        in_specs=[pl.BlockSpec((CHUNK, GI), _rows), pl.BlockSpec((CHUNK, GI), _rows),
                  pl.BlockSpec((G, CHUNK, CHUNK), lambda i: (0, 0, 0)), pl.BlockSpec((CHUNK, LANES), _row0)],
        out_specs=pl.BlockSpec((CHUNK, GI), _rows), out_shape=_sds((T, GI), BF16),
        scratch_shapes=[pltpu.VMEM((CHUNK, GI), F32)], compiler_params=_cp("arbitrary"))(u, vn, ws, bt)


def _gm_spatial_bwd(name, dgated, u, vn, ws, bt):
    T, GI = u.shape
    G = ws.shape[0]
    gd = GI // G
    nc = T // CHUNK

    def body(dg_ref, u_ref, v_ref, ws_ref, bt_ref, du_ref, dv_ref, dws_ref, dbt_ref, bias_scr, dm_scr):
        step = pl.program_id(0)

        @pl.when(step == 0)
        def _():
            bias_scr[...] = jnp.dot(bt_ref[...], _group_expand(LANES, GI, gd), precision=HIGHEST,
                                    preferred_element_type=F32)
            dm_scr[...] = jnp.zeros_like(dm_scr)
            dws_ref[...] = jnp.zeros_like(dws_ref)

        causal = _causal_mask()
        for g in range(G):
            sl = slice(g * gd, (g + 1) * gd)
            wc = jnp.where(causal, ws_ref[g], 0.0).astype(BF16)
            vv = v_ref[:, sl]
            dgv = dg_ref[:, sl].astype(F32)
            mixed = jnp.dot(wc, vv, preferred_element_type=F32) + bias_scr[:, sl]
            du_ref[:, sl] = dgv * mixed
            dm = dgv * u_ref[:, sl].astype(F32)
            dmb = dm.astype(BF16)
            dv_ref[:, sl] = lax.dot_general(wc, dmb, (((0,), (0,)), ((), ())), preferred_element_type=F32)
            dw = lax.dot_general(dmb, vv, (((1,), (1,)), ((), ())), preferred_element_type=F32)
            dws_ref[g] += jnp.where(causal, dw, 0.0)
            dm_scr[:, sl] += dm

        @pl.when(step == nc - 1)
        def _():
            dbt_ref[...] = lax.dot_general(dm_scr[...], _group_expand(LANES, GI, gd), (((1,), (1,)), ((), ())),
                                           precision=HIGHEST, preferred_element_type=F32)

    return pl.pallas_call(
        body, name=name, grid=(nc,),
        in_specs=[pl.BlockSpec((CHUNK, GI), _rows), pl.BlockSpec((CHUNK, GI), _rows), pl.BlockSpec((CHUNK, GI), _rows),
                  pl.BlockSpec((G, CHUNK, CHUNK), lambda i: (0, 0, 0)), pl.BlockSpec((CHUNK, LANES), _row0)],
        out_specs=[pl.BlockSpec((CHUNK, GI), _rows), pl.BlockSpec((CHUNK, GI), _rows),
                   pl.BlockSpec((G, CHUNK, CHUNK), lambda i: (0, 0, 0)), pl.BlockSpec((CHUNK, LANES), _row0)],
        out_shape=[_sds((T, GI), F32), _sds((T, GI), F32), _sds((G, CHUNK, CHUNK), F32), _sds((CHUNK, LANES), F32)],
        scratch_shapes=[pltpu.VMEM((CHUNK, GI), F32), pltpu.VMEM((CHUNK, GI), F32)],
        compiler_params=_cp("arbitrary"))(dgated, u, vn, ws, bt)


def _conv_taps(xv, w_ref, b_ref):
    rows = lax.broadcasted_iota(jnp.int32, xv.shape, 0)
    acc = xv * w_ref[pl.ds(SSM_CONV - 1, 1), :] + b_ref[...]
    for k in range(1, SSM_CONV):
        sh = jnp.where(rows >= k, pltpu.roll(xv, k, 0), 0.0)
        acc = acc + sh * w_ref[pl.ds(SSM_CONV - 1 - k, 1), :]
    return acc


def _ssd_conv_fwd(name, proj, wt, b, inner, cd):
    T = proj.shape[0]
    tc = _tile(_gcd(inner, cd), 512)
    off = inner // tc

    def body(x_ref, w_ref, b_ref, o_ref):
        pre = _conv_taps(x_ref[...], w_ref, b_ref)
        o_ref[...] = pre * _sigmoid(pre)

    return pl.pallas_call(
        body, name=name, grid=(cd // tc,),
        in_specs=[pl.BlockSpec((T, tc), lambda j: (0, off + j)), pl.BlockSpec((SSM_CONV, tc), lambda j: (0, j)),
                  pl.BlockSpec((1, tc), lambda j: (0, j))],
        out_specs=pl.BlockSpec((T, tc), lambda j: (0, j)), out_shape=_sds((T, cd), F32),
        compiler_params=_cp("parallel"))(proj, wt, b)


def _ssd_conv_bwd(name, dact, proj, wt, b, inner, cd):
    T = proj.shape[0]
    tc = _tile(_gcd(inner, cd), 512)
    off = inner // tc

    def body(da_ref, x_ref, w_ref, b_ref, dx_ref, dw_ref, db_ref):
        xv = x_ref[...]
        pre = _conv_taps(xv, w_ref, b_ref)
        sg = _sigmoid(pre)
        dpre = da_ref[...] * sg * (1.0 + pre * (1.0 - sg))
        rows = lax.broadcasted_iota(jnp.int32, xv.shape, 0)
        db_ref[...] = jnp.sum(dpre, axis=0, keepdims=True)
        dx = dpre * w_ref[pl.ds(SSM_CONV - 1, 1), :]
        dw_ref[pl.ds(SSM_CONV - 1, 1), :] = jnp.sum(dpre * xv, axis=0, keepdims=True)
        for k in range(1, SSM_CONV):
            sh = jnp.where(rows >= k, pltpu.roll(xv, k, 0), 0.0)
            dw_ref[pl.ds(SSM_CONV - 1 - k, 1), :] = jnp.sum(dpre * sh, axis=0, keepdims=True)
            fw = jnp.where(rows < T - k, pltpu.roll(dpre, T - k, 0), 0.0)
            dx = dx + fw * w_ref[pl.ds(SSM_CONV - 1 - k, 1), :]
        dx_ref[...] = dx.astype(BF16)

    return pl.pallas_call(
        body, name=name, grid=(cd // tc,),
        in_specs=[pl.BlockSpec((T, tc), lambda j: (0, j)), pl.BlockSpec((T, tc), lambda j: (0, off + j)),
                  pl.BlockSpec((SSM_CONV, tc), lambda j: (0, j)), pl.BlockSpec((1, tc), lambda j: (0, j))],
        out_specs=[pl.BlockSpec((T, tc), lambda j: (0, j)), pl.BlockSpec((SSM_CONV, tc), lambda j: (0, j)),
                   pl.BlockSpec((1, tc), lambda j: (0, j))],
        out_shape=[_sds((T, cd), BF16), _sds((SSM_CONV, cd), F32), _sds((1, cd), F32)],
        compiler_params=_cp("parallel"))(dact, proj, wt, b)


def _softplus(v):
    return jnp.maximum(v, 0.0) + jnp.log(1.0 + jnp.exp(-jnp.abs(v)))


def _tri(lower):
    r = lax.broadcasted_iota(jnp.int32, (CHUNK, CHUNK), 0)
    c = lax.broadcasted_iota(jnp.int32, (CHUNK, CHUNK), 1)
    return ((c <= r) if lower else (c >= r)).astype(F32)


def _ssd_dt_fwd(name, proj, bias, alog, dtcol, heads):
    T = proj.shape[0]

    def body(dt_ref, b_ref, al_ref, dtp_ref, ac_ref):
        live = lax.broadcasted_iota(jnp.int32, (CHUNK, LANES), 1) < heads
        dtp = jnp.where(live, _softplus(dt_ref[...] + b_ref[...]), 0.0)
        da = dtp * (-jnp.exp(al_ref[...]))
        dtp_ref[...] = dtp
        ac_ref[...] = jnp.dot(_tri(True), da, precision=HIGHEST, preferred_element_type=F32)

    return pl.pallas_call(
        body, name=name, grid=(T // CHUNK,),
        in_specs=[pl.BlockSpec((CHUNK, LANES), lambda i: (i, dtcol)), pl.BlockSpec((1, LANES), _row0),
                  pl.BlockSpec((1, LANES), _row0)],
        out_specs=[pl.BlockSpec((CHUNK, LANES), _rows), pl.BlockSpec((CHUNK, LANES), _rows)],
        out_shape=[_sds((T, LANES), F32), _sds((T, LANES), F32)], compiler_params=_cp("parallel"))(proj, bias, alog)


def _ssd_dt_bwd(name, ddtp_g, dacum_g, dd_g, proj, bias, alog, dtp, dtcol, heads):
    T = proj.shape[0]
    G = ddtp_g.shape[0]

    def body(ddtp_ref, dac_ref, dd_ref, dt_ref, b_ref, al_ref, dtp_ref, ddt_ref, db_ref, dal_ref, dds_ref, da_scr):
        step = pl.program_id(0)

        @pl.when(step == 0)
        def _():
            db_ref[...] = jnp.zeros_like(db_ref)
            da_scr[...] = jnp.zeros_like(da_scr)
            dds_ref[...] = jnp.sum(dd_ref[...], axis=0)

        live = lax.broadcasted_iota(jnp.int32, (CHUNK, LANES), 1) < heads
        a = -jnp.exp(al_ref[...])
        dac = jnp.sum(dac_ref[...], axis=0)
        dda = jnp.dot(_tri(False), dac, precision=HIGHEST, preferred_element_type=F32)
        dtp_v = dtp_ref[...]
        ddtp = jnp.sum(ddtp_ref[...], axis=0) + dda * a
        da_scr[...] += jnp.sum(dda * dtp_v, axis=0, keepdims=True)
        ddt = jnp.where(live, ddtp * _sigmoid(dt_ref[...] + b_ref[...]), 0.0)
        ddt_ref[...] = ddt.astype(BF16)
        db_ref[...] += jnp.sum(ddt, axis=0, keepdims=True)
        dal_ref[...] = da_scr[...] * a

    return pl.pallas_call(
        body, name=name, grid=(T // CHUNK,),
        in_specs=[pl.BlockSpec((G, CHUNK, LANES), lambda i: (0, i, 0)), pl.BlockSpec((G, CHUNK, LANES), lambda i: (0, i, 0)),
                  pl.BlockSpec((G, 8, LANES), lambda i: (0, 0, 0)),
                  pl.BlockSpec((CHUNK, LANES), lambda i: (i, dtcol)), pl.BlockSpec((1, LANES), _row0),
                  pl.BlockSpec((1, LANES), _row0), pl.BlockSpec((CHUNK, LANES), _rows)],
        out_specs=[pl.BlockSpec((CHUNK, LANES), _rows), pl.BlockSpec((8, LANES), _row0), pl.BlockSpec((8, LANES), _row0),
                   pl.BlockSpec((8, LANES), _row0)],
        out_shape=[_sds((T, LANES), BF16), _sds((8, LANES), F32), _sds((8, LANES), F32), _sds((8, LANES), F32)],
        scratch_shapes=[pltpu.VMEM((8, LANES), F32)],
        compiler_params=_cp("arbitrary"))(ddtp_g, dacum_g, dd_g, proj, bias, alog, dtp)


def _head_expand(g, gw):
    hi = lax.broadcasted_iota(jnp.int32, (LANES, gw), 0) - g * SSM_HPG
    fi = lax.broadcasted_iota(jnp.int32, (LANES, gw), 1)
    return ((fi >= hi * SSM_HEAD_DIM) & (fi < (hi + 1) * SSM_HEAD_DIM)).astype(F32)


def _dot(a, b, dims, exact=False):
    if exact:
        return lax.dot_general(a, b, (dims, ((), ())), precision=HIGHEST, preferred_element_type=F32)
    return lax.dot_general(a.astype(BF16), b.astype(BF16), (dims, ((), ())), preferred_element_type=F32)


_NN = ((1,), (0,))
_NT = ((1,), (1,))
_TN = ((0,), (0,))


def _pair_decay(g, q, acum, acum_t_ref, causal):
    lane = lax.broadcasted_iota(jnp.int32, (CHUNK, LANES), 1)
    out = []
    for e in range(2):
        h = g * SSM_HPG + 2 * q + e
        acol = jnp.sum(jnp.where(lane == h, acum, 0.0), axis=1, keepdims=True)
        arow = acum_t_ref[pl.ds(h, 1), :]
        out.append(jnp.exp(jnp.where(causal, acol - arow, -1e30)))
    return out


def _ssd_core_fwd(name, act, dtp, acum, dexp, inner, groups):
    T = act.shape[0]
    nc = T // CHUNK
    gw = SSM_HPG * SSM_HEAD_DIM
    npair = gw // LANES
    bcol, ccol = inner // SSM_STATE, inner // SSM_STATE + groups

    def body(x_ref, b_ref, c_ref, dtp_ref, ac_ref, d_ref, y_ref, sp_ref, st_scr, act_scr, ae_scr):
        g = pl.program_id(0)

        @pl.when(pl.program_id(1) == 0)
        def _():
            st_scr[...] = jnp.zeros_like(st_scr)

        st = st_scr[...]
        sp_ref[...] = st
        e = _head_expand(g, gw)
        acum = ac_ref[...]
        ae = _dot(acum, e, _NN, exact=True)
        dte = _dot(dtp_ref[...], e, _NN, exact=True)
        ae_scr[...] = ae
        act_scr[...] = acum.T
        xv = x_ref[...]
        xdt = xv * dte
        bm, cm = b_ref[...], c_ref[...]
        cb = _dot(cm, bm, _NT)
        causal = _causal_mask()
        lane = lax.broadcasted_iota(jnp.int32, (CHUNK, LANES), 1)
        yoff = _dot(cm, st, _NN) * jnp.exp(ae)
        skip = xv * d_ref[...]
        for q in range(npair):
            sl = slice(q * LANES, (q + 1) * LANES)
            dec = _pair_decay(g, q, acum, act_scr, causal)
            x2 = xdt[:, sl]
            xa = jnp.where(lane < SSM_HEAD_DIM, x2, 0.0)
            yd = _dot(dec[0] * cb, xa, _NN) + _dot(dec[1] * cb, x2 - xa, _NN)
            y_ref[:, sl] = yd + yoff[:, sl] + skip[:, sl]
        alast = ae_scr[pl.ds(CHUNK - 1, 1), :]
        z = xdt * jnp.exp(alast - ae)
        st_scr[...] = st * jnp.exp(alast) + _dot(bm, z, _TN)

    return pl.pallas_call(
        body, name=name, grid=(groups, nc),
        in_specs=[pl.BlockSpec((CHUNK, gw), lambda g, c: (c, g)),
                  pl.BlockSpec((CHUNK, SSM_STATE), lambda g, c: (c, bcol + g)),
                  pl.BlockSpec((CHUNK, SSM_STATE), lambda g, c: (c, ccol + g)),
                  pl.BlockSpec((CHUNK, LANES), lambda g, c: (c, 0)), pl.BlockSpec((CHUNK, LANES), lambda g, c: (c, 0)),
                  pl.BlockSpec((1, gw), lambda g, c: (0, g))],
        out_specs=[pl.BlockSpec((CHUNK, gw), lambda g, c: (c, g)),
                   pl.BlockSpec((None, SSM_STATE, gw), lambda g, c: (c, 0, g))],
        out_shape=[_sds((T, inner), F32), _sds((nc, SSM_STATE, inner), F32)],
        scratch_shapes=[pltpu.VMEM((SSM_STATE, gw), F32), pltpu.VMEM((CHUNK, LANES), F32), pltpu.VMEM((CHUNK, gw), F32)],
        compiler_params=_cp("arbitrary", "arbitrary"))(act, act, act, dtp, acum, dexp)


def _ssd_core_bwd(name, dy, act, dtp, acum, dexp, sprev, inner, groups):
    T = act.shape[0]
    nc = T // CHUNK
    gw = SSM_HPG * SSM_HEAD_DIM
    npair = gw // LANES
    bcol, ccol = inner // SSM_STATE, inner // SSM_STATE + groups

    def rc(g, c):
        return nc - 1 - c

    def body(dy_ref, x_ref, b_ref, c_ref, dtp_ref, ac_ref, d_ref, sp_ref,
             dx_ref, db_ref, dc_ref, ddtp_ref, dac_ref, dd_ref,
             dst_scr, act_scr, ae_scr, dxdt_scr, dd_scr, dact_scr):
        g = pl.program_id(0)
        step = pl.program_id(1)

        @pl.when(step == 0)
        def _():
            dst_scr[...] = jnp.zeros_like(dst_scr)
            dd_scr[...] = jnp.zeros_like(dd_scr)

        dst = dst_scr[...]
        sp = sp_ref[...]
        e = _head_expand(g, gw)
        acum = ac_ref[...]
        ae = _dot(acum, e, _NN, exact=True)
        dte = _dot(dtp_ref[...], e, _NN, exact=True)
        ae_scr[...] = ae
        act_scr[...] = acum.T
        alast = ae_scr[pl.ds(CHUNK - 1, 1), :]
        xv = x_ref[...]
        xdt = xv * dte
        bm, cm = b_ref[...], c_ref[...]
        dyv = dy_ref[...]
        cb = _dot(cm, bm, _NT)
        causal = _causal_mask()
        lane = lax.broadcasted_iota(jnp.int32, (CHUNK, LANES), 1)
        ea = jnp.exp(ae)
        cde = jnp.exp(alast)
        w = jnp.exp(alast - ae)
        z = xdt * w

        dd_scr[...] += jnp.sum(dyv * xv, axis=0, keepdims=True)
        qm = _dot(cm, sp, _NN)
        dq = dyv * ea
        dae = dq * qm
        dc = _dot(dq, sp, _NT)
        dsp = _dot(cm, dq, _TN) + dst * cde
        dal = jnp.sum(dst * sp, axis=0, keepdims=True) * cde
        db = _dot(z, dst, _NT)
        dz = _dot(bm, dst, _NN)
        gw_ = dz * z
        dae = dae - gw_
        dal = dal + jnp.sum(gw_, axis=0, keepdims=True)
        dxdt_scr[...] = dz * w
        dcb = jnp.zeros((CHUNK, CHUNK), F32)
        dacol = jnp.zeros((CHUNK, LANES), F32)
        dact_scr[...] = jnp.zeros_like(dact_scr)
        sub = lax.broadcasted_iota(jnp.int32, (CHUNK, LANES), 0)
        for q in range(npair):
            sl = slice(q * LANES, (q + 1) * LANES)
            dec = _pair_decay(g, q, acum, act_scr, causal)
            x2, dy2 = xdt[:, sl], dyv[:, sl]
            xs_ = (jnp.where(lane < SSM_HEAD_DIM, x2, 0.0),)
            xs_ = xs_ + (x2 - xs_[0],)
            dys = (jnp.where(lane < SSM_HEAD_DIM, dy2, 0.0),)
            dys = dys + (dy2 - dys[0],)
            dx2 = jnp.zeros((CHUNK, LANES), F32)
            for hh in range(2):
                h = g * SSM_HPG + 2 * q + hh
                m = dec[hh] * cb
                dm = _dot(dys[hh], xs_[hh], _NT)
                dx2 = dx2 + _dot(m, dys[hh], _TN)
                dcb = dcb + dm * dec[hh]
                r = dm * m
                dacol = dacol + jnp.where(lane == h, jnp.sum(r, axis=1, keepdims=True), 0.0)
                dact_scr[...] -= jnp.where(sub == h, jnp.sum(r, axis=0, keepdims=True), 0.0)
            dxdt_scr[:, sl] += dx2
        dc = dc + _dot(dcb, bm, _NN)
        db = db + _dot(dcb, cm, _TN)
        dxdt = dxdt_scr[...]
        dx_ref[...] = dyv * d_ref[...] + dxdt * dte
        db_ref[...] = db
        dc_ref[...] = dc
        ddtp_ref[...] = _dot(dxdt * xv, e, _NT, exact=True)
        dal_h = _dot(jnp.broadcast_to(dal, (8, gw)), e, _NT, exact=True)
        dal_row = jnp.max(dal_h, axis=0, keepdims=True)
        dac = _dot(dae, e, _NT, exact=True) + dacol + dact_scr[...].T
        dac_ref[...] = dac + jnp.where(sub == CHUNK - 1, dal_row, 0.0)
        dst_scr[...] = dsp

        @pl.when(step == nc - 1)
        def _():
            dd_ref[...] = _dot(jnp.broadcast_to(dd_scr[...], (8, gw)), e, _NT, exact=True)

    return pl.pallas_call(
        body, name=name, grid=(groups, nc),
        in_specs=[pl.BlockSpec((CHUNK, gw), lambda g, c: (rc(g, c), g)),
                  pl.BlockSpec((CHUNK, gw), lambda g, c: (rc(g, c), g)),
                  pl.BlockSpec((CHUNK, SSM_STATE), lambda g, c: (rc(g, c), bcol + g)),
                  pl.BlockSpec((CHUNK, SSM_STATE), lambda g, c: (rc(g, c), ccol + g)),
                  pl.BlockSpec((CHUNK, LANES), lambda g, c: (rc(g, c), 0)),
                  pl.BlockSpec((CHUNK, LANES), lambda g, c: (rc(g, c), 0)),
                  pl.BlockSpec((1, gw), lambda g, c: (0, g)),
                  pl.BlockSpec((None, SSM_STATE, gw), lambda g, c: (rc(g, c), 0, g))],
        out_specs=[pl.BlockSpec((CHUNK, gw), lambda g, c: (rc(g, c), g)),
                   pl.BlockSpec((CHUNK, SSM_STATE), lambda g, c: (rc(g, c), g)),
                   pl.BlockSpec((CHUNK, SSM_STATE), lambda g, c: (rc(g, c), g)),
                   pl.BlockSpec((None, CHUNK, LANES), lambda g, c: (g, rc(g, c), 0)),
                   pl.BlockSpec((None, CHUNK, LANES), lambda g, c: (g, rc(g, c), 0)),
                   pl.BlockSpec((None, 8, LANES), lambda g, c: (g, 0, 0))],
        out_shape=[_sds((T, inner), F32), _sds((T, groups * SSM_STATE), F32), _sds((T, groups * SSM_STATE), F32),
                   _sds((groups, T, LANES), F32), _sds((groups, T, LANES), F32), _sds((groups, 8, LANES), F32)],
        scratch_shapes=[pltpu.VMEM((SSM_STATE, gw), F32), pltpu.VMEM((CHUNK, LANES), F32), pltpu.VMEM((CHUNK, gw), F32),
                        pltpu.VMEM((CHUNK, gw), F32), pltpu.VMEM((1, gw), F32), pltpu.VMEM((CHUNK, LANES), F32)],
        compiler_params=_cp("arbitrary", "arbitrary"))(dy, act, act, act, dtp, acum, dexp, sprev)


def _ssd_post_fwd(name, y, proj, ng, inner, groups):
    T = y.shape[0]
    tr = _tile(T, 128, 8)
    gs = inner // groups

    def body(y_ref, z_ref, g_ref, o_ref):
        zv = z_ref[...]
        gy = y_ref[...] * (zv * _sigmoid(zv))
        for k in range(groups):
            sl = slice(k * gs, (k + 1) * gs)
            seg = gy[:, sl]
            r = lax.rsqrt(jnp.mean(seg * seg, axis=-1, keepdims=True) + EPS)
            o_ref[:, sl] = (seg * r * g_ref[:, sl]).astype(BF16)

    return pl.pallas_call(
        body, name=name, grid=(T // tr,),
        in_specs=[pl.BlockSpec((tr, inner), _rows), pl.BlockSpec((tr, inner), _rows), pl.BlockSpec((1, inner), _row0)],
        out_specs=pl.BlockSpec((tr, inner), _rows), out_shape=_sds((T, inner), BF16),
        compiler_params=_cp("parallel"))(y, proj, ng)


def _ssd_post_bwd(name, dyn, y, proj, ng, inner, groups):
    T = y.shape[0]
    tr = _tile(T, 128, 8)
    gs = inner // groups

    def body(dyn_ref, y_ref, z_ref, g_ref, dy_ref, dz_ref, dg_ref):
        @pl.when(pl.program_id(0) == 0)
        def _():
            dg_ref[...] = jnp.zeros_like(dg_ref)

        zv = z_ref[...]
        sg = _sigmoid(zv)
        sz = zv * sg
        yv = y_ref[...]
        gy = yv * sz
        dv = dyn_ref[...]
        for k in range(groups):
            sl = slice(k * gs, (k + 1) * gs)
            seg = gy[:, sl]
            r = lax.rsqrt(jnp.mean(seg * seg, axis=-1, keepdims=True) + EPS)
            xh = seg * r
            d = dv[:, sl]
            dg_ref[:, sl] += jnp.sum(d * xh, axis=0, keepdims=True)
            dxh = d * g_ref[:, sl]
            dgy = r * (dxh - xh * jnp.mean(dxh * xh, axis=-1, keepdims=True))
            dy_ref[:, sl] = dgy * sz[:, sl]
            dz_ref[:, sl] = (dgy * yv[:, sl] * (sg[:, sl] * (1.0 + zv[:, sl] * (1.0 - sg[:, sl])))).astype(BF16)

    return pl.pallas_call(
        body, name=name, grid=(T // tr,),
        in_specs=[pl.BlockSpec((tr, inner), _rows), pl.BlockSpec((tr, inner), _rows), pl.BlockSpec((tr, inner), _rows),
                  pl.BlockSpec((1, inner), _row0)],
        out_specs=[pl.BlockSpec((tr, inner), _rows), pl.BlockSpec((tr, inner), _rows), pl.BlockSpec((1, inner), _row0)],
        out_shape=[_sds((T, inner), F32), _sds((T, inner), BF16), _sds((1, inner), F32)],
        compiler_params=_cp("arbitrary"))(dyn, y, proj, ng)


def _place():
    return lax.axis_index("x"), lax.axis_index("y"), lax.axis_index("c")


def _other_chips(x, y):
    return [(1 - x, y), (x, 1 - y), (1 - x, 1 - y)]


def _remote(src, dst, ssem, rsem, dev):
    return pltpu.make_async_remote_copy(src_ref=src, dst_ref=dst, send_sem=ssem, recv_sem=rsem, device_id=dev,
                                        device_id_type=MESH)


def _all_gather_chips(name, shard):
    R, C = shard.shape
    hr = R // 2

    def body(x_ref, o_ref, ssem, rsem, lsem):
        x, y, c = _place()
        k = 2 * x + y
        sib = (x, y, 1 - c)
        chips = _other_chips(x, y)

        def half(blk, cc):
            return o_ref.at[blk, pl.ds(cc * hr, hr), :]

        local = pltpu.make_async_copy(x_ref, o_ref.at[k], lsem)
        local.start()
        first = [_remote(x_ref.at[pl.ds(c * hr, hr), :], half(k, c), ssem.at[r], rsem.at[r], (px, py, c))
                 for r, (px, py) in enumerate(chips)]
        for cp in first:
            cp.start()
        passed = []
        for r, (px, py) in enumerate(chips):
            kj = 2 * px + py
            _remote(half(kj, c), half(kj, c), ssem.at[r], rsem.at[r], (px, py, c)).wait_recv()
            fw = _remote(half(kj, c), half(kj, c), ssem.at[3 + r], rsem.at[3 + r], sib)
            fw.start()
            passed.append(fw)
        for r, (px, py) in enumerate(chips):
            kj = 2 * px + py
            _remote(half(kj, 1 - c), half(kj, 1 - c), ssem.at[3 + r], rsem.at[3 + r], sib).wait_recv()
        for cp in first + passed:
            cp.wait_send()
        local.wait()

    return pl.pallas_call(
        body, name=name, in_specs=[_ANY], out_specs=_ANY, out_shape=_sds((4, R, C), shard.dtype),
        scratch_shapes=[pltpu.SemaphoreType.DMA((6,)), pltpu.SemaphoreType.DMA((6,)), pltpu.SemaphoreType.DMA],
    )(shard)


_HBM = pl.BlockSpec(memory_space=pltpu.HBM)
_SEM = pl.BlockSpec(memory_space=pltpu.SEMAPHORE)
_EFFECT = pltpu.SideEffectType.DATAFLOW_SIDE_EFFECTING


def _hbm(a):
    return pltpu.with_memory_space_constraint(a, pltpu.HBM)


def _cast_into_block(name, w, layer, after):
    _, R, C = w.shape
    tr = _tile(R, 256, 16)

    def body(w_ref, a_ref, o_ref):
        o_ref[...] = w_ref[...].astype(BF16)

    return pl.pallas_call(
        body, name=name, grid=(R // tr,),
        in_specs=[pl.BlockSpec((None, tr, C), lambda i: (layer, i, 0)), _ANY],
        out_specs=pl.BlockSpec((None, tr, C), lambda i: (2 * lax.axis_index("x") + lax.axis_index("y"), i, 0)),
        out_shape=_sds((4, R, C), BF16), compiler_params=_cp("parallel"))(w, after)


def _gather_copies(refs, ssem, rsem):
    x, y, c = _place()
    k = 2 * x + y
    sends, arrivals = [], []
    for a, lr in enumerate(refs):
        hr = lr.shape[1] // 2
        mine = lr.at[k, pl.ds(c * hr, hr), :]
        for r, (px, py) in enumerate(_other_chips(x, y)):
            i = 3 * a + r
            sends.append(_remote(mine, mine, ssem.at[i], rsem.at[i], (px, py, c)))
            arrivals.append(_remote(mine, lr.at[2 * px + py, pl.ds(c * hr, hr), :], ssem.at[i], rsem.at[i], (px, py, c)))
    return sends, arrivals


def _swap_copies(refs, ssem, rsem):
    x, y, c = _place()
    n = len(refs) // 2
    sends = []
    for a, (g, o) in enumerate(zip(refs[:n], refs[n:])):
        hr = g.shape[1] // 2
        sends.append(_remote(g.at[:, pl.ds((1 - c) * hr, hr), :], o, ssem.at[a], rsem.at[a], (x, y, 1 - c)))
    return sends, sends


def _scatter_copies(refs, ssem, rsem):
    x, y, c = _place()
    k = 2 * x + y
    n = len(refs) // 2
    sends, arrivals = [], []
    for a, (xr, lr) in enumerate(zip(refs[:n], refs[n:])):
        for r, (px, py) in enumerate(_other_chips(x, y)):
            i = 3 * a + r
            kj = 2 * px + py
            sends.append(_remote(xr.at[kj], lr.at[k], ssem.at[i], rsem.at[i], (px, py, c)))
            arrivals.append(_remote(xr.at[kj], lr.at[kj], ssem.at[i], rsem.at[i], (px, py, c)))
    return sends, arrivals


def _split_start(name, bufs, ncopies, copies_fn):
    n = len(bufs)

    def body(*refs):
        sends, _ = copies_fn(refs[:n], refs[n], refs[n + 1])
        for cp in sends:
            cp.start()
        token = refs[-1]
        token[...] = jnp.zeros_like(token)

    outs = pl.pallas_call(
        body, name=name, in_specs=[_HBM] * n,
        out_specs=(_SEM, _SEM) + (_HBM,) * n + (pl.BlockSpec(memory_space=pltpu.VMEM),),
        out_shape=(pltpu.SemaphoreType.DMA((ncopies,)), pltpu.SemaphoreType.DMA((ncopies,)))
        + tuple(pltpu.HBM(b.shape, b.dtype) for b in bufs) + (_sds((8, LANES), F32),),
        input_output_aliases={i: 2 + i for i in range(n)},
        compiler_params=pltpu.CompilerParams(has_side_effects=_EFFECT),
    )(*[_hbm(b) for b in bufs])
    return outs[0], outs[1], list(outs[2:2 + n]), outs[-1]


def _split_wait(name, started, copies_fn, after):
    ssem, rsem, bufs, _ = started
    n = len(bufs)

    def body(*refs):
        sends, arrivals = copies_fn(refs[:n], refs[n], refs[n + 1])
        for cp in sends:
            cp.wait_send()
        for cp in arrivals:
            cp.wait_recv()

    outs = pl.pallas_call(
        body, name=name, in_specs=[_HBM] * n + [_SEM, _SEM, _ANY], out_specs=(_HBM,) * n,
        out_shape=tuple(pltpu.HBM(t.shape, t.dtype) for t in bufs),
        input_output_aliases={i: i for i in range(n)},
        compiler_params=pltpu.CompilerParams(has_side_effects=_EFFECT),
    )(*bufs, ssem, rsem, after)
    return list(outs)


def _share_sibling(name, lands):
    n = len(lands)

    def body(*refs):
        l_in, l_out = refs[:n], refs[n:2 * n]
        ssem, rsem = refs[2 * n:]
        x, y, c = _place()
        sib = (x, y, 1 - c)

        def rows(ref, a, px, py, cc):
            hr = ref[a].shape[1] // 2
            return ref[a].at[2 * px + py, pl.ds(cc * hr, hr), :]

        copies = []
        for a in range(n):
            for r, (px, py) in enumerate(_other_chips(x, y)):
                cp = _remote(rows(l_in, a, px, py, c), rows(l_out, a, px, py, c), ssem.at[3 * a + r], rsem.at[3 * a + r], sib)
                cp.start()
                copies.append(cp)
        for a in range(n):
            for r, (px, py) in enumerate(_other_chips(x, y)):
                theirs = rows(l_out, a, px, py, 1 - c)
                _remote(theirs, theirs, ssem.at[3 * a + r], rsem.at[3 * a + r], sib).wait_recv()
        for cp in copies:
            cp.wait_send()

    outs = pl.pallas_call(
        body, name=name, in_specs=[_ANY] * n, out_specs=[_ANY] * n,
        out_shape=[_sds(l.shape, l.dtype) for l in lands], input_output_aliases={i: i for i in range(n)},
        scratch_shapes=[pltpu.SemaphoreType.DMA((3 * n,)), pltpu.SemaphoreType.DMA((3 * n,))],
    )(*lands)
    return list(outs)


def _add_halves(name, full, other):
    nb, R, C = full.shape
    hr = R // 2
    tr = _tile(hr, 256, 16)
    nh = hr // tr

    def body(f_ref, o_ref, s_ref):
        s_ref[...] = (f_ref[...].astype(F32) + o_ref[...].astype(F32)).astype(s_ref.dtype)

    return pl.pallas_call(
        body, name=name, grid=(nb, nh),
        in_specs=[pl.BlockSpec((None, tr, C), lambda b, i: (b, lax.axis_index("c") * nh + i, 0)),
                  pl.BlockSpec((None, tr, C), lambda b, i: (b, i, 0))],
        out_specs=pl.BlockSpec((None, tr, C), lambda b, i: (b, i, 0)), out_shape=_sds((nb, hr, C), full.dtype),
        compiler_params=_cp("parallel", "parallel"))(full, other)


def _sum_owner(name, land, mine):
    nb, hr, C = land.shape
    tr = _tile(hr, 128, 16)

    def body(l_ref, m_ref, o_ref):
        k = 2 * lax.axis_index("x") + lax.axis_index("y")
        own = m_ref[...].astype(F32)
        acc = jnp.where(k == 0, own, l_ref[0].astype(F32))
        for j in range(1, nb):
            acc = acc + jnp.where(k == j, own, l_ref[j].astype(F32))
        o_ref[...] = acc

    return pl.pallas_call(
        body, name=name, grid=(hr // tr,),
        in_specs=[pl.BlockSpec((nb, tr, C), lambda i: (0, i, 0)),
                  pl.BlockSpec((None, tr, C), lambda i: (2 * lax.axis_index("x") + lax.axis_index("y"), i, 0))],
        out_specs=pl.BlockSpec((None, tr, C), lambda i: (lax.axis_index("c"), i, 0)),
        out_shape=_sds((2, hr, C), F32), compiler_params=_cp("parallel"))(land, mine)


def _join_copies(refs, ssem, rsem):
    x, y, c = _place()
    sends, arrivals = [], []
    for a, b in enumerate(refs):
        sends.append(_remote(b.at[c], b.at[c], ssem.at[a], rsem.at[a], (x, y, 1 - c)))
        arrivals.append(_remote(b.at[c], b.at[1 - c], ssem.at[a], rsem.at[a], (x, y, 1 - c)))
    return sends, arrivals


def _all_gather_devices(name, buf):
    def body(b_ref, o_ref, ssem, rsem):
        x, y, c = _place()
        me = 4 * x + 2 * y + c
        flips = [(fx, fy, fc) for fx in (0, 1) for fy in (0, 1) for fc in (0, 1) if fx or fy or fc]
        peers = [((1 - x) if fx else x, (1 - y) if fy else y, (1 - c) if fc else c) for fx, fy, fc in flips]
        sends = [_remote(b_ref, o_ref.at[me], ssem.at[r], rsem.at[r], p) for r, p in enumerate(peers)]
        for cp in sends:
            cp.start()
        for r, (px, py, pc) in enumerate(peers):
            pid = 4 * px + 2 * py + pc
            _remote(b_ref, o_ref.at[pid], ssem.at[r], rsem.at[r], (px, py, pc)).wait_recv()
        for cp in sends:
            cp.wait_send()

    return pl.pallas_call(
        body, name=name, in_specs=[_ANY], out_specs=_ANY, out_shape=_sds((8,) + buf.shape, buf.dtype),
        scratch_shapes=[pltpu.SemaphoreType.DMA((7,)), pltpu.SemaphoreType.DMA((7,))],
    )(buf)


def _sum_devices(name, parts, own):
    n, R, C = parts.shape
    tr = _tile(R, 256, 8)

    def body(p_ref, own_ref, o_ref):
        x, y, c = _place()
        me = 4 * x + 2 * y + c
        mine = own_ref[...]
        acc = jnp.where(me == 0, mine, p_ref[0])
        for j in range(1, n):
            acc = acc + jnp.where(me == j, mine, p_ref[j])
        o_ref[...] = acc

    return pl.pallas_call(
        body, name=name, grid=(R // tr,),
        in_specs=[pl.BlockSpec((n, tr, C), lambda i: (0, i, 0)), pl.BlockSpec((tr, C), _rows)],
        out_specs=pl.BlockSpec((tr, C), _rows), out_shape=_sds((R, C), F32), compiler_params=_cp("parallel"))(parts, own)


def _adamw(name, w, m, v, parts, tok=None):
    L, R, C = w.shape
    np_ = len(parts[0])
    tr = _tile(R, max(8, (3 * VMEM_LIMIT // 4) // (2 * 4 * C * (7 + L * np_))), 8)
    flat = [p for lp in parts for p in lp]
    if tok is not None:
        flat = flat + [tok]
    c1 = 1.0 / (1.0 - ADAM_B1 ** ADAM_STEP)
    c2 = 1.0 / (1.0 - ADAM_B2 ** ADAM_STEP)

    def body(*refs):
        w_ref, m_ref, v_ref = refs[:3]
        p_refs = refs[3:3 + L * np_]
        g_ref, d_ref, nm_ref, nv_ref = refs[-4:]
        layer = pl.program_id(0)
        g = jnp.zeros((tr, C), F32)
        for l in range(L):
            gl = p_refs[l * np_][...]
            for j in range(1, np_):
                gl = gl + p_refs[l * np_ + j][...]
            g = jnp.where(layer == l, gl, g) if L > 1 else gl
        if tok is not None:
            g = g + refs[3 + L * np_][...]
        nm = ADAM_B1 * m_ref[...] + (1.0 - ADAM_B1) * g
        nv = ADAM_B2 * v_ref[...] + (1.0 - ADAM_B2) * (g * g)
        g_ref[...] = g
        nm_ref[...] = nm
        nv_ref[...] = nv
        d_ref[...] = -ADAM_LR * ((nm * c1) / (jnp.sqrt(nv * c2) + ADAM_EPS) + ADAM_WD * w_ref[...])

    stacked = pl.BlockSpec((None, tr, C), lambda l, i: (l, i, 0))
    part_specs = [pl.BlockSpec((tr, C), (lambda l, i, ll=ll: (jnp.where(l == ll, i, 0), 0)))
                  for ll in range(L) for _ in range(np_)]
    if tok is not None:
        part_specs.append(pl.BlockSpec((1, 1), lambda l, i: (0, 0)))
    return pl.pallas_call(
        body, name=name, grid=(L, R // tr), in_specs=[stacked] * 3 + part_specs, out_specs=[stacked] * 4,
        out_shape=[_sds(w.shape, F32)] * 4, compiler_params=_cp("arbitrary", "arbitrary"))(w, m, v, *flat)


_PACK_ROWS = 16


def _pack(arrs):
    pieces = []
    for a in arrs:
        f = a.reshape(-1).astype(F32)
        unit = _PACK_ROWS * LANES
        pad = (-f.shape[0]) % unit
        pieces.append(jnp.pad(f, (0, pad)))
    return jnp.concatenate(pieces).reshape(-1, LANES)


def _unpack(buf, shapes):
    flat = buf.reshape(-1)
    out, off = [], 0
    unit = _PACK_ROWS * LANES
    for s in shapes:
        n = 1
        for d in s:
            n *= d
        out.append(flat[off:off + n].reshape(s))
        off += n + ((-n) % unit)
    return out


def _swap_start(tag, full):
    nb, R, C = full.shape
    return _split_start("sw_start_" + tag, [full, lax.empty((nb, R // 2, C), full.dtype)], 1, _swap_copies)


def _reduce_start(tag, swaps, after):
    pairs = [_split_wait("sw_wait_%s_%d" % (tag, i), st, _swap_copies, after) for i, st in enumerate(swaps)]
    sums = [_add_halves("rs_add", f, o) for f, o in pairs]
    lands = [lax.empty(t.shape, t.dtype) for t in sums]
    return _split_start("rs_start_" + tag, sums + lands, 3 * len(sums), _scatter_copies)


def _reduce_finish(tag, started, after):
    bufs = _split_wait("rs_wait_" + tag, started, _scatter_copies, after)
    n = len(bufs) // 2
    halves = [_sum_owner("rs_sum", l, s) for s, l in zip(bufs[:n], bufs[n:])]
    return _split_start("jn_start_" + tag, halves, len(halves), _join_copies)


def _join_finish(tag, started, after):
    joined = _split_wait("jn_wait_" + tag, started, _join_copies, after)
    return [j.reshape(2 * j.shape[1], j.shape[2]) for j in joined]


def kernel(x, ln_ffn_pre, ffn_pre_w_in, ffn_pre_w_out, ln_mix, ln_ffn_post, ffn_post_w_in, ffn_post_w_out, gm_w_in, gm_v_norm, gm_w_s, gm_b_s, gm_w_out, ssm_w_in, ssm_conv_w, ssm_conv_b, ssm_dt_bias, ssm_a_log, ssm_d, ssm_norm, ssm_w_out, ln_final, loss_target, m_ln_ffn_pre, m_ffn_pre_w_in, m_ffn_pre_w_out, m_ln_mix, m_ln_ffn_post, m_ffn_post_w_in, m_ffn_post_w_out, m_gm_w_in, m_gm_v_norm, m_gm_w_s, m_gm_b_s, m_gm_w_out, m_ssm_w_in, m_ssm_conv_w, m_ssm_conv_b, m_ssm_dt_bias, m_ssm_a_log, m_ssm_d, m_ssm_norm, m_ssm_w_out, m_ln_final, v_ln_ffn_pre, v_ffn_pre_w_in, v_ffn_pre_w_out, v_ln_mix, v_ln_ffn_post, v_ffn_post_w_in, v_ffn_post_w_out, v_gm_w_in, v_gm_v_norm, v_gm_w_s, v_gm_b_s, v_gm_w_out, v_ssm_w_in, v_ssm_conv_w, v_ssm_conv_b, v_ssm_dt_bias, v_ssm_a_log, v_ssm_d, v_ssm_norm, v_ssm_w_out, v_ln_final):
    T, D = x.shape[1], x.shape[2]
    depth = ln_ffn_pre.shape[0]
    n_gm, n_ssm = gm_w_in.shape[0], ssm_w_in.shape[0]
    F = ffn_pre_w_out.shape[1] * 4
    GI = gm_w_out.shape[1] * 4
    GG = gm_w_s.shape[1]
    inner = ssm_w_out.shape[1] * 4
    heads = ssm_dt_bias.shape[1]
    cd = ssm_conv_w.shape[1] * 4
    groups = (cd - inner) // (2 * SSM_STATE)
    pshard = ssm_w_in.shape[2]
    pw = inner + cd + LANES
    dtcol = (inner + cd) // LANES
    kchip = 2 * lax.axis_index("x") + lax.axis_index("y")

    small_shard = _pack([jnp.swapaxes(ssm_conv_w, 1, 2), ssm_conv_b, ssm_norm])
    small_all = _all_gather_chips("ag_small", small_shard)
    cq, iq = cd // 4, inner // 4
    parts = [_unpack(small_all[k], [(n_ssm, SSM_CONV, cq), (n_ssm, cq), (n_ssm, iq)]) for k in range(4)]
    conv_wt = jnp.concatenate([p[0] for p in parts], axis=2)
    conv_b = jnp.concatenate([p[1] for p in parts], axis=1)
    norm_g = jnp.concatenate([p[2] for p in parts], axis=1)

    def pad_lanes(v):
        return jnp.pad(v, (0, LANES - v.shape[0]))[None, :]

    xc = x[0]
    saved = []

    def shards_of(i, kind):
        if kind == "pre":
            return [(ffn_pre_w_in, i), (ffn_pre_w_out, i)]
        if kind == "post":
            return [(ffn_post_w_in, i), (ffn_post_w_out, i)]
        if i % 2 == 0:
            return [(gm_w_in, i // 2), (gm_w_out, i // 2)]
        return [(ssm_w_in, i // 2), (ssm_w_out, i // 2)]

    subs = [(i, kind) for i in range(depth) for kind in ("pre", "mix", "post")]
    nsub = len(subs)
    gathers = [None] * nsub
    ahead = 2

    first_out = []

    def gather_start(s, after):
        shards = shards_of(*subs[s])
        if s == 0:
            (w_in, l_in), (w_out, l_out) = shards
            gathers[0] = _split_start("ag_start_0", [_cast_into_block("ag_cast", w_in, l_in, after)], 3, _gather_copies)
            land = _cast_into_block("ag_cast", w_out, l_out, gathers[0][3])
            first_out.append(_split_start("ag_start_0_out", [land], 3, _gather_copies))
            return
        lands = [_cast_into_block("ag_cast", w, l, after) for w, l in shards]
        gathers[s] = _split_start("ag_start_%d" % s, lands, 3 * len(lands), _gather_copies)

    def gathered(s, after):
        lands = _split_wait("ag_wait_%d" % s, gathers[s], _gather_copies, after)
        full = _share_sibling("ag_share", lands)
        tok = jnp.zeros((1, 1), F32)
        depth_now = 1 if s == 0 else ahead
        for nxt in range(s + 1, min(s + depth_now, nsub - 1) + 1):
            if gathers[nxt] is None:
                gather_start(nxt, full[0])
                tok = tok + gathers[nxt][3][0:1, 0:1]
        return full, tok

    gather_start(0, small_all)

    def ffn_fwd(s, xin, g, l):
        full, tok = gathered(s, xin)
        wi = full[0]
        h = _rms_fwd("rms_fwd", xin, g[l][None, :] + tok)
        a, s_ = _ffn_in_swiglu("ffn_in", h, wi)
        if s == 0:
            full = full + _share_sibling("ag_share", _split_wait("ag_wait_0_out", first_out[0], _gather_copies, s_))
        wo = full[1].reshape(F, D)
        xo = _matmul("ffn_out", s_, wo, "nn", F32, res=xin, scale=0.5)
        return xo, (xin, h, a, s_, wi, wo)

    for i in range(depth):
        xc, sv_pre = ffn_fwd(3 * i, xc, ln_ffn_pre, i)
        j = i // 2
        (wi, wo), tok = gathered(3 * i + 1, xc)
        h = _rms_fwd("rms_fwd", xc, ln_mix[i][None, :] + tok)
        if i % 2 == 0:
            wo = wo.reshape(GI, D)
            zpre = _matmul("gm_in", h, wi, "nn", BF16, out_blocks=2)
            u, vn = _gm_act_fwd("gm_act_fwd", zpre, gm_v_norm[j][None, :])
            bt = jnp.pad(gm_b_s[j].T, ((0, 0), (0, LANES - GG)))
            gated = _gm_spatial_fwd("gm_spatial_fwd", u, vn, gm_w_s[j], bt)
            xn = _matmul("mix_out", gated, wo, "nn", F32, res=xc, scale=1.0)
            sv_mix = (xc, h, zpre, u, vn, bt, gated, wi, wo)
        else:
            wg, wo = wi, wo.reshape(inner, D)
            wp = jnp.pad(jnp.swapaxes(wg, 0, 1).reshape(D, 4 * pshard), ((0, 0), (0, pw - 4 * pshard)))
            proj = _matmul("ssm_in", h, wp, "nn", F32)
            wt, cb_ = conv_wt[j], conv_b[j][None, :]
            act = _ssd_conv_fwd("ssd_conv_fwd", proj, wt, cb_, inner, cd)
            bias, alog = pad_lanes(ssm_dt_bias[j]), pad_lanes(ssm_a_log[j])
            dtp, acum = _ssd_dt_fwd("ssd_dt_fwd", proj, bias, alog, dtcol, heads)
            dexp = jnp.repeat(ssm_d[j], SSM_HEAD_DIM)[None, :]
            ycore, sprev = _ssd_core_fwd("ssd_core_fwd", act, dtp, acum, dexp, inner, groups)
            ng = norm_g[j][None, :]
            yn = _ssd_post_fwd("ssd_post_fwd", ycore, proj, ng, inner, groups)
            xn = _matmul("mix_out", yn, wo, "nn", F32, res=xc, scale=1.0)
            sv_mix = (xc, h, proj, act, dtp, acum, dexp, ycore, sprev, yn, wt, cb_, bias, alog, ng, wp, wo)
        xc = xn
        xc, sv_post = ffn_fwd(3 * i + 2, xc, ln_ffn_post, i)
        saved.append((sv_pre, sv_mix, sv_post))

    loss_tile, dx, dxb, dg_final = _loss_head("loss_head", xc, ln_final[None, :], loss_target[0])
    loss = lax.psum(loss_tile[0, 0], ("x", "y", "c"))

    wgrad = [None] * nsub
    in_flight = []
    deferred = []

    def reduce_finish(after):
        tok = jnp.zeros((1, 1), F32)
        while in_flight:
            s, st = in_flight.pop(0)
            wgrad[s] = _reduce_finish(str(s), st, after)
            tok = tok + wgrad[s][3][0:1, 0:1]
        return tok

    def swap_start(s, which, full):
        st = _swap_start("%d_%s" % (s, which), full)
        return st, st[3][0:1, 0:1]

    def reduce_start(s, swaps, after):
        tok = reduce_finish(after)
        if s == 0:
            deferred.append(swaps)
            return tok
        st = _reduce_start(str(s), swaps, after)
        in_flight.append((s, st))
        return tok + st[3][0:1, 0:1]

    def ffn_bwd(s, dx, dxb, sv, g, l):
        xin, h, a, s_, wi, wo = sv
        dwo = _matmul("ffn_dwo", s_, dxb, "tn", BF16, scale=0.5)
        sw_o, tok_o = swap_start(s, "o", dwo.reshape(4, F // 4, D))
        da = _ffn_ds_swiglu("ffn_ds", dxb, wo, a, 0.5)
        dwi = _matmul("ffn_dwi", h, da, "tn", BF16, out_blocks=4, tok=tok_o)
        sw_i, tok_i = swap_start(s, "i", dwi)
        dh = _matmul("ffn_dh", da, wi, "nt", F32, tok=tok_i)
        tok = reduce_start(s, [sw_i, sw_o], dh)
        return _rms_bwd("rms_bwd", xin, g[l][None, :] + tok, dh, dx)

    gl = {n: [None] * depth for n in ("pre", "mix", "post")}
    gm_g = {n: [None] * n_gm for n in ("vnorm", "ws", "bs")}
    ssm_g = {n: [None] * n_ssm for n in ("convw", "convb", "dtb", "alog", "d", "norm")}

    for i in reversed(range(depth)):
        sv_pre, sv_mix, sv_post = saved[i]
        j = i // 2
        dx, dxb, gl["post"][i] = ffn_bwd(3 * i + 2, dx, dxb, sv_post, ln_ffn_post, i)
        if i % 2 == 0:
            xin, h, zpre, u, vn, bt, gated, wi, wo = sv_mix
            dgated = _matmul("mix_dy", dxb, wo, "nt", BF16)
            dwo = _matmul("mix_dwo", gated, dxb, "tn", BF16)
            sw_o, tok_o = swap_start(3 * i + 1, "o", dwo.reshape(4, GI // 4, D))
            du, dvn, dws, dbt = _gm_spatial_bwd("gm_spatial_bwd", dgated, u, vn, gm_w_s[j], bt)
            dzpre, dvnorm = _gm_act_bwd("gm_act_bwd", zpre, du, dvn, gm_v_norm[j][None, :])
            dwi = _matmul("gm_dwi", h, dzpre, "tn", BF16, out_blocks=4, tok=tok_o)
            sw_i, tok_i = swap_start(3 * i + 1, "i", dwi)
            dh = _matmul("gm_dh", dzpre, wi, "nt", F32, tok=tok_i)
            tok = reduce_start(3 * i + 1, [sw_i, sw_o], dh)
            gm_g["vnorm"][j], gm_g["ws"][j], gm_g["bs"][j] = dvnorm[0], dws, dbt.T[:GG]
        else:
            xin, h, proj, act, dtp, acum, dexp, ycore, sprev, yn, wt, cb_, bias, alog, ng, wp, wo = sv_mix
            dyn = _matmul("ssm_dy", dxb, wo, "nt", F32)
            dwo = _matmul("mix_dwo", yn, dxb, "tn", BF16)
            sw_o, tok_o = swap_start(3 * i + 1, "o", dwo.reshape(4, inner // 4, D))
            dyc, dz, dnorm = _ssd_post_bwd("ssd_post_bwd", dyn, ycore, proj, ng, inner, groups)
            dxs, db_, dc_, ddtp_g, dac_g, dd_g = _ssd_core_bwd("ssd_core_bwd", dyc, act, dtp, acum, dexp, sprev,
                                                               inner, groups)
            ddt, dbias, dalog, dds = _ssd_dt_bwd("ssd_dt_bwd", ddtp_g, dac_g, dd_g, proj, bias, alog, dtp, dtcol, heads)
            dact = jnp.concatenate([dxs, db_, dc_], axis=1)
            dxbc, dwt, dcb = _ssd_conv_bwd("ssd_conv_bwd", dact, proj, wt, cb_, inner, cd)
            dproj = jnp.concatenate([dz, dxbc, ddt], axis=1)
            dwp = _matmul("ssm_dwi", h, dproj, "tn", BF16, tok=tok_o)
            dwi = jnp.swapaxes(dwp[:, :4 * pshard].reshape(D, 4, pshard), 0, 1)
            sw_i, tok_i = swap_start(3 * i + 1, "i", dwi)
            dh = _matmul("ssm_dh", dproj, wp, "nt", F32, tok=tok_i)
            tok = reduce_start(3 * i + 1, [sw_i, sw_o], dh)
            ssm_g["convw"][j], ssm_g["convb"][j] = dwt.T, dcb[0]
            ssm_g["dtb"][j], ssm_g["alog"][j], ssm_g["d"][j] = dbias[0, :heads], dalog[0, :heads], dds[0, :heads]
            ssm_g["norm"][j] = dnorm[0]
        dx, dxb, gl["mix"][i] = _rms_bwd("rms_bwd", xin, ln_mix[i][None, :] + tok, dh, dx)
        dx, dxb, gl["pre"][i] = ffn_bwd(3 * i, dx, dxb, sv_pre, ln_ffn_pre, i)

    small_full = [
        jnp.concatenate(gl["pre"], 0), jnp.concatenate(gl["mix"], 0), jnp.concatenate(gl["post"], 0),
        jnp.stack(gm_g["vnorm"]), jnp.stack(gm_g["ws"]), jnp.stack(gm_g["bs"]),
        jnp.stack(ssm_g["convw"]), jnp.stack(ssm_g["convb"]), jnp.stack(ssm_g["dtb"]), jnp.stack(ssm_g["alog"]),
        jnp.stack(ssm_g["d"]), jnp.stack(ssm_g["norm"]), dg_final[0],
    ]
    full_shapes = [tuple(a.shape) for a in small_full]
    packed = _pack(small_full)
    summed = _sum_devices("sum8_small", _all_gather_devices("ag8_small", packed), packed)
    (g_pre, g_mix, g_post, g_vn, g_ws, g_bs, g_cw, g_cb, g_dtb, g_al, g_d, g_nm, g_fin) = _unpack(summed, full_shapes)
    g_cw = lax.dynamic_slice_in_dim(g_cw, kchip * cq, cq, axis=1)
    g_cb = lax.dynamic_slice_in_dim(g_cb, kchip * cq, cq, axis=1)
    g_nm = lax.dynamic_slice_in_dim(g_nm, kchip * iq, iq, axis=1)
    sm_g = [g_pre, g_mix, g_post, g_vn, g_ws, g_bs, g_cw, g_cb, g_dtb, g_al, g_d, g_nm, g_fin]
    sm_w = [ln_ffn_pre, ln_mix, ln_ffn_post, gm_v_norm, gm_w_s, gm_b_s, ssm_conv_w, ssm_conv_b, ssm_dt_bias,
            ssm_a_log, ssm_d, ssm_norm, ln_final]
    sm_m = [m_ln_ffn_pre, m_ln_mix, m_ln_ffn_post, m_gm_v_norm, m_gm_w_s, m_gm_b_s, m_ssm_conv_w, m_ssm_conv_b,
            m_ssm_dt_bias, m_ssm_a_log, m_ssm_d, m_ssm_norm, m_ln_final]
    sm_v = [v_ln_ffn_pre, v_ln_mix, v_ln_ffn_post, v_gm_v_norm, v_gm_w_s, v_gm_b_s, v_ssm_conv_w, v_ssm_conv_b,
            v_ssm_dt_bias, v_ssm_a_log, v_ssm_d, v_ssm_norm, v_ln_final]
    sm_shapes = [tuple(a.shape) for a in sm_w]
    pk = [_pack(lst)[None] for lst in (sm_w, sm_m, sm_v)]
    sg_, sd_, snm_, snv_ = _adamw("adamw_small", pk[0], pk[1], pk[2], [[_pack(sm_g)]])
    small_out = [_unpack(t[0], sm_shapes) for t in (sg_, sd_, snm_, snv_)]
    small_names = ["ln_ffn_pre", "ln_mix", "ln_ffn_post", "gm_v_norm", "gm_w_s", "gm_b_s", "ssm_conv_w", "ssm_conv_b",
                   "ssm_dt_bias", "ssm_a_log", "ssm_d", "ssm_norm", "ln_final"]

    def big_update(tag, w, m, v, sub_ids, which, tok=None):
        L = w.shape[0]
        shp = w.shape
        w2, m2, v2 = (t.reshape(L, -1, shp[-1]) for t in (w, m, v))
        outs = _adamw("adamw_" + tag, w2, m2, v2, [[wfull[s][which]] for s in sub_ids], tok)
        return [o.reshape(shp) for o in outs]

    last = _reduce_start("0", deferred[0], sd_)
    tok = last[3][0:1, 0:1]
    wfull = [None] * nsub
    for s in range(1, nsub):
        wfull[s] = _join_finish(str(s), wgrad[s], sd_)
    pre_ids = [3 * i for i in range(depth)]
    post_ids = [3 * i + 2 for i in range(depth)]
    gm_ids = [3 * i + 1 for i in range(depth) if i % 2 == 0]
    ssm_ids = [3 * i + 1 for i in range(depth) if i % 2 == 1]
    big_out = {
        "ffn_post_w_in": big_update("ffn_in", ffn_post_w_in, m_ffn_post_w_in, v_ffn_post_w_in, post_ids, 0, tok),
        "ffn_post_w_out": big_update("ffn_out", ffn_post_w_out, m_ffn_post_w_out, v_ffn_post_w_out, post_ids, 1, tok),
        "gm_w_in": big_update("gm_in", gm_w_in, m_gm_w_in, v_gm_w_in, gm_ids, 0, tok),
        "gm_w_out": big_update("mix_out", gm_w_out, m_gm_w_out, v_gm_w_out, gm_ids, 1, tok),
        "ssm_w_in": big_update("ssm_in", ssm_w_in, m_ssm_w_in, v_ssm_w_in, ssm_ids, 0, tok),
        "ssm_w_out": big_update("mix_out", ssm_w_out, m_ssm_w_out, v_ssm_w_out, ssm_ids, 1, tok),
    }
    after = big_out["ffn_post_w_in"][1][0, 0:1, 0:1]
    for n in ("ffn_post_w_out", "gm_w_in", "gm_w_out", "ssm_w_in", "ssm_w_out"):
        after = after + big_out[n][1][0, 0:1, 0:1]
    wfull[0] = _join_finish("0", _reduce_finish("0", last, after), after)
    big_out["ffn_pre_w_in"] = big_update("ffn_in", ffn_pre_w_in, m_ffn_pre_w_in, v_ffn_pre_w_in, pre_ids, 0)
    big_out["ffn_pre_w_out"] = big_update("ffn_out", ffn_pre_w_out, m_ffn_pre_w_out, v_ffn_pre_w_out, pre_ids, 1)

    order = ["ln_ffn_pre", "ffn_pre_w_in", "ffn_pre_w_out", "ln_mix", "ln_ffn_post", "ffn_post_w_in", "ffn_post_w_out",
             "gm_w_in", "gm_v_norm", "gm_w_s", "gm_b_s", "gm_w_out", "ssm_w_in", "ssm_conv_w", "ssm_conv_b",
             "ssm_dt_bias", "ssm_a_log", "ssm_d", "ssm_norm", "ssm_w_out", "ln_final"]

    def pick(kind, n):
        if n in big_out:
            return big_out[n][kind]
        return small_out[kind][small_names.index(n)]

    outs = [loss, dx[None]]
    for kind in range(4):
        outs.extend(pick(kind, n) for n in order)
    return tuple(outs)
```
